```python
import math
import jax
import jax.numpy as jnp
from jax import lax
import numpy as np

D_MODEL = 1024
BATCH = 8
SEQ = 2048
DEPTH = 2
DEC_BATCH = 128
DEC_SEQ = 4
PAST_LEN = 16384
PAGE_SIZE = 128

MIX_WIDTH = D_MODEL
N_MIXERS = 4
GROUP_WIDTH = MIX_WIDTH // N_MIXERS
HEAD_DIM = 64
N_HEADS = GROUP_WIDTH // HEAD_DIM
CONV_WIDTH = 4
GDN_CONV_CH = 3 * GROUP_WIDTH
RET_THETA = 10000.0
SSD_STATE = 128
SSD_GROUPS = 2
SSD_CONV_CH = GROUP_WIDTH + 2 * SSD_GROUPS * SSD_STATE
D_FF = 4 * D_MODEL
CHUNK = 64
CHUNK_VEC = 16
EPS = 1e-6
SPLIT_SIZES = (
    GROUP_WIDTH, GROUP_WIDTH, GROUP_WIDTH, GROUP_WIDTH,
    GDN_CONV_CH, GROUP_WIDTH, N_HEADS, N_HEADS,
    GROUP_WIDTH, GROUP_WIDTH, GROUP_WIDTH, GROUP_WIDTH,
    GROUP_WIDTH, SSD_CONV_CH, N_HEADS,
)
P_TOTAL = sum(SPLIT_SIZES)

kernel_name = 'hymba_quad_linear_hybrid_step'


def _rms(x):
    return x * lax.rsqrt(jnp.mean(x * x, axis=-1, keepdims=True) + EPS)


def _l2norm(x):
    return x * lax.rsqrt(jnp.sum(x * x, axis=-1, keepdims=True) + EPS)


def _heads(t):
    return t.reshape(t.shape[:-1] + (N_HEADS, t.shape[-1] // N_HEADS))


def _to_chunks(t, c):
    return t.reshape((t.shape[0], t.shape[1] // c, c) + t.shape[2:])


def _causal_conv(x, buf, w):
    seq_len = x.shape[1]
    xe = jnp.concatenate([buf.astype(jnp.float32), x], axis=1)
    y = xe[:, 0:seq_len] * w[0]
    for j in range(1, CONV_WIDTH):
        y = y + xe[:, j:j + seq_len] * w[j]
    return y, xe[:, seq_len:]


def _rotary(x, pos):
    half = HEAD_DIM // 2
    inv_freq = RET_THETA ** (-jnp.arange(half, dtype=jnp.float32) / half)
    ang = pos.astype(jnp.float32)[:, None] * inv_freq[None, :]
    cos = jnp.cos(ang)[None, :, None, :]
    sin = jnp.sin(ang)[None, :, None, :]
    x1, x2 = x[..., :half], x[..., half:]
    return jnp.concatenate([x1 * cos - x2 * sin, x1 * sin + x2 * cos], axis=-1)


def _chunk_scalar_decay(q, k, v, log_a, s0):
    bsz, seq_len = q.shape[:2]
    c = math.gcd(seq_len, CHUNK)
    q, k, v, log_a = (_to_chunks(t, c) for t in (q, k, v, log_a))
    g = jnp.cumsum(log_a, axis=2)
    g_last = g[:, :, -1]
    causal = jnp.tril(jnp.ones((c, c), dtype=bool))
    diff = g[:, :, :, None, :] - g[:, :, None, :, :]
    decay = jnp.exp(jnp.where(causal[:, :, None], diff, -jnp.inf))
    scores = jnp.einsum('bnihk,bnjhk->bnijh', q, k) * decay
    o_intra = jnp.einsum('bnijh,bnjhv->bnihv', scores, v)
    k_end = k * jnp.exp(g_last[:, :, None] - g)[..., None]
    ds = jnp.einsum('bnjhk,bnjhv->bnhkv', k_end, v)
    a_chunk = jnp.exp(g_last)

    def step(s, inp):
        a_c, ds_c = inp
        return a_c[..., None, None] * s + ds_c, s

    s_fin, s_start = lax.scan(step, s0, (jnp.moveaxis(a_chunk, 1, 0), jnp.moveaxis(ds, 1, 0)))
    s_start = jnp.moveaxis(s_start, 0, 1)
    o_inter = jnp.einsum('bnihk,bnhkv->bnihv', q * jnp.exp(g)[..., None], s_start)
    return (o_intra + o_inter).reshape(bsz, seq_len, q.shape[3], v.shape[-1]), s_fin


def _chunk_vector_decay(q, k, v, log_f, s0):
    bsz, seq_len = q.shape[:2]
    c = math.gcd(seq_len, CHUNK_VEC)
    q, k, v, log_f = (_to_chunks(t, c) for t in (q, k, v, log_f))
    g = jnp.cumsum(log_f, axis=2)
    g_last = g[:, :, -1]
    causal = jnp.tril(jnp.ones((c, c), dtype=bool))
    diff = g[:, :, :, None] - g[:, :, None]
    decay = jnp.exp(jnp.where(causal[:, :, None, None], diff, -jnp.inf))
    scores = jnp.einsum('bnihk,bnjhk,bnijhk->bnijh', q, k, decay)
    o_intra = jnp.einsum('bnijh,bnjhv->bnihv', scores, v)
    k_end = k * jnp.exp(g_last[:, :, None] - g)
    ds = jnp.einsum('bnjhk,bnjhv->bnhkv', k_end, v)
    a_chunk = jnp.exp(g_last)

    def step(s, inp):
        a_c, ds_c = inp
        return a_c[..., None] * s + ds_c, s

    s_fin, s_start = lax.scan(step, s0, (jnp.moveaxis(a_chunk, 1, 0), jnp.moveaxis(ds, 1, 0)))
    s_start = jnp.moveaxis(s_start, 0, 1)
    o_inter = jnp.einsum('bnihk,bnhkv->bnihv', q * jnp.exp(g), s_start)
    return (o_intra + o_inter).reshape(bsz, seq_len, q.shape[3], v.shape[-1]), s_fin


def _chunk_gated_delta(q, k, v, beta, log_a, s0):
    bsz, seq_len, n_h, dk = q.shape
    c = math.gcd(seq_len, CHUNK)
    q, k, v = (_to_chunks(t, c).transpose(0, 1, 3, 2, 4) for t in (q, k, v))
    beta, log_a = (_to_chunks(t, c).transpose(0, 1, 3, 2) for t in (beta, log_a))
    g = jnp.cumsum(log_a, axis=-1)
    g_last = g[..., -1]
    incl = jnp.tril(jnp.ones((c, c), dtype=bool))
    strict = jnp.tril(jnp.ones((c, c), dtype=bool), -1)
    decay = jnp.exp(jnp.where(incl, g[..., :, None] - g[..., None, :], -jnp.inf))
    kk = jnp.einsum('bnhik,bnhjk->bnhij', k, k)
    a_mat = jnp.where(strict, beta[..., None] * kk * decay, 0.0)
    rhs = jnp.concatenate([(beta * jnp.exp(g))[..., None] * k, beta[..., None] * v], axis=-1)
    sol = lax.linalg.triangular_solve(a_mat, rhs, left_side=True, lower=True, unit_diagonal=True)
    w_mat, u0 = sol[..., :dk], sol[..., dk:]
    p_mat = jnp.einsum('bnhik,bnhjk->bnhij', q, k) * decay
    q_g = q * jnp.exp(g)[..., None]
    k_end = k * jnp.exp(g_last[..., None] - g)[..., None]
    a_chunk = jnp.exp(g_last)

    def step(s, inp):
        w_c, u0_c, p_c, qg_c, ke_c, a_c = inp
        u = u0_c - jnp.einsum('bhck,bhkv->bhcv', w_c, s)
        o = jnp.einsum('bhck,bhkv->bhcv', qg_c, s) + jnp.einsum('bhij,bhjv->bhiv', p_c, u)
        s = a_c[..., None, None] * s + jnp.einsum('bhck,bhcv->bhkv', ke_c, u)
        return s, o

    xs = tuple(jnp.moveaxis(t, 1, 0) for t in (w_mat, u0, p_mat, q_g, k_end, a_chunk))
    s_fin, o = lax.scan(step, s0, xs)
    o = jnp.moveaxis(o, 0, 1).transpose(0, 1, 3, 2, 4)
    return o.reshape(bsz, seq_len, n_h, v.shape[-1]), s_fin


def _trunk(x, pos, states, p, out_dtype):
    f32 = jnp.float32
    st_hg, st_gd, st_gc, st_rt, st_sd, st_sc = states
    bsz, seq_len = x.shape[:2]
    x = x.astype(f32)
    split_at = [int(i) for i in np.cumsum(SPLIT_SIZES)[:-1]]
    sm = jax.nn.softmax(p['hgrn_lb_logits'].astype(f32), axis=0)
    lower_bounds = jnp.cumsum(sm, axis=0) - sm[0]
    log_gamma = jnp.log(1.0 - jnp.exp2(-5.0 - jnp.arange(N_HEADS, dtype=f32)))
    rep = N_HEADS // SSD_GROUPS
    gsz = GROUP_WIDTH // SSD_GROUPS
    new = ([], [], [], [], [], [])
    for l in range(DEPTH):
        h = _rms(x) * p['norm_mix'][l]
        proj = h @ p['w_in'][l]
        (hq, hf, hi, hg, gqkv, gz, ga, gb, rq, rk, rv, rg, sz, sxbc, sdt) = jnp.split(proj, split_at, axis=-1)

        f = lower_bounds[l] + (1.0 - lower_bounds[l]) * jax.nn.sigmoid(hf)
        o_a, s_a = _chunk_vector_decay(_heads(jax.nn.sigmoid(hq)), _heads(1.0 - f), _heads(hi),
                                       _heads(jnp.log(f)), st_hg[l].astype(f32))
        o_a = _rms(o_a) * p['hgrn_norm'][l] * jax.nn.silu(_heads(hg))

        qkv, c_b = _causal_conv(gqkv, st_gc[l], p['gdn_conv_w'][l])
        gq, gk, gv = jnp.split(jax.nn.silu(qkv), 3, axis=-1)
        beta = jax.nn.sigmoid(gb)
        log_alpha = -jnp.exp(p['gdn_a_log'][l]) * jax.nn.softplus(ga + p['gdn_dt_bias'][l])
        o_b, s_b = _chunk_gated_delta(_l2norm(_heads(gq)) * HEAD_DIM ** -0.5, _l2norm(_heads(gk)),
                                      _heads(gv), beta, log_alpha, st_gd[l].astype(f32))
        o_b = _rms(o_b) * p['gdn_norm'][l] * jax.nn.silu(_heads(gz))

        rq_h = _rotary(_heads(rq), pos)
        rk_h = _rotary(_heads(rk), pos) * HEAD_DIM ** -0.5
        log_a_c = jnp.broadcast_to(log_gamma, (bsz, seq_len, N_HEADS))
        o_c, s_c = _chunk_scalar_decay(rq_h, rk_h, _heads(rv), log_a_c, st_rt[l].astype(f32))
        o_c = _rms(o_c) * jax.nn.silu(_heads(rg))

        xbc, c_d = _causal_conv(sxbc, st_sc[l], p['ssd_conv_w'][l])
        xbc = jax.nn.silu(xbc + p['ssd_conv_b'][l])
        xs, bmat, cmat = jnp.split(xbc, [GROUP_WIDTH, GROUP_WIDTH + SSD_GROUPS * SSD_STATE], axis=-1)
        xs = _heads(xs)
        bmat = jnp.repeat(bmat.reshape(bsz, seq_len, SSD_GROUPS, SSD_STATE), rep, axis=2)
        cmat = jnp.repeat(cmat.reshape(bsz, seq_len, SSD_GROUPS, SSD_STATE), rep, axis=2)
        dt = jax.nn.softplus(sdt + p['ssd_dt_bias'][l])
        log_a_d = -jnp.exp(p['ssd_a_log'][l]) * dt
        y_d, s_d = _chunk_scalar_decay(cmat, bmat, xs * dt[..., None], log_a_d, st_sd[l].astype(f32))
        y_d = (y_d + p['ssd_d'][l][:, None] * xs).reshape(bsz, seq_len, GROUP_WIDTH) * jax.nn.silu(sz)
        y_d = _rms(y_d.reshape(bsz, seq_len, SSD_GROUPS, gsz)).reshape(bsz, seq_len, GROUP_WIDTH) * p['ssd_norm'][l]

        mix = jnp.concatenate([o_a.reshape(bsz, seq_len, GROUP_WIDTH), o_b.reshape(bsz, seq_len, GROUP_WIDTH),
                               o_c.reshape(bsz, seq_len, GROUP_WIDTH), y_d], axis=-1)
        x = x + mix @ p['w_out'][l]
        h = _rms(x) * p['norm_ffn'][l]
        x = x + jnp.square(jax.nn.relu(h @ p['w_up'][l])) @ p['w_down'][l]
        for lst, s in zip(new, (s_a, s_b, c_b, s_c, s_d, c_d)):
            lst.append(s.astype(out_dtype))
    y = (_rms(x) * p['norm_final']).astype(out_dtype)
    return y, tuple(jnp.stack(lst) for lst in new)


def _dt_bias(k, shape):
    dt = jnp.exp(jax.random.uniform(k, shape, jnp.float32, math.log(1e-3), math.log(1e-1)))
    return dt + jnp.log(-jnp.expm1(-dt))


def setup_inputs(seed: int = 0) -> dict:
    key = jax.random.key(seed)
    ks = jax.random.split(key, 32)
    f32 = jnp.float32

    def nrm(k, shape, s):
        return jax.random.normal(k, shape, f32) * s

    H, K = N_HEADS, HEAD_DIM
    return {
        'x_prompt': nrm(ks[0], (BATCH, SEQ, D_MODEL), 1.0),
        'x_sample': nrm(ks[1], (DEC_BATCH, DEC_SEQ, D_MODEL), 1.0),
        'state_hgrn': nrm(ks[2], (DEPTH, DEC_BATCH, H, K, K), 0.5),
        'state_gdn': nrm(ks[3], (DEPTH, DEC_BATCH, H, K, K), 0.3),
        'state_gdn_conv': nrm(ks[4], (DEPTH, DEC_BATCH, CONV_WIDTH - 1, GDN_CONV_CH), 1.0),
        'state_ret': nrm(ks[5], (DEPTH, DEC_BATCH, H, K, K), 1.0),
        'state_ssd': nrm(ks[6], (DEPTH, DEC_BATCH, H, SSD_STATE, K), 0.3),
        'state_ssd_conv': nrm(ks[7], (DEPTH, DEC_BATCH, CONV_WIDTH - 1, SSD_CONV_CH), 1.0),
        'norm_mix': 1.0 + nrm(ks[8], (DEPTH, D_MODEL), 0.02),
        'w_in': nrm(ks[9], (DEPTH, D_MODEL, P_TOTAL), D_MODEL ** -0.5),
        'hgrn_lb_logits': nrm(ks[10], (DEPTH, GROUP_WIDTH), 0.5),
        'hgrn_norm': 1.0 + nrm(ks[11], (DEPTH, HEAD_DIM), 0.02),
        'gdn_conv_w': nrm(ks[12], (DEPTH, CONV_WIDTH, GDN_CONV_CH), CONV_WIDTH ** -0.5),
        'gdn_a_log': jnp.log(jax.random.uniform(ks[13], (DEPTH, H), f32, 1.0, 16.0)),
        'gdn_dt_bias': _dt_bias(ks[14], (DEPTH, H)),
        'gdn_norm': 1.0 + nrm(ks[15], (DEPTH, HEAD_DIM), 0.02),
        'ssd_conv_w': nrm(ks[16], (DEPTH, CONV_WIDTH, SSD_CONV_CH), CONV_WIDTH ** -0.5),
        'ssd_conv_b': nrm(ks[17], (DEPTH, SSD_CONV_CH), 0.02),
        'ssd_dt_bias': _dt_bias(ks[18], (DEPTH, H)),
        'ssd_a_log': jnp.log(jax.random.uniform(ks[19], (DEPTH, H), f32, 1.0, 16.0)),
        'ssd_d': 1.0 + nrm(ks[20], (DEPTH, H), 0.1),
        'ssd_norm': 1.0 + nrm(ks[21], (DEPTH, GROUP_WIDTH), 0.02),
        'w_out': nrm(ks[22], (DEPTH, MIX_WIDTH, D_MODEL), MIX_WIDTH ** -0.5),
        'norm_ffn': 1.0 + nrm(ks[23], (DEPTH, D_MODEL), 0.02),
        'w_up': nrm(ks[24], (DEPTH, D_MODEL, D_FF), D_MODEL ** -0.5),
        'w_down': nrm(ks[25], (DEPTH, D_FF, D_MODEL), D_FF ** -0.5),
        'norm_final': 1.0 + nrm(ks[26], (D_MODEL,), 0.02),
    }


def reference(x_prompt, x_sample, state_hgrn, state_gdn, state_gdn_conv, state_ret, state_ssd, state_ssd_conv,
              norm_mix, w_in, hgrn_lb_logits, hgrn_norm, gdn_conv_w, gdn_a_log, gdn_dt_bias, gdn_norm,
              ssd_conv_w, ssd_conv_b, ssd_dt_bias, ssd_a_log, ssd_d, ssd_norm,
              w_out, norm_ffn, w_up, w_down, norm_final):
    params = dict(norm_mix=norm_mix, w_in=w_in, hgrn_lb_logits=hgrn_lb_logits, hgrn_norm=hgrn_norm,
                  gdn_conv_w=gdn_conv_w, gdn_a_log=gdn_a_log, gdn_dt_bias=gdn_dt_bias, gdn_norm=gdn_norm,
                  ssd_conv_w=ssd_conv_w, ssd_conv_b=ssd_conv_b, ssd_dt_bias=ssd_dt_bias, ssd_a_log=ssd_a_log,
                  ssd_d=ssd_d, ssd_norm=ssd_norm, w_out=w_out, norm_ffn=norm_ffn, w_up=w_up,
                  w_down=w_down, norm_final=norm_final)
    out_dtype = x_prompt.dtype
    bp, lp = x_prompt.shape[:2]
    ls = x_sample.shape[1]
    sample_states = (state_hgrn, state_gdn, state_gdn_conv, state_ret, state_ssd, state_ssd_conv)
    prompt_states = tuple(jnp.zeros((DEPTH, bp) + s.shape[2:], jnp.float32) for s in sample_states)
    y_prompt, (hgrn_p, gdn_p, gdnc_p, ret_p, ssd_p, ssdc_p) = _trunk(
        x_prompt, jnp.arange(lp), prompt_states, params, out_dtype)
    y_sample, (hgrn_s, gdn_s, gdnc_s, ret_s, ssd_s, ssdc_s) = _trunk(
        x_sample, PAST_LEN + jnp.arange(ls), sample_states, params, out_dtype)
    return (y_prompt, y_sample, hgrn_p, hgrn_s, gdn_p, gdn_s, gdnc_p, gdnc_s,
            ret_p, ret_s, ssd_p, ssd_s, ssdc_p, ssdc_s)
```

```python
import functools
import math

import numpy as np
import jax
import jax.numpy as jnp
from jax import lax
from jax.experimental import pallas as pl
from jax.experimental.pallas import tpu as pltpu

F32 = jnp.float32
BF16 = jnp.bfloat16

D_MODEL = 1024
GROUP_WIDTH = 256
HEAD_DIM = 64
N_HEADS = 4
CONV_WIDTH = 4
SSD_STATE = 128
D_FF = 4096
RET_THETA = 10000.0
EPS = 1e-6
DEPTH = 2

COL_HGRN = 0
COL_GDN = 1024
COL_RET = 2048
COL_SSD = 3072
COL_SMALL = 4096
P_PAD = 4224
SMALL_GA, SMALL_GB, SMALL_SDT = 0, 4, 8

VMEM_LIMIT = 56 * 1024 * 1024
LOG_GAMMA = [math.log(1.0 - 2.0 ** (-5.0 - h)) for h in range(N_HEADS)]


def _dot(a, b):
    return jnp.dot(a, b, preferred_element_type=F32)


def _dot_nt(a, b):
    return lax.dot_general(a, b, (((1,), (1,)), ((), ())), preferred_element_type=F32)


def _dot_tn(a, b):
    return lax.dot_general(a, b, (((0,), (0,)), ((), ())), preferred_element_type=F32)


def _split3(x):
    hi = x.astype(BF16)
    r1 = x - hi.astype(F32)
    mid = r1.astype(BF16)
    lo = (r1 - mid.astype(F32)).astype(BF16)
    return hi, mid, lo


def _exact_dot(x, sel):
    hi, mid, lo = _split3(x)
    return _dot(hi, sel) + _dot(mid, sel) + _dot(lo, sel)


def _exact_dot_left(sel, x):
    hi, mid, lo = _split3(x)
    return _dot(sel, hi) + _dot(sel, mid) + _dot(sel, lo)


def _iota(shape, dim):
    return lax.broadcasted_iota(jnp.int32, shape, dim)


def _head_of_lane(n_lanes, width=HEAD_DIM):
    return _iota((1, n_lanes), 1) // width


def _head_masks(n_lanes=GROUP_WIDTH, width=HEAD_DIM):
    hl = _head_of_lane(n_lanes, width)
    return [hl == h for h in range(n_lanes // width)]


def _stack_heads(x, masks):
    return jnp.concatenate([jnp.where(m, x, jnp.zeros_like(x)) for m in masks], axis=0)


def _unstack_heads(y, masks, c):
    out = jnp.where(masks[0], y[0:c], 0.0)
    for h in range(1, len(masks)):
        out = out + jnp.where(masks[h], y[h * c:(h + 1) * c], 0.0)
    return out


def _block_ones(n, width, dtype=BF16):
    r = _iota((n, n), 0) // width
    c = _iota((n, n), 1) // width
    return (r == c).astype(dtype)


def _block_mask(n, rwidth, cwidth):
    return (_iota((n, n), 0) // rwidth) == (_iota((n, n), 1) // cwidth)


def _lower_tri(c, dtype=BF16):
    return (_iota((c, c), 0) >= _iota((c, c), 1)).astype(dtype)


def _cumsum_rows(x, c):
    return _exact_dot_left(_lower_tri(c), x)


def _sigmoid(x):
    return 1.0 / (1.0 + jnp.exp(-x))


def _silu(x):
    return x * _sigmoid(x)


def _softplus(x):
    return jnp.maximum(x, 0.0) + jnp.log(1.0 + jnp.exp(-jnp.abs(x)))


def _rms_rows(x):
    return x * lax.rsqrt(jnp.mean(x * x, axis=-1, keepdims=True) + EPS)


def _head_sumsq(x, ones_bd):
    return _exact_dot(x * x, ones_bd)


def _expand_small(small, first_lane):
    r = _iota((128, GROUP_WIDTH), 0)
    c = _iota((128, GROUP_WIDTH), 1) // HEAD_DIM
    sel = (r == c + first_lane).astype(BF16)
    return _exact_dot(small, sel)


def _decay_diff_operands(g):
    hi, mid, lo = (x.astype(F32) for x in _split3(g))
    pos = _iota(g.shape, 1) % HEAD_DIM
    a = jnp.where(pos == 0, hi, jnp.where(pos == 1, mid, jnp.where(pos == 2, lo,
                  jnp.where(pos < 6, 1.0, 0.0))))
    b = jnp.where(pos < 3, 1.0, jnp.where(pos == 3, -hi, jnp.where(pos == 4, -mid,
                  jnp.where(pos == 5, -lo, 0.0))))
    return a, b


def _extract_blocks(s_wide, rows, width):
    sel = ((_iota((GROUP_WIDTH, width), 0) % width) == _iota((GROUP_WIDTH, width), 1)).astype(BF16)
    return _exact_dot(s_wide, sel)


def _proj_kernel(x_ref, nw_ref, w_ref, o_ref):
    h = _rms_rows(x_ref[...]) * nw_ref[...]
    o_ref[...] = _dot(h.astype(BF16), w_ref[...])


def _proj(x2d, norm_w, w_bf16):
    t = x2d.shape[0]
    tm = min(t, 512)
    return pl.pallas_call(
        _proj_kernel,
        grid=(t // tm,),
        in_specs=[pl.BlockSpec((tm, D_MODEL), lambda i: (i, 0)),
                  pl.BlockSpec((1, D_MODEL), lambda i: (0, 0)),
                  pl.BlockSpec((D_MODEL, P_PAD), lambda i: (0, 0))],
        out_specs=pl.BlockSpec((tm, P_PAD), lambda i: (i, 0)),
        out_shape=jax.ShapeDtypeStruct((t, P_PAD), F32),
        compiler_params=pltpu.CompilerParams(dimension_semantics=("arbitrary",),
                                             vmem_limit_bytes=VMEM_LIMIT),
        name="norm_in_proj",
    )(x2d, norm_w.reshape(1, D_MODEL), w_bf16)


def _ffn_kernel(x_ref, oa_ref, ob_ref, oc_ref, od_ref, wo_ref, nf_ref, wu_ref, wd_ref, nfin_ref,
                o_ref, *, final):
    mix = jnp.concatenate([oa_ref[...], ob_ref[...], oc_ref[...], od_ref[...]], axis=1)
    x = x_ref[...] + _dot(mix.astype(BF16), wo_ref[...])
    h = (_rms_rows(x) * nf_ref[...]).astype(BF16)
    acc = x
    ft = 1024
    for t in range(D_FF // ft):
        up = _dot(h, wu_ref[:, t * ft:(t + 1) * ft])
        up = jnp.square(jnp.maximum(up, 0.0)).astype(BF16)
        acc = acc + _dot(up, wd_ref[t * ft:(t + 1) * ft, :])
    if final:
        acc = _rms_rows(acc) * nfin_ref[...]
    o_ref[...] = acc


def _out_ffn(x2d, mixes, wo, nf, wu, wd, nfin, final):
    t = x2d.shape[0]
    tm = min(t, 512)
    row = lambda i: (i, 0)
    fixed = lambda i: (0, 0)
    return pl.pallas_call(
        functools.partial(_ffn_kernel, final=final),
        grid=(t // tm,),
        in_specs=[pl.BlockSpec((tm, D_MODEL), row)]
                 + [pl.BlockSpec((tm, GROUP_WIDTH), row)] * 4
                 + [pl.BlockSpec((D_MODEL, D_MODEL), fixed),
                    pl.BlockSpec((1, D_MODEL), fixed),
                    pl.BlockSpec((D_MODEL, D_FF), fixed),
                    pl.BlockSpec((D_FF, D_MODEL), fixed),
                    pl.BlockSpec((1, D_MODEL), fixed)],
        out_specs=pl.BlockSpec((tm, D_MODEL), row),
        out_shape=jax.ShapeDtypeStruct((t, D_MODEL), F32),
        compiler_params=pltpu.CompilerParams(dimension_semantics=("arbitrary",),
                                             vmem_limit_bytes=VMEM_LIMIT),
        name="out_proj_ffn",
    )(x2d, *mixes, wo, nf.reshape(1, D_MODEL), wu, wd, nfin.reshape(1, D_MODEL))


def _swap_halves(x):
    first = (_iota((1, 128), 1) % HEAD_DIM) < (HEAD_DIM // 2)
    parts = []
    for p in range(GROUP_WIDTH // 128):
        xp = x[:, p * 128:(p + 1) * 128]
        parts.append(jnp.where(first, pltpu.roll(xp, 96, 1), pltpu.roll(xp, 32, 1)))
    return jnp.concatenate(parts, axis=1)


def _conv_silu(xe_ref, halo, x, w, bias, first_chunk, c):
    xe_ref[0:8, :] = jnp.where(first_chunk, jnp.zeros_like(halo), halo)
    xe_ref[8:, :] = x
    y = w[3:4, :] * x
    for j in range(CONV_WIDTH - 1):
        y = y + w[j:j + 1, :] * xe_ref[5 + j:5 + j + c, :]
    if bias is not None:
        y = y + bias
    return _silu(y)


def _ret_prompt_kernel(blk_ref, cos_ref, sin_ref, o_ref, st_ref, s_scr, *, c, n_chunks):
    ci = pl.program_id(1)

    @pl.when(ci == 0)
    def _():
        s_scr[...] = jnp.zeros_like(s_scr)

    blk = blk_ref[0]
    rq, rk, rv, rg = (blk[:, i * GROUP_WIDTH:(i + 1) * GROUP_WIDTH] for i in range(4))
    cosv, sinv = cos_ref[...], sin_ref[...]
    q = rq * cosv + _swap_halves(rq) * sinv
    k = (rk * cosv + _swap_halves(rk) * sinv) * (HEAD_DIM ** -0.5)
    v = rv

    masks = _head_masks()
    hl = _head_of_lane(GROUP_WIDTH)
    lg = jnp.full((1, GROUP_WIDTH), LOG_GAMMA[0], F32)
    for h in range(1, N_HEADS):
        lg = jnp.where(hl == h, LOG_GAMMA[h], lg)
    ri = _iota((c, 1), 0).astype(F32)
    dij = (_iota((c, c), 0) - _iota((c, c), 1)).astype(F32)
    causal = dij >= 0.0
    decay = jnp.concatenate(
        [jnp.where(causal, jnp.exp(jnp.maximum(dij, 0.0) * LOG_GAMMA[h]), 0.0) for h in range(N_HEADS)],
        axis=0)

    scores = _dot_nt(_stack_heads(q, masks).astype(BF16), k.astype(BF16)) * decay
    o = _unstack_heads(_dot(scores.astype(BF16), v.astype(BF16)), masks, c)
    s = s_scr[...]
    o = o + _dot((q * jnp.exp((ri + 1.0) * lg)).astype(BF16), s.astype(BF16))
    kend = k * jnp.exp((float(c - 1) - ri) * lg)
    ds = _dot_tn(kend.astype(BF16), v.astype(BF16))
    s_new = jnp.exp(float(c) * lg) * s + jnp.where(_block_mask(GROUP_WIDTH, HEAD_DIM, HEAD_DIM), ds, 0.0)
    s_scr[...] = s_new

    ss = _head_sumsq(o, _block_ones(GROUP_WIDTH, HEAD_DIM))
    o_ref[0] = o * lax.rsqrt(ss * (1.0 / HEAD_DIM) + EPS) * _silu(rg)

    @pl.when(ci == n_chunks - 1)
    def _():
        st_ref[0] = _extract_blocks(s_new, GROUP_WIDTH, HEAD_DIM)


def _ret_prompt(proj3, cos_t, sin_t, c):
    b, l, _ = proj3.shape
    n = l // c
    return pl.pallas_call(
        functools.partial(_ret_prompt_kernel, c=c, n_chunks=n),
        grid=(b, n),
        in_specs=[pl.BlockSpec((1, c, 1024), lambda bi, ci: (bi, ci, COL_RET // 1024)),
                  pl.BlockSpec((c, GROUP_WIDTH), lambda bi, ci: (ci, 0)),
                  pl.BlockSpec((c, GROUP_WIDTH), lambda bi, ci: (ci, 0))],
        out_specs=[pl.BlockSpec((1, c, GROUP_WIDTH), lambda bi, ci: (bi, ci, 0)),
                   pl.BlockSpec((1, GROUP_WIDTH, HEAD_DIM), lambda bi, ci: (bi, 0, 0))],
        out_shape=[jax.ShapeDtypeStruct((b, l, GROUP_WIDTH), F32),
                   jax.ShapeDtypeStruct((b, GROUP_WIDTH, HEAD_DIM), F32)],
        scratch_shapes=[pltpu.VMEM((GROUP_WIDTH, GROUP_WIDTH), F32)],
        compiler_params=pltpu.CompilerParams(dimension_semantics=("arbitrary", "arbitrary"),
                                             vmem_limit_bytes=VMEM_LIMIT),
        name="retention_prompt",
    )(proj3, cos_t, sin_t)


def _ssd_prompt_kernel(blk_ref, halo_ref, small_ref, cw_ref, cb_ref, dtb_ref, alog_ref, dskip_ref, nw_ref,
                       o_ref, st_ref, s_scr, xe_scr, *, c, n_chunks):
    ci = pl.program_id(1)

    @pl.when(ci == 0)
    def _():
        s_scr[...] = jnp.zeros_like(s_scr)

    blk = blk_ref[0]
    sz = blk[:, 0:GROUP_WIDTH]
    xbc = _conv_silu(xe_scr, halo_ref[0][:, GROUP_WIDTH:], blk[:, GROUP_WIDTH:], cw_ref[...], cb_ref[...],
                     ci == 0, c)
    xs = xbc[:, 0:256]
    bmat = xbc[:, 256:512]
    cmat = xbc[:, 512:768]

    masks = _head_masks()
    dt = _softplus(_expand_small(small_ref[0], SMALL_SDT) + dtb_ref[...])
    la = -jnp.exp(alog_ref[...]) * dt
    g = _cumsum_rows(la, c)
    g_last = g[c - 1:c, :]
    da, db = _decay_diff_operands(g)
    diff = _dot_nt(_stack_heads(da, masks).astype(BF16), db.astype(BF16))
    causal = _iota((c, c), 0) >= _iota((c, c), 1)
    causal4 = jnp.concatenate([causal] * N_HEADS, axis=0)
    decay = jnp.where(causal4, jnp.exp(jnp.minimum(diff, 0.0)), 0.0)

    cb = [_dot_nt(cmat[:, gi * 128:(gi + 1) * 128].astype(BF16), bmat[:, gi * 128:(gi + 1) * 128].astype(BF16))
          for gi in range(2)]
    scores = jnp.concatenate([cb[0], cb[0], cb[1], cb[1]], axis=0) * decay
    v = xs * dt
    y = _unstack_heads(_dot(scores.astype(BF16), v.astype(BF16)), masks, c)
    s = s_scr[...]
    y = y + _dot(cmat.astype(BF16), s.astype(BF16)) * jnp.exp(g)
    vend = v * jnp.exp(g_last - g)
    ds = _dot_tn(bmat.astype(BF16), vend.astype(BF16))
    s_new = jnp.exp(g_last) * s + jnp.where(_block_mask(GROUP_WIDTH, 128, 128), ds, 0.0)
    s_scr[...] = s_new

    y = (y + dskip_ref[...] * xs) * _silu(sz)
    halves = [_rms_rows(y[:, gi * 128:(gi + 1) * 128]) for gi in range(2)]
    o_ref[0] = jnp.concatenate(halves, axis=1) * nw_ref[...]

    @pl.when(ci == n_chunks - 1)
    def _():
        for h in range(N_HEADS):
            gi = h // 2
            rows = jnp.where(masks[h], s_new[gi * 128:(gi + 1) * 128, :], 0.0)
            st_ref[0, h * 128:(h + 1) * 128, :] = _extract_blocks(rows, 128, HEAD_DIM)


def _lane_rep(p):
    return jnp.repeat(p.astype(F32), HEAD_DIM).reshape(1, GROUP_WIDTH)


def _ssd_prompt(proj3, conv_w, conv_b, dt_bias, a_log, d_skip, norm_w, c):
    b, l, _ = proj3.shape
    n = l // c
    fixed = lambda bi, ci: (0, 0)
    return pl.pallas_call(
        functools.partial(_ssd_prompt_kernel, c=c, n_chunks=n),
        grid=(b, n),
        in_specs=[pl.BlockSpec((1, c, 1024), lambda bi, ci: (bi, ci, COL_SSD // 1024)),
                  pl.BlockSpec((1, 8, 1024), lambda bi, ci: (bi, jnp.maximum(ci * (c // 8) - 1, 0), COL_SSD // 1024)),
                  pl.BlockSpec((1, c, 128), lambda bi, ci: (bi, ci, COL_SMALL // 128)),
                  pl.BlockSpec((CONV_WIDTH, 768), fixed),
                  pl.BlockSpec((1, 768), fixed),
                  pl.BlockSpec((1, GROUP_WIDTH), fixed),
                  pl.BlockSpec((1, GROUP_WIDTH), fixed),
                  pl.BlockSpec((1, GROUP_WIDTH), fixed),
                  pl.BlockSpec((1, GROUP_WIDTH), fixed)],
        out_specs=[pl.BlockSpec((1, c, GROUP_WIDTH), lambda bi, ci: (bi, ci, 0)),
                   pl.BlockSpec((1, N_HEADS * SSD_STATE, HEAD_DIM), lambda bi, ci: (bi, 0, 0))],
        out_shape=[jax.ShapeDtypeStruct((b, l, GROUP_WIDTH), F32),
                   jax.ShapeDtypeStruct((b, N_HEADS * SSD_STATE, HEAD_DIM), F32)],
        scratch_shapes=[pltpu.VMEM((GROUP_WIDTH, GROUP_WIDTH), F32),
                        pltpu.VMEM((c + 8, 768), F32)],
        compiler_params=pltpu.CompilerParams(dimension_semantics=("arbitrary", "arbitrary"),
                                             vmem_limit_bytes=VMEM_LIMIT),
        name="ssd_prompt",
    )(proj3, proj3, proj3, conv_w, conv_b.reshape(1, 768), _lane_rep(dt_bias), _lane_rep(a_log),
      _lane_rep(d_skip), norm_w.reshape(1, GROUP_WIDTH))


def _gdn_prompt_kernel(blk_ref, halo_ref, small_ref, cw_ref, alog_ref, dtb_ref, nw_ref,
                       o_ref, st_ref, s_scr, xe_scr, *, c, n_chunks):
    ci = pl.program_id(1)
    hc = N_HEADS * c

    @pl.when(ci == 0)
    def _():
        s_scr[...] = jnp.zeros_like(s_scr)

    blk = blk_ref[0]
    gz = blk[:, 768:1024]
    qkv = _conv_silu(xe_scr, halo_ref[0][:, 0:768], blk[:, 0:768], cw_ref[...], None, ci == 0, c)
    ones_bd = _block_ones(GROUP_WIDTH, HEAD_DIM)
    gq, gk, v = qkv[:, 0:256], qkv[:, 256:512], qkv[:, 512:768]
    q = gq * lax.rsqrt(_head_sumsq(gq, ones_bd) + EPS) * (HEAD_DIM ** -0.5)
    k = gk * lax.rsqrt(_head_sumsq(gk, ones_bd) + EPS)

    masks = _head_masks()
    small = small_ref[0]
    beta = _sigmoid(_expand_small(small, SMALL_GB))
    la = -jnp.exp(alog_ref[...]) * _softplus(_expand_small(small, SMALL_GA) + dtb_ref[...])
    g = _cumsum_rows(la, c)
    g_last = g[c - 1:c, :]
    eg = jnp.exp(g)

    da, db = _decay_diff_operands(g)
    diff = _dot_nt(_stack_heads(da, masks).astype(BF16),
                   _stack_heads(db, masks).astype(BF16))
    rr = _iota((hc, hc), 0)
    cc = _iota((hc, hc), 1)
    same = (rr // c) == (cc // c)
    incl = same & (rr >= cc)
    strict = same & (rr > cc)
    decay = jnp.exp(jnp.minimum(diff, 0.0))

    k_st = _stack_heads(k, masks).astype(BF16)
    a_mat = jnp.where(strict, _dot_nt(_stack_heads(beta * k, masks).astype(BF16), k_st) * decay, 0.0)
    p_mat = jnp.where(incl, _dot_nt(_stack_heads(q, masks).astype(BF16), k_st) * decay, 0.0)

    n_mat = None
    sz = 1
    while sz < c:
        off = ((rr // (2 * sz)) == (cc // (2 * sz))) & (((rr // sz) % 2) == 1) & (((cc // sz) % 2) == 0)
        a_off = jnp.where(off, a_mat, 0.0)
        if n_mat is None:
            n_mat = -a_off
        else:
            m = a_off + _dot(a_off.astype(BF16), n_mat.astype(BF16))
            n_mat = n_mat - m - _dot(n_mat.astype(BF16), m.astype(BF16))
        sz *= 2
    x = jnp.concatenate([_stack_heads(beta * eg * k, masks), _stack_heads(beta * v, masks)], axis=1)
    x = x + _dot(n_mat.astype(BF16), x.astype(BF16))
    w = x[0:c, 0:256]
    u0 = x[0:c, 256:512]
    for h in range(1, N_HEADS):
        w = w + x[h * c:(h + 1) * c, 0:256]
        u0 = u0 + x[h * c:(h + 1) * c, 256:512]

    s = s_scr[...]
    s_bf = s.astype(BF16)
    u = u0 - _dot(w.astype(BF16), s_bf)
    o = _dot((q * eg).astype(BF16), s_bf)
    pu = _dot(p_mat.astype(BF16), _stack_heads(u, masks).astype(BF16))
    for h in range(N_HEADS):
        o = o + pu[h * c:(h + 1) * c]
    kend = k * jnp.exp(g_last - g)
    ds = _dot_tn(kend.astype(BF16), u.astype(BF16))
    s_new = jnp.exp(g_last) * s + jnp.where(_block_mask(GROUP_WIDTH, HEAD_DIM, HEAD_DIM), ds, 0.0)
    s_scr[...] = s_new

    ss = _head_sumsq(o, ones_bd)
    o_ref[0] = o * lax.rsqrt(ss * (1.0 / HEAD_DIM) + EPS) * nw_ref[...] * _silu(gz)

    @pl.when(ci == n_chunks - 1)
    def _():
        st_ref[0] = _extract_blocks(s_new, GROUP_WIDTH, HEAD_DIM)


def _gdn_prompt(proj3, conv_w, a_log, dt_bias, norm_w, c):
    b, l, _ = proj3.shape
    n = l // c
    fixed = lambda bi, ci: (0, 0)
    return pl.pallas_call(
        functools.partial(_gdn_prompt_kernel, c=c, n_chunks=n),
        grid=(b, n),
        in_specs=[pl.BlockSpec((1, c, 1024), lambda bi, ci: (bi, ci, COL_GDN // 1024)),
                  pl.BlockSpec((1, 8, 1024), lambda bi, ci: (bi, jnp.maximum(ci * (c // 8) - 1, 0), COL_GDN // 1024)),
                  pl.BlockSpec((1, c, 128), lambda bi, ci: (bi, ci, COL_SMALL // 128)),
                  pl.BlockSpec((CONV_WIDTH, 768), fixed),
                  pl.BlockSpec((1, GROUP_WIDTH), fixed),
                  pl.BlockSpec((1, GROUP_WIDTH), fixed),
                  pl.BlockSpec((1, GROUP_WIDTH), fixed)],
        out_specs=[pl.BlockSpec((1, c, GROUP_WIDTH), lambda bi, ci: (bi, ci, 0)),
                   pl.BlockSpec((1, GROUP_WIDTH, HEAD_DIM), lambda bi, ci: (bi, 0, 0))],
        out_shape=[jax.ShapeDtypeStruct((b, l, GROUP_WIDTH), F32),
                   jax.ShapeDtypeStruct((b, GROUP_WIDTH, HEAD_DIM), F32)],
        scratch_shapes=[pltpu.VMEM((GROUP_WIDTH, GROUP_WIDTH), F32),
                        pltpu.VMEM((c + 8, 768), F32)],
        compiler_params=pltpu.CompilerParams(dimension_semantics=("arbitrary", "arbitrary"),
                                             vmem_limit_bytes=VMEM_LIMIT),
        name="gdn_prompt",
    )(proj3, proj3, proj3, conv_w, _lane_rep(a_log), _lane_rep(dt_bias),
      jnp.tile(norm_w.astype(F32), N_HEADS).reshape(1, GROUP_WIDTH))


HGRN_SUB = 16


def _hgrn_lower_bound(logits, layer):
    rows = [logits[d:d + 1, :] for d in range(DEPTH)]
    mx = functools.reduce(jnp.maximum, rows)
    es = [jnp.exp(x - mx) for x in rows]
    tot = functools.reduce(lambda a, b: a + b, es)
    sm = [e / tot for e in es]
    acc = sm[0]
    for d in range(1, layer + 1):
        acc = acc + sm[d]
    return acc - sm[0]


def _hgrn_prompt_kernel(blk_ref, lb_ref, nw_ref, o_ref, st_ref, s_scr, *, r, n_chunks, layer):
    ci = pl.program_id(1)
    sub = HGRN_SUB
    n_sub = r // sub

    @pl.when(ci == 0)
    def _():
        s_scr[...] = jnp.zeros_like(s_scr)

    blk = blk_ref[0]
    hq, hf, hi, hg = (blk[:, i * GROUP_WIDTH:(i + 1) * GROUP_WIDTH] for i in range(4))
    lb = _hgrn_lower_bound(lb_ref[...], layer)
    f = lb + (1.0 - lb) * _sigmoid(hf)
    q = _sigmoid(hq)
    k = 1.0 - f
    v = hi
    logf = jnp.log(f)

    rr = _iota((r, r), 0)
    cc = _iota((r, r), 1)
    same_sub = (rr // sub) == (cc // sub)
    g = _exact_dot_left((same_sub & (rr >= cc)).astype(BF16), logf)
    g_tot = _exact_dot_left(same_sub.astype(BF16), logf)
    qt = q * jnp.exp(g)
    kh = k * jnp.exp(g_tot - g)
    ones_bd = _block_ones(GROUP_WIDTH, HEAD_DIM)
    bd_mask = _block_mask(GROUP_WIDTH, HEAD_DIM, HEAD_DIM)
    ii = _iota((sub, 1), 0)

    s = s_scr[...]
    outs = []
    for j in range(n_sub):
        lo = j * sub
        q_j, k_j, v_j, g_j = q[lo:lo + sub], k[lo:lo + sub], v[lo:lo + sub], g[lo:lo + sub]
        prods = []
        for jj in range(sub):
            e = jnp.exp(jnp.minimum(g_j - g_j[jj:jj + 1, :], 0.0))
            prods.append(jnp.where(ii >= jj, q_j * e * k_j[jj:jj + 1, :], 0.0))
        sc = _dot(jnp.concatenate(prods, axis=0).astype(BF16), ones_bd)
        o_j = sc[0:sub] * v_j[0:1, :]
        for jj in range(1, sub):
            o_j = o_j + sc[jj * sub:(jj + 1) * sub] * v_j[jj:jj + 1, :]
        o_j = o_j + _dot_nt(qt[lo:lo + sub].astype(BF16), s.astype(BF16))
        outs.append(o_j)
        ds = _dot_tn(v_j.astype(BF16), kh[lo:lo + sub].astype(BF16))
        s = jnp.exp(g_tot[lo:lo + 1, :]) * s + jnp.where(bd_mask, ds, 0.0)
    s_scr[...] = s

    o = jnp.concatenate(outs, axis=0)
    ss = _head_sumsq(o, ones_bd)
    o_ref[0] = o * lax.rsqrt(ss * (1.0 / HEAD_DIM) + EPS) * nw_ref[...] * _silu(hg)

    @pl.when(ci == n_chunks - 1)
    def _():
        st_ref[0] = _extract_blocks(s.T, GROUP_WIDTH, HEAD_DIM)


def _hgrn_prompt(proj3, lb_logits, norm_w, layer, r):
    b, l, _ = proj3.shape
    n = l // r
    fixed = lambda bi, ci: (0, 0)
    return pl.pallas_call(
        functools.partial(_hgrn_prompt_kernel, r=r, n_chunks=n, layer=layer),
        grid=(b, n),
        in_specs=[pl.BlockSpec((1, r, 1024), lambda bi, ci: (bi, ci, COL_HGRN // 1024)),
                  pl.BlockSpec((DEPTH, GROUP_WIDTH), fixed),
                  pl.BlockSpec((1, GROUP_WIDTH), fixed)],
        out_specs=[pl.BlockSpec((1, r, GROUP_WIDTH), lambda bi, ci: (bi, ci, 0)),
                   pl.BlockSpec((1, GROUP_WIDTH, HEAD_DIM), lambda bi, ci: (bi, 0, 0))],
        out_shape=[jax.ShapeDtypeStruct((b, l, GROUP_WIDTH), F32),
                   jax.ShapeDtypeStruct((b, GROUP_WIDTH, HEAD_DIM), F32)],
        scratch_shapes=[pltpu.VMEM((GROUP_WIDTH, GROUP_WIDTH), F32)],
        compiler_params=pltpu.CompilerParams(dimension_semantics=("arbitrary", "arbitrary"),
                                             vmem_limit_bytes=VMEM_LIMIT),
        name="hgrn_prompt",
    )(proj3, lb_logits.astype(F32), jnp.tile(norm_w.astype(F32), N_HEADS).reshape(1, GROUP_WIDTH))


DEC_SEQS = 128
DEC_LEN = 4


def _state_in(st_ref, st_scr, n_tiles):
    for j in range(n_tiles):
        st_scr[j * 128:(j + 1) * 128, :] = st_ref[:, j * 128:(j + 1) * 128].T


def _state_out(st_scr, so_ref, n_tiles):
    for j in range(n_tiles):
        so_ref[:, j * 128:(j + 1) * 128] = st_scr[j * 128:(j + 1) * 128, :].T


def _head_rows(h):
    return pl.ds(pl.multiple_of(h * HEAD_DIM, HEAD_DIM), HEAD_DIM)


def _recur_head(st_scr, n_keys, decay_fn, k_fn, q_fn, v_blocks):
    def body(kk, accs):
        rows = pl.ds(pl.multiple_of(kk * HEAD_DIM, HEAD_DIM), HEAD_DIM)
        s = st_scr[rows, :]
        accs = list(accs)
        for t in range(DEC_LEN):
            s = decay_fn(t, kk) * s + k_fn(t, kk) * v_blocks[t]
            accs[t] = accs[t] + q_fn(t, kk) * s
        st_scr[rows, :] = s
        return tuple(accs)

    zero = jnp.zeros((HEAD_DIM, DEC_SEQS), F32)
    return lax.fori_loop(0, n_keys, body, (zero,) * DEC_LEN)


def _dec_ret_kernel(blk_ref, cos_ref, sin_ref, st_ref, o_ref, so_ref, q_scr, k_scr, v_scr, o_scr, st_scr):
    h = pl.program_id(0)

    @pl.when(h == 0)
    def _():
        for t in range(DEC_LEN):
            blk = blk_ref[t]
            rq, rk, rv = blk[:, 0:256], blk[:, 256:512], blk[:, 512:768]
            cosv, sinv = cos_ref[t:t + 1, :], sin_ref[t:t + 1, :]
            q_scr[t] = (rq * cosv + _swap_halves(rq) * sinv).T
            k_scr[t] = ((rk * cosv + _swap_halves(rk) * sinv) * (HEAD_DIM ** -0.5)).T
            v_scr[t] = rv.T

    _state_in(st_ref, st_scr, HEAD_DIM * HEAD_DIM // 128)
    lg = jnp.where(h == 0, LOG_GAMMA[0], jnp.where(h == 1, LOG_GAMMA[1], jnp.where(h == 2, LOG_GAMMA[2], LOG_GAMMA[3])))
    gamma = jnp.exp(jnp.full((1, DEC_SEQS), lg, F32))
    hr = _head_rows(h)
    v_blocks = [v_scr[t, hr, :] for t in range(DEC_LEN)]
    accs = _recur_head(
        st_scr, HEAD_DIM,
        lambda t, kk: gamma,
        lambda t, kk: k_scr[t, pl.ds(h * HEAD_DIM + kk, 1), :],
        lambda t, kk: q_scr[t, pl.ds(h * HEAD_DIM + kk, 1), :],
        v_blocks)
    for t in range(DEC_LEN):
        o_scr[t, hr, :] = accs[t]
    _state_out(st_scr, so_ref, HEAD_DIM * HEAD_DIM // 128)

    @pl.when(h == N_HEADS - 1)
    def _():
        ones_bd = _block_ones(GROUP_WIDTH, HEAD_DIM)
        for t in range(DEC_LEN):
            o = o_scr[t].T
            ss = _head_sumsq(o, ones_bd)
            o_ref[t] = o * lax.rsqrt(ss * (1.0 / HEAD_DIM) + EPS) * _silu(blk_ref[t][:, 768:1024])


def _dec_call(kernel_fn, name, col, n_state_cols, ins, in_specs, n_tok_scr, extra_scratch=()):
    blk_spec = pl.BlockSpec((DEC_LEN, DEC_SEQS, 1024), lambda h: (0, 0, col // 1024))
    st_spec = pl.BlockSpec((DEC_SEQS, n_state_cols), lambda h: (0, h))
    tok_scr = pltpu.VMEM((DEC_LEN, GROUP_WIDTH, DEC_SEQS), F32)
    return pl.pallas_call(
        kernel_fn,
        grid=(N_HEADS,),
        in_specs=[blk_spec] + in_specs + [st_spec],
        out_specs=[pl.BlockSpec((DEC_LEN, DEC_SEQS, GROUP_WIDTH), lambda h: (0, 0, 0)), st_spec],
        out_shape=[jax.ShapeDtypeStruct((DEC_LEN, DEC_SEQS, GROUP_WIDTH), F32),
                   jax.ShapeDtypeStruct((DEC_SEQS, N_HEADS * n_state_cols), F32)],
        scratch_shapes=[tok_scr] * n_tok_scr + list(extra_scratch) + [pltpu.VMEM((n_state_cols, DEC_SEQS), F32)],
        compiler_params=pltpu.CompilerParams(dimension_semantics=("arbitrary",), vmem_limit_bytes=VMEM_LIMIT),
        name=name,
    )(*ins)


def _fixed1(shape):
    return pl.BlockSpec(shape, lambda h: (0,) * len(shape))


def _dec_ret(projd, cos_t, sin_t, state):
    return _dec_call(_dec_ret_kernel, "retention_decode", COL_RET, HEAD_DIM * HEAD_DIM,
                     (projd, cos_t, sin_t, state),
                     [_fixed1((DEC_LEN, GROUP_WIDTH)), _fixed1((DEC_LEN, GROUP_WIDTH))], 4)


def _dec_hgrn_kernel(blk_ref, lb_ref, nw_ref, st_ref, o_ref, so_ref, q_scr, k_scr, v_scr, f_scr, o_scr, st_scr,
                     *, layer):
    h = pl.program_id(0)

    @pl.when(h == 0)
    def _():
        lb = _hgrn_lower_bound(lb_ref[...], layer)
        for t in range(DEC_LEN):
            blk = blk_ref[t]
            f = lb + (1.0 - lb) * _sigmoid(blk[:, 256:512])
            q_scr[t] = _sigmoid(blk[:, 0:256]).T
            k_scr[t] = (1.0 - f).T
            v_scr[t] = blk[:, 512:768].T
            f_scr[t] = f.T

    _state_in(st_ref, st_scr, HEAD_DIM * HEAD_DIM // 128)
    hr = _head_rows(h)
    v_blocks = [v_scr[t, hr, :] for t in range(DEC_LEN)]
    row = lambda scr: (lambda t, kk: scr[t, pl.ds(h * HEAD_DIM + kk, 1), :])
    accs = _recur_head(st_scr, HEAD_DIM, row(f_scr), row(k_scr), row(q_scr), v_blocks)
    for t in range(DEC_LEN):
        o_scr[t, hr, :] = accs[t]
    _state_out(st_scr, so_ref, HEAD_DIM * HEAD_DIM // 128)

    @pl.when(h == N_HEADS - 1)
    def _():
        ones_bd = _block_ones(GROUP_WIDTH, HEAD_DIM)
        for t in range(DEC_LEN):
            o = o_scr[t].T
            ss = _head_sumsq(o, ones_bd)
            o_ref[t] = o * lax.rsqrt(ss * (1.0 / HEAD_DIM) + EPS) * nw_ref[...] * _silu(blk_ref[t][:, 768:1024])


def _dec_hgrn(projd, lb_logits, norm_w, layer, state):
    return _dec_call(functools.partial(_dec_hgrn_kernel, layer=layer), "hgrn_decode", COL_HGRN, HEAD_DIM * HEAD_DIM,
                     (projd, lb_logits.astype(F32), jnp.tile(norm_w.astype(F32), N_HEADS).reshape(1, GROUP_WIDTH), state),
                     [_fixed1((DEPTH, GROUP_WIDTH)), _fixed1((1, GROUP_WIDTH))], 5)


def _dec_conv_silu(hist_ref, xs, w, bias):
    xe = [hist_ref[0], hist_ref[1], hist_ref[2]] + xs
    out = []
    for t in range(DEC_LEN):
        y = xe[t] * w[0:1, :]
        for j in range(1, CONV_WIDTH):
            y = y + xe[t + j] * w[j:j + 1, :]
        if bias is not None:
            y = y + bias
        out.append(_silu(y))
    return out


def _dec_ssd_kernel(blk_ref, small_ref, hist_ref, cw_ref, cb_ref, dtb_ref, alog_ref, dskip_ref, nw_ref, st_ref,
                    o_ref, so_ref, c_scr, b_scr, v_scr, a_scr, o_scr, x_scr, st_scr):
    h = pl.program_id(0)

    @pl.when(h == 0)
    def _():
        xbc = _dec_conv_silu(hist_ref, [blk_ref[t][:, 256:1024] for t in range(DEC_LEN)], cw_ref[...], cb_ref[...])
        for t in range(DEC_LEN):
            xs = xbc[t][:, 0:256]
            dt = _softplus(_expand_small(small_ref[t], SMALL_SDT) + dtb_ref[...])
            x_scr[t] = xs
            v_scr[t] = (xs * dt).T
            b_scr[t] = xbc[t][:, 256:512].T
            c_scr[t] = xbc[t][:, 512:768].T
            a_scr[t] = jnp.exp(-jnp.exp(alog_ref[...]) * dt).T

    _state_in(st_ref, st_scr, SSD_STATE * HEAD_DIM // 128)
    hr = _head_rows(h)
    g0 = (h // 2) * SSD_STATE
    v_blocks = [v_scr[t, hr, :] for t in range(DEC_LEN)]
    accs = _recur_head(
        st_scr, SSD_STATE,
        lambda t, kk: a_scr[t, pl.ds(h * HEAD_DIM, 1), :],
        lambda t, kk: b_scr[t, pl.ds(g0 + kk, 1), :],
        lambda t, kk: c_scr[t, pl.ds(g0 + kk, 1), :],
        v_blocks)
    for t in range(DEC_LEN):
        o_scr[t, hr, :] = accs[t]
    _state_out(st_scr, so_ref, SSD_STATE * HEAD_DIM // 128)

    @pl.when(h == N_HEADS - 1)
    def _():
        for t in range(DEC_LEN):
            y = (o_scr[t].T + dskip_ref[...] * x_scr[t]) * _silu(blk_ref[t][:, 0:256])
            halves = [_rms_rows(y[:, gi * 128:(gi + 1) * 128]) for gi in range(2)]
            o_ref[t] = jnp.concatenate(halves, axis=1) * nw_ref[...]


def _dec_ssd(projd, hist, conv_w, conv_b, dt_bias, a_log, d_skip, norm_w, state):
    small_spec = pl.BlockSpec((DEC_LEN, DEC_SEQS, 128), lambda h: (0, 0, COL_SMALL // 128))
    return _dec_call(_dec_ssd_kernel, "ssd_decode", COL_SSD, SSD_STATE * HEAD_DIM,
                     (projd, projd, hist, conv_w, conv_b.reshape(1, 768), _lane_rep(dt_bias), _lane_rep(a_log),
                      _lane_rep(d_skip), norm_w.reshape(1, GROUP_WIDTH), state),
                     [small_spec, _fixed1((CONV_WIDTH - 1, DEC_SEQS, 768)), _fixed1((CONV_WIDTH, 768)),
                      _fixed1((1, 768))] + [_fixed1((1, GROUP_WIDTH))] * 4, 5,
                     extra_scratch=[pltpu.VMEM((DEC_LEN, DEC_SEQS, GROUP_WIDTH), F32)])


def _dec_gdn_kernel(blk_ref, small_ref, hist_ref, cw_ref, alog_ref, dtb_ref, nw_ref, st_ref,
                    o_ref, so_ref, q_scr, k_scr, v_scr, a_scr, b_scr, o_scr, st_scr):
    h = pl.program_id(0)

    @pl.when(h == 0)
    def _():
        ones_bd = _block_ones(GROUP_WIDTH, HEAD_DIM)
        qkv = _dec_conv_silu(hist_ref, [blk_ref[t][:, 0:768] for t in range(DEC_LEN)], cw_ref[...], None)
        for t in range(DEC_LEN):
            gq, gk, gv = qkv[t][:, 0:256], qkv[t][:, 256:512], qkv[t][:, 512:768]
            q_scr[t] = (gq * lax.rsqrt(_head_sumsq(gq, ones_bd) + EPS) * (HEAD_DIM ** -0.5)).T
            k_scr[t] = (gk * lax.rsqrt(_head_sumsq(gk, ones_bd) + EPS)).T
            v_scr[t] = gv.T
            small = small_ref[t]
            b_scr[t] = _sigmoid(_expand_small(small, SMALL_GB)).T
            la = -jnp.exp(alog_ref[...]) * _softplus(_expand_small(small, SMALL_GA) + dtb_ref[...])
            a_scr[t] = jnp.exp(la).T

    _state_in(st_ref, st_scr, HEAD_DIM * HEAD_DIM // 128)
    hr = _head_rows(h)
    one_row = pl.ds(h * HEAD_DIM, 1)
    zero = jnp.zeros((HEAD_DIM, DEC_SEQS), F32)
    for t in range(DEC_LEN):
        a = a_scr[t, one_row, :]

        def kts(kk, r):
            rows = pl.ds(pl.multiple_of(kk * HEAD_DIM, HEAD_DIM), HEAD_DIM)
            return r + k_scr[t, pl.ds(h * HEAD_DIM + kk, 1), :] * st_scr[rows, :]

        r = lax.fori_loop(0, HEAD_DIM, kts, zero)
        u = b_scr[t, one_row, :] * (v_scr[t, hr, :] - a * r)

        def upd(kk, acc):
            rows = pl.ds(pl.multiple_of(kk * HEAD_DIM, HEAD_DIM), HEAD_DIM)
            s = a * st_scr[rows, :] + k_scr[t, pl.ds(h * HEAD_DIM + kk, 1), :] * u
            st_scr[rows, :] = s
            return acc + q_scr[t, pl.ds(h * HEAD_DIM + kk, 1), :] * s

        o_scr[t, hr, :] = lax.fori_loop(0, HEAD_DIM, upd, zero)
    _state_out(st_scr, so_ref, HEAD_DIM * HEAD_DIM // 128)

    @pl.when(h == N_HEADS - 1)
    def _():
        ones_bd = _block_ones(GROUP_WIDTH, HEAD_DIM)
        for t in range(DEC_LEN):
            o = o_scr[t].T
            ss = _head_sumsq(o, ones_bd)
            o_ref[t] = o * lax.rsqrt(ss * (1.0 / HEAD_DIM) + EPS) * nw_ref[...] * _silu(blk_ref[t][:, 768:1024])


def _dec_gdn(projd, hist, conv_w, a_log, dt_bias, norm_w, state):
    small_spec = pl.BlockSpec((DEC_LEN, DEC_SEQS, 128), lambda h: (0, 0, COL_SMALL // 128))
    return _dec_call(_dec_gdn_kernel, "gdn_decode", COL_GDN, HEAD_DIM * HEAD_DIM,
                     (projd, projd, hist, conv_w, _lane_rep(a_log), _lane_rep(dt_bias),
                      jnp.tile(norm_w.astype(F32), N_HEADS).reshape(1, GROUP_WIDTH), state),
                     [small_spec, _fixed1((CONV_WIDTH - 1, DEC_SEQS, 768)), _fixed1((CONV_WIDTH, 768))]
                     + [_fixed1((1, GROUP_WIDTH))] * 3, 6)


def _reorder_cols(w):
    lead = w.shape[:-1]
    small = jnp.concatenate([w[..., 2048:2056], w[..., 4104:4108],
                             jnp.zeros(lead + (P_PAD - COL_SMALL - 12,), w.dtype)], axis=-1)
    return jnp.concatenate([w[..., 0:2048], w[..., 2056:4104], small], axis=-1)


def _prep_w_in(w):
    return _reorder_cols(w).astype(BF16)


def _rotary_tables(pos):
    half = HEAD_DIM // 2
    inv_freq = RET_THETA ** (-jnp.arange(half, dtype=F32) / half)
    ang = pos.astype(F32)[:, None] * inv_freq[None, :]
    cos, sin = jnp.cos(ang), jnp.sin(ang)
    cos_t = jnp.tile(cos, (1, 2 * N_HEADS))
    sin_t = jnp.tile(jnp.concatenate([-sin, sin], axis=1), (1, N_HEADS))
    return cos_t, sin_t


RET_CHUNK = 128
SSD_CHUNK = 128
GDN_CHUNK = 64
HGRN_ROWS = 128


def _forward(x_prompt, x_sample, states, p, past_len):
    st_hg, st_gd, st_gc, st_rt, st_sd, st_sc = states
    bp, lp, _ = x_prompt.shape
    nd, ld, _ = x_sample.shape
    xp = x_prompt.astype(F32).reshape(bp * lp, D_MODEL)
    xd = jnp.transpose(x_sample.astype(F32), (1, 0, 2)).reshape(ld * nd, D_MODEL)
    cos_p, sin_p = _rotary_tables(jnp.arange(lp))
    cos_d, sin_d = _rotary_tables(past_len + jnp.arange(ld))
    outs = {k: [] for k in ("hp", "hs", "gp", "gs", "gcp", "gcs", "rp", "rs", "sp", "ss", "scp", "scs")}
    for l in range(DEPTH):
        w_in = _prep_w_in(p["w_in"][l])
        wo, wu, wd = (p[k][l].astype(BF16) for k in ("w_out", "w_up", "w_down"))
        pp = _proj(xp, p["norm_mix"][l], w_in).reshape(bp, lp, P_PAD)
        pd = _proj(xd, p["norm_mix"][l], w_in).reshape(ld, nd, P_PAD)

        oa, sa = _hgrn_prompt(pp, p["hgrn_lb_logits"], p["hgrn_norm"][l], l, HGRN_ROWS)
        ob, sb = _gdn_prompt(pp, p["gdn_conv_w"][l], p["gdn_a_log"][l], p["gdn_dt_bias"][l], p["gdn_norm"][l],
                             GDN_CHUNK)
        oc, sc = _ret_prompt(pp, cos_p, sin_p, RET_CHUNK)
        od, sd = _ssd_prompt(pp, p["ssd_conv_w"][l], p["ssd_conv_b"][l], p["ssd_dt_bias"][l], p["ssd_a_log"][l],
                             p["ssd_d"][l], p["ssd_norm"][l], SSD_CHUNK)
        outs["hp"].append(sa.reshape(bp, N_HEADS, HEAD_DIM, HEAD_DIM))
        outs["gp"].append(sb.reshape(bp, N_HEADS, HEAD_DIM, HEAD_DIM))
        outs["rp"].append(sc.reshape(bp, N_HEADS, HEAD_DIM, HEAD_DIM))
        outs["sp"].append(sd.reshape(bp, N_HEADS, SSD_STATE, HEAD_DIM))
        outs["gcp"].append(pp[:, lp - 3:, COL_GDN:COL_GDN + 768])
        outs["scp"].append(pp[:, lp - 3:, COL_SSD + 256:COL_SSD + 1024])
        xp = _out_ffn(xp, [o.reshape(bp * lp, GROUP_WIDTH) for o in (oa, ob, oc, od)], wo, p["norm_ffn"][l], wu, wd,
                      p["norm_final"], final=(l == DEPTH - 1))

        hist_g = jnp.transpose(st_gc[l].astype(F32), (1, 0, 2))
        hist_s = jnp.transpose(st_sc[l].astype(F32), (1, 0, 2))
        da, dsa = _dec_hgrn(pd, p["hgrn_lb_logits"], p["hgrn_norm"][l], l, st_hg[l].astype(F32).reshape(nd, -1))
        db, dsb = _dec_gdn(pd, hist_g, p["gdn_conv_w"][l], p["gdn_a_log"][l], p["gdn_dt_bias"][l], p["gdn_norm"][l],
                           st_gd[l].astype(F32).reshape(nd, -1))
        dc, dsc = _dec_ret(pd, cos_d, sin_d, st_rt[l].astype(F32).reshape(nd, -1))
        dd, dsd = _dec_ssd(pd, hist_s, p["ssd_conv_w"][l], p["ssd_conv_b"][l], p["ssd_dt_bias"][l], p["ssd_a_log"][l],
                           p["ssd_d"][l], p["ssd_norm"][l], st_sd[l].astype(F32).reshape(nd, -1))
        outs["hs"].append(dsa.reshape(nd, N_HEADS, HEAD_DIM, HEAD_DIM))
        outs["gs"].append(dsb.reshape(nd, N_HEADS, HEAD_DIM, HEAD_DIM))
        outs["rs"].append(dsc.reshape(nd, N_HEADS, HEAD_DIM, HEAD_DIM))
        outs["ss"].append(dsd.reshape(nd, N_HEADS, SSD_STATE, HEAD_DIM))
        outs["gcs"].append(jnp.transpose(pd[ld - 3:, :, COL_GDN:COL_GDN + 768], (1, 0, 2)))
        outs["scs"].append(jnp.transpose(pd[ld - 3:, :, COL_SSD + 256:COL_SSD + 1024], (1, 0, 2)))
        xd = _out_ffn(xd, [o.reshape(ld * nd, GROUP_WIDTH) for o in (da, db, dc, dd)], wo, p["norm_ffn"][l], wu, wd,
                      p["norm_final"], final=(l == DEPTH - 1))

    y_prompt = xp.reshape(bp, lp, D_MODEL)
    y_sample = jnp.transpose(xd.reshape(ld, nd, D_MODEL), (1, 0, 2))
    st = {k: jnp.stack(v) for k, v in outs.items()}
    return (y_prompt, y_sample, st["hp"], st["hs"], st["gp"], st["gs"], st["gcp"], st["gcs"],
            st["rp"], st["rs"], st["sp"], st["ss"], st["scp"], st["scs"])


def kernel(x_prompt, x_sample, state_hgrn, state_gdn, state_gdn_conv, state_ret, state_ssd, state_ssd_conv,
           norm_mix, w_in, hgrn_lb_logits, hgrn_norm, gdn_conv_w, gdn_a_log, gdn_dt_bias, gdn_norm,
           ssd_conv_w, ssd_conv_b, ssd_dt_bias, ssd_a_log, ssd_d, ssd_norm,
           w_out, norm_ffn, w_up, w_down, norm_final):
    params = dict(norm_mix=norm_mix, w_in=w_in, hgrn_lb_logits=hgrn_lb_logits, hgrn_norm=hgrn_norm,
                  gdn_conv_w=gdn_conv_w, gdn_a_log=gdn_a_log, gdn_dt_bias=gdn_dt_bias, gdn_norm=gdn_norm,
                  ssd_conv_w=ssd_conv_w, ssd_conv_b=ssd_conv_b, ssd_dt_bias=ssd_dt_bias, ssd_a_log=ssd_a_log,
                  ssd_d=ssd_d, ssd_norm=ssd_norm, w_out=w_out, norm_ffn=norm_ffn, w_up=w_up,
                  w_down=w_down, norm_final=norm_final)
    states = (state_hgrn, state_gdn, state_gdn_conv, state_ret, state_ssd, state_ssd_conv)
    return _forward(x_prompt, x_sample, states, params, 16384)
```

```python
import functools
import math

import numpy as np
import jax
import jax.numpy as jnp
from jax import lax
from jax.experimental import pallas as pl
from jax.experimental.pallas import tpu as pltpu

F32 = jnp.float32
BF16 = jnp.bfloat16

D_MODEL = 1024
GROUP_WIDTH = 256
HEAD_DIM = 64
N_HEADS = 4
CONV_WIDTH = 4
SSD_STATE = 128
D_FF = 4096
RET_THETA = 10000.0
EPS = 1e-6
DEPTH = 2

COL_HGRN = 0
COL_GDN = 1024
COL_RET = 2048
COL_SSD = 3072
COL_SMALL = 4096
P_PAD = 4224
SMALL_GA, SMALL_GB, SMALL_SDT = 0, 4, 8

VMEM_LIMIT = 56 * 1024 * 1024
LOG_GAMMA = [math.log(1.0 - 2.0 ** (-5.0 - h)) for h in range(N_HEADS)]


def _dot(a, b):
    return jnp.dot(a, b, preferred_element_type=F32)


def _dot_nt(a, b):
    return lax.dot_general(a, b, (((1,), (1,)), ((), ())), preferred_element_type=F32)


def _dot_tn(a, b):
    return lax.dot_general(a, b, (((0,), (0,)), ((), ())), preferred_element_type=F32)


def _round_robin(gens):
    live = list(gens)
    while live:
        nxt = []
        for g in live:
            try:
                next(g)
                nxt.append(g)
            except StopIteration:
                pass
        live = nxt


def _split3(x):
    hi = x.astype(BF16)
    r1 = x - hi.astype(F32)
    mid = r1.astype(BF16)
    lo = (r1 - mid.astype(F32)).astype(BF16)
    return hi, mid, lo


def _exact_dot(x, sel):
    hi, mid, lo = _split3(x)
    return _dot(hi, sel) + _dot(mid, sel) + _dot(lo, sel)


def _exact_dot_left(sel, x):
    hi, mid, lo = _split3(x)
    return _dot(sel, hi) + _dot(sel, mid) + _dot(sel, lo)


def _iota(shape, dim):
    return lax.broadcasted_iota(jnp.int32, shape, dim)


def _head_of_lane(n_lanes, width=HEAD_DIM):
    return _iota((1, n_lanes), 1) // width


def _head_masks(n_lanes=GROUP_WIDTH, width=HEAD_DIM):
    hl = _head_of_lane(n_lanes, width)
    return [hl == h for h in range(n_lanes // width)]


def _stack_heads(x, masks):
    return jnp.concatenate([jnp.where(m, x, jnp.zeros_like(x)) for m in masks], axis=0)


def _unstack_heads(y, masks, c):
    out = jnp.where(masks[0], y[0:c], 0.0)
    for h in range(1, len(masks)):
        out = out + jnp.where(masks[h], y[h * c:(h + 1) * c], 0.0)
    return out


def _block_ones(n, width, dtype=BF16):
    r = _iota((n, n), 0) // width
    c = _iota((n, n), 1) // width
    return (r == c).astype(dtype)


def _block_mask(n, rwidth, cwidth):
    return (_iota((n, n), 0) // rwidth) == (_iota((n, n), 1) // cwidth)


def _lower_tri(c, dtype=BF16):
    return (_iota((c, c), 0) >= _iota((c, c), 1)).astype(dtype)


def _cumsum_rows(x, c):
    return _exact_dot_left(_lower_tri(c), x)


def _sigmoid(x):
    return 1.0 / (1.0 + jnp.exp(-x))


def _silu(x):
    return x * _sigmoid(x)


def _softplus(x):
    return jnp.maximum(x, 0.0) + jnp.log(1.0 + jnp.exp(-jnp.abs(x)))


def _rms_rows(x):
    return x * lax.rsqrt(jnp.mean(x * x, axis=-1, keepdims=True) + EPS)


def _head_sumsq(x, ones_bd):
    return _exact_dot(x * x, ones_bd)


def _expand_small(small, first_lane):
    r = _iota((128, GROUP_WIDTH), 0)
    c = _iota((128, GROUP_WIDTH), 1) // HEAD_DIM
    sel = (r == c + first_lane).astype(BF16)
    return _exact_dot(small, sel)


def _decay_diff_operands(g):
    hi, mid, lo = (x.astype(F32) for x in _split3(g))
    pos = _iota(g.shape, 1) % HEAD_DIM
    a = jnp.where(pos == 0, hi, jnp.where(pos == 1, mid, jnp.where(pos == 2, lo,
                  jnp.where(pos < 6, 1.0, 0.0))))
    b = jnp.where(pos < 3, 1.0, jnp.where(pos == 3, -hi, jnp.where(pos == 4, -mid,
                  jnp.where(pos == 5, -lo, 0.0))))
    return a, b


def _extract_blocks(s_wide, rows, width):
    sel = ((_iota((GROUP_WIDTH, width), 0) % width) == _iota((GROUP_WIDTH, width), 1)).astype(BF16)
    return _exact_dot(s_wide, sel)


def _proj_kernel(x_ref, nw_ref, w_ref, o_ref):
    h = _rms_rows(x_ref[...]) * nw_ref[...]
    o_ref[...] = _dot(h.astype(BF16), w_ref[...])


def _proj(x2d, norm_w, w_bf16):
    t = x2d.shape[0]
    tm = min(t, 512)
    return pl.pallas_call(
        _proj_kernel,
        grid=(t // tm,),
        in_specs=[pl.BlockSpec((tm, D_MODEL), lambda i: (i, 0)),
                  pl.BlockSpec((1, D_MODEL), lambda i: (0, 0)),
                  pl.BlockSpec((D_MODEL, P_PAD), lambda i: (0, 0))],
        out_specs=pl.BlockSpec((tm, P_PAD), lambda i: (i, 0)),
        out_shape=jax.ShapeDtypeStruct((t, P_PAD), F32),
        compiler_params=pltpu.CompilerParams(dimension_semantics=("arbitrary",),
                                             vmem_limit_bytes=VMEM_LIMIT),
        name="norm_in_proj",
    )(x2d, norm_w.reshape(1, D_MODEL), w_bf16)


def _ffn_kernel(x_ref, oa_ref, ob_ref, oc_ref, od_ref, wo_ref, nf_ref, wu_ref, wd_ref, nfin_ref,
                o_ref, *, final):
    mix = jnp.concatenate([oa_ref[...], ob_ref[...], oc_ref[...], od_ref[...]], axis=1)
    x = x_ref[...] + _dot(mix.astype(BF16), wo_ref[...])
    h = (_rms_rows(x) * nf_ref[...]).astype(BF16)
    acc = x
    ft = 1024
    for t in range(D_FF // ft):
        up = _dot(h, wu_ref[:, t * ft:(t + 1) * ft])
        up = jnp.square(jnp.maximum(up, 0.0)).astype(BF16)
        acc = acc + _dot(up, wd_ref[t * ft:(t + 1) * ft, :])
    if final:
        acc = _rms_rows(acc) * nfin_ref[...]
    o_ref[...] = acc


def _out_ffn(x2d, mixes, wo, nf, wu, wd, nfin, final):
    t = x2d.shape[0]
    tm = min(t, 512)
    row = lambda i: (i, 0)
    fixed = lambda i: (0, 0)
    return pl.pallas_call(
        functools.partial(_ffn_kernel, final=final),
        grid=(t // tm,),
        in_specs=[pl.BlockSpec((tm, D_MODEL), row)]
                 + [pl.BlockSpec((tm, GROUP_WIDTH), row)] * 4
                 + [pl.BlockSpec((D_MODEL, D_MODEL), fixed),
                    pl.BlockSpec((1, D_MODEL), fixed),
                    pl.BlockSpec((D_MODEL, D_FF), fixed),
                    pl.BlockSpec((D_FF, D_MODEL), fixed),
                    pl.BlockSpec((1, D_MODEL), fixed)],
        out_specs=pl.BlockSpec((tm, D_MODEL), row),
        out_shape=jax.ShapeDtypeStruct((t, D_MODEL), F32),
        compiler_params=pltpu.CompilerParams(dimension_semantics=("arbitrary",),
                                             vmem_limit_bytes=VMEM_LIMIT),
        name="out_proj_ffn",
    )(x2d, *mixes, wo, nf.reshape(1, D_MODEL), wu, wd, nfin.reshape(1, D_MODEL))


def _swap_halves(x):
    first = (_iota((1, 128), 1) % HEAD_DIM) < (HEAD_DIM // 2)
    parts = []
    for p in range(GROUP_WIDTH // 128):
        xp = x[:, p * 128:(p + 1) * 128]
        parts.append(jnp.where(first, pltpu.roll(xp, 96, 1), pltpu.roll(xp, 32, 1)))
    return jnp.concatenate(parts, axis=1)


def _conv_silu(xe_ref, halo, x, w, bias, first_chunk, c):
    xe_ref[0:8, :] = jnp.where(first_chunk, jnp.zeros_like(halo), halo)
    xe_ref[8:, :] = x
    y = w[3:4, :] * x
    for j in range(CONV_WIDTH - 1):
        y = y + w[j:j + 1, :] * xe_ref[5 + j:5 + j + c, :]
    if bias is not None:
        y = y + bias
    return _silu(y)


def _ret_prompt_kernel(blk_ref, cos_ref, sin_ref, o_ref, st_ref, s_scr, *, c, n_chunks, nb):
    ci = pl.program_id(1)

    @pl.when(ci == 0)
    def _():
        s_scr[...] = jnp.zeros_like(s_scr)

    cosv, sinv = cos_ref[...], sin_ref[...]
    masks = _head_masks()
    hl = _head_of_lane(GROUP_WIDTH)
    lg = jnp.full((1, GROUP_WIDTH), LOG_GAMMA[0], F32)
    for h in range(1, N_HEADS):
        lg = jnp.where(hl == h, LOG_GAMMA[h], lg)
    ri = _iota((c, 1), 0).astype(F32)
    dij = (_iota((c, c), 0) - _iota((c, c), 1)).astype(F32)
    causal = dij >= 0.0
    decay = jnp.concatenate(
        [jnp.where(causal, jnp.exp(jnp.maximum(dij, 0.0) * LOG_GAMMA[h]), 0.0) for h in range(N_HEADS)],
        axis=0)
    q_scale = jnp.exp((ri + 1.0) * lg)
    k_scale = jnp.exp((float(c - 1) - ri) * lg) * (HEAD_DIM ** -0.5)
    s_scale = jnp.exp(float(c) * lg)
    bd_mask = _block_mask(GROUP_WIDTH, HEAD_DIM, HEAD_DIM)
    ones_bd = _block_ones(GROUP_WIDTH, HEAD_DIM)

    def one_sequence(sq):
        blk = blk_ref[sq]
        rq, rk, rv, rg = (blk[:, i * GROUP_WIDTH:(i + 1) * GROUP_WIDTH] for i in range(4))
        q = rq * cosv + _swap_halves(rq) * sinv
        k = rk * cosv + _swap_halves(rk) * sinv
        v = rv.astype(BF16)
        s = s_scr[sq]
        qk = _dot_nt(_stack_heads(q, masks).astype(BF16), k.astype(BF16))
        o_inter = _dot((q * q_scale).astype(BF16), s.astype(BF16))
        ds = _dot_tn((k * k_scale).astype(BF16), v)
        yield
        scores = qk * (decay * (HEAD_DIM ** -0.5))
        pv = _dot(scores.astype(BF16), v)
        s_scr[sq] = s_scale * s + jnp.where(bd_mask, ds, 0.0)
        yield
        o = _unstack_heads(pv, masks, c) + o_inter
        ss = _head_sumsq(o, ones_bd)
        yield
        o_ref[sq] = o * lax.rsqrt(ss * (1.0 / HEAD_DIM) + EPS) * _silu(rg)

    _round_robin([one_sequence(sq) for sq in range(nb)])

    @pl.when(ci == n_chunks - 1)
    def _():
        for sq in range(nb):
            st_ref[sq] = _extract_blocks(s_scr[sq], GROUP_WIDTH, HEAD_DIM)


PROMPT_SEQS_PER_STEP = 4


def _ret_prompt(proj3, cos_t, sin_t, c):
    b, l, _ = proj3.shape
    n = l // c
    nb = math.gcd(b, PROMPT_SEQS_PER_STEP)
    return pl.pallas_call(
        functools.partial(_ret_prompt_kernel, c=c, n_chunks=n, nb=nb),
        grid=(b // nb, n),
        in_specs=[pl.BlockSpec((nb, c, 1024), lambda bi, ci: (bi, ci, COL_RET // 1024)),
                  pl.BlockSpec((c, GROUP_WIDTH), lambda bi, ci: (ci, 0)),
                  pl.BlockSpec((c, GROUP_WIDTH), lambda bi, ci: (ci, 0))],
        out_specs=[pl.BlockSpec((nb, c, GROUP_WIDTH), lambda bi, ci: (bi, ci, 0)),
                   pl.BlockSpec((nb, GROUP_WIDTH, HEAD_DIM), lambda bi, ci: (bi, 0, 0))],
        out_shape=[jax.ShapeDtypeStruct((b, l, GROUP_WIDTH), F32),
                   jax.ShapeDtypeStruct((b, GROUP_WIDTH, HEAD_DIM), F32)],
        scratch_shapes=[pltpu.VMEM((nb, GROUP_WIDTH, GROUP_WIDTH), F32)],
        compiler_params=pltpu.CompilerParams(dimension_semantics=("arbitrary", "arbitrary"),
                                             vmem_limit_bytes=VMEM_LIMIT),
        name="retention_prompt",
    )(proj3, cos_t, sin_t)


def _ssd_prompt_kernel(blk_ref, halo_ref, small_ref, cw_ref, cb_ref, dtb_ref, alog_ref, dskip_ref, nw_ref,
                       o_ref, st_ref, s_scr, xe_scr, *, c, n_chunks, nb):
    ci = pl.program_id(1)

    @pl.when(ci == 0)
    def _():
        s_scr[...] = jnp.zeros_like(s_scr)

    masks = _head_masks()
    causal = _iota((c, c), 0) >= _iota((c, c), 1)
    causal4 = jnp.concatenate([causal] * N_HEADS, axis=0)
    group_mask = _block_mask(GROUP_WIDTH, 128, 128)
    tri = _lower_tri(c)
    neg_a = -jnp.exp(alog_ref[...])

    def one_sequence(sq):
        blk = blk_ref[sq]
        sz = blk[:, 0:GROUP_WIDTH]
        xbc = _conv_silu(xe_scr.at[sq], halo_ref[sq][:, GROUP_WIDTH:], blk[:, GROUP_WIDTH:], cw_ref[...],
                         cb_ref[...], ci == 0, c)
        xs = xbc[:, 0:256]
        bmat = xbc[:, 256:512].astype(BF16)
        cmat = xbc[:, 512:768].astype(BF16)
        s = s_scr[sq]
        cb = [_dot_nt(cmat[:, gi * 128:(gi + 1) * 128], bmat[:, gi * 128:(gi + 1) * 128]) for gi in range(2)]
        y_inter = _dot(cmat, s.astype(BF16))
        dt = _softplus(_expand_small(small_ref[sq], SMALL_SDT) + dtb_ref[...])
        yield
        g = _exact_dot_left(tri, neg_a * dt)
        yield
        g_last = g[c - 1:c, :]
        da, db = _decay_diff_operands(g)
        diff = _dot_nt(_stack_heads(da, masks).astype(BF16), db.astype(BF16))
        v = xs * dt
        vend = v * jnp.exp(g_last - g)
        ds = _dot_tn(bmat, vend.astype(BF16))
        yield
        decay = jnp.where(causal4, jnp.exp(jnp.minimum(diff, 0.0)), 0.0)
        scores = jnp.concatenate([cb[0], cb[0], cb[1], cb[1]], axis=0) * decay
        pv = _dot(scores.astype(BF16), v.astype(BF16))
        s_scr[sq] = jnp.exp(g_last) * s + jnp.where(group_mask, ds, 0.0)
        yield
        y = _unstack_heads(pv, masks, c) + y_inter * jnp.exp(g)
        y = (y + dskip_ref[...] * xs) * _silu(sz)
        halves = [_rms_rows(y[:, gi * 128:(gi + 1) * 128]) for gi in range(2)]
        o_ref[sq] = jnp.concatenate(halves, axis=1) * nw_ref[...]

    _round_robin([one_sequence(sq) for sq in range(nb)])

    @pl.when(ci == n_chunks - 1)
    def _():
        for sq in range(nb):
            for h in range(N_HEADS):
                gi = h // 2
                rows = jnp.where(masks[h], s_scr[sq, gi * 128:(gi + 1) * 128, :], 0.0)
                st_ref[sq, h * 128:(h + 1) * 128, :] = _extract_blocks(rows, 128, HEAD_DIM)


def _lane_rep(p):
    return jnp.repeat(p.astype(F32), HEAD_DIM).reshape(1, GROUP_WIDTH)


def _ssd_prompt(proj3, conv_w, conv_b, dt_bias, a_log, d_skip, norm_w, c):
    b, l, _ = proj3.shape
    n = l // c
    fixed = lambda bi, ci: (0, 0)
    nb = math.gcd(b, PROMPT_SEQS_PER_STEP)
    return pl.pallas_call(
        functools.partial(_ssd_prompt_kernel, c=c, n_chunks=n, nb=nb),
        grid=(b // nb, n),
        in_specs=[pl.BlockSpec((nb, c, 1024), lambda bi, ci: (bi, ci, COL_SSD // 1024)),
                  pl.BlockSpec((nb, 8, 1024), lambda bi, ci: (bi, jnp.maximum(ci * (c // 8) - 1, 0), COL_SSD // 1024)),
                  pl.BlockSpec((nb, c, 128), lambda bi, ci: (bi, ci, COL_SMALL // 128)),
                  pl.BlockSpec((CONV_WIDTH, 768), fixed),
                  pl.BlockSpec((1, 768), fixed),
                  pl.BlockSpec((1, GROUP_WIDTH), fixed),
                  pl.BlockSpec((1, GROUP_WIDTH), fixed),
                  pl.BlockSpec((1, GROUP_WIDTH), fixed),
                  pl.BlockSpec((1, GROUP_WIDTH), fixed)],
        out_specs=[pl.BlockSpec((nb, c, GROUP_WIDTH), lambda bi, ci: (bi, ci, 0)),
                   pl.BlockSpec((nb, N_HEADS * SSD_STATE, HEAD_DIM), lambda bi, ci: (bi, 0, 0))],
        out_shape=[jax.ShapeDtypeStruct((b, l, GROUP_WIDTH), F32),
                   jax.ShapeDtypeStruct((b, N_HEADS * SSD_STATE, HEAD_DIM), F32)],
        scratch_shapes=[pltpu.VMEM((nb, GROUP_WIDTH, GROUP_WIDTH), F32),
                        pltpu.VMEM((nb, c + 8, 768), F32)],
        compiler_params=pltpu.CompilerParams(dimension_semantics=("arbitrary", "arbitrary"),
                                             vmem_limit_bytes=VMEM_LIMIT),
        name="ssd_prompt",
    )(proj3, proj3, proj3, conv_w, conv_b.reshape(1, 768), _lane_rep(dt_bias), _lane_rep(a_log),
      _lane_rep(d_skip), norm_w.reshape(1, GROUP_WIDTH))


def _gdn_prompt_kernel(blk_ref, halo_ref, small_ref, cw_ref, alog_ref, dtb_ref, nw_ref,
                       o_ref, st_ref, s_scr, xe_scr, m_scr, *, c, n_chunks, nb):
    ci = pl.program_id(1)
    hc = N_HEADS * c
    n_lvl = int(math.log2(c))

    @pl.when(ci == 0)
    def _():
        s_scr[...] = jnp.zeros_like(s_scr)

    @pl.when((pl.program_id(0) == 0) & (ci == 0))
    def _():
        rr = _iota((hc, hc), 0)
        cc = _iota((hc, hc), 1)
        same = (rr // c) == (cc // c)
        m_scr[0] = (same & (rr >= cc)).astype(F32)
        m_scr[1] = (same & (rr > cc)).astype(F32)
        for lv in range(n_lvl):
            sz = 1 << lv
            off = ((rr // (2 * sz)) == (cc // (2 * sz))) & (((rr // sz) % 2) == 1) & (((cc // sz) % 2) == 0)
            m_scr[2 + lv] = off.astype(F32)

    ones_bd = _block_ones(GROUP_WIDTH, HEAD_DIM)
    bd_mask = _block_mask(GROUP_WIDTH, HEAD_DIM, HEAD_DIM)
    masks = _head_masks()
    tri = _lower_tri(c)
    neg_a = -jnp.exp(alog_ref[...])

    def one_sequence(sq):
        blk = blk_ref[sq]
        gz = blk[:, 768:1024]
        qkv = _conv_silu(xe_scr.at[sq], halo_ref[sq][:, 0:768], blk[:, 0:768], cw_ref[...], None, ci == 0, c)
        gq, gk, v = qkv[:, 0:256], qkv[:, 256:512], qkv[:, 512:768]
        q = gq * lax.rsqrt(_head_sumsq(gq, ones_bd) + EPS) * (HEAD_DIM ** -0.5)
        k = gk * lax.rsqrt(_head_sumsq(gk, ones_bd) + EPS)
        yield
        small = small_ref[sq]
        beta = _sigmoid(_expand_small(small, SMALL_GB))
        g = _exact_dot_left(tri, neg_a * _softplus(_expand_small(small, SMALL_GA) + dtb_ref[...]))
        g_last = g[c - 1:c, :]
        eg = jnp.exp(g)
        yield
        da, db = _decay_diff_operands(g)
        diff = _dot_nt(_stack_heads(da, masks).astype(BF16),
                       _stack_heads(db, masks).astype(BF16))
        k_st = _stack_heads(k, masks).astype(BF16)
        kk = _dot_nt(_stack_heads(beta * k, masks).astype(BF16), k_st)
        qk = _dot_nt(_stack_heads(q, masks).astype(BF16), k_st)
        yield
        decay = jnp.exp(jnp.minimum(diff, 0.0))
        a_mat = kk * (decay * m_scr[1])
        p_mat = (qk * (decay * m_scr[0])).astype(BF16)
        x = jnp.concatenate([_stack_heads(beta * eg * k, masks), _stack_heads(beta * v, masks)], axis=1)

        n_mat = -(a_mat * m_scr[2])
        for lv in range(1, n_lvl):
            a_off = a_mat * m_scr[2 + lv]
            m = a_off + _dot(a_off.astype(BF16), n_mat.astype(BF16))
            yield
            n_mat = n_mat - m - _dot(n_mat.astype(BF16), m.astype(BF16))
            yield
        x = x + _dot(n_mat.astype(BF16), x.astype(BF16))
        yield
        w = x[0:c, 0:256]
        u0 = x[0:c, 256:512]
        for h in range(1, N_HEADS):
            w = w + x[h * c:(h + 1) * c, 0:256]
            u0 = u0 + x[h * c:(h + 1) * c, 256:512]

        s = s_scr[sq]
        s_bf = s.astype(BF16)
        u = u0 - _dot(w.astype(BF16), s_bf)
        o = _dot((q * eg).astype(BF16), s_bf)
        yield
        pu = _dot(p_mat, _stack_heads(u, masks).astype(BF16))
        kend = k * jnp.exp(g_last - g)
        ds = _dot_tn(kend.astype(BF16), u.astype(BF16))
        yield
        for h in range(N_HEADS):
            o = o + pu[h * c:(h + 1) * c]
        s_scr[sq] = jnp.exp(g_last) * s + jnp.where(bd_mask, ds, 0.0)
        ss = _head_sumsq(o, ones_bd)
        o_ref[sq] = o * lax.rsqrt(ss * (1.0 / HEAD_DIM) + EPS) * nw_ref[...] * _silu(gz)

    _round_robin([one_sequence(sq) for sq in range(nb)])

    @pl.when(ci == n_chunks - 1)
    def _():
        for sq in range(nb):
            st_ref[sq] = _extract_blocks(s_scr[sq], GROUP_WIDTH, HEAD_DIM)


def _gdn_prompt(proj3, conv_w, a_log, dt_bias, norm_w, c):
    b, l, _ = proj3.shape
    n = l // c
    fixed = lambda bi, ci: (0, 0)
    nb = math.gcd(b, PROMPT_SEQS_PER_STEP)
    hc = N_HEADS * c
    return pl.pallas_call(
        functools.partial(_gdn_prompt_kernel, c=c, n_chunks=n, nb=nb),
        grid=(b // nb, n),
        in_specs=[pl.BlockSpec((nb, c, 1024), lambda bi, ci: (bi, ci, COL_GDN // 1024)),
                  pl.BlockSpec((nb, 8, 1024), lambda bi, ci: (bi, jnp.maximum(ci * (c // 8) - 1, 0), COL_GDN // 1024)),
                  pl.BlockSpec((nb, c, 128), lambda bi, ci: (bi, ci, COL_SMALL // 128)),
                  pl.BlockSpec((CONV_WIDTH, 768), fixed),
                  pl.BlockSpec((1, GROUP_WIDTH), fixed),
                  pl.BlockSpec((1, GROUP_WIDTH), fixed),
                  pl.BlockSpec((1, GROUP_WIDTH), fixed)],
        out_specs=[pl.BlockSpec((nb, c, GROUP_WIDTH), lambda bi, ci: (bi, ci, 0)),
                   pl.BlockSpec((nb, GROUP_WIDTH, HEAD_DIM), lambda bi, ci: (bi, 0, 0))],
        out_shape=[jax.ShapeDtypeStruct((b, l, GROUP_WIDTH), F32),
                   jax.ShapeDtypeStruct((b, GROUP_WIDTH, HEAD_DIM), F32)],
        scratch_shapes=[pltpu.VMEM((nb, GROUP_WIDTH, GROUP_WIDTH), F32),
                        pltpu.VMEM((nb, c + 8, 768), F32),
                        pltpu.VMEM((2 + int(math.log2(c)), hc, hc), F32)],
        compiler_params=pltpu.CompilerParams(dimension_semantics=("arbitrary", "arbitrary"),
                                             vmem_limit_bytes=VMEM_LIMIT),
        name="gdn_prompt",
    )(proj3, proj3, proj3, conv_w, _lane_rep(a_log), _lane_rep(dt_bias),
      jnp.tile(norm_w.astype(F32), N_HEADS).reshape(1, GROUP_WIDTH))


HGRN_SUB = 16


def _hgrn_lower_bound(logits, layer):
    rows = [logits[d:d + 1, :] for d in range(DEPTH)]
    mx = functools.reduce(jnp.maximum, rows)
    es = [jnp.exp(x - mx) for x in rows]
    tot = functools.reduce(lambda a, b: a + b, es)
    sm = [e / tot for e in es]
    acc = sm[0]
    for d in range(1, layer + 1):
        acc = acc + sm[d]
    return acc - sm[0]


def _hgrn_prompt_kernel(blk_ref, lb_ref, nw_ref, o_ref, st_ref, s_scr, *, r, n_chunks, layer, nb):
    ci = pl.program_id(1)
    sub = HGRN_SUB
    n_sub = r // sub

    @pl.when(ci == 0)
    def _():
        s_scr[...] = jnp.zeros_like(s_scr)

    lb = _hgrn_lower_bound(lb_ref[...], layer)
    rr = _iota((r, r), 0)
    cc = _iota((r, r), 1)
    same_sub = (rr // sub) == (cc // sub)
    cum_sel = (same_sub & (rr >= cc)).astype(BF16)
    tot_sel = same_sub.astype(BF16)
    ones_bd = _block_ones(GROUP_WIDTH, HEAD_DIM)
    bd_mask = _block_mask(GROUP_WIDTH, HEAD_DIM, HEAD_DIM)
    ii = _iota((sub, 1), 0)

    def one_sequence(sq):
        blk = blk_ref[sq]
        hq, hf, hi, hg = (blk[:, i * GROUP_WIDTH:(i + 1) * GROUP_WIDTH] for i in range(4))
        f = lb + (1.0 - lb) * _sigmoid(hf)
        q = _sigmoid(hq)
        k = 1.0 - f
        v = hi
        logf = jnp.log(f)
        g = _exact_dot_left(cum_sel, logf)
        g_tot = _exact_dot_left(tot_sel, logf)
        yield
        qt = (q * jnp.exp(g)).astype(BF16)
        kh = (k * jnp.exp(g_tot - g)).astype(BF16)
        v_bf = v.astype(BF16)

        s = s_scr[sq]
        outs = []
        for j in range(n_sub):
            lo = j * sub
            q_j, k_j, v_j, g_j = q[lo:lo + sub], k[lo:lo + sub], v[lo:lo + sub], g[lo:lo + sub]
            prods = []
            for jj in range(sub):
                e = jnp.exp(jnp.minimum(g_j - g_j[jj:jj + 1, :], 0.0))
                prods.append(jnp.where(ii >= jj, q_j * e * k_j[jj:jj + 1, :], 0.0))
            sc = _dot(jnp.concatenate(prods, axis=0).astype(BF16), ones_bd)
            o_inter = _dot_nt(qt[lo:lo + sub], s.astype(BF16))
            ds = _dot_tn(v_bf[lo:lo + sub], kh[lo:lo + sub])
            yield
            o_j = sc[0:sub] * v_j[0:1, :]
            for jj in range(1, sub):
                o_j = o_j + sc[jj * sub:(jj + 1) * sub] * v_j[jj:jj + 1, :]
            outs.append(o_j + o_inter)
            s = jnp.exp(g_tot[lo:lo + 1, :]) * s + jnp.where(bd_mask, ds, 0.0)
        s_scr[sq] = s

        o = jnp.concatenate(outs, axis=0)
        ss = _head_sumsq(o, ones_bd)
        yield
        o_ref[sq] = o * lax.rsqrt(ss * (1.0 / HEAD_DIM) + EPS) * nw_ref[...] * _silu(hg)

    _round_robin([one_sequence(sq) for sq in range(nb)])

    @pl.when(ci == n_chunks - 1)
    def _():
        for sq in range(nb):
            st_ref[sq] = _extract_blocks(s_scr[sq].T, GROUP_WIDTH, HEAD_DIM)


def _hgrn_prompt(proj3, lb_logits, norm_w, layer, r):
    b, l, _ = proj3.shape
    n = l // r
    fixed = lambda bi, ci: (0, 0)
    nb = math.gcd(b, PROMPT_SEQS_PER_STEP)
    return pl.pallas_call(
        functools.partial(_hgrn_prompt_kernel, r=r, n_chunks=n, layer=layer, nb=nb),
        grid=(b // nb, n),
        in_specs=[pl.BlockSpec((nb, r, 1024), lambda bi, ci: (bi, ci, COL_HGRN // 1024)),
                  pl.BlockSpec((DEPTH, GROUP_WIDTH), fixed),
                  pl.BlockSpec((1, GROUP_WIDTH), fixed)],
        out_specs=[pl.BlockSpec((nb, r, GROUP_WIDTH), lambda bi, ci: (bi, ci, 0)),
                   pl.BlockSpec((nb, GROUP_WIDTH, HEAD_DIM), lambda bi, ci: (bi, 0, 0))],
        out_shape=[jax.ShapeDtypeStruct((b, l, GROUP_WIDTH), F32),
                   jax.ShapeDtypeStruct((b, GROUP_WIDTH, HEAD_DIM), F32)],
        scratch_shapes=[pltpu.VMEM((nb, GROUP_WIDTH, GROUP_WIDTH), F32)],
        compiler_params=pltpu.CompilerParams(dimension_semantics=("arbitrary", "arbitrary"),
                                             vmem_limit_bytes=VMEM_LIMIT),
        name="hgrn_prompt",
    )(proj3, lb_logits.astype(F32), jnp.tile(norm_w.astype(F32), N_HEADS).reshape(1, GROUP_WIDTH))


DEC_SEQS = 128
DEC_LEN = 4


def _state_in(st_ref, st_scr, n_tiles):
    for j in range(n_tiles):
        st_scr[j * 128:(j + 1) * 128, :] = st_ref[:, j * 128:(j + 1) * 128].T


def _state_out(st_scr, so_ref, n_tiles):
    for j in range(n_tiles):
        so_ref[:, j * 128:(j + 1) * 128] = st_scr[j * 128:(j + 1) * 128, :].T


def _head_rows(h):
    return pl.ds(pl.multiple_of(h * HEAD_DIM, HEAD_DIM), HEAD_DIM)


def _recur_head(st_scr, n_keys, decay_fn, k_fn, q_fn, v_blocks):
    def body(kk, accs):
        rows = pl.ds(pl.multiple_of(kk * HEAD_DIM, HEAD_DIM), HEAD_DIM)
        s = st_scr[rows, :]
        accs = list(accs)
        for t in range(DEC_LEN):
            s = decay_fn(t, kk) * s + k_fn(t, kk) * v_blocks[t]
            accs[t] = accs[t] + q_fn(t, kk) * s
        st_scr[rows, :] = s
        return tuple(accs)

    zero = jnp.zeros((HEAD_DIM, DEC_SEQS), F32)
    return lax.fori_loop(0, n_keys, body, (zero,) * DEC_LEN)


def _dec_ret_kernel(blk_ref, cos_ref, sin_ref, st_ref, o_ref, so_ref, q_scr, k_scr, v_scr, o_scr, st_scr):
    h = pl.program_id(0)

    @pl.when(h == 0)
    def _():
        for t in range(DEC_LEN):
            blk = blk_ref[t]
            rq, rk, rv = blk[:, 0:256], blk[:, 256:512], blk[:, 512:768]
            cosv, sinv = cos_ref[t:t + 1, :], sin_ref[t:t + 1, :]
            q_scr[t] = (rq * cosv + _swap_halves(rq) * sinv).T
            k_scr[t] = ((rk * cosv + _swap_halves(rk) * sinv) * (HEAD_DIM ** -0.5)).T
            v_scr[t] = rv.T

    _state_in(st_ref, st_scr, HEAD_DIM * HEAD_DIM // 128)
    lg = jnp.where(h == 0, LOG_GAMMA[0], jnp.where(h == 1, LOG_GAMMA[1], jnp.where(h == 2, LOG_GAMMA[2], LOG_GAMMA[3])))
    gamma = jnp.exp(jnp.full((1, DEC_SEQS), lg, F32))
    hr = _head_rows(h)
    v_blocks = [v_scr[t, hr, :] for t in range(DEC_LEN)]
    accs = _recur_head(
        st_scr, HEAD_DIM,
        lambda t, kk: gamma,
        lambda t, kk: k_scr[t, pl.ds(h * HEAD_DIM + kk, 1), :],
        lambda t, kk: q_scr[t, pl.ds(h * HEAD_DIM + kk, 1), :],
        v_blocks)
    for t in range(DEC_LEN):
        o_scr[t, hr, :] = accs[t]
    _state_out(st_scr, so_ref, HEAD_DIM * HEAD_DIM // 128)

    @pl.when(h == N_HEADS - 1)
    def _():
        ones_bd = _block_ones(GROUP_WIDTH, HEAD_DIM)
        for t in range(DEC_LEN):
            o = o_scr[t].T
            ss = _head_sumsq(o, ones_bd)
            o_ref[t] = o * lax.rsqrt(ss * (1.0 / HEAD_DIM) + EPS) * _silu(blk_ref[t][:, 768:1024])


def _dec_call(kernel_fn, name, col, n_state_cols, ins, in_specs, n_tok_scr, extra_scratch=()):
    blk_spec = pl.BlockSpec((DEC_LEN, DEC_SEQS, 1024), lambda h: (0, 0, col // 1024))
    st_spec = pl.BlockSpec((DEC_SEQS, n_state_cols), lambda h: (0, h))
    tok_scr = pltpu.VMEM((DEC_LEN, GROUP_WIDTH, DEC_SEQS), F32)
    return pl.pallas_call(
        kernel_fn,
        grid=(N_HEADS,),
        in_specs=[blk_spec] + in_specs + [st_spec],
        out_specs=[pl.BlockSpec((DEC_LEN, DEC_SEQS, GROUP_WIDTH), lambda h: (0, 0, 0)), st_spec],
        out_shape=[jax.ShapeDtypeStruct((DEC_LEN, DEC_SEQS, GROUP_WIDTH), F32),
                   jax.ShapeDtypeStruct((DEC_SEQS, N_HEADS * n_state_cols), F32)],
        scratch_shapes=[tok_scr] * n_tok_scr + list(extra_scratch) + [pltpu.VMEM((n_state_cols, DEC_SEQS), F32)],
        compiler_params=pltpu.CompilerParams(dimension_semantics=("arbitrary",), vmem_limit_bytes=VMEM_LIMIT),
        name=name,
    )(*ins)


def _fixed1(shape):
    return pl.BlockSpec(shape, lambda h: (0,) * len(shape))


def _dec_ret(projd, cos_t, sin_t, state):
    return _dec_call(_dec_ret_kernel, "retention_decode", COL_RET, HEAD_DIM * HEAD_DIM,
                     (projd, cos_t, sin_t, state),
                     [_fixed1((DEC_LEN, GROUP_WIDTH)), _fixed1((DEC_LEN, GROUP_WIDTH))], 4)


def _dec_hgrn_kernel(blk_ref, lb_ref, nw_ref, st_ref, o_ref, so_ref, q_scr, k_scr, v_scr, f_scr, o_scr, st_scr,
                     *, layer):
    h = pl.program_id(0)

    @pl.when(h == 0)
    def _():
        lb = _hgrn_lower_bound(lb_ref[...], layer)
        for t in range(DEC_LEN):
            blk = blk_ref[t]
            f = lb + (1.0 - lb) * _sigmoid(blk[:, 256:512])
            q_scr[t] = _sigmoid(blk[:, 0:256]).T
            k_scr[t] = (1.0 - f).T
            v_scr[t] = blk[:, 512:768].T
            f_scr[t] = f.T

    _state_in(st_ref, st_scr, HEAD_DIM * HEAD_DIM // 128)
    hr = _head_rows(h)
    v_blocks = [v_scr[t, hr, :] for t in range(DEC_LEN)]
    row = lambda scr: (lambda t, kk: scr[t, pl.ds(h * HEAD_DIM + kk, 1), :])
    accs = _recur_head(st_scr, HEAD_DIM, row(f_scr), row(k_scr), row(q_scr), v_blocks)
    for t in range(DEC_LEN):
        o_scr[t, hr, :] = accs[t]
    _state_out(st_scr, so_ref, HEAD_DIM * HEAD_DIM // 128)

    @pl.when(h == N_HEADS - 1)
    def _():
        ones_bd = _block_ones(GROUP_WIDTH, HEAD_DIM)
        for t in range(DEC_LEN):
            o = o_scr[t].T
            ss = _head_sumsq(o, ones_bd)
            o_ref[t] = o * lax.rsqrt(ss * (1.0 / HEAD_DIM) + EPS) * nw_ref[...] * _silu(blk_ref[t][:, 768:1024])


def _dec_hgrn(projd, lb_logits, norm_w, layer, state):
    return _dec_call(functools.partial(_dec_hgrn_kernel, layer=layer), "hgrn_decode", COL_HGRN, HEAD_DIM * HEAD_DIM,
                     (projd, lb_logits.astype(F32), jnp.tile(norm_w.astype(F32), N_HEADS).reshape(1, GROUP_WIDTH), state),
                     [_fixed1((DEPTH, GROUP_WIDTH)), _fixed1((1, GROUP_WIDTH))], 5)


def _dec_conv_silu(hist_ref, xs, w, bias):
    xe = [hist_ref[0], hist_ref[1], hist_ref[2]] + xs
    out = []
    for t in range(DEC_LEN):
        y = xe[t] * w[0:1, :]
        for j in range(1, CONV_WIDTH):
            y = y + xe[t + j] * w[j:j + 1, :]
        if bias is not None:
            y = y + bias
        out.append(_silu(y))
    return out


def _dec_ssd_kernel(blk_ref, small_ref, hist_ref, cw_ref, cb_ref, dtb_ref, alog_ref, dskip_ref, nw_ref, st_ref,
                    o_ref, so_ref, c_scr, b_scr, v_scr, a_scr, o_scr, x_scr, st_scr):
    h = pl.program_id(0)

    @pl.when(h == 0)
    def _():
        xbc = _dec_conv_silu(hist_ref, [blk_ref[t][:, 256:1024] for t in range(DEC_LEN)], cw_ref[...], cb_ref[...])
        for t in range(DEC_LEN):
            xs = xbc[t][:, 0:256]
            dt = _softplus(_expand_small(small_ref[t], SMALL_SDT) + dtb_ref[...])
            x_scr[t] = xs
            v_scr[t] = (xs * dt).T
            b_scr[t] = xbc[t][:, 256:512].T
            c_scr[t] = xbc[t][:, 512:768].T
            a_scr[t] = jnp.exp(-jnp.exp(alog_ref[...]) * dt).T

    _state_in(st_ref, st_scr, SSD_STATE * HEAD_DIM // 128)
    hr = _head_rows(h)
    g0 = (h // 2) * SSD_STATE
    v_blocks = [v_scr[t, hr, :] for t in range(DEC_LEN)]
    accs = _recur_head(
        st_scr, SSD_STATE,
        lambda t, kk: a_scr[t, pl.ds(h * HEAD_DIM, 1), :],
        lambda t, kk: b_scr[t, pl.ds(g0 + kk, 1), :],
        lambda t, kk: c_scr[t, pl.ds(g0 + kk, 1), :],
        v_blocks)
    for t in range(DEC_LEN):
        o_scr[t, hr, :] = accs[t]
    _state_out(st_scr, so_ref, SSD_STATE * HEAD_DIM // 128)

    @pl.when(h == N_HEADS - 1)
    def _():
        for t in range(DEC_LEN):
            y = (o_scr[t].T + dskip_ref[...] * x_scr[t]) * _silu(blk_ref[t][:, 0:256])
            halves = [_rms_rows(y[:, gi * 128:(gi + 1) * 128]) for gi in range(2)]
            o_ref[t] = jnp.concatenate(halves, axis=1) * nw_ref[...]


def _dec_ssd(projd, hist, conv_w, conv_b, dt_bias, a_log, d_skip, norm_w, state):
    small_spec = pl.BlockSpec((DEC_LEN, DEC_SEQS, 128), lambda h: (0, 0, COL_SMALL // 128))
    return _dec_call(_dec_ssd_kernel, "ssd_decode", COL_SSD, SSD_STATE * HEAD_DIM,
                     (projd, projd, hist, conv_w, conv_b.reshape(1, 768), _lane_rep(dt_bias), _lane_rep(a_log),
                      _lane_rep(d_skip), norm_w.reshape(1, GROUP_WIDTH), state),
                     [small_spec, _fixed1((CONV_WIDTH - 1, DEC_SEQS, 768)), _fixed1((CONV_WIDTH, 768)),
                      _fixed1((1, 768))] + [_fixed1((1, GROUP_WIDTH))] * 4, 5,
                     extra_scratch=[pltpu.VMEM((DEC_LEN, DEC_SEQS, GROUP_WIDTH), F32)])


def _dec_gdn_kernel(blk_ref, small_ref, hist_ref, cw_ref, alog_ref, dtb_ref, nw_ref, st_ref,
                    o_ref, so_ref, q_scr, k_scr, v_scr, a_scr, b_scr, o_scr, st_scr):
    h = pl.program_id(0)

    @pl.when(h == 0)
    def _():
        ones_bd = _block_ones(GROUP_WIDTH, HEAD_DIM)
        qkv = _dec_conv_silu(hist_ref, [blk_ref[t][:, 0:768] for t in range(DEC_LEN)], cw_ref[...], None)
        for t in range(DEC_LEN):
            gq, gk, gv = qkv[t][:, 0:256], qkv[t][:, 256:512], qkv[t][:, 512:768]
            q_scr[t] = (gq * lax.rsqrt(_head_sumsq(gq, ones_bd) + EPS) * (HEAD_DIM ** -0.5)).T
            k_scr[t] = (gk * lax.rsqrt(_head_sumsq(gk, ones_bd) + EPS)).T
            v_scr[t] = gv.T
            small = small_ref[t]
            b_scr[t] = _sigmoid(_expand_small(small, SMALL_GB)).T
            la = -jnp.exp(alog_ref[...]) * _softplus(_expand_small(small, SMALL_GA) + dtb_ref[...])
            a_scr[t] = jnp.exp(la).T

    _state_in(st_ref, st_scr, HEAD_DIM * HEAD_DIM // 128)
    hr = _head_rows(h)
    one_row = pl.ds(h * HEAD_DIM, 1)
    zero = jnp.zeros((HEAD_DIM, DEC_SEQS), F32)
    for t in range(DEC_LEN):
        a = a_scr[t, one_row, :]

        def kts(kk, r):
            rows = pl.ds(pl.multiple_of(kk * HEAD_DIM, HEAD_DIM), HEAD_DIM)
            return r + k_scr[t, pl.ds(h * HEAD_DIM + kk, 1), :] * st_scr[rows, :]

        r = lax.fori_loop(0, HEAD_DIM, kts, zero)
        u = b_scr[t, one_row, :] * (v_scr[t, hr, :] - a * r)

        def upd(kk, acc):
            rows = pl.ds(pl.multiple_of(kk * HEAD_DIM, HEAD_DIM), HEAD_DIM)
            s = a * st_scr[rows, :] + k_scr[t, pl.ds(h * HEAD_DIM + kk, 1), :] * u
            st_scr[rows, :] = s
            return acc + q_scr[t, pl.ds(h * HEAD_DIM + kk, 1), :] * s

        o_scr[t, hr, :] = lax.fori_loop(0, HEAD_DIM, upd, zero)
    _state_out(st_scr, so_ref, HEAD_DIM * HEAD_DIM // 128)

    @pl.when(h == N_HEADS - 1)
    def _():
        ones_bd = _block_ones(GROUP_WIDTH, HEAD_DIM)
        for t in range(DEC_LEN):
            o = o_scr[t].T
            ss = _head_sumsq(o, ones_bd)
            o_ref[t] = o * lax.rsqrt(ss * (1.0 / HEAD_DIM) + EPS) * nw_ref[...] * _silu(blk_ref[t][:, 768:1024])


def _dec_gdn(projd, hist, conv_w, a_log, dt_bias, norm_w, state):
    small_spec = pl.BlockSpec((DEC_LEN, DEC_SEQS, 128), lambda h: (0, 0, COL_SMALL // 128))
    return _dec_call(_dec_gdn_kernel, "gdn_decode", COL_GDN, HEAD_DIM * HEAD_DIM,
                     (projd, projd, hist, conv_w, _lane_rep(a_log), _lane_rep(dt_bias),
                      jnp.tile(norm_w.astype(F32), N_HEADS).reshape(1, GROUP_WIDTH), state),
                     [small_spec, _fixed1((CONV_WIDTH - 1, DEC_SEQS, 768)), _fixed1((CONV_WIDTH, 768))]
                     + [_fixed1((1, GROUP_WIDTH))] * 3, 6)


def _reorder_cols(w):
    lead = w.shape[:-1]
    small = jnp.concatenate([w[..., 2048:2056], w[..., 4104:4108],
                             jnp.zeros(lead + (P_PAD - COL_SMALL - 12,), w.dtype)], axis=-1)
    return jnp.concatenate([w[..., 0:2048], w[..., 2056:4104], small], axis=-1)


def _prep_w_in(w):
    return _reorder_cols(w).astype(BF16)


def _w_in_prep_kernel(w_ref, tail_ref, o_ref):
    x = w_ref[0]
    o_ref[0, :, 0:COL_RET] = x[:, 0:COL_RET].astype(BF16)
    o_ref[0, :, COL_RET:COL_SMALL] = x[:, COL_RET + 8:COL_SMALL + 8].astype(BF16)
    lane = _iota((x.shape[0], 128), 1)
    small = jnp.where(lane < 8, x[:, COL_RET:COL_RET + 128], jnp.where(lane < 12, tail_ref[0], 0.0))
    o_ref[0, :, COL_SMALL:P_PAD] = small.astype(BF16)


def _prep_w_in_all(w_in):
    d, r, p = w_in.shape
    rows = 256
    return pl.pallas_call(
        _w_in_prep_kernel,
        grid=(d, r // rows),
        in_specs=[pl.BlockSpec((1, rows, p), lambda l, i: (l, i, 0)),
                  pl.BlockSpec((1, rows, 128), lambda l, i: (l, i, COL_SMALL // 128))],
        out_specs=pl.BlockSpec((1, rows, P_PAD), lambda l, i: (l, i, 0)),
        out_shape=jax.ShapeDtypeStruct((d, r, P_PAD), BF16),
        compiler_params=pltpu.CompilerParams(dimension_semantics=("arbitrary", "arbitrary"),
                                             vmem_limit_bytes=VMEM_LIMIT),
        name="w_in_prep",
    )(w_in, w_in)


def _rotary_tables(pos):
    half = HEAD_DIM // 2
    inv_freq = RET_THETA ** (-jnp.arange(half, dtype=F32) / half)
    ang = pos.astype(F32)[:, None] * inv_freq[None, :]
    cos, sin = jnp.cos(ang), jnp.sin(ang)
    cos_t = jnp.tile(cos, (1, 2 * N_HEADS))
    sin_t = jnp.tile(jnp.concatenate([-sin, sin], axis=1), (1, N_HEADS))
    return cos_t, sin_t


RET_CHUNK = 128
SSD_CHUNK = 128
GDN_CHUNK = 64
HGRN_ROWS = 128


def _forward(x_prompt, x_sample, states, p, past_len):
    st_hg, st_gd, st_gc, st_rt, st_sd, st_sc = states
    bp, lp, _ = x_prompt.shape
    nd, ld, _ = x_sample.shape
    xp = x_prompt.astype(F32).reshape(bp * lp, D_MODEL)
    xd = jnp.transpose(x_sample.astype(F32), (1, 0, 2)).reshape(ld * nd, D_MODEL)
    cos_p, sin_p = _rotary_tables(jnp.arange(lp))
    cos_d, sin_d = _rotary_tables(past_len + jnp.arange(ld))
    outs = {k: [] for k in ("hp", "hs", "gp", "gs", "gcp", "gcs", "rp", "rs", "sp", "ss", "scp", "scs")}
    w_in_all = _prep_w_in_all(p["w_in"].astype(F32))
    for l in range(DEPTH):
        w_in = w_in_all[l]
        wo, wu, wd = (p[k][l].astype(BF16) for k in ("w_out", "w_up", "w_down"))
        pp = _proj(xp, p["norm_mix"][l], w_in).reshape(bp, lp, P_PAD)
        pd = _proj(xd, p["norm_mix"][l], w_in).reshape(ld, nd, P_PAD)

        oa, sa = _hgrn_prompt(pp, p["hgrn_lb_logits"], p["hgrn_norm"][l], l, HGRN_ROWS)
        ob, sb = _gdn_prompt(pp, p["gdn_conv_w"][l], p["gdn_a_log"][l], p["gdn_dt_bias"][l], p["gdn_norm"][l],
                             GDN_CHUNK)
        oc, sc = _ret_prompt(pp, cos_p, sin_p, RET_CHUNK)
        od, sd = _ssd_prompt(pp, p["ssd_conv_w"][l], p["ssd_conv_b"][l], p["ssd_dt_bias"][l], p["ssd_a_log"][l],
                             p["ssd_d"][l], p["ssd_norm"][l], SSD_CHUNK)
        outs["hp"].append(sa.reshape(bp, N_HEADS, HEAD_DIM, HEAD_DIM))
        outs["gp"].append(sb.reshape(bp, N_HEADS, HEAD_DIM, HEAD_DIM))
        outs["rp"].append(sc.reshape(bp, N_HEADS, HEAD_DIM, HEAD_DIM))
        outs["sp"].append(sd.reshape(bp, N_HEADS, SSD_STATE, HEAD_DIM))
        outs["gcp"].append(pp[:, lp - 3:, COL_GDN:COL_GDN + 768])
        outs["scp"].append(pp[:, lp - 3:, COL_SSD + 256:COL_SSD + 1024])
        xp = _out_ffn(xp, [o.reshape(bp * lp, GROUP_WIDTH) for o in (oa, ob, oc, od)], wo, p["norm_ffn"][l], wu, wd,
                      p["norm_final"], final=(l == DEPTH - 1))

        hist_g = jnp.transpose(st_gc[l].astype(F32), (1, 0, 2))
        hist_s = jnp.transpose(st_sc[l].astype(F32), (1, 0, 2))
        da, dsa = _dec_hgrn(pd, p["hgrn_lb_logits"], p["hgrn_norm"][l], l, st_hg[l].astype(F32).reshape(nd, -1))
        db, dsb = _dec_gdn(pd, hist_g, p["gdn_conv_w"][l], p["gdn_a_log"][l], p["gdn_dt_bias"][l], p["gdn_norm"][l],
                           st_gd[l].astype(F32).reshape(nd, -1))
        dc, dsc = _dec_ret(pd, cos_d, sin_d, st_rt[l].astype(F32).reshape(nd, -1))
        dd, dsd = _dec_ssd(pd, hist_s, p["ssd_conv_w"][l], p["ssd_conv_b"][l], p["ssd_dt_bias"][l], p["ssd_a_log"][l],
                           p["ssd_d"][l], p["ssd_norm"][l], st_sd[l].astype(F32).reshape(nd, -1))
        outs["hs"].append(dsa.reshape(nd, N_HEADS, HEAD_DIM, HEAD_DIM))
        outs["gs"].append(dsb.reshape(nd, N_HEADS, HEAD_DIM, HEAD_DIM))
        outs["rs"].append(dsc.reshape(nd, N_HEADS, HEAD_DIM, HEAD_DIM))
        outs["ss"].append(dsd.reshape(nd, N_HEADS, SSD_STATE, HEAD_DIM))
        outs["gcs"].append(jnp.transpose(pd[ld - 3:, :, COL_GDN:COL_GDN + 768], (1, 0, 2)))
        outs["scs"].append(jnp.transpose(pd[ld - 3:, :, COL_SSD + 256:COL_SSD + 1024], (1, 0, 2)))
        xd = _out_ffn(xd, [o.reshape(ld * nd, GROUP_WIDTH) for o in (da, db, dc, dd)], wo, p["norm_ffn"][l], wu, wd,
                      p["norm_final"], final=(l == DEPTH - 1))

    y_prompt = xp.reshape(bp, lp, D_MODEL)
    y_sample = jnp.transpose(xd.reshape(ld, nd, D_MODEL), (1, 0, 2))
    st = {k: jnp.stack(v) for k, v in outs.items()}
    return (y_prompt, y_sample, st["hp"], st["hs"], st["gp"], st["gs"], st["gcp"], st["gcs"],
            st["rp"], st["rs"], st["sp"], st["ss"], st["scp"], st["scs"])


def kernel(x_prompt, x_sample, state_hgrn, state_gdn, state_gdn_conv, state_ret, state_ssd, state_ssd_conv,
           norm_mix, w_in, hgrn_lb_logits, hgrn_norm, gdn_conv_w, gdn_a_log, gdn_dt_bias, gdn_norm,
           ssd_conv_w, ssd_conv_b, ssd_dt_bias, ssd_a_log, ssd_d, ssd_norm,
           w_out, norm_ffn, w_up, w_down, norm_final):
    params = dict(norm_mix=norm_mix, w_in=w_in, hgrn_lb_logits=hgrn_lb_logits, hgrn_norm=hgrn_norm,
                  gdn_conv_w=gdn_conv_w, gdn_a_log=gdn_a_log, gdn_dt_bias=gdn_dt_bias, gdn_norm=gdn_norm,
                  ssd_conv_w=ssd_conv_w, ssd_conv_b=ssd_conv_b, ssd_dt_bias=ssd_dt_bias, ssd_a_log=ssd_a_log,
                  ssd_d=ssd_d, ssd_norm=ssd_norm, w_out=w_out, norm_ffn=norm_ffn, w_up=w_up,
                  w_down=w_down, norm_final=norm_final)
    states = (state_hgrn, state_gdn, state_gdn_conv, state_ret, state_ssd, state_ssd_conv)
    return _forward(x_prompt, x_sample, states, params, 16384)
```

```python
import functools
import math

import numpy as np
import jax
import jax.numpy as jnp
from jax import lax
from jax.experimental import pallas as pl
from jax.experimental.pallas import tpu as pltpu

F32 = jnp.float32
BF16 = jnp.bfloat16

D_MODEL = 1024
GROUP_WIDTH = 256
HEAD_DIM = 64
N_HEADS = 4
CONV_WIDTH = 4
SSD_STATE = 128
D_FF = 4096
RET_THETA = 10000.0
EPS = 1e-6
DEPTH = 2

COL_HGRN = 0
COL_GDN = 1024
COL_RET = 2048
COL_SSD = 3072
COL_SMALL = 4096
P_PAD = 4224
SMALL_GA, SMALL_GB, SMALL_SDT = 0, 4, 8

VMEM_LIMIT = 56 * 1024 * 1024
LOG_GAMMA = [math.log(1.0 - 2.0 ** (-5.0 - h)) for h in range(N_HEADS)]


def _dot(a, b):
    return jnp.dot(a, b, preferred_element_type=F32)


def _dot_nt(a, b):
    return lax.dot_general(a, b, (((1,), (1,)), ((), ())), preferred_element_type=F32)


def _dot_tn(a, b):
    return lax.dot_general(a, b, (((0,), (0,)), ((), ())), preferred_element_type=F32)


def _round_robin(gens):
    live = list(gens)
    while live:
        nxt = []
        for g in live:
            try:
                next(g)
                nxt.append(g)
            except StopIteration:
                pass
        live = nxt


def _split3(x):
    hi = x.astype(BF16)
    r1 = x - hi.astype(F32)
    mid = r1.astype(BF16)
    lo = (r1 - mid.astype(F32)).astype(BF16)
    return hi, mid, lo


def _exact_dot(x, sel):
    hi, mid, lo = _split3(x)
    return _dot(hi, sel) + _dot(mid, sel) + _dot(lo, sel)


def _exact_dot_left(sel, x):
    hi, mid, lo = _split3(x)
    return _dot(sel, hi) + _dot(sel, mid) + _dot(sel, lo)


def _iota(shape, dim):
    return lax.broadcasted_iota(jnp.int32, shape, dim)


def _head_of_lane(n_lanes, width=HEAD_DIM):
    return _iota((1, n_lanes), 1) // width


def _head_masks(n_lanes=GROUP_WIDTH, width=HEAD_DIM):
    hl = _head_of_lane(n_lanes, width)
    return [hl == h for h in range(n_lanes // width)]


def _stack_heads(x, masks):
    return jnp.concatenate([jnp.where(m, x, jnp.zeros_like(x)) for m in masks], axis=0)


def _unstack_heads(y, masks, c):
    out = jnp.where(masks[0], y[0:c], 0.0)
    for h in range(1, len(masks)):
        out = out + jnp.where(masks[h], y[h * c:(h + 1) * c], 0.0)
    return out


def _block_ones(n, width, dtype=BF16):
    r = _iota((n, n), 0) // width
    c = _iota((n, n), 1) // width
    return (r == c).astype(dtype)


def _block_mask(n, rwidth, cwidth):
    return (_iota((n, n), 0) // rwidth) == (_iota((n, n), 1) // cwidth)


def _lower_tri(c, dtype=BF16):
    return (_iota((c, c), 0) >= _iota((c, c), 1)).astype(dtype)


def _cumsum_rows(x, c):
    return _exact_dot_left(_lower_tri(c), x)


def _sigmoid(x):
    return 1.0 / (1.0 + jnp.exp(-x))


def _silu(x):
    return x * _sigmoid(x)


def _softplus(x):
    return jnp.maximum(x, 0.0) + jnp.log(1.0 + jnp.exp(-jnp.abs(x)))


def _rms_rows(x):
    return x * lax.rsqrt(jnp.mean(x * x, axis=-1, keepdims=True) + EPS)


def _head_sumsq(x, ones_bd):
    sq = x * x
    hi = sq.astype(BF16)
    lo = (sq - hi.astype(F32)).astype(BF16)
    return _dot(hi, ones_bd) + _dot(lo, ones_bd)


def _expand_small(small, first_lane):
    r = _iota((128, GROUP_WIDTH), 0)
    c = _iota((128, GROUP_WIDTH), 1) // HEAD_DIM
    sel = (r == c + first_lane).astype(BF16)
    return _exact_dot(small, sel)


def _decay_diff_operands(g):
    hi, mid, lo = (x.astype(F32) for x in _split3(g))
    pos = _iota(g.shape, 1) % HEAD_DIM
    a = jnp.where(pos == 0, hi, jnp.where(pos == 1, mid, jnp.where(pos == 2, lo,
                  jnp.where(pos < 6, 1.0, 0.0))))
    b = jnp.where(pos < 3, 1.0, jnp.where(pos == 3, -hi, jnp.where(pos == 4, -mid,
                  jnp.where(pos == 5, -lo, 0.0))))
    return a, b


def _extract_blocks(s_wide, rows, width):
    sel = ((_iota((GROUP_WIDTH, width), 0) % width) == _iota((GROUP_WIDTH, width), 1)).astype(BF16)
    return _exact_dot(s_wide, sel)


def _proj_kernel(x_ref, nw_ref, w_ref, o_ref):
    h = _rms_rows(x_ref[...]) * nw_ref[0]
    o_ref[...] = _dot(h.astype(BF16), w_ref[0])


def _proj(x2d, norm_w, w_bf16, layer):
    t = x2d.shape[0]
    tm = min(t, 512)
    return pl.pallas_call(
        _proj_kernel,
        grid=(t // tm,),
        in_specs=[pl.BlockSpec((tm, D_MODEL), lambda i: (i, 0)),
                  pl.BlockSpec((1, 1, D_MODEL), lambda i: (layer, 0, 0)),
                  pl.BlockSpec((1, D_MODEL, P_PAD), lambda i: (layer, 0, 0))],
        out_specs=pl.BlockSpec((tm, P_PAD), lambda i: (i, 0)),
        out_shape=jax.ShapeDtypeStruct((t, P_PAD), F32),
        compiler_params=pltpu.CompilerParams(dimension_semantics=("arbitrary",),
                                             vmem_limit_bytes=VMEM_LIMIT),
        name="norm_in_proj",
    )(x2d, norm_w, w_bf16)


def _ffn_kernel(x_ref, oa_ref, ob_ref, oc_ref, od_ref, wo_ref, nf_ref, wu_ref, wd_ref, nfin_ref,
                o_ref, *, final):
    mix = jnp.concatenate([oa_ref[...], ob_ref[...], oc_ref[...], od_ref[...]], axis=1)
    x = x_ref[...] + _dot(mix.astype(BF16), wo_ref[0])
    h = (_rms_rows(x) * nf_ref[0]).astype(BF16)
    acc = x
    ft = 1024
    for t in range(D_FF // ft):
        up = _dot(h, wu_ref[0, :, t * ft:(t + 1) * ft])
        up = jnp.square(jnp.maximum(up, 0.0)).astype(BF16)
        acc = acc + _dot(up, wd_ref[0, t * ft:(t + 1) * ft, :])
    if final:
        acc = _rms_rows(acc) * nfin_ref[...]
    o_ref[...] = acc


def _out_ffn(x2d, mixes, wo, nf, wu, wd, nfin, layer):
    t = x2d.shape[0]
    tm = min(t, 512)
    row = lambda i: (i, 0)
    lay = lambda i: (layer, 0, 0)
    return pl.pallas_call(
        functools.partial(_ffn_kernel, final=(layer == DEPTH - 1)),
        grid=(t // tm,),
        in_specs=[pl.BlockSpec((tm, D_MODEL), row)]
                 + [pl.BlockSpec((tm, GROUP_WIDTH), row)] * 4
                 + [pl.BlockSpec((1, D_MODEL, D_MODEL), lay),
                    pl.BlockSpec((1, 1, D_MODEL), lay),
                    pl.BlockSpec((1, D_MODEL, D_FF), lay),
                    pl.BlockSpec((1, D_FF, D_MODEL), lay),
                    pl.BlockSpec((1, D_MODEL), lambda i: (0, 0))],
        out_specs=pl.BlockSpec((tm, D_MODEL), row),
        out_shape=jax.ShapeDtypeStruct((t, D_MODEL), F32),
        compiler_params=pltpu.CompilerParams(dimension_semantics=("arbitrary",),
                                             vmem_limit_bytes=VMEM_LIMIT),
        name="out_proj_ffn",
    )(x2d, *mixes, wo, nf, wu, wd, nfin.reshape(1, D_MODEL))


def _swap_halves(x):
    first = (_iota((1, 128), 1) % HEAD_DIM) < (HEAD_DIM // 2)
    parts = []
    for p in range(GROUP_WIDTH // 128):
        xp = x[:, p * 128:(p + 1) * 128]
        parts.append(jnp.where(first, pltpu.roll(xp, 96, 1), pltpu.roll(xp, 32, 1)))
    return jnp.concatenate(parts, axis=1)


def _conv_silu(xe_ref, halo, x, w, bias, first_chunk, c):
    xe_ref[0:8, :] = jnp.where(first_chunk, jnp.zeros_like(halo), halo)
    xe_ref[8:, :] = x
    y = w[3:4, :] * x
    for j in range(CONV_WIDTH - 1):
        y = y + w[j:j + 1, :] * xe_ref[5 + j:5 + j + c, :]
    if bias is not None:
        y = y + bias
    return _silu(y)


def _ret_prompt_kernel(blk_ref, cos_ref, sin_ref, o_ref, st_ref, s_scr, *, c, n_chunks, nb):
    ci = pl.program_id(1)

    @pl.when(ci == 0)
    def _():
        s_scr[...] = jnp.zeros_like(s_scr)

    cosv, sinv = cos_ref[...], sin_ref[...]
    masks = _head_masks()
    hl = _head_of_lane(GROUP_WIDTH)
    lg = jnp.full((1, GROUP_WIDTH), LOG_GAMMA[0], F32)
    for h in range(1, N_HEADS):
        lg = jnp.where(hl == h, LOG_GAMMA[h], lg)
    ri = _iota((c, 1), 0).astype(F32)
    dij = (_iota((c, c), 0) - _iota((c, c), 1)).astype(F32)
    causal = dij >= 0.0
    decay = jnp.concatenate(
        [jnp.where(causal, jnp.exp(jnp.maximum(dij, 0.0) * LOG_GAMMA[h]), 0.0) for h in range(N_HEADS)],
        axis=0)
    q_scale = jnp.exp((ri + 1.0) * lg)
    k_scale = jnp.exp((float(c - 1) - ri) * lg) * (HEAD_DIM ** -0.5)
    s_scale = jnp.exp(float(c) * lg)
    bd_mask = _block_mask(GROUP_WIDTH, HEAD_DIM, HEAD_DIM)
    ones_bd = _block_ones(GROUP_WIDTH, HEAD_DIM)

    def one_sequence(sq):
        blk = blk_ref[sq]
        rq, rk, rv, rg = (blk[:, i * GROUP_WIDTH:(i + 1) * GROUP_WIDTH] for i in range(4))
        q = rq * cosv + _swap_halves(rq) * sinv
        k = rk * cosv + _swap_halves(rk) * sinv
        v = rv.astype(BF16)
        s = s_scr[sq]
        qk = _dot_nt(_stack_heads(q, masks).astype(BF16), k.astype(BF16))
        o_inter = _dot((q * q_scale).astype(BF16), s.astype(BF16))
        ds = _dot_tn((k * k_scale).astype(BF16), v)
        yield
        scores = qk * (decay * (HEAD_DIM ** -0.5))
        pv = _dot(scores.astype(BF16), v)
        s_scr[sq] = s_scale * s + jnp.where(bd_mask, ds, 0.0)
        yield
        o = _unstack_heads(pv, masks, c) + o_inter
        ss = _head_sumsq(o, ones_bd)
        yield
        o_ref[sq] = o * lax.rsqrt(ss * (1.0 / HEAD_DIM) + EPS) * _silu(rg)

    _round_robin([one_sequence(sq) for sq in range(nb)])

    @pl.when(ci == n_chunks - 1)
    def _():
        for sq in range(nb):
            st_ref[sq] = _extract_blocks(s_scr[sq], GROUP_WIDTH, HEAD_DIM)


PROMPT_SEQS_PER_STEP = 4


def _ret_prompt(proj3, cos_t, sin_t, c):
    b, l, _ = proj3.shape
    n = l // c
    nb = math.gcd(b, PROMPT_SEQS_PER_STEP)
    return pl.pallas_call(
        functools.partial(_ret_prompt_kernel, c=c, n_chunks=n, nb=nb),
        grid=(b // nb, n),
        in_specs=[pl.BlockSpec((nb, c, 1024), lambda bi, ci: (bi, ci, COL_RET // 1024)),
                  pl.BlockSpec((c, GROUP_WIDTH), lambda bi, ci: (ci, 0)),
                  pl.BlockSpec((c, GROUP_WIDTH), lambda bi, ci: (ci, 0))],
        out_specs=[pl.BlockSpec((nb, c, GROUP_WIDTH), lambda bi, ci: (bi, ci, 0)),
                   pl.BlockSpec((nb, GROUP_WIDTH, HEAD_DIM), lambda bi, ci: (bi, 0, 0))],
        out_shape=[jax.ShapeDtypeStruct((b, l, GROUP_WIDTH), F32),
                   jax.ShapeDtypeStruct((b, GROUP_WIDTH, HEAD_DIM), F32)],
        scratch_shapes=[pltpu.VMEM((nb, GROUP_WIDTH, GROUP_WIDTH), F32)],
        compiler_params=pltpu.CompilerParams(dimension_semantics=("arbitrary", "arbitrary"),
                                             vmem_limit_bytes=VMEM_LIMIT),
        name="retention_prompt",
    )(proj3, cos_t, sin_t)


def _ssd_prompt_kernel(blk_ref, halo_ref, small_ref, cw_ref, cb_ref, dtb_ref, alog_ref, dskip_ref, nw_ref,
                       o_ref, st_ref, s_scr, xe_scr, *, c, n_chunks, nb):
    ci = pl.program_id(1)

    @pl.when(ci == 0)
    def _():
        s_scr[...] = jnp.zeros_like(s_scr)

    masks = _head_masks()
    causal = _iota((c, c), 0) >= _iota((c, c), 1)
    causal4 = jnp.concatenate([causal] * N_HEADS, axis=0)
    group_mask = _block_mask(GROUP_WIDTH, 128, 128)
    tri = _lower_tri(c)
    neg_a = -jnp.exp(alog_ref[...])

    def one_sequence(sq):
        blk = blk_ref[sq]
        sz = blk[:, 0:GROUP_WIDTH]
        xbc = _conv_silu(xe_scr.at[sq], halo_ref[sq][:, GROUP_WIDTH:], blk[:, GROUP_WIDTH:], cw_ref[...],
                         cb_ref[...], ci == 0, c)
        xs = xbc[:, 0:256]
        bmat = xbc[:, 256:512].astype(BF16)
        cmat = xbc[:, 512:768].astype(BF16)
        s = s_scr[sq]
        cb = [_dot_nt(cmat[:, gi * 128:(gi + 1) * 128], bmat[:, gi * 128:(gi + 1) * 128]) for gi in range(2)]
        y_inter = _dot(cmat, s.astype(BF16))
        dt = _softplus(_expand_small(small_ref[sq], SMALL_SDT) + dtb_ref[...])
        yield
        g = _exact_dot_left(tri, neg_a * dt)
        yield
        g_last = g[c - 1:c, :]
        da, db = _decay_diff_operands(g)
        diff = _dot_nt(_stack_heads(da, masks).astype(BF16), db.astype(BF16))
        v = xs * dt
        vend = v * jnp.exp(g_last - g)
        ds = _dot_tn(bmat, vend.astype(BF16))
        yield
        decay = jnp.where(causal4, jnp.exp(jnp.minimum(diff, 0.0)), 0.0)
        scores = jnp.concatenate([cb[0], cb[0], cb[1], cb[1]], axis=0) * decay
        pv = _dot(scores.astype(BF16), v.astype(BF16))
        s_scr[sq] = jnp.exp(g_last) * s + jnp.where(group_mask, ds, 0.0)
        yield
        y = _unstack_heads(pv, masks, c) + y_inter * jnp.exp(g)
        y = (y + dskip_ref[...] * xs) * _silu(sz)
        halves = [_rms_rows(y[:, gi * 128:(gi + 1) * 128]) for gi in range(2)]
        o_ref[sq] = jnp.concatenate(halves, axis=1) * nw_ref[...]

    _round_robin([one_sequence(sq) for sq in range(nb)])

    @pl.when(ci == n_chunks - 1)
    def _():
        for sq in range(nb):
            for h in range(N_HEADS):
                gi = h // 2
                rows = jnp.where(masks[h], s_scr[sq, gi * 128:(gi + 1) * 128, :], 0.0)
                st_ref[sq, h * 128:(h + 1) * 128, :] = _extract_blocks(rows, 128, HEAD_DIM)


def _lane_rep(p):
    return jnp.repeat(p.astype(F32), HEAD_DIM).reshape(1, GROUP_WIDTH)


def _ssd_prompt(proj3, conv_w, conv_b, dt_bias, a_log, d_skip, norm_w, c):
    b, l, _ = proj3.shape
    n = l // c
    fixed = lambda bi, ci: (0, 0)
    nb = math.gcd(b, PROMPT_SEQS_PER_STEP)
    return pl.pallas_call(
        functools.partial(_ssd_prompt_kernel, c=c, n_chunks=n, nb=nb),
        grid=(b // nb, n),
        in_specs=[pl.BlockSpec((nb, c, 1024), lambda bi, ci: (bi, ci, COL_SSD // 1024)),
                  pl.BlockSpec((nb, 8, 1024), lambda bi, ci: (bi, jnp.maximum(ci * (c // 8) - 1, 0), COL_SSD // 1024)),
                  pl.BlockSpec((nb, c, 128), lambda bi, ci: (bi, ci, COL_SMALL // 128)),
                  pl.BlockSpec((CONV_WIDTH, 768), fixed),
                  pl.BlockSpec((1, 768), fixed),
                  pl.BlockSpec((1, GROUP_WIDTH), fixed),
                  pl.BlockSpec((1, GROUP_WIDTH), fixed),
                  pl.BlockSpec((1, GROUP_WIDTH), fixed),
                  pl.BlockSpec((1, GROUP_WIDTH), fixed)],
        out_specs=[pl.BlockSpec((nb, c, GROUP_WIDTH), lambda bi, ci: (bi, ci, 0)),
                   pl.BlockSpec((nb, N_HEADS * SSD_STATE, HEAD_DIM), lambda bi, ci: (bi, 0, 0))],
        out_shape=[jax.ShapeDtypeStruct((b, l, GROUP_WIDTH), F32),
                   jax.ShapeDtypeStruct((b, N_HEADS * SSD_STATE, HEAD_DIM), F32)],
        scratch_shapes=[pltpu.VMEM((nb, GROUP_WIDTH, GROUP_WIDTH), F32),
                        pltpu.VMEM((nb, c + 8, 768), F32)],
        compiler_params=pltpu.CompilerParams(dimension_semantics=("arbitrary", "arbitrary"),
                                             vmem_limit_bytes=VMEM_LIMIT),
        name="ssd_prompt",
    )(proj3, proj3, proj3, conv_w, conv_b.reshape(1, 768), _lane_rep(dt_bias), _lane_rep(a_log),
      _lane_rep(d_skip), norm_w.reshape(1, GROUP_WIDTH))


def _gdn_prompt_kernel(blk_ref, halo_ref, small_ref, cw_ref, alog_ref, dtb_ref, nw_ref,
                       o_ref, st_ref, s_scr, xe_scr, m_scr, *, c, n_chunks, nb):
    ci = pl.program_id(1)
    hc = 2 * c
    n_lvl = int(math.log2(c))

    @pl.when(ci == 0)
    def _():
        s_scr[...] = jnp.zeros_like(s_scr)

    @pl.when((pl.program_id(0) == 0) & (ci == 0))
    def _():
        rr = _iota((hc, hc), 0)
        cc = _iota((hc, hc), 1)
        same = (rr // c) == (cc // c)
        m_scr[0] = (same & (rr >= cc)).astype(F32)
        m_scr[1] = (same & (rr > cc)).astype(F32)
        for lv in range(n_lvl):
            sz = 1 << lv
            off = ((rr // (2 * sz)) == (cc // (2 * sz))) & (((rr // sz) % 2) == 1) & (((cc // sz) % 2) == 0)
            m_scr[2 + lv] = off.astype(F32)

    ones_bd = _block_ones(GROUP_WIDTH, HEAD_DIM)
    bd_mask = _block_mask(GROUP_WIDTH, HEAD_DIM, HEAD_DIM)
    masks = _head_masks()
    pair_masks = [masks[0:2], masks[2:4]]
    tri = _lower_tri(c)
    neg_a = -jnp.exp(alog_ref[...])

    def one_sequence(sq):
        blk = blk_ref[sq]
        gz = blk[:, 768:1024]
        qkv = _conv_silu(xe_scr.at[sq], halo_ref[sq][:, 0:768], blk[:, 0:768], cw_ref[...], None, ci == 0, c)
        gq, gk, v = qkv[:, 0:256], qkv[:, 256:512], qkv[:, 512:768]
        q = gq * lax.rsqrt(_head_sumsq(gq, ones_bd) + EPS) * (HEAD_DIM ** -0.5)
        k = gk * lax.rsqrt(_head_sumsq(gk, ones_bd) + EPS)
        yield
        small = small_ref[sq]
        beta = _sigmoid(_expand_small(small, SMALL_GB))
        g = _exact_dot_left(tri, neg_a * _softplus(_expand_small(small, SMALL_GA) + dtb_ref[...]))
        g_last = g[c - 1:c, :]
        eg = jnp.exp(g)
        yield
        da, db = _decay_diff_operands(g)
        bk = beta * k
        bkg = bk * eg
        bv = beta * v
        a_mat, p_mat, x = [], [], []
        for pm in pair_masks:
            diff = _dot_nt(_stack_heads(da, pm).astype(BF16), _stack_heads(db, pm).astype(BF16))
            k_st = _stack_heads(k, pm).astype(BF16)
            kk = _dot_nt(_stack_heads(bk, pm).astype(BF16), k_st)
            qk = _dot_nt(_stack_heads(q, pm).astype(BF16), k_st)
            decay = jnp.exp(jnp.minimum(diff, 0.0))
            a_mat.append(kk * (decay * m_scr[1]))
            p_mat.append((qk * (decay * m_scr[0])).astype(BF16))
            x.append(jnp.concatenate([_stack_heads(bkg, pm), _stack_heads(bv, pm)], axis=1))
        yield

        n_mat = [-(a * m_scr[2]) for a in a_mat]
        for lv in range(1, n_lvl):
            a_off = [a * m_scr[2 + lv] for a in a_mat]
            m = [ao + _dot(ao.astype(BF16), n.astype(BF16)) for ao, n in zip(a_off, n_mat)]
            yield
            n_mat = [n - mm - _dot(n.astype(BF16), mm.astype(BF16)) for n, mm in zip(n_mat, m)]
            yield
        x = [xx + _dot(n.astype(BF16), xx.astype(BF16)) for xx, n in zip(x, n_mat)]
        yield
        w = x[0][0:c, 0:256] + x[0][c:2 * c, 0:256] + x[1][0:c, 0:256] + x[1][c:2 * c, 0:256]
        u0 = x[0][0:c, 256:512] + x[0][c:2 * c, 256:512] + x[1][0:c, 256:512] + x[1][c:2 * c, 256:512]

        s = s_scr[sq]
        s_bf = s.astype(BF16)
        u = u0 - _dot(w.astype(BF16), s_bf)
        o = _dot((q * eg).astype(BF16), s_bf)
        yield
        pu = [_dot(pmat, _stack_heads(u, pm).astype(BF16)) for pmat, pm in zip(p_mat, pair_masks)]
        kend = k * jnp.exp(g_last - g)
        ds = _dot_tn(kend.astype(BF16), u.astype(BF16))
        yield
        for pu_p in pu:
            o = o + pu_p[0:c] + pu_p[c:2 * c]
        s_scr[sq] = jnp.exp(g_last) * s + jnp.where(bd_mask, ds, 0.0)
        ss = _head_sumsq(o, ones_bd)
        o_ref[sq] = o * lax.rsqrt(ss * (1.0 / HEAD_DIM) + EPS) * nw_ref[...] * _silu(gz)

    _round_robin([one_sequence(sq) for sq in range(nb)])

    @pl.when(ci == n_chunks - 1)
    def _():
        for sq in range(nb):
            st_ref[sq] = _extract_blocks(s_scr[sq], GROUP_WIDTH, HEAD_DIM)


def _gdn_prompt(proj3, conv_w, a_log, dt_bias, norm_w, c):
    b, l, _ = proj3.shape
    n = l // c
    fixed = lambda bi, ci: (0, 0)
    nb = math.gcd(b, PROMPT_SEQS_PER_STEP)
    hc = 2 * c
    return pl.pallas_call(
        functools.partial(_gdn_prompt_kernel, c=c, n_chunks=n, nb=nb),
        grid=(b // nb, n),
        in_specs=[pl.BlockSpec((nb, c, 1024), lambda bi, ci: (bi, ci, COL_GDN // 1024)),
                  pl.BlockSpec((nb, 8, 1024), lambda bi, ci: (bi, jnp.maximum(ci * (c // 8) - 1, 0), COL_GDN // 1024)),
                  pl.BlockSpec((nb, c, 128), lambda bi, ci: (bi, ci, COL_SMALL // 128)),
                  pl.BlockSpec((CONV_WIDTH, 768), fixed),
                  pl.BlockSpec((1, GROUP_WIDTH), fixed),
                  pl.BlockSpec((1, GROUP_WIDTH), fixed),
                  pl.BlockSpec((1, GROUP_WIDTH), fixed)],
        out_specs=[pl.BlockSpec((nb, c, GROUP_WIDTH), lambda bi, ci: (bi, ci, 0)),
                   pl.BlockSpec((nb, GROUP_WIDTH, HEAD_DIM), lambda bi, ci: (bi, 0, 0))],
        out_shape=[jax.ShapeDtypeStruct((b, l, GROUP_WIDTH), F32),
                   jax.ShapeDtypeStruct((b, GROUP_WIDTH, HEAD_DIM), F32)],
        scratch_shapes=[pltpu.VMEM((nb, GROUP_WIDTH, GROUP_WIDTH), F32),
                        pltpu.VMEM((nb, c + 8, 768), F32),
                        pltpu.VMEM((2 + int(math.log2(c)), hc, hc), F32)],
        compiler_params=pltpu.CompilerParams(dimension_semantics=("arbitrary", "arbitrary"),
                                             vmem_limit_bytes=VMEM_LIMIT),
        name="gdn_prompt",
    )(proj3, proj3, proj3, conv_w, _lane_rep(a_log), _lane_rep(dt_bias),
      jnp.tile(norm_w.astype(F32), N_HEADS).reshape(1, GROUP_WIDTH))


HGRN_SUB = 16


def _hgrn_lower_bound(logits, layer):
    rows = [logits[d:d + 1, :] for d in range(DEPTH)]
    mx = functools.reduce(jnp.maximum, rows)
    es = [jnp.exp(x - mx) for x in rows]
    tot = functools.reduce(lambda a, b: a + b, es)
    sm = [e / tot for e in es]
    acc = sm[0]
    for d in range(1, layer + 1):
        acc = acc + sm[d]
    return acc - sm[0]


def _hgrn_prompt_kernel(blk_ref, lb_ref, nw_ref, o_ref, st_ref, s_scr, *, r, n_chunks, layer, nb):
    ci = pl.program_id(1)
    sub = HGRN_SUB
    n_sub = r // sub

    @pl.when(ci == 0)
    def _():
        s_scr[...] = jnp.zeros_like(s_scr)

    lb = _hgrn_lower_bound(lb_ref[...], layer)
    rr = _iota((r, r), 0)
    cc = _iota((r, r), 1)
    same_sub = (rr // sub) == (cc // sub)
    cum_sel = (same_sub & (rr >= cc)).astype(BF16)
    tot_sel = same_sub.astype(BF16)
    ones_bd = _block_ones(GROUP_WIDTH, HEAD_DIM)
    bd_mask = _block_mask(GROUP_WIDTH, HEAD_DIM, HEAD_DIM)
    ii = _iota((sub, 1), 0)

    def one_sequence(sq):
        blk = blk_ref[sq]
        hq, hf, hi, hg = (blk[:, i * GROUP_WIDTH:(i + 1) * GROUP_WIDTH] for i in range(4))
        f = lb + (1.0 - lb) * _sigmoid(hf)
        q = _sigmoid(hq)
        k = 1.0 - f
        v = hi
        logf = jnp.log(f)
        g = _exact_dot_left(cum_sel, logf)
        g_tot = _exact_dot_left(tot_sel, logf)
        yield
        qt = (q * jnp.exp(g)).astype(BF16)
        kh = (k * jnp.exp(g_tot - g)).astype(BF16)
        v_bf = v.astype(BF16)

        s = s_scr[sq]
        outs = []
        for j in range(n_sub):
            lo = j * sub
            q_j, k_j, v_j, g_j = q[lo:lo + sub], k[lo:lo + sub], v[lo:lo + sub], g[lo:lo + sub]
            prods = []
            for jj in range(sub):
                e = jnp.exp(jnp.minimum(g_j - g_j[jj:jj + 1, :], 0.0))
                prods.append(jnp.where(ii >= jj, q_j * e * k_j[jj:jj + 1, :], 0.0))
            sc = _dot(jnp.concatenate(prods, axis=0).astype(BF16), ones_bd)
            o_inter = _dot_nt(qt[lo:lo + sub], s.astype(BF16))
            ds = _dot_tn(v_bf[lo:lo + sub], kh[lo:lo + sub])
            yield
            o_j = sc[0:sub] * v_j[0:1, :]
            for jj in range(1, sub):
                o_j = o_j + sc[jj * sub:(jj + 1) * sub] * v_j[jj:jj + 1, :]
            outs.append(o_j + o_inter)
            s = jnp.exp(g_tot[lo:lo + 1, :]) * s + jnp.where(bd_mask, ds, 0.0)
        s_scr[sq] = s

        o = jnp.concatenate(outs, axis=0)
        ss = _head_sumsq(o, ones_bd)
        yield
        o_ref[sq] = o * lax.rsqrt(ss * (1.0 / HEAD_DIM) + EPS) * nw_ref[...] * _silu(hg)

    _round_robin([one_sequence(sq) for sq in range(nb)])

    @pl.when(ci == n_chunks - 1)
    def _():
        for sq in range(nb):
            st_ref[sq] = _extract_blocks(s_scr[sq].T, GROUP_WIDTH, HEAD_DIM)


def _hgrn_prompt(proj3, lb_logits, norm_w, layer, r):
    b, l, _ = proj3.shape
    n = l // r
    fixed = lambda bi, ci: (0, 0)
    nb = math.gcd(b, PROMPT_SEQS_PER_STEP)
    return pl.pallas_call(
        functools.partial(_hgrn_prompt_kernel, r=r, n_chunks=n, layer=layer, nb=nb),
        grid=(b // nb, n),
        in_specs=[pl.BlockSpec((nb, r, 1024), lambda bi, ci: (bi, ci, COL_HGRN // 1024)),
                  pl.BlockSpec((DEPTH, GROUP_WIDTH), fixed),
                  pl.BlockSpec((1, GROUP_WIDTH), fixed)],
        out_specs=[pl.BlockSpec((nb, r, GROUP_WIDTH), lambda bi, ci: (bi, ci, 0)),
                   pl.BlockSpec((nb, GROUP_WIDTH, HEAD_DIM), lambda bi, ci: (bi, 0, 0))],
        out_shape=[jax.ShapeDtypeStruct((b, l, GROUP_WIDTH), F32),
                   jax.ShapeDtypeStruct((b, GROUP_WIDTH, HEAD_DIM), F32)],
        scratch_shapes=[pltpu.VMEM((nb, GROUP_WIDTH, GROUP_WIDTH), F32)],
        compiler_params=pltpu.CompilerParams(dimension_semantics=("arbitrary", "arbitrary"),
                                             vmem_limit_bytes=VMEM_LIMIT),
        name="hgrn_prompt",
    )(proj3, lb_logits.astype(F32), jnp.tile(norm_w.astype(F32), N_HEADS).reshape(1, GROUP_WIDTH))


DEC_SEQS = 128
DEC_LEN = 4


def _head_rows(h):
    return pl.ds(pl.multiple_of(h * HEAD_DIM, HEAD_DIM), HEAD_DIM)


def _recur_head(load_s, store_s, n_keys, decay_fn, k_fn, q_fn, v_blocks):
    def body(kk, accs):
        s = load_s(kk)
        accs = list(accs)
        for t in range(DEC_LEN):
            s = decay_fn(t, kk) * s + k_fn(t, kk) * v_blocks[t]
            accs[t] = accs[t] + q_fn(t, kk) * s
        store_s(kk, s)
        return tuple(accs)

    zero = jnp.zeros((HEAD_DIM, DEC_SEQS), F32)
    return lax.fori_loop(0, n_keys, body, (zero,) * DEC_LEN)


def _dec_ret_kernel(blk_ref, cos_ref, sin_ref, st_ref, o_ref, so_ref, q_scr, k_scr, v_scr, o_scr):
    h = pl.program_id(0)

    @pl.when(h == 0)
    def _():
        for t in range(DEC_LEN):
            blk = blk_ref[t]
            rq, rk, rv = blk[:, 0:256], blk[:, 256:512], blk[:, 512:768]
            cosv, sinv = cos_ref[t:t + 1, :], sin_ref[t:t + 1, :]
            q_scr[t] = (rq * cosv + _swap_halves(rq) * sinv).T
            k_scr[t] = ((rk * cosv + _swap_halves(rk) * sinv) * (HEAD_DIM ** -0.5)).T
            v_scr[t] = rv.T

    lg = jnp.where(h == 0, LOG_GAMMA[0], jnp.where(h == 1, LOG_GAMMA[1], jnp.where(h == 2, LOG_GAMMA[2], LOG_GAMMA[3])))
    gamma = jnp.exp(jnp.full((1, DEC_SEQS), lg, F32))
    hr = _head_rows(h)
    v_blocks = [v_scr[t, hr, :] for t in range(DEC_LEN)]
    accs = _recur_head(
        lambda kk: st_ref[0, 0, kk], functools.partial(_store_state, so_ref), HEAD_DIM,
        lambda t, kk: gamma,
        lambda t, kk: k_scr[t, pl.ds(h * HEAD_DIM + kk, 1), :],
        lambda t, kk: q_scr[t, pl.ds(h * HEAD_DIM + kk, 1), :],
        v_blocks)
    for t in range(DEC_LEN):
        o_scr[t, hr, :] = accs[t]

    @pl.when(h == N_HEADS - 1)
    def _():
        ones_bd = _block_ones(GROUP_WIDTH, HEAD_DIM)
        for t in range(DEC_LEN):
            o = o_scr[t].T
            ss = _head_sumsq(o, ones_bd)
            o_ref[t] = o * lax.rsqrt(ss * (1.0 / HEAD_DIM) + EPS) * _silu(blk_ref[t][:, 768:1024])


def _store_state(so_ref, kk, s):
    so_ref[0, 0, kk] = s


def _without_ref(kernel_fn, idx):
    def wrapped(*refs):
        return kernel_fn(*refs[:idx], *refs[idx + 1:])
    return wrapped


def _dec_call(kernel_fn, name, col, ins, in_specs, n_tok_scr, state_view, layer, carried, extra_scratch=()):
    blk_spec = pl.BlockSpec((DEC_LEN, DEC_SEQS, 1024), lambda h: (0, 0, col // 1024))
    st_spec = pl.BlockSpec((1, 1) + state_view.shape[2:], lambda h: (layer, h, 0, 0, 0))
    tok_scr = pltpu.VMEM((DEC_LEN, GROUP_WIDTH, DEC_SEQS), F32)
    ins = tuple(ins) + (state_view,)
    specs = [blk_spec] + in_specs + [st_spec]
    aliases = {}
    if carried is not None:
        kernel_fn = _without_ref(kernel_fn, len(ins))
        aliases = {len(ins): 1}
        ins = ins + (carried,)
        specs = specs + [pl.BlockSpec(memory_space=pl.ANY)]
    return pl.pallas_call(
        kernel_fn,
        grid=(N_HEADS,),
        in_specs=specs,
        out_specs=[pl.BlockSpec((DEC_LEN, DEC_SEQS, GROUP_WIDTH), lambda h: (0, 0, 0)), st_spec],
        out_shape=[jax.ShapeDtypeStruct((DEC_LEN, DEC_SEQS, GROUP_WIDTH), F32),
                   jax.ShapeDtypeStruct(state_view.shape, F32)],
        scratch_shapes=[tok_scr] * n_tok_scr + list(extra_scratch),
        input_output_aliases=aliases,
        compiler_params=pltpu.CompilerParams(dimension_semantics=("arbitrary",), vmem_limit_bytes=VMEM_LIMIT),
        name=name,
    )(*ins)


def _fixed1(shape):
    return pl.BlockSpec(shape, lambda h: (0,) * len(shape))


def _dec_ret(projd, cos_t, sin_t, state_view, layer, carried):
    return _dec_call(_dec_ret_kernel, "retention_decode", COL_RET, (projd, cos_t, sin_t),
                     [_fixed1((DEC_LEN, GROUP_WIDTH)), _fixed1((DEC_LEN, GROUP_WIDTH))], 4,
                     state_view, layer, carried)


def _dec_hgrn_kernel(blk_ref, lb_ref, nw_ref, st_ref, o_ref, so_ref, q_scr, k_scr, v_scr, f_scr, o_scr, *, layer):
    h = pl.program_id(0)

    @pl.when(h == 0)
    def _():
        lb = _hgrn_lower_bound(lb_ref[...], layer)
        for t in range(DEC_LEN):
            blk = blk_ref[t]
            f = lb + (1.0 - lb) * _sigmoid(blk[:, 256:512])
            q_scr[t] = _sigmoid(blk[:, 0:256]).T
            k_scr[t] = (1.0 - f).T
            v_scr[t] = blk[:, 512:768].T
            f_scr[t] = f.T

    hr = _head_rows(h)
    v_blocks = [v_scr[t, hr, :] for t in range(DEC_LEN)]
    row = lambda scr: (lambda t, kk: scr[t, pl.ds(h * HEAD_DIM + kk, 1), :])
    accs = _recur_head(lambda kk: st_ref[0, 0, kk], functools.partial(_store_state, so_ref), HEAD_DIM,
                       row(f_scr), row(k_scr), row(q_scr), v_blocks)
    for t in range(DEC_LEN):
        o_scr[t, hr, :] = accs[t]

    @pl.when(h == N_HEADS - 1)
    def _():
        ones_bd = _block_ones(GROUP_WIDTH, HEAD_DIM)
        for t in range(DEC_LEN):
            o = o_scr[t].T
            ss = _head_sumsq(o, ones_bd)
            o_ref[t] = o * lax.rsqrt(ss * (1.0 / HEAD_DIM) + EPS) * nw_ref[...] * _silu(blk_ref[t][:, 768:1024])


def _dec_hgrn(projd, lb_logits, norm_w, state_view, layer, carried):
    return _dec_call(functools.partial(_dec_hgrn_kernel, layer=layer), "hgrn_decode", COL_HGRN,
                     (projd, lb_logits.astype(F32), jnp.tile(norm_w.astype(F32), N_HEADS).reshape(1, GROUP_WIDTH)),
                     [_fixed1((DEPTH, GROUP_WIDTH)), _fixed1((1, GROUP_WIDTH))], 5, state_view, layer, carried)


def _hist_spec(layer):
    return pl.BlockSpec((1, CONV_WIDTH - 1, DEC_SEQS, 768), lambda h: (layer, 0, 0, 0))


def _dec_conv_silu(hist_ref, xs, w, bias):
    xe = [hist_ref[0, j] for j in range(CONV_WIDTH - 1)] + xs
    out = []
    for t in range(DEC_LEN):
        y = xe[t] * w[0:1, :]
        for j in range(1, CONV_WIDTH):
            y = y + xe[t + j] * w[j:j + 1, :]
        if bias is not None:
            y = y + bias
        out.append(_silu(y))
    return out


def _dec_ssd_kernel(blk_ref, small_ref, hist_ref, cw_ref, cb_ref, dtb_ref, alog_ref, dskip_ref, nw_ref, st_ref,
                    o_ref, so_ref, c_scr, b_scr, v_scr, a_scr, o_scr, x_scr):
    h = pl.program_id(0)

    @pl.when(h == 0)
    def _():
        xbc = _dec_conv_silu(hist_ref, [blk_ref[t][:, 256:1024] for t in range(DEC_LEN)], cw_ref[...], cb_ref[...])
        for t in range(DEC_LEN):
            xs = xbc[t][:, 0:256]
            dt = _softplus(_expand_small(small_ref[t], SMALL_SDT) + dtb_ref[...])
            x_scr[t] = xs
            v_scr[t] = (xs * dt).T
            b_scr[t] = xbc[t][:, 256:512].T
            c_scr[t] = xbc[t][:, 512:768].T
            a_scr[t] = jnp.exp(-jnp.exp(alog_ref[...]) * dt).T

    hr = _head_rows(h)
    g0 = (h // 2) * SSD_STATE
    v_blocks = [v_scr[t, hr, :] for t in range(DEC_LEN)]
    accs = _recur_head(
        lambda kk: st_ref[0, 0, kk], functools.partial(_store_state, so_ref), SSD_STATE,
        lambda t, kk: a_scr[t, pl.ds(h * HEAD_DIM, 1), :],
        lambda t, kk: b_scr[t, pl.ds(g0 + kk, 1), :],
        lambda t, kk: c_scr[t, pl.ds(g0 + kk, 1), :],
        v_blocks)
    for t in range(DEC_LEN):
        o_scr[t, hr, :] = accs[t]

    @pl.when(h == N_HEADS - 1)
    def _():
        for t in range(DEC_LEN):
            y = (o_scr[t].T + dskip_ref[...] * x_scr[t]) * _silu(blk_ref[t][:, 0:256])
            halves = [_rms_rows(y[:, gi * 128:(gi + 1) * 128]) for gi in range(2)]
            o_ref[t] = jnp.concatenate(halves, axis=1) * nw_ref[...]


def _dec_ssd(projd, hist, conv_w, conv_b, dt_bias, a_log, d_skip, norm_w, state_view, layer, carried):
    small_spec = pl.BlockSpec((DEC_LEN, DEC_SEQS, 128), lambda h: (0, 0, COL_SMALL // 128))
    return _dec_call(_dec_ssd_kernel, "ssd_decode", COL_SSD,
                     (projd, projd, hist, conv_w, conv_b.reshape(1, 768), _lane_rep(dt_bias), _lane_rep(a_log),
                      _lane_rep(d_skip), norm_w.reshape(1, GROUP_WIDTH)),
                     [small_spec, _hist_spec(layer), _fixed1((CONV_WIDTH, 768)),
                      _fixed1((1, 768))] + [_fixed1((1, GROUP_WIDTH))] * 4, 5, state_view, layer, carried,
                     extra_scratch=[pltpu.VMEM((DEC_LEN, DEC_SEQS, GROUP_WIDTH), F32)])


def _dec_gdn_kernel(blk_ref, small_ref, hist_ref, cw_ref, alog_ref, dtb_ref, nw_ref, st_ref,
                    o_ref, so_ref, q_scr, k_scr, v_scr, a_scr, b_scr, o_scr):
    h = pl.program_id(0)

    @pl.when(h == 0)
    def _():
        ones_bd = _block_ones(GROUP_WIDTH, HEAD_DIM)
        qkv = _dec_conv_silu(hist_ref, [blk_ref[t][:, 0:768] for t in range(DEC_LEN)], cw_ref[...], None)
        for t in range(DEC_LEN):
            gq, gk, gv = qkv[t][:, 0:256], qkv[t][:, 256:512], qkv[t][:, 512:768]
            q_scr[t] = (gq * lax.rsqrt(_head_sumsq(gq, ones_bd) + EPS) * (HEAD_DIM ** -0.5)).T
            k_scr[t] = (gk * lax.rsqrt(_head_sumsq(gk, ones_bd) + EPS)).T
            v_scr[t] = gv.T
            small = small_ref[t]
            b_scr[t] = _sigmoid(_expand_small(small, SMALL_GB)).T
            la = -jnp.exp(alog_ref[...]) * _softplus(_expand_small(small, SMALL_GA) + dtb_ref[...])
            a_scr[t] = jnp.exp(la).T

    hr = _head_rows(h)
    one_row = pl.ds(h * HEAD_DIM, 1)
    zero = jnp.zeros((HEAD_DIM, DEC_SEQS), F32)
    for t in range(DEC_LEN):
        a = a_scr[t, one_row, :]
        cur = st_ref if t == 0 else so_ref

        def kts(kk, r):
            return r + k_scr[t, pl.ds(h * HEAD_DIM + kk, 1), :] * cur[0, 0, kk]

        r = lax.fori_loop(0, HEAD_DIM, kts, zero)
        u = b_scr[t, one_row, :] * (v_scr[t, hr, :] - a * r)

        def upd(kk, acc):
            s = a * cur[0, 0, kk] + k_scr[t, pl.ds(h * HEAD_DIM + kk, 1), :] * u
            so_ref[0, 0, kk] = s
            return acc + q_scr[t, pl.ds(h * HEAD_DIM + kk, 1), :] * s

        o_scr[t, hr, :] = lax.fori_loop(0, HEAD_DIM, upd, zero)

    @pl.when(h == N_HEADS - 1)
    def _():
        ones_bd = _block_ones(GROUP_WIDTH, HEAD_DIM)
        for t in range(DEC_LEN):
            o = o_scr[t].T
            ss = _head_sumsq(o, ones_bd)
            o_ref[t] = o * lax.rsqrt(ss * (1.0 / HEAD_DIM) + EPS) * nw_ref[...] * _silu(blk_ref[t][:, 768:1024])


def _dec_gdn(projd, hist, conv_w, a_log, dt_bias, norm_w, state_view, layer, carried):
    small_spec = pl.BlockSpec((DEC_LEN, DEC_SEQS, 128), lambda h: (0, 0, COL_SMALL // 128))
    return _dec_call(_dec_gdn_kernel, "gdn_decode", COL_GDN,
                     (projd, projd, hist, conv_w, _lane_rep(a_log), _lane_rep(dt_bias),
                      jnp.tile(norm_w.astype(F32), N_HEADS).reshape(1, GROUP_WIDTH)),
                     [small_spec, _hist_spec(layer), _fixed1((CONV_WIDTH, 768))]
                     + [_fixed1((1, GROUP_WIDTH))] * 3, 6, state_view, layer, carried)


def _reorder_cols(w):
    lead = w.shape[:-1]
    small = jnp.concatenate([w[..., 2048:2056], w[..., 4104:4108],
                             jnp.zeros(lead + (P_PAD - COL_SMALL - 12,), w.dtype)], axis=-1)
    return jnp.concatenate([w[..., 0:2048], w[..., 2056:4104], small], axis=-1)


def _prep_w_in(w):
    return _reorder_cols(w).astype(BF16)


def _w_in_prep_kernel(w_ref, tail_ref, o_ref):
    x = w_ref[0]
    o_ref[0, :, 0:COL_RET] = x[:, 0:COL_RET].astype(BF16)
    o_ref[0, :, COL_RET:COL_SMALL] = x[:, COL_RET + 8:COL_SMALL + 8].astype(BF16)
    lane = _iota((x.shape[0], 128), 1)
    small = jnp.where(lane < 8, x[:, COL_RET:COL_RET + 128], jnp.where(lane < 12, tail_ref[0], 0.0))
    o_ref[0, :, COL_SMALL:P_PAD] = small.astype(BF16)


def _prep_w_in_all(w_in):
    d, r, p = w_in.shape
    rows = 256
    return pl.pallas_call(
        _w_in_prep_kernel,
        grid=(d, r // rows),
        in_specs=[pl.BlockSpec((1, rows, p), lambda l, i: (l, i, 0)),
                  pl.BlockSpec((1, rows, 128), lambda l, i: (l, i, COL_SMALL // 128))],
        out_specs=pl.BlockSpec((1, rows, P_PAD), lambda l, i: (l, i, 0)),
        out_shape=jax.ShapeDtypeStruct((d, r, P_PAD), BF16),
        compiler_params=pltpu.CompilerParams(dimension_semantics=("arbitrary", "arbitrary"),
                                             vmem_limit_bytes=VMEM_LIMIT),
        name="w_in_prep",
    )(w_in, w_in)


def _rotary_tables(pos):
    half = HEAD_DIM // 2
    inv_freq = RET_THETA ** (-jnp.arange(half, dtype=F32) / half)
    ang = pos.astype(F32)[:, None] * inv_freq[None, :]
    cos, sin = jnp.cos(ang), jnp.sin(ang)
    cos_t = jnp.tile(cos, (1, 2 * N_HEADS))
    sin_t = jnp.tile(jnp.concatenate([-sin, sin], axis=1), (1, N_HEADS))
    return cos_t, sin_t


RET_CHUNK = 128
SSD_CHUNK = 128
GDN_CHUNK = 64
HGRN_ROWS = 128


def _forward(x_prompt, x_sample, states, p, past_len):
    st_hg, st_gd, st_gc, st_rt, st_sd, st_sc = states
    bp, lp, _ = x_prompt.shape
    nd, ld, _ = x_sample.shape
    xp = x_prompt.astype(F32).reshape(bp * lp, D_MODEL)
    xd = jnp.transpose(x_sample.astype(F32), (1, 0, 2)).reshape(ld * nd, D_MODEL)
    cos_p, sin_p = _rotary_tables(jnp.arange(lp))
    cos_d, sin_d = _rotary_tables(past_len + jnp.arange(ld))
    outs = {k: [] for k in ("hp", "gp", "gcp", "gcs", "rp", "sp", "scp", "scs")}
    w_in_all = _prep_w_in_all(p["w_in"].astype(F32))
    wo, wu, wd = (p[k].astype(BF16) for k in ("w_out", "w_up", "w_down"))
    norm_mix = p["norm_mix"].astype(F32).reshape(DEPTH, 1, D_MODEL)
    norm_ffn = p["norm_ffn"].astype(F32).reshape(DEPTH, 1, D_MODEL)
    sv_hg, sv_gd, sv_rt, sv_sd = (jnp.transpose(s.astype(F32), (0, 2, 3, 4, 1)) for s in (st_hg, st_gd, st_rt, st_sd))
    hist_g = jnp.transpose(st_gc.astype(F32), (0, 2, 1, 3))
    hist_s = jnp.transpose(st_sc.astype(F32), (0, 2, 1, 3))
    new_hg = new_gd = new_rt = new_sd = None
    for l in range(DEPTH):
        pp = _proj(xp, norm_mix, w_in_all, l).reshape(bp, lp, P_PAD)
        pd = _proj(xd, norm_mix, w_in_all, l).reshape(ld, nd, P_PAD)

        oa, sa = _hgrn_prompt(pp, p["hgrn_lb_logits"], p["hgrn_norm"][l], l, HGRN_ROWS)
        ob, sb = _gdn_prompt(pp, p["gdn_conv_w"][l], p["gdn_a_log"][l], p["gdn_dt_bias"][l], p["gdn_norm"][l],
                             GDN_CHUNK)
        oc, sc = _ret_prompt(pp, cos_p, sin_p, RET_CHUNK)
        od, sd = _ssd_prompt(pp, p["ssd_conv_w"][l], p["ssd_conv_b"][l], p["ssd_dt_bias"][l], p["ssd_a_log"][l],
                             p["ssd_d"][l], p["ssd_norm"][l], SSD_CHUNK)
        outs["hp"].append(sa.reshape(bp, N_HEADS, HEAD_DIM, HEAD_DIM))
        outs["gp"].append(sb.reshape(bp, N_HEADS, HEAD_DIM, HEAD_DIM))
        outs["rp"].append(sc.reshape(bp, N_HEADS, HEAD_DIM, HEAD_DIM))
        outs["sp"].append(sd.reshape(bp, N_HEADS, SSD_STATE, HEAD_DIM))
        outs["gcp"].append(pp[:, lp - 3:, COL_GDN:COL_GDN + 768])
        outs["scp"].append(pp[:, lp - 3:, COL_SSD + 256:COL_SSD + 1024])
        xp = _out_ffn(xp, [o.reshape(bp * lp, GROUP_WIDTH) for o in (oa, ob, oc, od)], wo, norm_ffn, wu, wd,
                      p["norm_final"], l)

        da, new_hg = _dec_hgrn(pd, p["hgrn_lb_logits"], p["hgrn_norm"][l], sv_hg, l, new_hg)
        db, new_gd = _dec_gdn(pd, hist_g, p["gdn_conv_w"][l], p["gdn_a_log"][l], p["gdn_dt_bias"][l],
                              p["gdn_norm"][l], sv_gd, l, new_gd)
        dc, new_rt = _dec_ret(pd, cos_d, sin_d, sv_rt, l, new_rt)
        dd, new_sd = _dec_ssd(pd, hist_s, p["ssd_conv_w"][l], p["ssd_conv_b"][l], p["ssd_dt_bias"][l],
                              p["ssd_a_log"][l], p["ssd_d"][l], p["ssd_norm"][l], sv_sd, l, new_sd)
        outs["gcs"].append(jnp.transpose(pd[ld - 3:, :, COL_GDN:COL_GDN + 768], (1, 0, 2)))
        outs["scs"].append(jnp.transpose(pd[ld - 3:, :, COL_SSD + 256:COL_SSD + 1024], (1, 0, 2)))
        xd = _out_ffn(xd, [o.reshape(ld * nd, GROUP_WIDTH) for o in (da, db, dc, dd)], wo, norm_ffn, wu, wd,
                      p["norm_final"], l)

    y_prompt = xp.reshape(bp, lp, D_MODEL)
    y_sample = jnp.transpose(xd.reshape(ld, nd, D_MODEL), (1, 0, 2))
    st = {k: jnp.stack(v) for k, v in outs.items()}
    hs, gs, rs, ss = (jnp.transpose(s, (0, 4, 1, 2, 3)) for s in (new_hg, new_gd, new_rt, new_sd))
    return (y_prompt, y_sample, st["hp"], hs, st["gp"], gs, st["gcp"], st["gcs"],
            st["rp"], rs, st["sp"], ss, st["scp"], st["scs"])


def kernel(x_prompt, x_sample, state_hgrn, state_gdn, state_gdn_conv, state_ret, state_ssd, state_ssd_conv,
           norm_mix, w_in, hgrn_lb_logits, hgrn_norm, gdn_conv_w, gdn_a_log, gdn_dt_bias, gdn_norm,
           ssd_conv_w, ssd_conv_b, ssd_dt_bias, ssd_a_log, ssd_d, ssd_norm,
           w_out, norm_ffn, w_up, w_down, norm_final):
    params = dict(norm_mix=norm_mix, w_in=w_in, hgrn_lb_logits=hgrn_lb_logits, hgrn_norm=hgrn_norm,
                  gdn_conv_w=gdn_conv_w, gdn_a_log=gdn_a_log, gdn_dt_bias=gdn_dt_bias, gdn_norm=gdn_norm,
                  ssd_conv_w=ssd_conv_w, ssd_conv_b=ssd_conv_b, ssd_dt_bias=ssd_dt_bias, ssd_a_log=ssd_a_log,
                  ssd_d=ssd_d, ssd_norm=ssd_norm, w_out=w_out, norm_ffn=norm_ffn, w_up=w_up,
                  w_down=w_down, norm_final=norm_final)
    states = (state_hgrn, state_gdn, state_gdn_conv, state_ret, state_ssd, state_ssd_conv)
    return _forward(x_prompt, x_sample, states, params, 16384)
```

```python
import functools
import math

import numpy as np
import jax
import jax.numpy as jnp
from jax import lax
from jax.experimental import pallas as pl
from jax.experimental.pallas import tpu as pltpu

F32 = jnp.float32
BF16 = jnp.bfloat16

D_MODEL = 1024
GROUP_WIDTH = 256
HEAD_DIM = 64
N_HEADS = 4
CONV_WIDTH = 4
SSD_STATE = 128
D_FF = 4096
RET_THETA = 10000.0
EPS = 1e-6
DEPTH = 2

COL_HGRN = 0
COL_GDN = 1024
COL_RET = 2048
COL_SSD = 3072
COL_SMALL = 4096
P_PAD = 4224
SMALL_GA, SMALL_GB, SMALL_SDT = 0, 4, 8

VMEM_LIMIT = 56 * 1024 * 1024
LOG_GAMMA = [math.log(1.0 - 2.0 ** (-5.0 - h)) for h in range(N_HEADS)]


def _dot(a, b):
    return jnp.dot(a, b, preferred_element_type=F32)


def _dot_nt(a, b):
    return lax.dot_general(a, b, (((1,), (1,)), ((), ())), preferred_element_type=F32)


def _dot_tn(a, b):
    return lax.dot_general(a, b, (((0,), (0,)), ((), ())), preferred_element_type=F32)


def _round_robin(gens):
    live = list(gens)
    while live:
        nxt = []
        for g in live:
            try:
                next(g)
                nxt.append(g)
            except StopIteration:
                pass
        live = nxt


def _split3(x):
    hi = x.astype(BF16)
    r1 = x - hi.astype(F32)
    mid = r1.astype(BF16)
    lo = (r1 - mid.astype(F32)).astype(BF16)
    return hi, mid, lo


def _exact_dot(x, sel):
    hi, mid, lo = _split3(x)
    return _dot(hi, sel) + _dot(mid, sel) + _dot(lo, sel)


def _exact_dot_left(sel, x):
    hi, mid, lo = _split3(x)
    return _dot(sel, hi) + _dot(sel, mid) + _dot(sel, lo)


def _iota(shape, dim):
    return lax.broadcasted_iota(jnp.int32, shape, dim)


def _head_of_lane(n_lanes, width=HEAD_DIM):
    return _iota((1, n_lanes), 1) // width


def _head_masks(n_lanes=GROUP_WIDTH, width=HEAD_DIM):
    hl = _head_of_lane(n_lanes, width)
    return [hl == h for h in range(n_lanes // width)]


def _stack_heads(x, masks):
    return jnp.concatenate([jnp.where(m, x, jnp.zeros_like(x)) for m in masks], axis=0)


def _unstack_heads(y, masks, c):
    out = jnp.where(masks[0], y[0:c], 0.0)
    for h in range(1, len(masks)):
        out = out + jnp.where(masks[h], y[h * c:(h + 1) * c], 0.0)
    return out


def _block_ones(n, width, dtype=BF16):
    r = _iota((n, n), 0) // width
    c = _iota((n, n), 1) // width
    return (r == c).astype(dtype)


def _block_mask(n, rwidth, cwidth):
    return (_iota((n, n), 0) // rwidth) == (_iota((n, n), 1) // cwidth)


def _lower_tri(c, dtype=BF16):
    return (_iota((c, c), 0) >= _iota((c, c), 1)).astype(dtype)


def _cumsum_rows(x, c):
    return _exact_dot_left(_lower_tri(c), x)


def _sigmoid(x):
    return 1.0 / (1.0 + jnp.exp(-x))


def _silu(x):
    return x * _sigmoid(x)


def _softplus(x):
    return jnp.maximum(x, 0.0) + jnp.log(1.0 + jnp.exp(-jnp.abs(x)))


def _rms_rows(x):
    return x * lax.rsqrt(jnp.mean(x * x, axis=-1, keepdims=True) + EPS)


def _head_sumsq(x, ones_bd):
    sq = x * x
    hi = sq.astype(BF16)
    lo = (sq - hi.astype(F32)).astype(BF16)
    return _dot(hi, ones_bd) + _dot(lo, ones_bd)


def _expand_small(small, first_lane):
    r = _iota((128, GROUP_WIDTH), 0)
    c = _iota((128, GROUP_WIDTH), 1) // HEAD_DIM
    sel = (r == c + first_lane).astype(BF16)
    return _exact_dot(small, sel)


def _decay_diff_operands(g):
    hi, mid, lo = (x.astype(F32) for x in _split3(g))
    pos = _iota(g.shape, 1) % HEAD_DIM
    a = jnp.where(pos == 0, hi, jnp.where(pos == 1, mid, jnp.where(pos == 2, lo,
                  jnp.where(pos < 6, 1.0, 0.0))))
    b = jnp.where(pos < 3, 1.0, jnp.where(pos == 3, -hi, jnp.where(pos == 4, -mid,
                  jnp.where(pos == 5, -lo, 0.0))))
    return a, b


def _extract_blocks(s_wide, rows, width):
    sel = ((_iota((GROUP_WIDTH, width), 0) % width) == _iota((GROUP_WIDTH, width), 1)).astype(BF16)
    return _exact_dot(s_wide, sel)


def _proj_kernel(x_ref, nw_ref, w_ref, o_ref):
    h = _rms_rows(x_ref[...]) * nw_ref[0]
    o_ref[...] = _dot(h.astype(BF16), w_ref[0])


def _proj(x2d, norm_w, w_bf16, layer):
    t = x2d.shape[0]
    tm = min(t, 512)
    return pl.pallas_call(
        _proj_kernel,
        grid=(t // tm,),
        in_specs=[pl.BlockSpec((tm, D_MODEL), lambda i: (i, 0)),
                  pl.BlockSpec((1, 1, D_MODEL), lambda i: (layer, 0, 0)),
                  pl.BlockSpec((1, D_MODEL, P_PAD), lambda i: (layer, 0, 0))],
        out_specs=pl.BlockSpec((tm, P_PAD), lambda i: (i, 0)),
        out_shape=jax.ShapeDtypeStruct((t, P_PAD), F32),
        compiler_params=pltpu.CompilerParams(dimension_semantics=("arbitrary",),
                                             vmem_limit_bytes=VMEM_LIMIT),
        name="norm_in_proj",
    )(x2d, norm_w, w_bf16)


def _ffn_kernel(x_ref, oa_ref, ob_ref, oc_ref, od_ref, wo_ref, nf_ref, wu_ref, wd_ref, nfin_ref,
                o_ref, *, final):
    mix = jnp.concatenate([oa_ref[...], ob_ref[...], oc_ref[...], od_ref[...]], axis=1)
    x = x_ref[...] + _dot(mix.astype(BF16), wo_ref[0])
    h = (_rms_rows(x) * nf_ref[0]).astype(BF16)
    acc = x
    ft = 1024
    for t in range(D_FF // ft):
        up = _dot(h, wu_ref[0, :, t * ft:(t + 1) * ft])
        up = jnp.square(jnp.maximum(up, 0.0)).astype(BF16)
        acc = acc + _dot(up, wd_ref[0, t * ft:(t + 1) * ft, :])
    if final:
        acc = _rms_rows(acc) * nfin_ref[...]
    o_ref[...] = acc


def _out_ffn(x2d, mixes, wo, nf, wu, wd, nfin, layer):
    t = x2d.shape[0]
    tm = min(t, 512)
    row = lambda i: (i, 0)
    lay = lambda i: (layer, 0, 0)
    return pl.pallas_call(
        functools.partial(_ffn_kernel, final=(layer == DEPTH - 1)),
        grid=(t // tm,),
        in_specs=[pl.BlockSpec((tm, D_MODEL), row)]
                 + [pl.BlockSpec((tm, GROUP_WIDTH), row)] * 4
                 + [pl.BlockSpec((1, D_MODEL, D_MODEL), lay),
                    pl.BlockSpec((1, 1, D_MODEL), lay),
                    pl.BlockSpec((1, D_MODEL, D_FF), lay),
                    pl.BlockSpec((1, D_FF, D_MODEL), lay),
                    pl.BlockSpec((1, D_MODEL), lambda i: (0, 0))],
        out_specs=pl.BlockSpec((tm, D_MODEL), row),
        out_shape=jax.ShapeDtypeStruct((t, D_MODEL), F32),
        compiler_params=pltpu.CompilerParams(dimension_semantics=("arbitrary",),
                                             vmem_limit_bytes=VMEM_LIMIT),
        name="out_proj_ffn",
    )(x2d, *mixes, wo, nf, wu, wd, nfin.reshape(1, D_MODEL))


def _swap_halves(x):
    first = (_iota((1, 128), 1) % HEAD_DIM) < (HEAD_DIM // 2)
    parts = []
    for p in range(GROUP_WIDTH // 128):
        xp = x[:, p * 128:(p + 1) * 128]
        parts.append(jnp.where(first, pltpu.roll(xp, 96, 1), pltpu.roll(xp, 32, 1)))
    return jnp.concatenate(parts, axis=1)


def _conv_silu(xe_ref, halo, x, w, bias, first_chunk, c):
    xe_ref[0:8, :] = jnp.where(first_chunk, jnp.zeros_like(halo), halo)
    xe_ref[8:, :] = x
    y = w[3:4, :] * x
    for j in range(CONV_WIDTH - 1):
        y = y + w[j:j + 1, :] * xe_ref[5 + j:5 + j + c, :]
    if bias is not None:
        y = y + bias
    return _silu(y)


def _ret_prompt_kernel(blk_ref, cos_ref, sin_ref, o_ref, st_ref, s_scr, *, c, n_chunks, nb):
    ci = pl.program_id(1)

    @pl.when(ci == 0)
    def _():
        s_scr[...] = jnp.zeros_like(s_scr)

    cosv, sinv = cos_ref[...], sin_ref[...]
    masks = _head_masks()
    hl = _head_of_lane(GROUP_WIDTH)
    lg = jnp.full((1, GROUP_WIDTH), LOG_GAMMA[0], F32)
    for h in range(1, N_HEADS):
        lg = jnp.where(hl == h, LOG_GAMMA[h], lg)
    ri = _iota((c, 1), 0).astype(F32)
    dij = (_iota((c, c), 0) - _iota((c, c), 1)).astype(F32)
    causal = dij >= 0.0
    decay = jnp.concatenate(
        [jnp.where(causal, jnp.exp(jnp.maximum(dij, 0.0) * LOG_GAMMA[h]), 0.0) for h in range(N_HEADS)],
        axis=0)
    q_scale = jnp.exp((ri + 1.0) * lg)
    k_scale = jnp.exp((float(c - 1) - ri) * lg) * (HEAD_DIM ** -0.5)
    s_scale = jnp.exp(float(c) * lg)
    bd_mask = _block_mask(GROUP_WIDTH, HEAD_DIM, HEAD_DIM)
    ones_bd = _block_ones(GROUP_WIDTH, HEAD_DIM)

    def one_sequence(sq):
        blk = blk_ref[sq]
        rq, rk, rv, rg = (blk[:, i * GROUP_WIDTH:(i + 1) * GROUP_WIDTH] for i in range(4))
        q = rq * cosv + _swap_halves(rq) * sinv
        k = rk * cosv + _swap_halves(rk) * sinv
        v = rv.astype(BF16)
        s = s_scr[sq]
        qk = _dot_nt(_stack_heads(q, masks).astype(BF16), k.astype(BF16))
        o_inter = _dot((q * q_scale).astype(BF16), s.astype(BF16))
        ds = _dot_tn((k * k_scale).astype(BF16), v)
        yield
        scores = qk * (decay * (HEAD_DIM ** -0.5))
        pv = _dot(scores.astype(BF16), v)
        s_scr[sq] = s_scale * s + jnp.where(bd_mask, ds, 0.0)
        yield
        o = _unstack_heads(pv, masks, c) + o_inter
        ss = _head_sumsq(o, ones_bd)
        yield
        o_ref[sq] = o * lax.rsqrt(ss * (1.0 / HEAD_DIM) + EPS) * _silu(rg)

    _round_robin([one_sequence(sq) for sq in range(nb)])

    @pl.when(ci == n_chunks - 1)
    def _():
        for sq in range(nb):
            st_ref[sq] = _extract_blocks(s_scr[sq], GROUP_WIDTH, HEAD_DIM)


PROMPT_SEQS_PER_STEP = 8
GDN_SEQS_PER_STEP = 4


def _ret_prompt(proj3, cos_t, sin_t, c):
    b, l, _ = proj3.shape
    n = l // c
    nb = math.gcd(b, PROMPT_SEQS_PER_STEP)
    return pl.pallas_call(
        functools.partial(_ret_prompt_kernel, c=c, n_chunks=n, nb=nb),
        grid=(b // nb, n),
        in_specs=[pl.BlockSpec((nb, c, 1024), lambda bi, ci: (bi, ci, COL_RET // 1024)),
                  pl.BlockSpec((c, GROUP_WIDTH), lambda bi, ci: (ci, 0)),
                  pl.BlockSpec((c, GROUP_WIDTH), lambda bi, ci: (ci, 0))],
        out_specs=[pl.BlockSpec((nb, c, GROUP_WIDTH), lambda bi, ci: (bi, ci, 0)),
                   pl.BlockSpec((nb, GROUP_WIDTH, HEAD_DIM), lambda bi, ci: (bi, 0, 0))],
        out_shape=[jax.ShapeDtypeStruct((b, l, GROUP_WIDTH), F32),
                   jax.ShapeDtypeStruct((b, GROUP_WIDTH, HEAD_DIM), F32)],
        scratch_shapes=[pltpu.VMEM((nb, GROUP_WIDTH, GROUP_WIDTH), F32)],
        compiler_params=pltpu.CompilerParams(dimension_semantics=("arbitrary", "arbitrary"),
                                             vmem_limit_bytes=VMEM_LIMIT),
        name="retention_prompt",
    )(proj3, cos_t, sin_t)


def _ssd_prompt_kernel(blk_ref, halo_ref, small_ref, cw_ref, cb_ref, dtb_ref, alog_ref, dskip_ref, nw_ref,
                       o_ref, st_ref, s_scr, xe_scr, *, c, n_chunks, nb):
    ci = pl.program_id(1)

    @pl.when(ci == 0)
    def _():
        s_scr[...] = jnp.zeros_like(s_scr)

    masks = _head_masks()
    causal = _iota((c, c), 0) >= _iota((c, c), 1)
    causal4 = jnp.concatenate([causal] * N_HEADS, axis=0)
    group_mask = _block_mask(GROUP_WIDTH, 128, 128)
    tri = _lower_tri(c)
    neg_a = -jnp.exp(alog_ref[...]) * LOG2E

    def one_sequence(sq):
        blk = blk_ref[sq]
        sz = blk[:, 0:GROUP_WIDTH]
        xbc = _conv_silu(xe_scr.at[sq], halo_ref[sq][:, GROUP_WIDTH:], blk[:, GROUP_WIDTH:], cw_ref[...],
                         cb_ref[...], ci == 0, c)
        xs = xbc[:, 0:256]
        bmat = xbc[:, 256:512].astype(BF16)
        cmat = xbc[:, 512:768].astype(BF16)
        s = s_scr[sq]
        cb = [_dot_nt(cmat[:, gi * 128:(gi + 1) * 128], bmat[:, gi * 128:(gi + 1) * 128]) for gi in range(2)]
        y_inter = _dot(cmat, s.astype(BF16))
        dt = _softplus(_expand_small(small_ref[sq], SMALL_SDT) + dtb_ref[...])
        yield
        g = _exact_dot_left(tri, neg_a * dt)
        yield
        g_last = g[c - 1:c, :]
        da, db = _decay_diff_operands(g)
        diff = _dot_nt(_stack_heads(da, masks).astype(BF16), db.astype(BF16))
        v = xs * dt
        vend = v * jnp.exp2(g_last - g)
        ds = _dot_tn(bmat, vend.astype(BF16))
        yield
        decay = jnp.where(causal4, jnp.exp2(diff), 0.0)
        scores = jnp.concatenate([cb[0], cb[0], cb[1], cb[1]], axis=0) * decay
        pv = _dot(scores.astype(BF16), v.astype(BF16))
        s_scr[sq] = jnp.exp2(g_last) * s + jnp.where(group_mask, ds, 0.0)
        yield
        y = _unstack_heads(pv, masks, c) + y_inter * jnp.exp2(g)
        y = (y + dskip_ref[...] * xs) * _silu(sz)
        halves = [_rms_rows(y[:, gi * 128:(gi + 1) * 128]) for gi in range(2)]
        o_ref[sq] = jnp.concatenate(halves, axis=1) * nw_ref[...]

    _round_robin([one_sequence(sq) for sq in range(nb)])

    @pl.when(ci == n_chunks - 1)
    def _():
        for sq in range(nb):
            for h in range(N_HEADS):
                gi = h // 2
                rows = jnp.where(masks[h], s_scr[sq, gi * 128:(gi + 1) * 128, :], 0.0)
                st_ref[sq, h * 128:(h + 1) * 128, :] = _extract_blocks(rows, 128, HEAD_DIM)


def _lane_rep(p):
    return jnp.repeat(p.astype(F32), HEAD_DIM).reshape(1, GROUP_WIDTH)


def _ssd_prompt(proj3, conv_w, conv_b, dt_bias, a_log, d_skip, norm_w, c):
    b, l, _ = proj3.shape
    n = l // c
    fixed = lambda bi, ci: (0, 0)
    nb = math.gcd(b, PROMPT_SEQS_PER_STEP)
    return pl.pallas_call(
        functools.partial(_ssd_prompt_kernel, c=c, n_chunks=n, nb=nb),
        grid=(b // nb, n),
        in_specs=[pl.BlockSpec((nb, c, 1024), lambda bi, ci: (bi, ci, COL_SSD // 1024)),
                  pl.BlockSpec((nb, 8, 1024), lambda bi, ci: (bi, jnp.maximum(ci * (c // 8) - 1, 0), COL_SSD // 1024)),
                  pl.BlockSpec((nb, c, 128), lambda bi, ci: (bi, ci, COL_SMALL // 128)),
                  pl.BlockSpec((CONV_WIDTH, 768), fixed),
                  pl.BlockSpec((1, 768), fixed),
                  pl.BlockSpec((1, GROUP_WIDTH), fixed),
                  pl.BlockSpec((1, GROUP_WIDTH), fixed),
                  pl.BlockSpec((1, GROUP_WIDTH), fixed),
                  pl.BlockSpec((1, GROUP_WIDTH), fixed)],
        out_specs=[pl.BlockSpec((nb, c, GROUP_WIDTH), lambda bi, ci: (bi, ci, 0)),
                   pl.BlockSpec((nb, N_HEADS * SSD_STATE, HEAD_DIM), lambda bi, ci: (bi, 0, 0))],
        out_shape=[jax.ShapeDtypeStruct((b, l, GROUP_WIDTH), F32),
                   jax.ShapeDtypeStruct((b, N_HEADS * SSD_STATE, HEAD_DIM), F32)],
        scratch_shapes=[pltpu.VMEM((nb, GROUP_WIDTH, GROUP_WIDTH), F32),
                        pltpu.VMEM((nb, c + 8, 768), F32)],
        compiler_params=pltpu.CompilerParams(dimension_semantics=("arbitrary", "arbitrary"),
                                             vmem_limit_bytes=VMEM_LIMIT),
        name="ssd_prompt",
    )(proj3, proj3, proj3, conv_w, conv_b.reshape(1, 768), _lane_rep(dt_bias), _lane_rep(a_log),
      _lane_rep(d_skip), norm_w.reshape(1, GROUP_WIDTH))


def _gdn_prompt_kernel(blk_ref, halo_ref, small_ref, cw_ref, alog_ref, dtb_ref, nw_ref,
                       o_ref, st_ref, s_scr, xe_scr, m_scr, *, c, n_chunks, nb):
    ci = pl.program_id(1)
    hc = 2 * c
    n_lvl = int(math.log2(c))

    @pl.when(ci == 0)
    def _():
        s_scr[...] = jnp.zeros_like(s_scr)

    @pl.when((pl.program_id(0) == 0) & (ci == 0))
    def _():
        rr = _iota((hc, hc), 0)
        cc = _iota((hc, hc), 1)
        same = (rr // c) == (cc // c)
        m_scr[0] = (same & (rr >= cc)).astype(F32)
        m_scr[1] = (same & (rr > cc)).astype(F32)
        for lv in range(n_lvl):
            sz = 1 << lv
            off = ((rr // (2 * sz)) == (cc // (2 * sz))) & (((rr // sz) % 2) == 1) & (((cc // sz) % 2) == 0)
            m_scr[2 + lv] = off.astype(F32)

    ones_bd = _block_ones(GROUP_WIDTH, HEAD_DIM)
    bd_mask = _block_mask(GROUP_WIDTH, HEAD_DIM, HEAD_DIM)
    masks = _head_masks()
    pair_masks = [masks[0:2], masks[2:4]]
    tri = _lower_tri(c)
    neg_a = -jnp.exp(alog_ref[...]) * LOG2E

    def one_sequence(sq):
        blk = blk_ref[sq]
        gz = blk[:, 768:1024]
        qkv = _conv_silu(xe_scr.at[sq], halo_ref[sq][:, 0:768], blk[:, 0:768], cw_ref[...], None, ci == 0, c)
        gq, gk, v = qkv[:, 0:256], qkv[:, 256:512], qkv[:, 512:768]
        q = gq * lax.rsqrt(_head_sumsq(gq, ones_bd) + EPS) * (HEAD_DIM ** -0.5)
        k = gk * lax.rsqrt(_head_sumsq(gk, ones_bd) + EPS)
        yield
        small = small_ref[sq]
        beta = _sigmoid(_expand_small(small, SMALL_GB))
        g = _exact_dot_left(tri, neg_a * _softplus(_expand_small(small, SMALL_GA) + dtb_ref[...]))
        g_last = g[c - 1:c, :]
        eg = jnp.exp2(g)
        yield
        da, db = _decay_diff_operands(g)
        bk = beta * k
        bkg = bk * eg
        bv = beta * v
        a_mat, p_mat, x = [], [], []
        for pm in pair_masks:
            diff = _dot_nt(_stack_heads(da, pm).astype(BF16), _stack_heads(db, pm).astype(BF16))
            k_st = _stack_heads(k, pm).astype(BF16)
            kk = _dot_nt(_stack_heads(bk, pm).astype(BF16), k_st)
            qk = _dot_nt(_stack_heads(q, pm).astype(BF16), k_st)
            decay = jnp.exp2(jnp.minimum(diff, 0.0))
            a_mat.append(kk * (decay * m_scr[1]))
            p_mat.append((qk * (decay * m_scr[0])).astype(BF16))
            x.append(jnp.concatenate([_stack_heads(bkg, pm), _stack_heads(bv, pm)], axis=1))
        yield

        n_mat = [-(a * m_scr[2]) for a in a_mat]
        for lv in range(1, n_lvl):
            a_off = [a * m_scr[2 + lv] for a in a_mat]
            m = [ao + _dot(ao.astype(BF16), n.astype(BF16)) for ao, n in zip(a_off, n_mat)]
            yield
            n_mat = [n - mm - _dot(n.astype(BF16), mm.astype(BF16)) for n, mm in zip(n_mat, m)]
            yield
        x = [xx + _dot(n.astype(BF16), xx.astype(BF16)) for xx, n in zip(x, n_mat)]
        yield
        w = x[0][0:c, 0:256] + x[0][c:2 * c, 0:256] + x[1][0:c, 0:256] + x[1][c:2 * c, 0:256]
        u0 = x[0][0:c, 256:512] + x[0][c:2 * c, 256:512] + x[1][0:c, 256:512] + x[1][c:2 * c, 256:512]

        s = s_scr[sq]
        s_bf = s.astype(BF16)
        u = u0 - _dot(w.astype(BF16), s_bf)
        o = _dot((q * eg).astype(BF16), s_bf)
        yield
        pu = [_dot(pmat, _stack_heads(u, pm).astype(BF16)) for pmat, pm in zip(p_mat, pair_masks)]
        kend = k * jnp.exp2(g_last - g)
        ds = _dot_tn(kend.astype(BF16), u.astype(BF16))
        yield
        for pu_p in pu:
            o = o + pu_p[0:c] + pu_p[c:2 * c]
        s_scr[sq] = jnp.exp2(g_last) * s + jnp.where(bd_mask, ds, 0.0)
        ss = _head_sumsq(o, ones_bd)
        o_ref[sq] = o * lax.rsqrt(ss * (1.0 / HEAD_DIM) + EPS) * nw_ref[...] * _silu(gz)

    _round_robin([one_sequence(sq) for sq in range(nb)])

    @pl.when(ci == n_chunks - 1)
    def _():
        for sq in range(nb):
            st_ref[sq] = _extract_blocks(s_scr[sq], GROUP_WIDTH, HEAD_DIM)


def _gdn_prompt(proj3, conv_w, a_log, dt_bias, norm_w, c):
    b, l, _ = proj3.shape
    n = l // c
    fixed = lambda bi, ci: (0, 0)
    nb = math.gcd(b, GDN_SEQS_PER_STEP)
    hc = 2 * c
    return pl.pallas_call(
        functools.partial(_gdn_prompt_kernel, c=c, n_chunks=n, nb=nb),
        grid=(b // nb, n),
        in_specs=[pl.BlockSpec((nb, c, 1024), lambda bi, ci: (bi, ci, COL_GDN // 1024)),
                  pl.BlockSpec((nb, 8, 1024), lambda bi, ci: (bi, jnp.maximum(ci * (c // 8) - 1, 0), COL_GDN // 1024)),
                  pl.BlockSpec((nb, c, 128), lambda bi, ci: (bi, ci, COL_SMALL // 128)),
                  pl.BlockSpec((CONV_WIDTH, 768), fixed),
                  pl.BlockSpec((1, GROUP_WIDTH), fixed),
                  pl.BlockSpec((1, GROUP_WIDTH), fixed),
                  pl.BlockSpec((1, GROUP_WIDTH), fixed)],
        out_specs=[pl.BlockSpec((nb, c, GROUP_WIDTH), lambda bi, ci: (bi, ci, 0)),
                   pl.BlockSpec((nb, GROUP_WIDTH, HEAD_DIM), lambda bi, ci: (bi, 0, 0))],
        out_shape=[jax.ShapeDtypeStruct((b, l, GROUP_WIDTH), F32),
                   jax.ShapeDtypeStruct((b, GROUP_WIDTH, HEAD_DIM), F32)],
        scratch_shapes=[pltpu.VMEM((nb, GROUP_WIDTH, GROUP_WIDTH), F32),
                        pltpu.VMEM((nb, c + 8, 768), F32),
                        pltpu.VMEM((2 + int(math.log2(c)), hc, hc), F32)],
        compiler_params=pltpu.CompilerParams(dimension_semantics=("arbitrary", "arbitrary"),
                                             vmem_limit_bytes=VMEM_LIMIT),
        name="gdn_prompt",
    )(proj3, proj3, proj3, conv_w, _lane_rep(a_log), _lane_rep(dt_bias),
      jnp.tile(norm_w.astype(F32), N_HEADS).reshape(1, GROUP_WIDTH))


HGRN_SUB = 16
LOG2E = 1.4426950408889634


def _hgrn_lower_bound(logits, layer):
    rows = [logits[d:d + 1, :] for d in range(DEPTH)]
    mx = functools.reduce(jnp.maximum, rows)
    es = [jnp.exp(x - mx) for x in rows]
    tot = functools.reduce(lambda a, b: a + b, es)
    sm = [e / tot for e in es]
    acc = sm[0]
    for d in range(1, layer + 1):
        acc = acc + sm[d]
    return acc - sm[0]


def _hgrn_prompt_kernel(blk_ref, lb_ref, nw_ref, o_ref, st_ref, s_scr, *, r, n_chunks, layer, nb):
    ci = pl.program_id(1)
    sub = HGRN_SUB
    n_sub = r // sub

    @pl.when(ci == 0)
    def _():
        s_scr[...] = jnp.zeros_like(s_scr)

    lb = _hgrn_lower_bound(lb_ref[...], layer)
    rr = _iota((r, r), 0)
    cc = _iota((r, r), 1)
    same_sub = (rr // sub) == (cc // sub)
    cum_sel = (same_sub & (rr >= cc)).astype(BF16)
    tot_sel = same_sub.astype(BF16)
    ones_bd = _block_ones(GROUP_WIDTH, HEAD_DIM)
    bd_mask = _block_mask(GROUP_WIDTH, HEAD_DIM, HEAD_DIM)
    half = sub // 2
    i8 = _iota((half, 1), 0)

    def one_sequence(sq):
        blk = blk_ref[sq]
        hq, hf, hi, hg = (blk[:, i * GROUP_WIDTH:(i + 1) * GROUP_WIDTH] for i in range(4))
        f = lb + (1.0 - lb) * _sigmoid(hf)
        q = _sigmoid(hq)
        k = 1.0 - f
        v = hi
        logf = jnp.log(f)
        g = _exact_dot_left(cum_sel, logf)
        g_tot = _exact_dot_left(tot_sel, logf)
        yield
        a2 = (g + jnp.log(q)) * LOG2E
        h2 = (g - jnp.log(k)) * LOG2E
        gt2 = g_tot * LOG2E
        qt = jnp.exp2(a2).astype(BF16)
        kh = jnp.exp2(gt2 - h2).astype(BF16)
        v_bf = v.astype(BF16)

        s = s_scr[sq]
        outs = []
        for j in range(n_sub):
            lo = j * sub
            v_j = v[lo:lo + sub]
            a_lo, a_hi, h_j = a2[lo:lo + half], a2[lo + half:lo + sub], h2[lo:lo + sub]
            lo_blocks, hi_blocks = [], []
            for jj in range(sub):
                h_row = h_j[jj:jj + 1, :]
                if jj < half:
                    e_lo = jnp.exp2(a_lo - h_row)
                    lo_blocks.append(e_lo if jj == 0 else jnp.where(i8 >= jj, e_lo, 0.0))
                    hi_blocks.append(jnp.exp2(a_hi - h_row))
                else:
                    e_hi = jnp.exp2(a_hi - h_row)
                    hi_blocks.append(e_hi if jj == half else jnp.where(i8 >= jj - half, e_hi, 0.0))
            sc = _dot(jnp.concatenate(lo_blocks + hi_blocks, axis=0).astype(BF16), ones_bd)
            o_inter = _dot_nt(qt[lo:lo + sub], s.astype(BF16))
            ds = _dot_tn(v_bf[lo:lo + sub], kh[lo:lo + sub])
            yield
            n_lo = half * half
            o_lo = sc[0:half] * v_j[0:1, :]
            o_hi = sc[n_lo:n_lo + half] * v_j[0:1, :]
            for jj in range(1, sub):
                if jj < half:
                    o_lo = o_lo + sc[jj * half:(jj + 1) * half] * v_j[jj:jj + 1, :]
                o_hi = o_hi + sc[n_lo + jj * half:n_lo + (jj + 1) * half] * v_j[jj:jj + 1, :]
            outs.append(jnp.concatenate([o_lo, o_hi], axis=0) + o_inter)
            s = jnp.exp2(gt2[lo:lo + 1, :]) * s + jnp.where(bd_mask, ds, 0.0)
        s_scr[sq] = s

        o = jnp.concatenate(outs, axis=0)
        ss = _head_sumsq(o, ones_bd)
        yield
        o_ref[sq] = o * lax.rsqrt(ss * (1.0 / HEAD_DIM) + EPS) * nw_ref[...] * _silu(hg)

    _round_robin([one_sequence(sq) for sq in range(nb)])

    @pl.when(ci == n_chunks - 1)
    def _():
        for sq in range(nb):
            st_ref[sq] = _extract_blocks(s_scr[sq].T, GROUP_WIDTH, HEAD_DIM)


def _hgrn_prompt(proj3, lb_logits, norm_w, layer, r):
    b, l, _ = proj3.shape
    n = l // r
    fixed = lambda bi, ci: (0, 0)
    nb = math.gcd(b, PROMPT_SEQS_PER_STEP)
    return pl.pallas_call(
        functools.partial(_hgrn_prompt_kernel, r=r, n_chunks=n, layer=layer, nb=nb),
        grid=(b // nb, n),
        in_specs=[pl.BlockSpec((nb, r, 1024), lambda bi, ci: (bi, ci, COL_HGRN // 1024)),
                  pl.BlockSpec((DEPTH, GROUP_WIDTH), fixed),
                  pl.BlockSpec((1, GROUP_WIDTH), fixed)],
        out_specs=[pl.BlockSpec((nb, r, GROUP_WIDTH), lambda bi, ci: (bi, ci, 0)),
                   pl.BlockSpec((nb, GROUP_WIDTH, HEAD_DIM), lambda bi, ci: (bi, 0, 0))],
        out_shape=[jax.ShapeDtypeStruct((b, l, GROUP_WIDTH), F32),
                   jax.ShapeDtypeStruct((b, GROUP_WIDTH, HEAD_DIM), F32)],
        scratch_shapes=[pltpu.VMEM((nb, GROUP_WIDTH, GROUP_WIDTH), F32)],
        compiler_params=pltpu.CompilerParams(dimension_semantics=("arbitrary", "arbitrary"),
                                             vmem_limit_bytes=VMEM_LIMIT),
        name="hgrn_prompt",
    )(proj3, lb_logits.astype(F32), jnp.tile(norm_w.astype(F32), N_HEADS).reshape(1, GROUP_WIDTH))


DEC_SEQS = 128
DEC_LEN = 4


def _head_rows(h):
    return pl.ds(pl.multiple_of(h * HEAD_DIM, HEAD_DIM), HEAD_DIM)


def _recur_head(load_s, store_s, n_keys, decay_fn, k_fn, q_fn, v_blocks):
    def body(kk, accs):
        s = load_s(kk)
        accs = list(accs)
        for t in range(DEC_LEN):
            s = decay_fn(t, kk) * s + k_fn(t, kk) * v_blocks[t]
            accs[t] = accs[t] + q_fn(t, kk) * s
        store_s(kk, s)
        return tuple(accs)

    zero = jnp.zeros((HEAD_DIM, DEC_SEQS), F32)
    return lax.fori_loop(0, n_keys, body, (zero,) * DEC_LEN)


def _dec_ret_kernel(blk_ref, cos_ref, sin_ref, st_ref, o_ref, so_ref, q_scr, k_scr, v_scr, o_scr):
    h = pl.program_id(0)

    @pl.when(h == 0)
    def _():
        for t in range(DEC_LEN):
            blk = blk_ref[t]
            rq, rk, rv = blk[:, 0:256], blk[:, 256:512], blk[:, 512:768]
            cosv, sinv = cos_ref[t:t + 1, :], sin_ref[t:t + 1, :]
            q_scr[t] = (rq * cosv + _swap_halves(rq) * sinv).T
            k_scr[t] = ((rk * cosv + _swap_halves(rk) * sinv) * (HEAD_DIM ** -0.5)).T
            v_scr[t] = rv.T

    lg = jnp.where(h == 0, LOG_GAMMA[0], jnp.where(h == 1, LOG_GAMMA[1], jnp.where(h == 2, LOG_GAMMA[2], LOG_GAMMA[3])))
    gamma = jnp.exp(jnp.full((1, DEC_SEQS), lg, F32))
    hr = _head_rows(h)
    v_blocks = [v_scr[t, hr, :] for t in range(DEC_LEN)]
    accs = _recur_head(
        lambda kk: st_ref[0, 0, kk], functools.partial(_store_state, so_ref), HEAD_DIM,
        lambda t, kk: gamma,
        lambda t, kk: k_scr[t, pl.ds(h * HEAD_DIM + kk, 1), :],
        lambda t, kk: q_scr[t, pl.ds(h * HEAD_DIM + kk, 1), :],
        v_blocks)
    for t in range(DEC_LEN):
        o_scr[t, hr, :] = accs[t]

    @pl.when(h == N_HEADS - 1)
    def _():
        ones_bd = _block_ones(GROUP_WIDTH, HEAD_DIM)
        for t in range(DEC_LEN):
            o = o_scr[t].T
            ss = _head_sumsq(o, ones_bd)
            o_ref[t] = o * lax.rsqrt(ss * (1.0 / HEAD_DIM) + EPS) * _silu(blk_ref[t][:, 768:1024])


def _store_state(so_ref, kk, s):
    so_ref[0, 0, kk] = s


def _without_ref(kernel_fn, idx):
    def wrapped(*refs):
        return kernel_fn(*refs[:idx], *refs[idx + 1:])
    return wrapped


def _dec_call(kernel_fn, name, col, ins, in_specs, n_tok_scr, state_view, layer, carried, extra_scratch=()):
    blk_spec = pl.BlockSpec((DEC_LEN, DEC_SEQS, 1024), lambda h: (0, 0, col // 1024))
    st_spec = pl.BlockSpec((1, 1) + state_view.shape[2:], lambda h: (layer, h, 0, 0, 0))
    tok_scr = pltpu.VMEM((DEC_LEN, GROUP_WIDTH, DEC_SEQS), F32)
    ins = tuple(ins) + (state_view,)
    specs = [blk_spec] + in_specs + [st_spec]
    aliases = {}
    if carried is not None:
        kernel_fn = _without_ref(kernel_fn, len(ins))
        aliases = {len(ins): 1}
        ins = ins + (carried,)
        specs = specs + [pl.BlockSpec(memory_space=pl.ANY)]
    return pl.pallas_call(
        kernel_fn,
        grid=(N_HEADS,),
        in_specs=specs,
        out_specs=[pl.BlockSpec((DEC_LEN, DEC_SEQS, GROUP_WIDTH), lambda h: (0, 0, 0)), st_spec],
        out_shape=[jax.ShapeDtypeStruct((DEC_LEN, DEC_SEQS, GROUP_WIDTH), F32),
                   jax.ShapeDtypeStruct(state_view.shape, F32)],
        scratch_shapes=[tok_scr] * n_tok_scr + list(extra_scratch),
        input_output_aliases=aliases,
        compiler_params=pltpu.CompilerParams(dimension_semantics=("arbitrary",), vmem_limit_bytes=VMEM_LIMIT),
        name=name,
    )(*ins)


def _fixed1(shape):
    return pl.BlockSpec(shape, lambda h: (0,) * len(shape))


def _dec_ret(projd, cos_t, sin_t, state_view, layer, carried):
    return _dec_call(_dec_ret_kernel, "retention_decode", COL_RET, (projd, cos_t, sin_t),
                     [_fixed1((DEC_LEN, GROUP_WIDTH)), _fixed1((DEC_LEN, GROUP_WIDTH))], 4,
                     state_view, layer, carried)


def _dec_hgrn_kernel(blk_ref, lb_ref, nw_ref, st_ref, o_ref, so_ref, q_scr, k_scr, v_scr, f_scr, o_scr, *, layer):
    h = pl.program_id(0)

    @pl.when(h == 0)
    def _():
        lb = _hgrn_lower_bound(lb_ref[...], layer)
        for t in range(DEC_LEN):
            blk = blk_ref[t]
            f = lb + (1.0 - lb) * _sigmoid(blk[:, 256:512])
            q_scr[t] = _sigmoid(blk[:, 0:256]).T
            k_scr[t] = (1.0 - f).T
            v_scr[t] = blk[:, 512:768].T
            f_scr[t] = f.T

    hr = _head_rows(h)
    v_blocks = [v_scr[t, hr, :] for t in range(DEC_LEN)]
    row = lambda scr: (lambda t, kk: scr[t, pl.ds(h * HEAD_DIM + kk, 1), :])
    accs = _recur_head(lambda kk: st_ref[0, 0, kk], functools.partial(_store_state, so_ref), HEAD_DIM,
                       row(f_scr), row(k_scr), row(q_scr), v_blocks)
    for t in range(DEC_LEN):
        o_scr[t, hr, :] = accs[t]

    @pl.when(h == N_HEADS - 1)
    def _():
        ones_bd = _block_ones(GROUP_WIDTH, HEAD_DIM)
        for t in range(DEC_LEN):
            o = o_scr[t].T
            ss = _head_sumsq(o, ones_bd)
            o_ref[t] = o * lax.rsqrt(ss * (1.0 / HEAD_DIM) + EPS) * nw_ref[...] * _silu(blk_ref[t][:, 768:1024])


def _dec_hgrn(projd, lb_logits, norm_w, state_view, layer, carried):
    return _dec_call(functools.partial(_dec_hgrn_kernel, layer=layer), "hgrn_decode", COL_HGRN,
                     (projd, lb_logits.astype(F32), jnp.tile(norm_w.astype(F32), N_HEADS).reshape(1, GROUP_WIDTH)),
                     [_fixed1((DEPTH, GROUP_WIDTH)), _fixed1((1, GROUP_WIDTH))], 5, state_view, layer, carried)


def _hist_spec(layer):
    return pl.BlockSpec((1, CONV_WIDTH - 1, DEC_SEQS, 768), lambda h: (layer, 0, 0, 0))


def _dec_conv_silu(hist_ref, xs, w, bias):
    xe = [hist_ref[0, j] for j in range(CONV_WIDTH - 1)] + xs
    out = []
    for t in range(DEC_LEN):
        y = xe[t] * w[0:1, :]
        for j in range(1, CONV_WIDTH):
            y = y + xe[t + j] * w[j:j + 1, :]
        if bias is not None:
            y = y + bias
        out.append(_silu(y))
    return out


def _dec_ssd_kernel(blk_ref, small_ref, hist_ref, cw_ref, cb_ref, dtb_ref, alog_ref, dskip_ref, nw_ref, st_ref,
                    o_ref, so_ref, c_scr, b_scr, v_scr, a_scr, o_scr, x_scr):
    h = pl.program_id(0)

    @pl.when(h == 0)
    def _():
        xbc = _dec_conv_silu(hist_ref, [blk_ref[t][:, 256:1024] for t in range(DEC_LEN)], cw_ref[...], cb_ref[...])
        for t in range(DEC_LEN):
            xs = xbc[t][:, 0:256]
            dt = _softplus(_expand_small(small_ref[t], SMALL_SDT) + dtb_ref[...])
            x_scr[t] = xs
            v_scr[t] = (xs * dt).T
            b_scr[t] = xbc[t][:, 256:512].T
            c_scr[t] = xbc[t][:, 512:768].T
            a_scr[t] = jnp.exp(-jnp.exp(alog_ref[...]) * dt).T

    hr = _head_rows(h)
    g0 = (h // 2) * SSD_STATE
    v_blocks = [v_scr[t, hr, :] for t in range(DEC_LEN)]
    accs = _recur_head(
        lambda kk: st_ref[0, 0, kk], functools.partial(_store_state, so_ref), SSD_STATE,
        lambda t, kk: a_scr[t, pl.ds(h * HEAD_DIM, 1), :],
        lambda t, kk: b_scr[t, pl.ds(g0 + kk, 1), :],
        lambda t, kk: c_scr[t, pl.ds(g0 + kk, 1), :],
        v_blocks)
    for t in range(DEC_LEN):
        o_scr[t, hr, :] = accs[t]

    @pl.when(h == N_HEADS - 1)
    def _():
        for t in range(DEC_LEN):
            y = (o_scr[t].T + dskip_ref[...] * x_scr[t]) * _silu(blk_ref[t][:, 0:256])
            halves = [_rms_rows(y[:, gi * 128:(gi + 1) * 128]) for gi in range(2)]
            o_ref[t] = jnp.concatenate(halves, axis=1) * nw_ref[...]


def _dec_ssd(projd, hist, conv_w, conv_b, dt_bias, a_log, d_skip, norm_w, state_view, layer, carried):
    small_spec = pl.BlockSpec((DEC_LEN, DEC_SEQS, 128), lambda h: (0, 0, COL_SMALL // 128))
    return _dec_call(_dec_ssd_kernel, "ssd_decode", COL_SSD,
                     (projd, projd, hist, conv_w, conv_b.reshape(1, 768), _lane_rep(dt_bias), _lane_rep(a_log),
                      _lane_rep(d_skip), norm_w.reshape(1, GROUP_WIDTH)),
                     [small_spec, _hist_spec(layer), _fixed1((CONV_WIDTH, 768)),
                      _fixed1((1, 768))] + [_fixed1((1, GROUP_WIDTH))] * 4, 5, state_view, layer, carried,
                     extra_scratch=[pltpu.VMEM((DEC_LEN, DEC_SEQS, GROUP_WIDTH), F32)])


def _dec_gdn_kernel(blk_ref, small_ref, hist_ref, cw_ref, alog_ref, dtb_ref, nw_ref, st_ref,
                    o_ref, so_ref, q_scr, k_scr, v_scr, a_scr, b_scr, o_scr):
    h = pl.program_id(0)

    @pl.when(h == 0)
    def _():
        ones_bd = _block_ones(GROUP_WIDTH, HEAD_DIM)
        qkv = _dec_conv_silu(hist_ref, [blk_ref[t][:, 0:768] for t in range(DEC_LEN)], cw_ref[...], None)
        for t in range(DEC_LEN):
            gq, gk, gv = qkv[t][:, 0:256], qkv[t][:, 256:512], qkv[t][:, 512:768]
            q_scr[t] = (gq * lax.rsqrt(_head_sumsq(gq, ones_bd) + EPS) * (HEAD_DIM ** -0.5)).T
            k_scr[t] = (gk * lax.rsqrt(_head_sumsq(gk, ones_bd) + EPS)).T
            v_scr[t] = gv.T
            small = small_ref[t]
            b_scr[t] = _sigmoid(_expand_small(small, SMALL_GB)).T
            la = -jnp.exp(alog_ref[...]) * _softplus(_expand_small(small, SMALL_GA) + dtb_ref[...])
            a_scr[t] = jnp.exp(la).T

    hr = _head_rows(h)
    one_row = pl.ds(h * HEAD_DIM, 1)
    zero = jnp.zeros((HEAD_DIM, DEC_SEQS), F32)
    for t in range(DEC_LEN):
        a = a_scr[t, one_row, :]
        cur = st_ref if t == 0 else so_ref

        def kts(kk, r):
            return r + k_scr[t, pl.ds(h * HEAD_DIM + kk, 1), :] * cur[0, 0, kk]

        r = lax.fori_loop(0, HEAD_DIM, kts, zero)
        u = b_scr[t, one_row, :] * (v_scr[t, hr, :] - a * r)

        def upd(kk, acc):
            s = a * cur[0, 0, kk] + k_scr[t, pl.ds(h * HEAD_DIM + kk, 1), :] * u
            so_ref[0, 0, kk] = s
            return acc + q_scr[t, pl.ds(h * HEAD_DIM + kk, 1), :] * s

        o_scr[t, hr, :] = lax.fori_loop(0, HEAD_DIM, upd, zero)

    @pl.when(h == N_HEADS - 1)
    def _():
        ones_bd = _block_ones(GROUP_WIDTH, HEAD_DIM)
        for t in range(DEC_LEN):
            o = o_scr[t].T
            ss = _head_sumsq(o, ones_bd)
            o_ref[t] = o * lax.rsqrt(ss * (1.0 / HEAD_DIM) + EPS) * nw_ref[...] * _silu(blk_ref[t][:, 768:1024])


def _dec_gdn(projd, hist, conv_w, a_log, dt_bias, norm_w, state_view, layer, carried):
    small_spec = pl.BlockSpec((DEC_LEN, DEC_SEQS, 128), lambda h: (0, 0, COL_SMALL // 128))
    return _dec_call(_dec_gdn_kernel, "gdn_decode", COL_GDN,
                     (projd, projd, hist, conv_w, _lane_rep(a_log), _lane_rep(dt_bias),
                      jnp.tile(norm_w.astype(F32), N_HEADS).reshape(1, GROUP_WIDTH)),
                     [small_spec, _hist_spec(layer), _fixed1((CONV_WIDTH, 768))]
                     + [_fixed1((1, GROUP_WIDTH))] * 3, 6, state_view, layer, carried)


def _reorder_cols(w):
    lead = w.shape[:-1]
    small = jnp.concatenate([w[..., 2048:2056], w[..., 4104:4108],
                             jnp.zeros(lead + (P_PAD - COL_SMALL - 12,), w.dtype)], axis=-1)
    return jnp.concatenate([w[..., 0:2048], w[..., 2056:4104], small], axis=-1)


def _prep_w_in(w):
    return _reorder_cols(w).astype(BF16)


def _w_in_prep_kernel(w_ref, tail_ref, o_ref):
    x = w_ref[0]
    o_ref[0, :, 0:COL_RET] = x[:, 0:COL_RET].astype(BF16)
    o_ref[0, :, COL_RET:COL_SMALL] = x[:, COL_RET + 8:COL_SMALL + 8].astype(BF16)
    lane = _iota((x.shape[0], 128), 1)
    small = jnp.where(lane < 8, x[:, COL_RET:COL_RET + 128], jnp.where(lane < 12, tail_ref[0], 0.0))
    o_ref[0, :, COL_SMALL:P_PAD] = small.astype(BF16)


def _prep_w_in_all(w_in):
    d, r, p = w_in.shape
    rows = 256
    return pl.pallas_call(
        _w_in_prep_kernel,
        grid=(d, r // rows),
        in_specs=[pl.BlockSpec((1, rows, p), lambda l, i: (l, i, 0)),
                  pl.BlockSpec((1, rows, 128), lambda l, i: (l, i, COL_SMALL // 128))],
        out_specs=pl.BlockSpec((1, rows, P_PAD), lambda l, i: (l, i, 0)),
        out_shape=jax.ShapeDtypeStruct((d, r, P_PAD), BF16),
        compiler_params=pltpu.CompilerParams(dimension_semantics=("arbitrary", "arbitrary"),
                                             vmem_limit_bytes=VMEM_LIMIT),
        name="w_in_prep",
    )(w_in, w_in)


def _rotary_tables(pos):
    half = HEAD_DIM // 2
    inv_freq = RET_THETA ** (-jnp.arange(half, dtype=F32) / half)
    ang = pos.astype(F32)[:, None] * inv_freq[None, :]
    cos, sin = jnp.cos(ang), jnp.sin(ang)
    cos_t = jnp.tile(cos, (1, 2 * N_HEADS))
    sin_t = jnp.tile(jnp.concatenate([-sin, sin], axis=1), (1, N_HEADS))
    return cos_t, sin_t


RET_CHUNK = 128
SSD_CHUNK = 128
GDN_CHUNK = 64
HGRN_ROWS = 128


def _forward(x_prompt, x_sample, states, p, past_len):
    st_hg, st_gd, st_gc, st_rt, st_sd, st_sc = states
    bp, lp, _ = x_prompt.shape
    nd, ld, _ = x_sample.shape
    xp = x_prompt.astype(F32).reshape(bp * lp, D_MODEL)
    xd = jnp.transpose(x_sample.astype(F32), (1, 0, 2)).reshape(ld * nd, D_MODEL)
    cos_p, sin_p = _rotary_tables(jnp.arange(lp))
    cos_d, sin_d = _rotary_tables(past_len + jnp.arange(ld))
    outs = {k: [] for k in ("hp", "gp", "gcp", "gcs", "rp", "sp", "scp", "scs")}
    w_in_all = _prep_w_in_all(p["w_in"].astype(F32))
    wo, wu, wd = (p[k].astype(BF16) for k in ("w_out", "w_up", "w_down"))
    norm_mix = p["norm_mix"].astype(F32).reshape(DEPTH, 1, D_MODEL)
    norm_ffn = p["norm_ffn"].astype(F32).reshape(DEPTH, 1, D_MODEL)
    sv_hg, sv_gd, sv_rt, sv_sd = (jnp.transpose(s.astype(F32), (0, 2, 3, 4, 1)) for s in (st_hg, st_gd, st_rt, st_sd))
    hist_g = jnp.transpose(st_gc.astype(F32), (0, 2, 1, 3))
    hist_s = jnp.transpose(st_sc.astype(F32), (0, 2, 1, 3))
    new_hg = new_gd = new_rt = new_sd = None
    for l in range(DEPTH):
        pp = _proj(xp, norm_mix, w_in_all, l).reshape(bp, lp, P_PAD)
        pd = _proj(xd, norm_mix, w_in_all, l).reshape(ld, nd, P_PAD)

        oa, sa = _hgrn_prompt(pp, p["hgrn_lb_logits"], p["hgrn_norm"][l], l, HGRN_ROWS)
        ob, sb = _gdn_prompt(pp, p["gdn_conv_w"][l], p["gdn_a_log"][l], p["gdn_dt_bias"][l], p["gdn_norm"][l],
                             GDN_CHUNK)
        oc, sc = _ret_prompt(pp, cos_p, sin_p, RET_CHUNK)
        od, sd = _ssd_prompt(pp, p["ssd_conv_w"][l], p["ssd_conv_b"][l], p["ssd_dt_bias"][l], p["ssd_a_log"][l],
                             p["ssd_d"][l], p["ssd_norm"][l], SSD_CHUNK)
        outs["hp"].append(sa.reshape(bp, N_HEADS, HEAD_DIM, HEAD_DIM))
        outs["gp"].append(sb.reshape(bp, N_HEADS, HEAD_DIM, HEAD_DIM))
        outs["rp"].append(sc.reshape(bp, N_HEADS, HEAD_DIM, HEAD_DIM))
        outs["sp"].append(sd.reshape(bp, N_HEADS, SSD_STATE, HEAD_DIM))
        outs["gcp"].append(pp[:, lp - 3:, COL_GDN:COL_GDN + 768])
        outs["scp"].append(pp[:, lp - 3:, COL_SSD + 256:COL_SSD + 1024])
        xp = _out_ffn(xp, [o.reshape(bp * lp, GROUP_WIDTH) for o in (oa, ob, oc, od)], wo, norm_ffn, wu, wd,
                      p["norm_final"], l)

        da, new_hg = _dec_hgrn(pd, p["hgrn_lb_logits"], p["hgrn_norm"][l], sv_hg, l, new_hg)
        db, new_gd = _dec_gdn(pd, hist_g, p["gdn_conv_w"][l], p["gdn_a_log"][l], p["gdn_dt_bias"][l],
                              p["gdn_norm"][l], sv_gd, l, new_gd)
        dc, new_rt = _dec_ret(pd, cos_d, sin_d, sv_rt, l, new_rt)
        dd, new_sd = _dec_ssd(pd, hist_s, p["ssd_conv_w"][l], p["ssd_conv_b"][l], p["ssd_dt_bias"][l],
                              p["ssd_a_log"][l], p["ssd_d"][l], p["ssd_norm"][l], sv_sd, l, new_sd)
        outs["gcs"].append(jnp.transpose(pd[ld - 3:, :, COL_GDN:COL_GDN + 768], (1, 0, 2)))
        outs["scs"].append(jnp.transpose(pd[ld - 3:, :, COL_SSD + 256:COL_SSD + 1024], (1, 0, 2)))
        xd = _out_ffn(xd, [o.reshape(ld * nd, GROUP_WIDTH) for o in (da, db, dc, dd)], wo, norm_ffn, wu, wd,
                      p["norm_final"], l)

    y_prompt = xp.reshape(bp, lp, D_MODEL)
    y_sample = jnp.transpose(xd.reshape(ld, nd, D_MODEL), (1, 0, 2))
    st = {k: jnp.stack(v) for k, v in outs.items()}
    hs, gs, rs, ss = (jnp.transpose(s, (0, 4, 1, 2, 3)) for s in (new_hg, new_gd, new_rt, new_sd))
    return (y_prompt, y_sample, st["hp"], hs, st["gp"], gs, st["gcp"], st["gcs"],
            st["rp"], rs, st["sp"], ss, st["scp"], st["scs"])


def kernel(x_prompt, x_sample, state_hgrn, state_gdn, state_gdn_conv, state_ret, state_ssd, state_ssd_conv,
           norm_mix, w_in, hgrn_lb_logits, hgrn_norm, gdn_conv_w, gdn_a_log, gdn_dt_bias, gdn_norm,
           ssd_conv_w, ssd_conv_b, ssd_dt_bias, ssd_a_log, ssd_d, ssd_norm,
           w_out, norm_ffn, w_up, w_down, norm_final):
    params = dict(norm_mix=norm_mix, w_in=w_in, hgrn_lb_logits=hgrn_lb_logits, hgrn_norm=hgrn_norm,
                  gdn_conv_w=gdn_conv_w, gdn_a_log=gdn_a_log, gdn_dt_bias=gdn_dt_bias, gdn_norm=gdn_norm,
                  ssd_conv_w=ssd_conv_w, ssd_conv_b=ssd_conv_b, ssd_dt_bias=ssd_dt_bias, ssd_a_log=ssd_a_log,
                  ssd_d=ssd_d, ssd_norm=ssd_norm, w_out=w_out, norm_ffn=norm_ffn, w_up=w_up,
                  w_down=w_down, norm_final=norm_final)
    states = (state_hgrn, state_gdn, state_gdn_conv, state_ret, state_ssd, state_ssd_conv)
    return _forward(x_prompt, x_sample, states, params, 16384)
```

```python
import functools
import math

import numpy as np
import jax
import jax.numpy as jnp
from jax import lax
from jax.experimental import pallas as pl
from jax.experimental.pallas import tpu as pltpu

F32 = jnp.float32
BF16 = jnp.bfloat16

D_MODEL = 1024
GROUP_WIDTH = 256
HEAD_DIM = 64
N_HEADS = 4
CONV_WIDTH = 4
SSD_STATE = 128
D_FF = 4096
RET_THETA = 10000.0
EPS = 1e-6
DEPTH = 2

COL_HGRN = 0
COL_GDN = 1024
COL_RET = 2048
COL_SSD = 3072
COL_SMALL = 4096
P_PAD = 4224
SMALL_GA, SMALL_GB, SMALL_SDT = 0, 4, 8

VMEM_LIMIT = 56 * 1024 * 1024
LOG_GAMMA = [math.log(1.0 - 2.0 ** (-5.0 - h)) for h in range(N_HEADS)]


def _dot(a, b):
    return jnp.dot(a, b, preferred_element_type=F32)


def _dot_nt(a, b):
    return lax.dot_general(a, b, (((1,), (1,)), ((), ())), preferred_element_type=F32)


def _dot_tn(a, b):
    return lax.dot_general(a, b, (((0,), (0,)), ((), ())), preferred_element_type=F32)


def _round_robin(gens):
    live = list(gens)
    while live:
        nxt = []
        for g in live:
            try:
                next(g)
                nxt.append(g)
            except StopIteration:
                pass
        live = nxt


def _split3(x):
    hi = x.astype(BF16)
    r1 = x - hi.astype(F32)
    mid = r1.astype(BF16)
    lo = (r1 - mid.astype(F32)).astype(BF16)
    return hi, mid, lo


def _exact_dot(x, sel):
    hi, mid, lo = _split3(x)
    return _dot(hi, sel) + _dot(mid, sel) + _dot(lo, sel)


def _exact_dot_left(sel, x):
    hi, mid, lo = _split3(x)
    return _dot(sel, hi) + _dot(sel, mid) + _dot(sel, lo)


def _iota(shape, dim):
    return lax.broadcasted_iota(jnp.int32, shape, dim)


def _head_of_lane(n_lanes, width=HEAD_DIM):
    return _iota((1, n_lanes), 1) // width


def _head_masks(n_lanes=GROUP_WIDTH, width=HEAD_DIM):
    hl = _head_of_lane(n_lanes, width)
    return [hl == h for h in range(n_lanes // width)]


def _stack_heads(x, masks):
    return jnp.concatenate([jnp.where(m, x, jnp.zeros_like(x)) for m in masks], axis=0)


def _unstack_heads(y, masks, c):
    out = jnp.where(masks[0], y[0:c], 0.0)
    for h in range(1, len(masks)):
        out = out + jnp.where(masks[h], y[h * c:(h + 1) * c], 0.0)
    return out


def _block_ones(n, width, dtype=BF16):
    r = _iota((n, n), 0) // width
    c = _iota((n, n), 1) // width
    return (r == c).astype(dtype)


def _block_mask(n, rwidth, cwidth):
    return (_iota((n, n), 0) // rwidth) == (_iota((n, n), 1) // cwidth)


def _lower_tri(c, dtype=BF16):
    return (_iota((c, c), 0) >= _iota((c, c), 1)).astype(dtype)


def _cumsum_rows(x, c):
    return _exact_dot_left(_lower_tri(c), x)


def _sigmoid(x):
    return 1.0 / (1.0 + jnp.exp(-x))


def _silu(x):
    return x * _sigmoid(x)


def _softplus(x):
    return jnp.maximum(x, 0.0) + jnp.log(1.0 + jnp.exp(-jnp.abs(x)))


def _rms_rows(x):
    return x * lax.rsqrt(jnp.mean(x * x, axis=-1, keepdims=True) + EPS)


def _head_sumsq(x, ones_bd):
    sq = x * x
    hi = sq.astype(BF16)
    lo = (sq - hi.astype(F32)).astype(BF16)
    return _dot(hi, ones_bd) + _dot(lo, ones_bd)


def _expand_small(small, first_lane):
    r = _iota((128, GROUP_WIDTH), 0)
    c = _iota((128, GROUP_WIDTH), 1) // HEAD_DIM
    sel = (r == c + first_lane).astype(BF16)
    return _exact_dot(small, sel)


def _decay_diff_operands(g):
    hi, mid, lo = (x.astype(F32) for x in _split3(g))
    pos = _iota(g.shape, 1) % HEAD_DIM
    a = jnp.where(pos == 0, hi, jnp.where(pos == 1, mid, jnp.where(pos == 2, lo,
                  jnp.where(pos < 6, 1.0, 0.0))))
    b = jnp.where(pos < 3, 1.0, jnp.where(pos == 3, -hi, jnp.where(pos == 4, -mid,
                  jnp.where(pos == 5, -lo, 0.0))))
    return a, b


def _extract_blocks(s_wide, rows, width):
    sel = ((_iota((GROUP_WIDTH, width), 0) % width) == _iota((GROUP_WIDTH, width), 1)).astype(BF16)
    return _exact_dot(s_wide, sel)


def _proj_kernel(x_ref, nw_ref, w_ref, o_ref):
    h = _rms_rows(x_ref[...]) * nw_ref[0]
    o_ref[...] = _dot(h.astype(BF16), w_ref[0])


def _proj(x2d, norm_w, w_bf16, layer):
    t = x2d.shape[0]
    tm = min(t, 512)
    return pl.pallas_call(
        _proj_kernel,
        grid=(t // tm,),
        in_specs=[pl.BlockSpec((tm, D_MODEL), lambda i: (i, 0)),
                  pl.BlockSpec((1, 1, D_MODEL), lambda i: (layer, 0, 0)),
                  pl.BlockSpec((1, D_MODEL, P_PAD), lambda i: (layer, 0, 0))],
        out_specs=pl.BlockSpec((tm, P_PAD), lambda i: (i, 0)),
        out_shape=jax.ShapeDtypeStruct((t, P_PAD), F32),
        compiler_params=pltpu.CompilerParams(dimension_semantics=("arbitrary",),
                                             vmem_limit_bytes=VMEM_LIMIT),
        name="norm_in_proj",
    )(x2d, norm_w, w_bf16)


def _ffn_kernel(x_ref, oa_ref, ob_ref, oc_ref, od_ref, wo_ref, nf_ref, wu_ref, wd_ref, nfin_ref,
                o_ref, *, final):
    mix = jnp.concatenate([oa_ref[...], ob_ref[...], oc_ref[...], od_ref[...]], axis=1)
    x = x_ref[...] + _dot(mix.astype(BF16), wo_ref[0])
    h = (_rms_rows(x) * nf_ref[0]).astype(BF16)
    acc = x
    ft = 1024
    for t in range(D_FF // ft):
        up = _dot(h, wu_ref[0, :, t * ft:(t + 1) * ft])
        up = jnp.square(jnp.maximum(up, 0.0)).astype(BF16)
        acc = acc + _dot(up, wd_ref[0, t * ft:(t + 1) * ft, :])
    if final:
        acc = _rms_rows(acc) * nfin_ref[...]
    o_ref[...] = acc


def _out_ffn(x2d, mixes, wo, nf, wu, wd, nfin, layer):
    t = x2d.shape[0]
    tm = min(t, 512)
    row = lambda i: (i, 0)
    lay = lambda i: (layer, 0, 0)
    return pl.pallas_call(
        functools.partial(_ffn_kernel, final=(layer == DEPTH - 1)),
        grid=(t // tm,),
        in_specs=[pl.BlockSpec((tm, D_MODEL), row)]
                 + [pl.BlockSpec((tm, GROUP_WIDTH), row)] * 4
                 + [pl.BlockSpec((1, D_MODEL, D_MODEL), lay),
                    pl.BlockSpec((1, 1, D_MODEL), lay),
                    pl.BlockSpec((1, D_MODEL, D_FF), lay),
                    pl.BlockSpec((1, D_FF, D_MODEL), lay),
                    pl.BlockSpec((1, D_MODEL), lambda i: (0, 0))],
        out_specs=pl.BlockSpec((tm, D_MODEL), row),
        out_shape=jax.ShapeDtypeStruct((t, D_MODEL), F32),
        compiler_params=pltpu.CompilerParams(dimension_semantics=("arbitrary",),
                                             vmem_limit_bytes=VMEM_LIMIT),
        name="out_proj_ffn",
    )(x2d, *mixes, wo, nf, wu, wd, nfin.reshape(1, D_MODEL))


def _swap_halves(x):
    first = (_iota((1, 128), 1) % HEAD_DIM) < (HEAD_DIM // 2)
    parts = []
    for p in range(GROUP_WIDTH // 128):
        xp = x[:, p * 128:(p + 1) * 128]
        parts.append(jnp.where(first, pltpu.roll(xp, 96, 1), pltpu.roll(xp, 32, 1)))
    return jnp.concatenate(parts, axis=1)


def _conv_silu(xe_ref, halo, x, w, bias, first_chunk, c):
    xe_ref[0:8, :] = jnp.where(first_chunk, jnp.zeros_like(halo), halo)
    xe_ref[8:, :] = x
    y = w[3:4, :] * x
    for j in range(CONV_WIDTH - 1):
        y = y + w[j:j + 1, :] * xe_ref[5 + j:5 + j + c, :]
    if bias is not None:
        y = y + bias
    return _silu(y)


def _ret_prompt_kernel(blk_ref, cos_ref, sin_ref, o_ref, st_ref, s_scr, *, c, n_chunks, nb):
    ci = pl.program_id(1)

    @pl.when(ci == 0)
    def _():
        s_scr[...] = jnp.zeros_like(s_scr)

    cosv, sinv = cos_ref[...], sin_ref[...]
    masks = _head_masks()
    hl = _head_of_lane(GROUP_WIDTH)
    lg = jnp.full((1, GROUP_WIDTH), LOG_GAMMA[0], F32)
    for h in range(1, N_HEADS):
        lg = jnp.where(hl == h, LOG_GAMMA[h], lg)
    ri = _iota((c, 1), 0).astype(F32)
    dij = (_iota((c, c), 0) - _iota((c, c), 1)).astype(F32)
    causal = dij >= 0.0
    decay = jnp.concatenate(
        [jnp.where(causal, jnp.exp(jnp.maximum(dij, 0.0) * LOG_GAMMA[h]), 0.0) for h in range(N_HEADS)],
        axis=0)
    q_scale = jnp.exp((ri + 1.0) * lg)
    k_scale = jnp.exp((float(c - 1) - ri) * lg) * (HEAD_DIM ** -0.5)
    s_scale = jnp.exp(float(c) * lg)
    bd_mask = _block_mask(GROUP_WIDTH, HEAD_DIM, HEAD_DIM)
    ones_bd = _block_ones(GROUP_WIDTH, HEAD_DIM)

    def one_sequence(sq):
        blk = blk_ref[sq]
        rq, rk, rv, rg = (blk[:, i * GROUP_WIDTH:(i + 1) * GROUP_WIDTH] for i in range(4))
        q = rq * cosv + _swap_halves(rq) * sinv
        k = rk * cosv + _swap_halves(rk) * sinv
        v = rv.astype(BF16)
        s = s_scr[sq]
        qk = _dot_nt(_stack_heads(q, masks).astype(BF16), k.astype(BF16))
        o_inter = _dot((q * q_scale).astype(BF16), s.astype(BF16))
        ds = _dot_tn((k * k_scale).astype(BF16), v)
        yield
        scores = qk * (decay * (HEAD_DIM ** -0.5))
        pv = _dot(scores.astype(BF16), v)
        s_scr[sq] = s_scale * s + jnp.where(bd_mask, ds, 0.0)
        yield
        o = _unstack_heads(pv, masks, c) + o_inter
        ss = _head_sumsq(o, ones_bd)
        yield
        o_ref[sq] = o * lax.rsqrt(ss * (1.0 / HEAD_DIM) + EPS) * _silu(rg)

    _round_robin([one_sequence(sq) for sq in range(nb)])

    @pl.when(ci == n_chunks - 1)
    def _():
        for sq in range(nb):
            st_ref[sq] = _extract_blocks(s_scr[sq], GROUP_WIDTH, HEAD_DIM)


PROMPT_SEQS_PER_STEP = 8
GDN_SEQS_PER_STEP = 8


def _ret_prompt(proj3, cos_t, sin_t, c):
    b, l, _ = proj3.shape
    n = l // c
    nb = math.gcd(b, PROMPT_SEQS_PER_STEP)
    return pl.pallas_call(
        functools.partial(_ret_prompt_kernel, c=c, n_chunks=n, nb=nb),
        grid=(b // nb, n),
        in_specs=[pl.BlockSpec((nb, c, 1024), lambda bi, ci: (bi, ci, COL_RET // 1024)),
                  pl.BlockSpec((c, GROUP_WIDTH), lambda bi, ci: (ci, 0)),
                  pl.BlockSpec((c, GROUP_WIDTH), lambda bi, ci: (ci, 0))],
        out_specs=[pl.BlockSpec((nb, c, GROUP_WIDTH), lambda bi, ci: (bi, ci, 0)),
                   pl.BlockSpec((nb, GROUP_WIDTH, HEAD_DIM), lambda bi, ci: (bi, 0, 0))],
        out_shape=[jax.ShapeDtypeStruct((b, l, GROUP_WIDTH), F32),
                   jax.ShapeDtypeStruct((b, GROUP_WIDTH, HEAD_DIM), F32)],
        scratch_shapes=[pltpu.VMEM((nb, GROUP_WIDTH, GROUP_WIDTH), F32)],
        compiler_params=pltpu.CompilerParams(dimension_semantics=("arbitrary", "arbitrary"),
                                             vmem_limit_bytes=VMEM_LIMIT),
        name="retention_prompt",
    )(proj3, cos_t, sin_t)


def _ssd_prompt_kernel(blk_ref, halo_ref, small_ref, cw_ref, cb_ref, dtb_ref, alog_ref, dskip_ref, nw_ref,
                       o_ref, st_ref, s_scr, xe_scr, *, c, n_chunks, nb):
    ci = pl.program_id(1)

    @pl.when(ci == 0)
    def _():
        s_scr[...] = jnp.zeros_like(s_scr)

    masks = _head_masks()
    causal = _iota((c, c), 0) >= _iota((c, c), 1)
    causal4 = jnp.concatenate([causal] * N_HEADS, axis=0)
    group_mask = _block_mask(GROUP_WIDTH, 128, 128)
    tri = _lower_tri(c)
    neg_a = -jnp.exp(alog_ref[...]) * LOG2E

    def one_sequence(sq):
        blk = blk_ref[sq]
        sz = blk[:, 0:GROUP_WIDTH]
        xbc = _conv_silu(xe_scr.at[sq], halo_ref[sq][:, GROUP_WIDTH:], blk[:, GROUP_WIDTH:], cw_ref[...],
                         cb_ref[...], ci == 0, c)
        xs = xbc[:, 0:256]
        bmat = xbc[:, 256:512].astype(BF16)
        cmat = xbc[:, 512:768].astype(BF16)
        s = s_scr[sq]
        cb = [_dot_nt(cmat[:, gi * 128:(gi + 1) * 128], bmat[:, gi * 128:(gi + 1) * 128]) for gi in range(2)]
        y_inter = _dot(cmat, s.astype(BF16))
        dt = _softplus(_expand_small(small_ref[sq], SMALL_SDT) + dtb_ref[...])
        yield
        g = _exact_dot_left(tri, neg_a * dt)
        yield
        g_last = g[c - 1:c, :]
        da, db = _decay_diff_operands(g)
        diff = _dot_nt(_stack_heads(da, masks).astype(BF16), db.astype(BF16))
        v = xs * dt
        vend = v * jnp.exp2(g_last - g)
        ds = _dot_tn(bmat, vend.astype(BF16))
        yield
        decay = jnp.where(causal4, jnp.exp2(diff), 0.0)
        scores = jnp.concatenate([cb[0], cb[0], cb[1], cb[1]], axis=0) * decay
        pv = _dot(scores.astype(BF16), v.astype(BF16))
        s_scr[sq] = jnp.exp2(g_last) * s + jnp.where(group_mask, ds, 0.0)
        yield
        y = _unstack_heads(pv, masks, c) + y_inter * jnp.exp2(g)
        y = (y + dskip_ref[...] * xs) * _silu(sz)
        halves = [_rms_rows(y[:, gi * 128:(gi + 1) * 128]) for gi in range(2)]
        o_ref[sq] = jnp.concatenate(halves, axis=1) * nw_ref[...]

    _round_robin([one_sequence(sq) for sq in range(nb)])

    @pl.when(ci == n_chunks - 1)
    def _():
        for sq in range(nb):
            for h in range(N_HEADS):
                gi = h // 2
                rows = jnp.where(masks[h], s_scr[sq, gi * 128:(gi + 1) * 128, :], 0.0)
                st_ref[sq, h * 128:(h + 1) * 128, :] = _extract_blocks(rows, 128, HEAD_DIM)


def _lane_rep(p):
    return jnp.repeat(p.astype(F32), HEAD_DIM).reshape(1, GROUP_WIDTH)


def _ssd_prompt(proj3, conv_w, conv_b, dt_bias, a_log, d_skip, norm_w, c):
    b, l, _ = proj3.shape
    n = l // c
    fixed = lambda bi, ci: (0, 0)
    nb = math.gcd(b, PROMPT_SEQS_PER_STEP)
    return pl.pallas_call(
        functools.partial(_ssd_prompt_kernel, c=c, n_chunks=n, nb=nb),
        grid=(b // nb, n),
        in_specs=[pl.BlockSpec((nb, c, 1024), lambda bi, ci: (bi, ci, COL_SSD // 1024)),
                  pl.BlockSpec((nb, 8, 1024), lambda bi, ci: (bi, jnp.maximum(ci * (c // 8) - 1, 0), COL_SSD // 1024)),
                  pl.BlockSpec((nb, c, 128), lambda bi, ci: (bi, ci, COL_SMALL // 128)),
                  pl.BlockSpec((CONV_WIDTH, 768), fixed),
                  pl.BlockSpec((1, 768), fixed),
                  pl.BlockSpec((1, GROUP_WIDTH), fixed),
                  pl.BlockSpec((1, GROUP_WIDTH), fixed),
                  pl.BlockSpec((1, GROUP_WIDTH), fixed),
                  pl.BlockSpec((1, GROUP_WIDTH), fixed)],
        out_specs=[pl.BlockSpec((nb, c, GROUP_WIDTH), lambda bi, ci: (bi, ci, 0)),
                   pl.BlockSpec((nb, N_HEADS * SSD_STATE, HEAD_DIM), lambda bi, ci: (bi, 0, 0))],
        out_shape=[jax.ShapeDtypeStruct((b, l, GROUP_WIDTH), F32),
                   jax.ShapeDtypeStruct((b, N_HEADS * SSD_STATE, HEAD_DIM), F32)],
        scratch_shapes=[pltpu.VMEM((nb, GROUP_WIDTH, GROUP_WIDTH), F32),
                        pltpu.VMEM((nb, c + 8, 768), F32)],
        compiler_params=pltpu.CompilerParams(dimension_semantics=("arbitrary", "arbitrary"),
                                             vmem_limit_bytes=VMEM_LIMIT),
        name="ssd_prompt",
    )(proj3, proj3, proj3, conv_w, conv_b.reshape(1, 768), _lane_rep(dt_bias), _lane_rep(a_log),
      _lane_rep(d_skip), norm_w.reshape(1, GROUP_WIDTH))


def _gdn_prompt_kernel(blk_ref, halo_ref, small_ref, cw_ref, alog_ref, dtb_ref, nw_ref,
                       o_ref, st_ref, s_scr, xe_scr, m_scr, *, c, n_chunks, nb):
    ci = pl.program_id(1)
    hc = 2 * c
    n_lvl = int(math.log2(c))

    @pl.when(ci == 0)
    def _():
        s_scr[...] = jnp.zeros_like(s_scr)

    @pl.when((pl.program_id(0) == 0) & (ci == 0))
    def _():
        rr = _iota((hc, hc), 0)
        cc = _iota((hc, hc), 1)
        same = (rr // c) == (cc // c)
        m_scr[0] = (same & (rr >= cc)).astype(F32)
        m_scr[1] = (same & (rr > cc)).astype(F32)
        for lv in range(n_lvl):
            sz = 1 << lv
            off = ((rr // (2 * sz)) == (cc // (2 * sz))) & (((rr // sz) % 2) == 1) & (((cc // sz) % 2) == 0)
            m_scr[2 + lv] = off.astype(F32)

    ones_bd = _block_ones(GROUP_WIDTH, HEAD_DIM)
    bd_mask = _block_mask(GROUP_WIDTH, HEAD_DIM, HEAD_DIM)
    masks = _head_masks()
    pair_masks = [masks[0:2], masks[2:4]]
    tri = _lower_tri(c)
    neg_a = -jnp.exp(alog_ref[...]) * LOG2E

    def one_sequence(sq):
        blk = blk_ref[sq]
        gz = blk[:, 768:1024]
        qkv = _conv_silu(xe_scr.at[sq], halo_ref[sq][:, 0:768], blk[:, 0:768], cw_ref[...], None, ci == 0, c)
        gq, gk, v = qkv[:, 0:256], qkv[:, 256:512], qkv[:, 512:768]
        q = gq * lax.rsqrt(_head_sumsq(gq, ones_bd) + EPS) * (HEAD_DIM ** -0.5)
        k = gk * lax.rsqrt(_head_sumsq(gk, ones_bd) + EPS)
        yield
        small = small_ref[sq]
        beta = _sigmoid(_expand_small(small, SMALL_GB))
        g = _exact_dot_left(tri, neg_a * _softplus(_expand_small(small, SMALL_GA) + dtb_ref[...]))
        g_last = g[c - 1:c, :]
        eg = jnp.exp2(g)
        yield
        da, db = _decay_diff_operands(g)
        bk = beta * k
        bkg = bk * eg
        bv = beta * v
        a_mat, p_mat, x = [], [], []
        for pm in pair_masks:
            diff = _dot_nt(_stack_heads(da, pm).astype(BF16), _stack_heads(db, pm).astype(BF16))
            k_st = _stack_heads(k, pm).astype(BF16)
            kk = _dot_nt(_stack_heads(bk, pm).astype(BF16), k_st)
            qk = _dot_nt(_stack_heads(q, pm).astype(BF16), k_st)
            decay = jnp.exp2(jnp.minimum(diff, 0.0))
            a_mat.append(kk * (decay * m_scr[1]))
            p_mat.append((qk * (decay * m_scr[0])).astype(BF16))
            x.append(jnp.concatenate([_stack_heads(bkg, pm), _stack_heads(bv, pm)], axis=1))
        yield

        n_mat = [-(a * m_scr[2]) for a in a_mat]
        for lv in range(1, n_lvl):
            a_off = [a * m_scr[2 + lv] for a in a_mat]
            m = [ao + _dot(ao.astype(BF16), n.astype(BF16)) for ao, n in zip(a_off, n_mat)]
            yield
            n_mat = [n - mm - _dot(n.astype(BF16), mm.astype(BF16)) for n, mm in zip(n_mat, m)]
            yield
        x = [xx + _dot(n.astype(BF16), xx.astype(BF16)) for xx, n in zip(x, n_mat)]
        yield
        w = x[0][0:c, 0:256] + x[0][c:2 * c, 0:256] + x[1][0:c, 0:256] + x[1][c:2 * c, 0:256]
        u0 = x[0][0:c, 256:512] + x[0][c:2 * c, 256:512] + x[1][0:c, 256:512] + x[1][c:2 * c, 256:512]

        s = s_scr[sq]
        s_bf = s.astype(BF16)
        u = u0 - _dot(w.astype(BF16), s_bf)
        o = _dot((q * eg).astype(BF16), s_bf)
        yield
        pu = [_dot(pmat, _stack_heads(u, pm).astype(BF16)) for pmat, pm in zip(p_mat, pair_masks)]
        kend = k * jnp.exp2(g_last - g)
        ds = _dot_tn(kend.astype(BF16), u.astype(BF16))
        yield
        for pu_p in pu:
            o = o + pu_p[0:c] + pu_p[c:2 * c]
        s_scr[sq] = jnp.exp2(g_last) * s + jnp.where(bd_mask, ds, 0.0)
        ss = _head_sumsq(o, ones_bd)
        o_ref[sq] = o * lax.rsqrt(ss * (1.0 / HEAD_DIM) + EPS) * nw_ref[...] * _silu(gz)

    _round_robin([one_sequence(sq) for sq in range(nb)])

    @pl.when(ci == n_chunks - 1)
    def _():
        for sq in range(nb):
            st_ref[sq] = _extract_blocks(s_scr[sq], GROUP_WIDTH, HEAD_DIM)


def _gdn_prompt(proj3, conv_w, a_log, dt_bias, norm_w, c):
    b, l, _ = proj3.shape
    n = l // c
    fixed = lambda bi, ci: (0, 0)
    nb = math.gcd(b, GDN_SEQS_PER_STEP)
    hc = 2 * c
    return pl.pallas_call(
        functools.partial(_gdn_prompt_kernel, c=c, n_chunks=n, nb=nb),
        grid=(b // nb, n),
        in_specs=[pl.BlockSpec((nb, c, 1024), lambda bi, ci: (bi, ci, COL_GDN // 1024)),
                  pl.BlockSpec((nb, 8, 1024), lambda bi, ci: (bi, jnp.maximum(ci * (c // 8) - 1, 0), COL_GDN // 1024)),
                  pl.BlockSpec((nb, c, 128), lambda bi, ci: (bi, ci, COL_SMALL // 128)),
                  pl.BlockSpec((CONV_WIDTH, 768), fixed),
                  pl.BlockSpec((1, GROUP_WIDTH), fixed),
                  pl.BlockSpec((1, GROUP_WIDTH), fixed),
                  pl.BlockSpec((1, GROUP_WIDTH), fixed)],
        out_specs=[pl.BlockSpec((nb, c, GROUP_WIDTH), lambda bi, ci: (bi, ci, 0)),
                   pl.BlockSpec((nb, GROUP_WIDTH, HEAD_DIM), lambda bi, ci: (bi, 0, 0))],
        out_shape=[jax.ShapeDtypeStruct((b, l, GROUP_WIDTH), F32),
                   jax.ShapeDtypeStruct((b, GROUP_WIDTH, HEAD_DIM), F32)],
        scratch_shapes=[pltpu.VMEM((nb, GROUP_WIDTH, GROUP_WIDTH), F32),
                        pltpu.VMEM((nb, c + 8, 768), F32),
                        pltpu.VMEM((2 + int(math.log2(c)), hc, hc), F32)],
        compiler_params=pltpu.CompilerParams(dimension_semantics=("arbitrary", "arbitrary"),
                                             vmem_limit_bytes=VMEM_LIMIT),
        name="gdn_prompt",
    )(proj3, proj3, proj3, conv_w, _lane_rep(a_log), _lane_rep(dt_bias),
      jnp.tile(norm_w.astype(F32), N_HEADS).reshape(1, GROUP_WIDTH))


HGRN_SUB = 16
LOG2E = 1.4426950408889634


def _hgrn_lower_bound(logits, layer):
    rows = [logits[d:d + 1, :] for d in range(DEPTH)]
    mx = functools.reduce(jnp.maximum, rows)
    es = [jnp.exp(x - mx) for x in rows]
    tot = functools.reduce(lambda a, b: a + b, es)
    sm = [e / tot for e in es]
    acc = sm[0]
    for d in range(1, layer + 1):
        acc = acc + sm[d]
    return acc - sm[0]


def _hgrn_prompt_kernel(blk_ref, lb_ref, nw_ref, o_ref, st_ref, s_scr, *, r, n_chunks, layer, nb):
    ci = pl.program_id(1)
    sub = HGRN_SUB
    n_sub = r // sub

    @pl.when(ci == 0)
    def _():
        s_scr[...] = jnp.zeros_like(s_scr)

    lb = _hgrn_lower_bound(lb_ref[...], layer)
    rr = _iota((r, r), 0)
    cc = _iota((r, r), 1)
    same_sub = (rr // sub) == (cc // sub)
    cum_sel = (same_sub & (rr >= cc)).astype(BF16)
    tot_sel = same_sub.astype(BF16)
    ones_bd = _block_ones(GROUP_WIDTH, HEAD_DIM)
    bd_mask = _block_mask(GROUP_WIDTH, HEAD_DIM, HEAD_DIM)
    half = sub // 2
    i8 = _iota((half, 1), 0)

    def one_sequence(sq):
        blk = blk_ref[sq]
        hq, hf, hi, hg = (blk[:, i * GROUP_WIDTH:(i + 1) * GROUP_WIDTH] for i in range(4))
        f = lb + (1.0 - lb) * _sigmoid(hf)
        q = _sigmoid(hq)
        k = 1.0 - f
        v = hi
        logf = jnp.log(f)
        g = _exact_dot_left(cum_sel, logf)
        g_tot = _exact_dot_left(tot_sel, logf)
        yield
        a2 = (g + jnp.log(q)) * LOG2E
        h2 = (g - jnp.log(k)) * LOG2E
        gt2 = g_tot * LOG2E
        qt = jnp.exp2(a2).astype(BF16)
        kh = jnp.exp2(gt2 - h2).astype(BF16)
        v_bf = v.astype(BF16)

        s = s_scr[sq]
        outs = []
        for j in range(n_sub):
            lo = j * sub
            v_j = v[lo:lo + sub]
            a_lo, a_hi, h_j = a2[lo:lo + half], a2[lo + half:lo + sub], h2[lo:lo + sub]
            lo_blocks, hi_blocks = [], []
            for jj in range(sub):
                h_row = h_j[jj:jj + 1, :]
                if jj < half:
                    e_lo = jnp.exp2(a_lo - h_row)
                    lo_blocks.append(e_lo if jj == 0 else jnp.where(i8 >= jj, e_lo, 0.0))
                    hi_blocks.append(jnp.exp2(a_hi - h_row))
                else:
                    e_hi = jnp.exp2(a_hi - h_row)
                    hi_blocks.append(e_hi if jj == half else jnp.where(i8 >= jj - half, e_hi, 0.0))
            sc = _dot(jnp.concatenate(lo_blocks + hi_blocks, axis=0).astype(BF16), ones_bd)
            o_inter = _dot_nt(qt[lo:lo + sub], s.astype(BF16))
            ds = _dot_tn(v_bf[lo:lo + sub], kh[lo:lo + sub])
            yield
            n_lo = half * half
            o_lo = sc[0:half] * v_j[0:1, :]
            o_hi = sc[n_lo:n_lo + half] * v_j[0:1, :]
            for jj in range(1, sub):
                if jj < half:
                    o_lo = o_lo + sc[jj * half:(jj + 1) * half] * v_j[jj:jj + 1, :]
                o_hi = o_hi + sc[n_lo + jj * half:n_lo + (jj + 1) * half] * v_j[jj:jj + 1, :]
            outs.append(jnp.concatenate([o_lo, o_hi], axis=0) + o_inter)
            s = jnp.exp2(gt2[lo:lo + 1, :]) * s + jnp.where(bd_mask, ds, 0.0)
        s_scr[sq] = s

        o = jnp.concatenate(outs, axis=0)
        ss = _head_sumsq(o, ones_bd)
        yield
        o_ref[sq] = o * lax.rsqrt(ss * (1.0 / HEAD_DIM) + EPS) * nw_ref[...] * _silu(hg)

    _round_robin([one_sequence(sq) for sq in range(nb)])

    @pl.when(ci == n_chunks - 1)
    def _():
        for sq in range(nb):
            st_ref[sq] = _extract_blocks(s_scr[sq].T, GROUP_WIDTH, HEAD_DIM)


def _hgrn_prompt(proj3, lb_logits, norm_w, layer, r):
    b, l, _ = proj3.shape
    n = l // r
    fixed = lambda bi, ci: (0, 0)
    nb = math.gcd(b, PROMPT_SEQS_PER_STEP)
    return pl.pallas_call(
        functools.partial(_hgrn_prompt_kernel, r=r, n_chunks=n, layer=layer, nb=nb),
        grid=(b // nb, n),
        in_specs=[pl.BlockSpec((nb, r, 1024), lambda bi, ci: (bi, ci, COL_HGRN // 1024)),
                  pl.BlockSpec((DEPTH, GROUP_WIDTH), fixed),
                  pl.BlockSpec((1, GROUP_WIDTH), fixed)],
        out_specs=[pl.BlockSpec((nb, r, GROUP_WIDTH), lambda bi, ci: (bi, ci, 0)),
                   pl.BlockSpec((nb, GROUP_WIDTH, HEAD_DIM), lambda bi, ci: (bi, 0, 0))],
        out_shape=[jax.ShapeDtypeStruct((b, l, GROUP_WIDTH), F32),
                   jax.ShapeDtypeStruct((b, GROUP_WIDTH, HEAD_DIM), F32)],
        scratch_shapes=[pltpu.VMEM((nb, GROUP_WIDTH, GROUP_WIDTH), F32)],
        compiler_params=pltpu.CompilerParams(dimension_semantics=("arbitrary", "arbitrary"),
                                             vmem_limit_bytes=VMEM_LIMIT),
        name="hgrn_prompt",
    )(proj3, lb_logits.astype(F32), jnp.tile(norm_w.astype(F32), N_HEADS).reshape(1, GROUP_WIDTH))


DEC_SEQS = 128
DEC_LEN = 4


def _head_rows(h):
    return pl.ds(pl.multiple_of(h * HEAD_DIM, HEAD_DIM), HEAD_DIM)


def _recur_head(load_s, store_s, n_keys, decay_fn, k_fn, q_fn, v_blocks):
    def body(kk, accs):
        s = load_s(kk)
        accs = list(accs)
        for t in range(DEC_LEN):
            s = decay_fn(t, kk) * s + k_fn(t, kk) * v_blocks[t]
            accs[t] = accs[t] + q_fn(t, kk) * s
        store_s(kk, s)
        return tuple(accs)

    zero = jnp.zeros((HEAD_DIM, DEC_SEQS), F32)
    return lax.fori_loop(0, n_keys, body, (zero,) * DEC_LEN)


def _dec_ret_kernel(blk_ref, cos_ref, sin_ref, st_ref, o_ref, so_ref, q_scr, k_scr, v_scr, o_scr):
    h = pl.program_id(0)

    @pl.when(h == 0)
    def _():
        for t in range(DEC_LEN):
            blk = blk_ref[t]
            rq, rk, rv = blk[:, 0:256], blk[:, 256:512], blk[:, 512:768]
            cosv, sinv = cos_ref[t:t + 1, :], sin_ref[t:t + 1, :]
            q_scr[t] = (rq * cosv + _swap_halves(rq) * sinv).T
            k_scr[t] = ((rk * cosv + _swap_halves(rk) * sinv) * (HEAD_DIM ** -0.5)).T
            v_scr[t] = rv.T

    lg = jnp.where(h == 0, LOG_GAMMA[0], jnp.where(h == 1, LOG_GAMMA[1], jnp.where(h == 2, LOG_GAMMA[2], LOG_GAMMA[3])))
    gamma = jnp.exp(jnp.full((1, DEC_SEQS), lg, F32))
    hr = _head_rows(h)
    v_blocks = [v_scr[t, hr, :] for t in range(DEC_LEN)]
    accs = _recur_head(
        lambda kk: st_ref[0, 0, kk], functools.partial(_store_state, so_ref), HEAD_DIM,
        lambda t, kk: gamma,
        lambda t, kk: k_scr[t, pl.ds(h * HEAD_DIM + kk, 1), :],
        lambda t, kk: q_scr[t, pl.ds(h * HEAD_DIM + kk, 1), :],
        v_blocks)
    for t in range(DEC_LEN):
        o_scr[t, hr, :] = accs[t]

    @pl.when(h == N_HEADS - 1)
    def _():
        ones_bd = _block_ones(GROUP_WIDTH, HEAD_DIM)
        for t in range(DEC_LEN):
            o = o_scr[t].T
            ss = _head_sumsq(o, ones_bd)
            o_ref[t] = o * lax.rsqrt(ss * (1.0 / HEAD_DIM) + EPS) * _silu(blk_ref[t][:, 768:1024])


def _store_state(so_ref, kk, s):
    so_ref[0, 0, kk] = s


def _without_ref(kernel_fn, idx):
    def wrapped(*refs):
        return kernel_fn(*refs[:idx], *refs[idx + 1:])
    return wrapped


def _dec_call(kernel_fn, name, col, ins, in_specs, n_tok_scr, state_view, layer, carried, extra_scratch=()):
    blk_spec = pl.BlockSpec((DEC_LEN, DEC_SEQS, 1024), lambda h: (0, 0, col // 1024))
    st_spec = pl.BlockSpec((1, 1) + state_view.shape[2:], lambda h: (layer, h, 0, 0, 0))
    tok_scr = pltpu.VMEM((DEC_LEN, GROUP_WIDTH, DEC_SEQS), F32)
    ins = tuple(ins) + (state_view, carried)
    specs = [blk_spec] + in_specs + [st_spec, pl.BlockSpec(memory_space=pl.ANY)]
    return pl.pallas_call(
        _without_ref(kernel_fn, len(ins) - 1),
        grid=(N_HEADS,),
        in_specs=specs,
        out_specs=[pl.BlockSpec((DEC_LEN, DEC_SEQS, GROUP_WIDTH), lambda h: (0, 0, 0)), st_spec],
        out_shape=[jax.ShapeDtypeStruct((DEC_LEN, DEC_SEQS, GROUP_WIDTH), F32),
                   jax.ShapeDtypeStruct(state_view.shape, F32)],
        scratch_shapes=[tok_scr] * n_tok_scr + list(extra_scratch),
        input_output_aliases={len(ins) - 1: 1},
        compiler_params=pltpu.CompilerParams(dimension_semantics=("arbitrary",), vmem_limit_bytes=VMEM_LIMIT),
        name=name,
    )(*ins)


def _fixed1(shape):
    return pl.BlockSpec(shape, lambda h: (0,) * len(shape))


def _dec_ret(projd, cos_t, sin_t, state_view, layer, carried):
    return _dec_call(_dec_ret_kernel, "retention_decode", COL_RET, (projd, cos_t, sin_t),
                     [_fixed1((DEC_LEN, GROUP_WIDTH)), _fixed1((DEC_LEN, GROUP_WIDTH))], 4,
                     state_view, layer, carried)


def _dec_hgrn_kernel(blk_ref, lb_ref, nw_ref, st_ref, o_ref, so_ref, q_scr, k_scr, v_scr, f_scr, o_scr, *, layer):
    h = pl.program_id(0)

    @pl.when(h == 0)
    def _():
        lb = _hgrn_lower_bound(lb_ref[...], layer)
        for t in range(DEC_LEN):
            blk = blk_ref[t]
            f = lb + (1.0 - lb) * _sigmoid(blk[:, 256:512])
            q_scr[t] = _sigmoid(blk[:, 0:256]).T
            k_scr[t] = (1.0 - f).T
            v_scr[t] = blk[:, 512:768].T
            f_scr[t] = f.T

    hr = _head_rows(h)
    v_blocks = [v_scr[t, hr, :] for t in range(DEC_LEN)]
    row = lambda scr: (lambda t, kk: scr[t, pl.ds(h * HEAD_DIM + kk, 1), :])
    accs = _recur_head(lambda kk: st_ref[0, 0, kk], functools.partial(_store_state, so_ref), HEAD_DIM,
                       row(f_scr), row(k_scr), row(q_scr), v_blocks)
    for t in range(DEC_LEN):
        o_scr[t, hr, :] = accs[t]

    @pl.when(h == N_HEADS - 1)
    def _():
        ones_bd = _block_ones(GROUP_WIDTH, HEAD_DIM)
        for t in range(DEC_LEN):
            o = o_scr[t].T
            ss = _head_sumsq(o, ones_bd)
            o_ref[t] = o * lax.rsqrt(ss * (1.0 / HEAD_DIM) + EPS) * nw_ref[...] * _silu(blk_ref[t][:, 768:1024])


def _dec_hgrn(projd, lb_logits, norm_w, state_view, layer, carried):
    return _dec_call(functools.partial(_dec_hgrn_kernel, layer=layer), "hgrn_decode", COL_HGRN,
                     (projd, lb_logits.astype(F32), jnp.tile(norm_w.astype(F32), N_HEADS).reshape(1, GROUP_WIDTH)),
                     [_fixed1((DEPTH, GROUP_WIDTH)), _fixed1((1, GROUP_WIDTH))], 5, state_view, layer, carried)


def _hist_spec(layer):
    return pl.BlockSpec((1, CONV_WIDTH - 1, DEC_SEQS, 768), lambda h: (layer, 0, 0, 0))


def _dec_conv_silu(hist_ref, xs, w, bias):
    xe = [hist_ref[0, j] for j in range(CONV_WIDTH - 1)] + xs
    out = []
    for t in range(DEC_LEN):
        y = xe[t] * w[0:1, :]
        for j in range(1, CONV_WIDTH):
            y = y + xe[t + j] * w[j:j + 1, :]
        if bias is not None:
            y = y + bias
        out.append(_silu(y))
    return out


def _dec_ssd_kernel(blk_ref, small_ref, hist_ref, cw_ref, cb_ref, dtb_ref, alog_ref, dskip_ref, nw_ref, st_ref,
                    o_ref, so_ref, c_scr, b_scr, v_scr, a_scr, o_scr, x_scr):
    h = pl.program_id(0)

    @pl.when(h == 0)
    def _():
        xbc = _dec_conv_silu(hist_ref, [blk_ref[t][:, 256:1024] for t in range(DEC_LEN)], cw_ref[...], cb_ref[...])
        for t in range(DEC_LEN):
            xs = xbc[t][:, 0:256]
            dt = _softplus(_expand_small(small_ref[t], SMALL_SDT) + dtb_ref[...])
            x_scr[t] = xs
            v_scr[t] = (xs * dt).T
            b_scr[t] = xbc[t][:, 256:512].T
            c_scr[t] = xbc[t][:, 512:768].T
            a_scr[t] = jnp.exp(-jnp.exp(alog_ref[...]) * dt).T

    hr = _head_rows(h)
    g0 = (h // 2) * SSD_STATE
    v_blocks = [v_scr[t, hr, :] for t in range(DEC_LEN)]
    accs = _recur_head(
        lambda kk: st_ref[0, 0, kk], functools.partial(_store_state, so_ref), SSD_STATE,
        lambda t, kk: a_scr[t, pl.ds(h * HEAD_DIM, 1), :],
        lambda t, kk: b_scr[t, pl.ds(g0 + kk, 1), :],
        lambda t, kk: c_scr[t, pl.ds(g0 + kk, 1), :],
        v_blocks)
    for t in range(DEC_LEN):
        o_scr[t, hr, :] = accs[t]

    @pl.when(h == N_HEADS - 1)
    def _():
        for t in range(DEC_LEN):
            y = (o_scr[t].T + dskip_ref[...] * x_scr[t]) * _silu(blk_ref[t][:, 0:256])
            halves = [_rms_rows(y[:, gi * 128:(gi + 1) * 128]) for gi in range(2)]
            o_ref[t] = jnp.concatenate(halves, axis=1) * nw_ref[...]


def _dec_ssd(projd, hist, conv_w, conv_b, dt_bias, a_log, d_skip, norm_w, state_view, layer, carried):
    small_spec = pl.BlockSpec((DEC_LEN, DEC_SEQS, 128), lambda h: (0, 0, COL_SMALL // 128))
    return _dec_call(_dec_ssd_kernel, "ssd_decode", COL_SSD,
                     (projd, projd, hist, conv_w, conv_b.reshape(1, 768), _lane_rep(dt_bias), _lane_rep(a_log),
                      _lane_rep(d_skip), norm_w.reshape(1, GROUP_WIDTH)),
                     [small_spec, _hist_spec(layer), _fixed1((CONV_WIDTH, 768)),
                      _fixed1((1, 768))] + [_fixed1((1, GROUP_WIDTH))] * 4, 5, state_view, layer, carried,
                     extra_scratch=[pltpu.VMEM((DEC_LEN, DEC_SEQS, GROUP_WIDTH), F32)])


def _dec_gdn_kernel(blk_ref, small_ref, hist_ref, cw_ref, alog_ref, dtb_ref, nw_ref, st_ref,
                    o_ref, so_ref, q_scr, k_scr, v_scr, a_scr, b_scr, o_scr):
    h = pl.program_id(0)

    @pl.when(h == 0)
    def _():
        ones_bd = _block_ones(GROUP_WIDTH, HEAD_DIM)
        qkv = _dec_conv_silu(hist_ref, [blk_ref[t][:, 0:768] for t in range(DEC_LEN)], cw_ref[...], None)
        for t in range(DEC_LEN):
            gq, gk, gv = qkv[t][:, 0:256], qkv[t][:, 256:512], qkv[t][:, 512:768]
            q_scr[t] = (gq * lax.rsqrt(_head_sumsq(gq, ones_bd) + EPS) * (HEAD_DIM ** -0.5)).T
            k_scr[t] = (gk * lax.rsqrt(_head_sumsq(gk, ones_bd) + EPS)).T
            v_scr[t] = gv.T
            small = small_ref[t]
            b_scr[t] = _sigmoid(_expand_small(small, SMALL_GB)).T
            la = -jnp.exp(alog_ref[...]) * _softplus(_expand_small(small, SMALL_GA) + dtb_ref[...])
            a_scr[t] = jnp.exp(la).T

    hr = _head_rows(h)
    one_row = pl.ds(h * HEAD_DIM, 1)
    zero = jnp.zeros((HEAD_DIM, DEC_SEQS), F32)
    for t in range(DEC_LEN):
        a = a_scr[t, one_row, :]
        cur = st_ref if t == 0 else so_ref

        def kts(kk, r):
            return r + k_scr[t, pl.ds(h * HEAD_DIM + kk, 1), :] * cur[0, 0, kk]

        r = lax.fori_loop(0, HEAD_DIM, kts, zero)
        u = b_scr[t, one_row, :] * (v_scr[t, hr, :] - a * r)

        def upd(kk, acc):
            s = a * cur[0, 0, kk] + k_scr[t, pl.ds(h * HEAD_DIM + kk, 1), :] * u
            so_ref[0, 0, kk] = s
            return acc + q_scr[t, pl.ds(h * HEAD_DIM + kk, 1), :] * s

        o_scr[t, hr, :] = lax.fori_loop(0, HEAD_DIM, upd, zero)

    @pl.when(h == N_HEADS - 1)
    def _():
        ones_bd = _block_ones(GROUP_WIDTH, HEAD_DIM)
        for t in range(DEC_LEN):
            o = o_scr[t].T
            ss = _head_sumsq(o, ones_bd)
            o_ref[t] = o * lax.rsqrt(ss * (1.0 / HEAD_DIM) + EPS) * nw_ref[...] * _silu(blk_ref[t][:, 768:1024])


def _dec_gdn(projd, hist, conv_w, a_log, dt_bias, norm_w, state_view, layer, carried):
    small_spec = pl.BlockSpec((DEC_LEN, DEC_SEQS, 128), lambda h: (0, 0, COL_SMALL // 128))
    return _dec_call(_dec_gdn_kernel, "gdn_decode", COL_GDN,
                     (projd, projd, hist, conv_w, _lane_rep(a_log), _lane_rep(dt_bias),
                      jnp.tile(norm_w.astype(F32), N_HEADS).reshape(1, GROUP_WIDTH)),
                     [small_spec, _hist_spec(layer), _fixed1((CONV_WIDTH, 768))]
                     + [_fixed1((1, GROUP_WIDTH))] * 3, 6, state_view, layer, carried)


def _reorder_cols(w):
    lead = w.shape[:-1]
    small = jnp.concatenate([w[..., 2048:2056], w[..., 4104:4108],
                             jnp.zeros(lead + (P_PAD - COL_SMALL - 12,), w.dtype)], axis=-1)
    return jnp.concatenate([w[..., 0:2048], w[..., 2056:4104], small], axis=-1)


def _prep_w_in(w):
    return _reorder_cols(w).astype(BF16)


def _w_in_prep_kernel(w_ref, tail_ref, o_ref):
    x = w_ref[0]
    o_ref[0, :, 0:COL_RET] = x[:, 0:COL_RET].astype(BF16)
    o_ref[0, :, COL_RET:COL_SMALL] = x[:, COL_RET + 8:COL_SMALL + 8].astype(BF16)
    lane = _iota((x.shape[0], 128), 1)
    small = jnp.where(lane < 8, x[:, COL_RET:COL_RET + 128], jnp.where(lane < 12, tail_ref[0], 0.0))
    o_ref[0, :, COL_SMALL:P_PAD] = small.astype(BF16)


def _prep_w_in_all(w_in):
    d, r, p = w_in.shape
    rows = 256
    return pl.pallas_call(
        _w_in_prep_kernel,
        grid=(d, r // rows),
        in_specs=[pl.BlockSpec((1, rows, p), lambda l, i: (l, i, 0)),
                  pl.BlockSpec((1, rows, 128), lambda l, i: (l, i, COL_SMALL // 128))],
        out_specs=pl.BlockSpec((1, rows, P_PAD), lambda l, i: (l, i, 0)),
        out_shape=jax.ShapeDtypeStruct((d, r, P_PAD), BF16),
        compiler_params=pltpu.CompilerParams(dimension_semantics=("arbitrary", "arbitrary"),
                                             vmem_limit_bytes=VMEM_LIMIT),
        name="w_in_prep",
    )(w_in, w_in)


def _rotary_tables(pos):
    half = HEAD_DIM // 2
    inv_freq = RET_THETA ** (-jnp.arange(half, dtype=F32) / half)
    ang = pos.astype(F32)[:, None] * inv_freq[None, :]
    cos, sin = jnp.cos(ang), jnp.sin(ang)
    cos_t = jnp.tile(cos, (1, 2 * N_HEADS))
    sin_t = jnp.tile(jnp.concatenate([-sin, sin], axis=1), (1, N_HEADS))
    return cos_t, sin_t


RET_CHUNK = 128
SSD_CHUNK = 128
GDN_CHUNK = 64
HGRN_ROWS = 128


def _forward(x_prompt, x_sample, states, p, past_len):
    st_hg, st_gd, st_gc, st_rt, st_sd, st_sc = states
    bp, lp, _ = x_prompt.shape
    nd, ld, _ = x_sample.shape
    xp = x_prompt.astype(F32).reshape(bp * lp, D_MODEL)
    xd = jnp.transpose(x_sample.astype(F32), (1, 0, 2)).reshape(ld * nd, D_MODEL)
    cos_p, sin_p = _rotary_tables(jnp.arange(lp))
    cos_d, sin_d = _rotary_tables(past_len + jnp.arange(ld))
    outs = {k: [] for k in ("hp", "gp", "gcp", "gcs", "rp", "sp", "scp", "scs")}
    w_in_all = _prep_w_in_all(p["w_in"].astype(F32))
    wo, wu, wd = (p[k].astype(BF16) for k in ("w_out", "w_up", "w_down"))
    norm_mix = p["norm_mix"].astype(F32).reshape(DEPTH, 1, D_MODEL)
    norm_ffn = p["norm_ffn"].astype(F32).reshape(DEPTH, 1, D_MODEL)
    sv_hg, sv_gd, sv_rt, sv_sd = (jnp.transpose(s.astype(F32), (0, 2, 3, 4, 1)) for s in (st_hg, st_gd, st_rt, st_sd))
    hist_g = jnp.transpose(st_gc.astype(F32), (0, 2, 1, 3))
    hist_s = jnp.transpose(st_sc.astype(F32), (0, 2, 1, 3))
    new_hg, new_gd, new_rt, new_sd = (jnp.zeros(s.shape, F32) for s in (sv_hg, sv_gd, sv_rt, sv_sd))
    for l in range(DEPTH):
        pp = _proj(xp, norm_mix, w_in_all, l).reshape(bp, lp, P_PAD)
        pd = _proj(xd, norm_mix, w_in_all, l).reshape(ld, nd, P_PAD)

        oa, sa = _hgrn_prompt(pp, p["hgrn_lb_logits"], p["hgrn_norm"][l], l, HGRN_ROWS)
        ob, sb = _gdn_prompt(pp, p["gdn_conv_w"][l], p["gdn_a_log"][l], p["gdn_dt_bias"][l], p["gdn_norm"][l],
                             GDN_CHUNK)
        oc, sc = _ret_prompt(pp, cos_p, sin_p, RET_CHUNK)
        od, sd = _ssd_prompt(pp, p["ssd_conv_w"][l], p["ssd_conv_b"][l], p["ssd_dt_bias"][l], p["ssd_a_log"][l],
                             p["ssd_d"][l], p["ssd_norm"][l], SSD_CHUNK)
        outs["hp"].append(sa.reshape(bp, N_HEADS, HEAD_DIM, HEAD_DIM))
        outs["gp"].append(sb.reshape(bp, N_HEADS, HEAD_DIM, HEAD_DIM))
        outs["rp"].append(sc.reshape(bp, N_HEADS, HEAD_DIM, HEAD_DIM))
        outs["sp"].append(sd.reshape(bp, N_HEADS, SSD_STATE, HEAD_DIM))
        outs["gcp"].append(pp[:, lp - 3:, COL_GDN:COL_GDN + 768])
        outs["scp"].append(pp[:, lp - 3:, COL_SSD + 256:COL_SSD + 1024])
        xp = _out_ffn(xp, [o.reshape(bp * lp, GROUP_WIDTH) for o in (oa, ob, oc, od)], wo, norm_ffn, wu, wd,
                      p["norm_final"], l)

        da, new_hg = _dec_hgrn(pd, p["hgrn_lb_logits"], p["hgrn_norm"][l], sv_hg, l, new_hg)
        db, new_gd = _dec_gdn(pd, hist_g, p["gdn_conv_w"][l], p["gdn_a_log"][l], p["gdn_dt_bias"][l],
                              p["gdn_norm"][l], sv_gd, l, new_gd)
        dc, new_rt = _dec_ret(pd, cos_d, sin_d, sv_rt, l, new_rt)
        dd, new_sd = _dec_ssd(pd, hist_s, p["ssd_conv_w"][l], p["ssd_conv_b"][l], p["ssd_dt_bias"][l],
                              p["ssd_a_log"][l], p["ssd_d"][l], p["ssd_norm"][l], sv_sd, l, new_sd)
        outs["gcs"].append(jnp.transpose(pd[ld - 3:, :, COL_GDN:COL_GDN + 768], (1, 0, 2)))
        outs["scs"].append(jnp.transpose(pd[ld - 3:, :, COL_SSD + 256:COL_SSD + 1024], (1, 0, 2)))
        xd = _out_ffn(xd, [o.reshape(ld * nd, GROUP_WIDTH) for o in (da, db, dc, dd)], wo, norm_ffn, wu, wd,
                      p["norm_final"], l)

    y_prompt = xp.reshape(bp, lp, D_MODEL)
    y_sample = jnp.transpose(xd.reshape(ld, nd, D_MODEL), (1, 0, 2))
    st = {k: jnp.stack(v) for k, v in outs.items()}
    hs, gs, rs, ss = (jnp.transpose(s, (0, 4, 1, 2, 3)) for s in (new_hg, new_gd, new_rt, new_sd))
    return (y_prompt, y_sample, st["hp"], hs, st["gp"], gs, st["gcp"], st["gcs"],
            st["rp"], rs, st["sp"], ss, st["scp"], st["scs"])


def kernel(x_prompt, x_sample, state_hgrn, state_gdn, state_gdn_conv, state_ret, state_ssd, state_ssd_conv,
           norm_mix, w_in, hgrn_lb_logits, hgrn_norm, gdn_conv_w, gdn_a_log, gdn_dt_bias, gdn_norm,
           ssd_conv_w, ssd_conv_b, ssd_dt_bias, ssd_a_log, ssd_d, ssd_norm,
           w_out, norm_ffn, w_up, w_down, norm_final):
    params = dict(norm_mix=norm_mix, w_in=w_in, hgrn_lb_logits=hgrn_lb_logits, hgrn_norm=hgrn_norm,
                  gdn_conv_w=gdn_conv_w, gdn_a_log=gdn_a_log, gdn_dt_bias=gdn_dt_bias, gdn_norm=gdn_norm,
                  ssd_conv_w=ssd_conv_w, ssd_conv_b=ssd_conv_b, ssd_dt_bias=ssd_dt_bias, ssd_a_log=ssd_a_log,
                  ssd_d=ssd_d, ssd_norm=ssd_norm, w_out=w_out, norm_ffn=norm_ffn, w_up=w_up,
                  w_down=w_down, norm_final=norm_final)
    states = (state_hgrn, state_gdn, state_gdn_conv, state_ret, state_ssd, state_ssd_conv)
    return _forward(x_prompt, x_sample, states, params, 16384)
```

```python
import functools
import math

import numpy as np
import jax
import jax.numpy as jnp
from jax import lax
from jax.experimental import pallas as pl
from jax.experimental.pallas import tpu as pltpu

F32 = jnp.float32
BF16 = jnp.bfloat16

D_MODEL = 1024
GROUP_WIDTH = 256
HEAD_DIM = 64
N_HEADS = 4
CONV_WIDTH = 4
SSD_STATE = 128
D_FF = 4096
RET_THETA = 10000.0
EPS = 1e-6
DEPTH = 2

COL_HGRN = 0
COL_GDN = 1024
COL_RET = 2048
COL_SSD = 3072
COL_SMALL = 4096
P_PAD = 4224
SMALL_GA, SMALL_GB, SMALL_SDT = 0, 4, 8

VMEM_LIMIT = 56 * 1024 * 1024
LOG_GAMMA = [math.log(1.0 - 2.0 ** (-5.0 - h)) for h in range(N_HEADS)]


def _dot(a, b):
    return jnp.dot(a, b, preferred_element_type=F32)


def _dot_nt(a, b):
    return lax.dot_general(a, b, (((1,), (1,)), ((), ())), preferred_element_type=F32)


def _dot_tn(a, b):
    return lax.dot_general(a, b, (((0,), (0,)), ((), ())), preferred_element_type=F32)


def _round_robin(gens):
    live = list(gens)
    while live:
        nxt = []
        for g in live:
            try:
                next(g)
                nxt.append(g)
            except StopIteration:
                pass
        live = nxt


def _split3(x):
    hi = x.astype(BF16)
    r1 = x - hi.astype(F32)
    mid = r1.astype(BF16)
    lo = (r1 - mid.astype(F32)).astype(BF16)
    return hi, mid, lo


def _exact_dot(x, sel):
    hi, mid, lo = _split3(x)
    return _dot(hi, sel) + _dot(mid, sel) + _dot(lo, sel)


def _exact_dot_left(sel, x):
    hi, mid, lo = _split3(x)
    return _dot(sel, hi) + _dot(sel, mid) + _dot(sel, lo)


def _iota(shape, dim):
    return lax.broadcasted_iota(jnp.int32, shape, dim)


def _head_of_lane(n_lanes, width=HEAD_DIM):
    return _iota((1, n_lanes), 1) // width


def _head_masks(n_lanes=GROUP_WIDTH, width=HEAD_DIM):
    hl = _head_of_lane(n_lanes, width)
    return [hl == h for h in range(n_lanes // width)]


def _stack_heads(x, masks):
    return jnp.concatenate([jnp.where(m, x, jnp.zeros_like(x)) for m in masks], axis=0)


def _unstack_heads(y, masks, c):
    out = jnp.where(masks[0], y[0:c], 0.0)
    for h in range(1, len(masks)):
        out = out + jnp.where(masks[h], y[h * c:(h + 1) * c], 0.0)
    return out


def _block_ones(n, width, dtype=BF16):
    r = _iota((n, n), 0) // width
    c = _iota((n, n), 1) // width
    return (r == c).astype(dtype)


def _block_mask(n, rwidth, cwidth):
    return (_iota((n, n), 0) // rwidth) == (_iota((n, n), 1) // cwidth)


def _lower_tri(c, dtype=BF16):
    return (_iota((c, c), 0) >= _iota((c, c), 1)).astype(dtype)


def _cumsum_rows(x, c):
    return _exact_dot_left(_lower_tri(c), x)


def _sigmoid(x):
    return 1.0 / (1.0 + jnp.exp(-x))


def _silu(x):
    return x * _sigmoid(x)


def _softplus(x):
    return jnp.maximum(x, 0.0) + jnp.log(1.0 + jnp.exp(-jnp.abs(x)))


def _rms_rows(x):
    return x * lax.rsqrt(jnp.mean(x * x, axis=-1, keepdims=True) + EPS)


def _head_sumsq(x, ones_bd):
    sq = x * x
    hi = sq.astype(BF16)
    lo = (sq - hi.astype(F32)).astype(BF16)
    return _dot(hi, ones_bd) + _dot(lo, ones_bd)


def _expand_small(small, first_lane):
    r = _iota((128, GROUP_WIDTH), 0)
    c = _iota((128, GROUP_WIDTH), 1) // HEAD_DIM
    sel = (r == c + first_lane).astype(BF16)
    return _exact_dot(small, sel)


def _decay_diff_operands(g):
    hi, mid, lo = (x.astype(F32) for x in _split3(g))
    pos = _iota(g.shape, 1) % HEAD_DIM
    a = jnp.where(pos == 0, hi, jnp.where(pos == 1, mid, jnp.where(pos == 2, lo,
                  jnp.where(pos < 6, 1.0, 0.0))))
    b = jnp.where(pos < 3, 1.0, jnp.where(pos == 3, -hi, jnp.where(pos == 4, -mid,
                  jnp.where(pos == 5, -lo, 0.0))))
    return a, b


def _extract_blocks(s_wide, rows, width):
    sel = ((_iota((GROUP_WIDTH, width), 0) % width) == _iota((GROUP_WIDTH, width), 1)).astype(BF16)
    return _exact_dot(s_wide, sel)


def _proj_kernel(x_ref, nw_ref, w_ref, o_ref):
    h = _rms_rows(x_ref[...]) * nw_ref[0]
    o_ref[...] = _dot_nt(h.astype(BF16), w_ref[0])


def _proj(x2d, norm_w, w_bf16, layer):
    t = x2d.shape[0]
    tm = min(t, 512)
    return pl.pallas_call(
        _proj_kernel,
        grid=(t // tm,),
        in_specs=[pl.BlockSpec((tm, D_MODEL), lambda i: (i, 0)),
                  pl.BlockSpec((1, 1, D_MODEL), lambda i: (layer, 0, 0)),
                  pl.BlockSpec((1, P_PAD, D_MODEL), lambda i: (layer, 0, 0))],
        out_specs=pl.BlockSpec((tm, P_PAD), lambda i: (i, 0)),
        out_shape=jax.ShapeDtypeStruct((t, P_PAD), F32),
        compiler_params=pltpu.CompilerParams(dimension_semantics=("arbitrary",),
                                             vmem_limit_bytes=VMEM_LIMIT),
        name="norm_in_proj",
    )(x2d, norm_w, w_bf16)


def _ffn_kernel(x_ref, oa_ref, ob_ref, oc_ref, od_ref, wo_ref, nf_ref, wu_ref, wd_ref, nfin_ref,
                o_ref, *, final):
    mix = jnp.concatenate([oa_ref[...], ob_ref[...], oc_ref[...], od_ref[...]], axis=1)
    x = x_ref[...] + _dot(mix.astype(BF16), wo_ref[0])
    h = (_rms_rows(x) * nf_ref[0]).astype(BF16)
    acc = x
    ft = 1024
    for t in range(D_FF // ft):
        up = _dot(h, wu_ref[0, :, t * ft:(t + 1) * ft])
        up = jnp.square(jnp.maximum(up, 0.0)).astype(BF16)
        acc = acc + _dot(up, wd_ref[0, t * ft:(t + 1) * ft, :])
    if final:
        acc = _rms_rows(acc) * nfin_ref[...]
    o_ref[...] = acc


def _out_ffn(x2d, mixes, wo, nf, wu, wd, nfin, layer):
    t = x2d.shape[0]
    tm = min(t, 512)
    row = lambda i: (i, 0)
    lay = lambda i: (layer, 0, 0)
    return pl.pallas_call(
        functools.partial(_ffn_kernel, final=(layer == DEPTH - 1)),
        grid=(t // tm,),
        in_specs=[pl.BlockSpec((tm, D_MODEL), row)]
                 + [pl.BlockSpec((tm, GROUP_WIDTH), row)] * 4
                 + [pl.BlockSpec((1, D_MODEL, D_MODEL), lay),
                    pl.BlockSpec((1, 1, D_MODEL), lay),
                    pl.BlockSpec((1, D_MODEL, D_FF), lay),
                    pl.BlockSpec((1, D_FF, D_MODEL), lay),
                    pl.BlockSpec((1, D_MODEL), lambda i: (0, 0))],
        out_specs=pl.BlockSpec((tm, D_MODEL), row),
        out_shape=jax.ShapeDtypeStruct((t, D_MODEL), F32),
        compiler_params=pltpu.CompilerParams(dimension_semantics=("arbitrary",),
                                             vmem_limit_bytes=VMEM_LIMIT),
        name="out_proj_ffn",
    )(x2d, *mixes, wo, nf, wu, wd, nfin.reshape(1, D_MODEL))


def _swap_halves(x):
    first = (_iota((1, 128), 1) % HEAD_DIM) < (HEAD_DIM // 2)
    parts = []
    for p in range(GROUP_WIDTH // 128):
        xp = x[:, p * 128:(p + 1) * 128]
        parts.append(jnp.where(first, pltpu.roll(xp, 96, 1), pltpu.roll(xp, 32, 1)))
    return jnp.concatenate(parts, axis=1)


def _conv_silu(xe_ref, halo, x, w, bias, first_chunk, c):
    xe_ref[0:8, :] = jnp.where(first_chunk, jnp.zeros_like(halo), halo)
    xe_ref[8:, :] = x
    y = w[3:4, :] * x
    for j in range(CONV_WIDTH - 1):
        y = y + w[j:j + 1, :] * xe_ref[5 + j:5 + j + c, :]
    if bias is not None:
        y = y + bias
    return _silu(y)


def _ret_prompt_kernel(blk_ref, cos_ref, sin_ref, o_ref, st_ref, s_scr, *, c, n_chunks, nb):
    ci = pl.program_id(1)

    @pl.when(ci == 0)
    def _():
        s_scr[...] = jnp.zeros_like(s_scr)

    cosv, sinv = cos_ref[...], sin_ref[...]
    masks = _head_masks()
    hl = _head_of_lane(GROUP_WIDTH)
    lg = jnp.full((1, GROUP_WIDTH), LOG_GAMMA[0], F32)
    for h in range(1, N_HEADS):
        lg = jnp.where(hl == h, LOG_GAMMA[h], lg)
    ri = _iota((c, 1), 0).astype(F32)
    dij = (_iota((c, c), 0) - _iota((c, c), 1)).astype(F32)
    causal = dij >= 0.0
    decay = jnp.concatenate(
        [jnp.where(causal, jnp.exp(jnp.maximum(dij, 0.0) * LOG_GAMMA[h]), 0.0) for h in range(N_HEADS)],
        axis=0)
    q_scale = jnp.exp((ri + 1.0) * lg)
    k_scale = jnp.exp((float(c - 1) - ri) * lg) * (HEAD_DIM ** -0.5)
    s_scale = jnp.exp(float(c) * lg)
    bd_mask = _block_mask(GROUP_WIDTH, HEAD_DIM, HEAD_DIM)
    ones_bd = _block_ones(GROUP_WIDTH, HEAD_DIM)

    def one_sequence(sq):
        blk = blk_ref[sq]
        rq, rk, rv, rg = (blk[:, i * GROUP_WIDTH:(i + 1) * GROUP_WIDTH] for i in range(4))
        q = rq * cosv + _swap_halves(rq) * sinv
        k = rk * cosv + _swap_halves(rk) * sinv
        v = rv.astype(BF16)
        s = s_scr[sq]
        qk = _dot_nt(_stack_heads(q, masks).astype(BF16), k.astype(BF16))
        o_inter = _dot((q * q_scale).astype(BF16), s.astype(BF16))
        ds = _dot_tn((k * k_scale).astype(BF16), v)
        yield
        scores = qk * (decay * (HEAD_DIM ** -0.5))
        pv = _dot(scores.astype(BF16), v)
        s_scr[sq] = s_scale * s + jnp.where(bd_mask, ds, 0.0)
        yield
        o = _unstack_heads(pv, masks, c) + o_inter
        ss = _head_sumsq(o, ones_bd)
        yield
        o_ref[sq] = o * lax.rsqrt(ss * (1.0 / HEAD_DIM) + EPS) * _silu(rg)

    _round_robin([one_sequence(sq) for sq in range(nb)])

    @pl.when(ci == n_chunks - 1)
    def _():
        for sq in range(nb):
            st_ref[sq] = _extract_blocks(s_scr[sq], GROUP_WIDTH, HEAD_DIM)


PROMPT_SEQS_PER_STEP = 8
GDN_SEQS_PER_STEP = 8
PROMPT_ROWS_PER_STEP = 2048


def _ret_prompt(proj3, cos_t, sin_t, c):
    b, l, _ = proj3.shape
    n = l // c
    nb = math.gcd(b, min(PROMPT_SEQS_PER_STEP, max(1, PROMPT_ROWS_PER_STEP // (l // n))))
    return pl.pallas_call(
        functools.partial(_ret_prompt_kernel, c=c, n_chunks=n, nb=nb),
        grid=(b // nb, n),
        in_specs=[pl.BlockSpec((nb, c, 1024), lambda bi, ci: (bi, ci, COL_RET // 1024)),
                  pl.BlockSpec((c, GROUP_WIDTH), lambda bi, ci: (ci, 0)),
                  pl.BlockSpec((c, GROUP_WIDTH), lambda bi, ci: (ci, 0))],
        out_specs=[pl.BlockSpec((nb, c, GROUP_WIDTH), lambda bi, ci: (bi, ci, 0)),
                   pl.BlockSpec((nb, GROUP_WIDTH, HEAD_DIM), lambda bi, ci: (bi, 0, 0))],
        out_shape=[jax.ShapeDtypeStruct((b, l, GROUP_WIDTH), F32),
                   jax.ShapeDtypeStruct((b, GROUP_WIDTH, HEAD_DIM), F32)],
        scratch_shapes=[pltpu.VMEM((nb, GROUP_WIDTH, GROUP_WIDTH), F32)],
        compiler_params=pltpu.CompilerParams(dimension_semantics=("arbitrary", "arbitrary"),
                                             vmem_limit_bytes=VMEM_LIMIT),
        name="retention_prompt",
    )(proj3, cos_t, sin_t)


def _ssd_prompt_kernel(blk_ref, halo_ref, small_ref, cw_ref, cb_ref, dtb_ref, alog_ref, dskip_ref, nw_ref,
                       o_ref, st_ref, s_scr, xe_scr, *, c, n_chunks, nb):
    ci = pl.program_id(1)

    @pl.when(ci == 0)
    def _():
        s_scr[...] = jnp.zeros_like(s_scr)

    masks = _head_masks()
    causal = _iota((c, c), 0) >= _iota((c, c), 1)
    causal4 = jnp.concatenate([causal] * N_HEADS, axis=0)
    group_mask = _block_mask(GROUP_WIDTH, 128, 128)
    tri = _lower_tri(c)
    neg_a = -jnp.exp(alog_ref[...]) * LOG2E

    def one_sequence(sq):
        blk = blk_ref[sq]
        sz = blk[:, 0:GROUP_WIDTH]
        xbc = _conv_silu(xe_scr.at[sq], halo_ref[sq][:, GROUP_WIDTH:], blk[:, GROUP_WIDTH:], cw_ref[...],
                         cb_ref[...], ci == 0, c)
        xs = xbc[:, 0:256]
        bmat = xbc[:, 256:512].astype(BF16)
        cmat = xbc[:, 512:768].astype(BF16)
        s = s_scr[sq]
        cb = [_dot_nt(cmat[:, gi * 128:(gi + 1) * 128], bmat[:, gi * 128:(gi + 1) * 128]) for gi in range(2)]
        y_inter = _dot(cmat, s.astype(BF16))
        dt = _softplus(_expand_small(small_ref[sq], SMALL_SDT) + dtb_ref[...])
        yield
        g = _exact_dot_left(tri, neg_a * dt)
        yield
        g_last = g[c - 1:c, :]
        da, db = _decay_diff_operands(g)
        diff = _dot_nt(_stack_heads(da, masks).astype(BF16), db.astype(BF16))
        v = xs * dt
        vend = v * jnp.exp2(g_last - g)
        ds = _dot_tn(bmat, vend.astype(BF16))
        yield
        decay = jnp.where(causal4, jnp.exp2(diff), 0.0)
        scores = jnp.concatenate([cb[0], cb[0], cb[1], cb[1]], axis=0) * decay
        pv = _dot(scores.astype(BF16), v.astype(BF16))
        s_scr[sq] = jnp.exp2(g_last) * s + jnp.where(group_mask, ds, 0.0)
        yield
        y = _unstack_heads(pv, masks, c) + y_inter * jnp.exp2(g)
        y = (y + dskip_ref[...] * xs) * _silu(sz)
        halves = [_rms_rows(y[:, gi * 128:(gi + 1) * 128]) for gi in range(2)]
        o_ref[sq] = jnp.concatenate(halves, axis=1) * nw_ref[...]

    _round_robin([one_sequence(sq) for sq in range(nb)])

    @pl.when(ci == n_chunks - 1)
    def _():
        for sq in range(nb):
            for h in range(N_HEADS):
                gi = h // 2
                rows = jnp.where(masks[h], s_scr[sq, gi * 128:(gi + 1) * 128, :], 0.0)
                st_ref[sq, h * 128:(h + 1) * 128, :] = _extract_blocks(rows, 128, HEAD_DIM)


def _lane_rep(p):
    return jnp.repeat(p.astype(F32), HEAD_DIM).reshape(1, GROUP_WIDTH)


def _ssd_prompt(proj3, conv_w, conv_b, dt_bias, a_log, d_skip, norm_w, c):
    b, l, _ = proj3.shape
    n = l // c
    fixed = lambda bi, ci: (0, 0)
    nb = math.gcd(b, min(PROMPT_SEQS_PER_STEP, max(1, PROMPT_ROWS_PER_STEP // (l // n))))
    return pl.pallas_call(
        functools.partial(_ssd_prompt_kernel, c=c, n_chunks=n, nb=nb),
        grid=(b // nb, n),
        in_specs=[pl.BlockSpec((nb, c, 1024), lambda bi, ci: (bi, ci, COL_SSD // 1024)),
                  pl.BlockSpec((nb, 8, 1024), lambda bi, ci: (bi, jnp.maximum(ci * (c // 8) - 1, 0), COL_SSD // 1024)),
                  pl.BlockSpec((nb, c, 128), lambda bi, ci: (bi, ci, COL_SMALL // 128)),
                  pl.BlockSpec((CONV_WIDTH, 768), fixed),
                  pl.BlockSpec((1, 768), fixed),
                  pl.BlockSpec((1, GROUP_WIDTH), fixed),
                  pl.BlockSpec((1, GROUP_WIDTH), fixed),
                  pl.BlockSpec((1, GROUP_WIDTH), fixed),
                  pl.BlockSpec((1, GROUP_WIDTH), fixed)],
        out_specs=[pl.BlockSpec((nb, c, GROUP_WIDTH), lambda bi, ci: (bi, ci, 0)),
                   pl.BlockSpec((nb, N_HEADS * SSD_STATE, HEAD_DIM), lambda bi, ci: (bi, 0, 0))],
        out_shape=[jax.ShapeDtypeStruct((b, l, GROUP_WIDTH), F32),
                   jax.ShapeDtypeStruct((b, N_HEADS * SSD_STATE, HEAD_DIM), F32)],
        scratch_shapes=[pltpu.VMEM((nb, GROUP_WIDTH, GROUP_WIDTH), F32),
                        pltpu.VMEM((nb, c + 8, 768), F32)],
        compiler_params=pltpu.CompilerParams(dimension_semantics=("arbitrary", "arbitrary"),
                                             vmem_limit_bytes=VMEM_LIMIT),
        name="ssd_prompt",
    )(proj3, proj3, proj3, conv_w, conv_b.reshape(1, 768), _lane_rep(dt_bias), _lane_rep(a_log),
      _lane_rep(d_skip), norm_w.reshape(1, GROUP_WIDTH))


def _gdn_prompt_kernel(blk_ref, halo_ref, small_ref, cw_ref, alog_ref, dtb_ref, nw_ref,
                       o_ref, st_ref, s_scr, xe_scr, m_scr, *, c, n_chunks, nb):
    ci = pl.program_id(1)
    hc = 2 * c
    n_lvl = int(math.log2(c))

    @pl.when(ci == 0)
    def _():
        s_scr[...] = jnp.zeros_like(s_scr)

    @pl.when((pl.program_id(0) == 0) & (ci == 0))
    def _():
        rr = _iota((hc, hc), 0)
        cc = _iota((hc, hc), 1)
        same = (rr // c) == (cc // c)
        m_scr[0] = (same & (rr >= cc)).astype(F32)
        m_scr[1] = (same & (rr > cc)).astype(F32)
        for lv in range(n_lvl):
            sz = 1 << lv
            off = ((rr // (2 * sz)) == (cc // (2 * sz))) & (((rr // sz) % 2) == 1) & (((cc // sz) % 2) == 0)
            m_scr[2 + lv] = off.astype(F32)

    ones_bd = _block_ones(GROUP_WIDTH, HEAD_DIM)
    bd_mask = _block_mask(GROUP_WIDTH, HEAD_DIM, HEAD_DIM)
    masks = _head_masks()
    pair_masks = [masks[0:2], masks[2:4]]
    tri = _lower_tri(c)
    neg_a = -jnp.exp(alog_ref[...]) * LOG2E

    def one_sequence(sq):
        blk = blk_ref[sq]
        gz = blk[:, 768:1024]
        qkv = _conv_silu(xe_scr.at[sq], halo_ref[sq][:, 0:768], blk[:, 0:768], cw_ref[...], None, ci == 0, c)
        gq, gk, v = qkv[:, 0:256], qkv[:, 256:512], qkv[:, 512:768]
        q = gq * lax.rsqrt(_head_sumsq(gq, ones_bd) + EPS) * (HEAD_DIM ** -0.5)
        k = gk * lax.rsqrt(_head_sumsq(gk, ones_bd) + EPS)
        yield
        small = small_ref[sq]
        beta = _sigmoid(_expand_small(small, SMALL_GB))
        g = _exact_dot_left(tri, neg_a * _softplus(_expand_small(small, SMALL_GA) + dtb_ref[...]))
        g_last = g[c - 1:c, :]
        eg = jnp.exp2(g)
        yield
        da, db = _decay_diff_operands(g)
        bk = beta * k
        bkg = bk * eg
        bv = beta * v
        a_mat, p_mat, x = [], [], []
        for pm in pair_masks:
            diff = _dot_nt(_stack_heads(da, pm).astype(BF16), _stack_heads(db, pm).astype(BF16))
            k_st = _stack_heads(k, pm).astype(BF16)
            kk = _dot_nt(_stack_heads(bk, pm).astype(BF16), k_st)
            qk = _dot_nt(_stack_heads(q, pm).astype(BF16), k_st)
            decay = jnp.exp2(jnp.minimum(diff, 0.0))
            a_mat.append(kk * (decay * m_scr[1]))
            p_mat.append((qk * (decay * m_scr[0])).astype(BF16))
            x.append(jnp.concatenate([_stack_heads(bkg, pm), _stack_heads(bv, pm)], axis=1))
        yield

        n_mat = [-(a * m_scr[2]) for a in a_mat]
        for lv in range(1, n_lvl):
            a_off = [a * m_scr[2 + lv] for a in a_mat]
            m = [ao + _dot(ao.astype(BF16), n.astype(BF16)) for ao, n in zip(a_off, n_mat)]
            yield
            n_mat = [n - mm - _dot(n.astype(BF16), mm.astype(BF16)) for n, mm in zip(n_mat, m)]
            yield
        x = [xx + _dot(n.astype(BF16), xx.astype(BF16)) for xx, n in zip(x, n_mat)]
        yield
        w = x[0][0:c, 0:256] + x[0][c:2 * c, 0:256] + x[1][0:c, 0:256] + x[1][c:2 * c, 0:256]
        u0 = x[0][0:c, 256:512] + x[0][c:2 * c, 256:512] + x[1][0:c, 256:512] + x[1][c:2 * c, 256:512]

        s = s_scr[sq]
        s_bf = s.astype(BF16)
        u = u0 - _dot(w.astype(BF16), s_bf)
        o = _dot((q * eg).astype(BF16), s_bf)
        yield
        pu = [_dot(pmat, _stack_heads(u, pm).astype(BF16)) for pmat, pm in zip(p_mat, pair_masks)]
        kend = k * jnp.exp2(g_last - g)
        ds = _dot_tn(kend.astype(BF16), u.astype(BF16))
        yield
        for pu_p in pu:
            o = o + pu_p[0:c] + pu_p[c:2 * c]
        s_scr[sq] = jnp.exp2(g_last) * s + jnp.where(bd_mask, ds, 0.0)
        ss = _head_sumsq(o, ones_bd)
        o_ref[sq] = o * lax.rsqrt(ss * (1.0 / HEAD_DIM) + EPS) * nw_ref[...] * _silu(gz)

    _round_robin([one_sequence(sq) for sq in range(nb)])

    @pl.when(ci == n_chunks - 1)
    def _():
        for sq in range(nb):
            st_ref[sq] = _extract_blocks(s_scr[sq], GROUP_WIDTH, HEAD_DIM)


def _gdn_prompt(proj3, conv_w, a_log, dt_bias, norm_w, c):
    b, l, _ = proj3.shape
    n = l // c
    fixed = lambda bi, ci: (0, 0)
    nb = math.gcd(b, GDN_SEQS_PER_STEP)
    hc = 2 * c
    return pl.pallas_call(
        functools.partial(_gdn_prompt_kernel, c=c, n_chunks=n, nb=nb),
        grid=(b // nb, n),
        in_specs=[pl.BlockSpec((nb, c, 1024), lambda bi, ci: (bi, ci, COL_GDN // 1024)),
                  pl.BlockSpec((nb, 8, 1024), lambda bi, ci: (bi, jnp.maximum(ci * (c // 8) - 1, 0), COL_GDN // 1024)),
                  pl.BlockSpec((nb, c, 128), lambda bi, ci: (bi, ci, COL_SMALL // 128)),
                  pl.BlockSpec((CONV_WIDTH, 768), fixed),
                  pl.BlockSpec((1, GROUP_WIDTH), fixed),
                  pl.BlockSpec((1, GROUP_WIDTH), fixed),
                  pl.BlockSpec((1, GROUP_WIDTH), fixed)],
        out_specs=[pl.BlockSpec((nb, c, GROUP_WIDTH), lambda bi, ci: (bi, ci, 0)),
                   pl.BlockSpec((nb, GROUP_WIDTH, HEAD_DIM), lambda bi, ci: (bi, 0, 0))],
        out_shape=[jax.ShapeDtypeStruct((b, l, GROUP_WIDTH), F32),
                   jax.ShapeDtypeStruct((b, GROUP_WIDTH, HEAD_DIM), F32)],
        scratch_shapes=[pltpu.VMEM((nb, GROUP_WIDTH, GROUP_WIDTH), F32),
                        pltpu.VMEM((nb, c + 8, 768), F32),
                        pltpu.VMEM((2 + int(math.log2(c)), hc, hc), F32)],
        compiler_params=pltpu.CompilerParams(dimension_semantics=("arbitrary", "arbitrary"),
                                             vmem_limit_bytes=VMEM_LIMIT),
        name="gdn_prompt",
    )(proj3, proj3, proj3, conv_w, _lane_rep(a_log), _lane_rep(dt_bias),
      jnp.tile(norm_w.astype(F32), N_HEADS).reshape(1, GROUP_WIDTH))


HGRN_SUB = 16
LOG2E = 1.4426950408889634


def _hgrn_lower_bound(logits, layer):
    rows = [logits[d:d + 1, :] for d in range(DEPTH)]
    mx = functools.reduce(jnp.maximum, rows)
    es = [jnp.exp(x - mx) for x in rows]
    tot = functools.reduce(lambda a, b: a + b, es)
    sm = [e / tot for e in es]
    acc = sm[0]
    for d in range(1, layer + 1):
        acc = acc + sm[d]
    return acc - sm[0]


def _hgrn_prompt_kernel(blk_ref, lb_ref, nw_ref, o_ref, st_ref, s_scr, *, r, n_chunks, layer, nb):
    ci = pl.program_id(1)
    sub = HGRN_SUB
    n_sub = r // sub

    @pl.when(ci == 0)
    def _():
        s_scr[...] = jnp.zeros_like(s_scr)

    lb = _hgrn_lower_bound(lb_ref[...], layer)
    rr = _iota((r, r), 0)
    cc = _iota((r, r), 1)
    same_sub = (rr // sub) == (cc // sub)
    cum_sel = (same_sub & (rr >= cc)).astype(BF16)
    tot_sel = same_sub.astype(BF16)
    ones_bd = _block_ones(GROUP_WIDTH, HEAD_DIM)
    bd_mask = _block_mask(GROUP_WIDTH, HEAD_DIM, HEAD_DIM)
    half = sub // 2
    i8 = _iota((half, 1), 0)

    def one_sequence(sq):
        blk = blk_ref[sq]
        hq, hf, hi, hg = (blk[:, i * GROUP_WIDTH:(i + 1) * GROUP_WIDTH] for i in range(4))
        f = lb + (1.0 - lb) * _sigmoid(hf)
        q = _sigmoid(hq)
        k = 1.0 - f
        v = hi
        logf = jnp.log(f)
        g = _exact_dot_left(cum_sel, logf)
        g_tot = _exact_dot_left(tot_sel, logf)
        yield
        a2 = (g + jnp.log(q)) * LOG2E
        h2 = (g - jnp.log(k)) * LOG2E
        gt2 = g_tot * LOG2E
        qt = jnp.exp2(a2).astype(BF16)
        kh = jnp.exp2(gt2 - h2).astype(BF16)
        v_bf = v.astype(BF16)

        s = s_scr[sq]
        outs = []
        for j in range(n_sub):
            lo = j * sub
            v_j = v[lo:lo + sub]
            a_lo, a_hi, h_j = a2[lo:lo + half], a2[lo + half:lo + sub], h2[lo:lo + sub]
            lo_blocks, hi_blocks = [], []
            for jj in range(sub):
                h_row = h_j[jj:jj + 1, :]
                if jj < half:
                    e_lo = jnp.exp2(a_lo - h_row)
                    lo_blocks.append(e_lo if jj == 0 else jnp.where(i8 >= jj, e_lo, 0.0))
                    hi_blocks.append(jnp.exp2(a_hi - h_row))
                else:
                    e_hi = jnp.exp2(a_hi - h_row)
                    hi_blocks.append(e_hi if jj == half else jnp.where(i8 >= jj - half, e_hi, 0.0))
            sc = _dot(jnp.concatenate(lo_blocks + hi_blocks, axis=0).astype(BF16), ones_bd)
            o_inter = _dot_nt(qt[lo:lo + sub], s.astype(BF16))
            ds = _dot_tn(v_bf[lo:lo + sub], kh[lo:lo + sub])
            yield
            n_lo = half * half
            o_lo = sc[0:half] * v_j[0:1, :]
            o_hi = sc[n_lo:n_lo + half] * v_j[0:1, :]
            for jj in range(1, sub):
                if jj < half:
                    o_lo = o_lo + sc[jj * half:(jj + 1) * half] * v_j[jj:jj + 1, :]
                o_hi = o_hi + sc[n_lo + jj * half:n_lo + (jj + 1) * half] * v_j[jj:jj + 1, :]
            outs.append(jnp.concatenate([o_lo, o_hi], axis=0) + o_inter)
            s = jnp.exp2(gt2[lo:lo + 1, :]) * s + jnp.where(bd_mask, ds, 0.0)
        s_scr[sq] = s

        o = jnp.concatenate(outs, axis=0)
        ss = _head_sumsq(o, ones_bd)
        yield
        o_ref[sq] = o * lax.rsqrt(ss * (1.0 / HEAD_DIM) + EPS) * nw_ref[...] * _silu(hg)

    _round_robin([one_sequence(sq) for sq in range(nb)])

    @pl.when(ci == n_chunks - 1)
    def _():
        for sq in range(nb):
            st_ref[sq] = _extract_blocks(s_scr[sq].T, GROUP_WIDTH, HEAD_DIM)


def _hgrn_prompt(proj3, lb_logits, norm_w, layer, r):
    b, l, _ = proj3.shape
    n = l // r
    fixed = lambda bi, ci: (0, 0)
    nb = math.gcd(b, min(PROMPT_SEQS_PER_STEP, max(1, PROMPT_ROWS_PER_STEP // (l // n))))
    return pl.pallas_call(
        functools.partial(_hgrn_prompt_kernel, r=r, n_chunks=n, layer=layer, nb=nb),
        grid=(b // nb, n),
        in_specs=[pl.BlockSpec((nb, r, 1024), lambda bi, ci: (bi, ci, COL_HGRN // 1024)),
                  pl.BlockSpec((DEPTH, GROUP_WIDTH), fixed),
                  pl.BlockSpec((1, GROUP_WIDTH), fixed)],
        out_specs=[pl.BlockSpec((nb, r, GROUP_WIDTH), lambda bi, ci: (bi, ci, 0)),
                   pl.BlockSpec((nb, GROUP_WIDTH, HEAD_DIM), lambda bi, ci: (bi, 0, 0))],
        out_shape=[jax.ShapeDtypeStruct((b, l, GROUP_WIDTH), F32),
                   jax.ShapeDtypeStruct((b, GROUP_WIDTH, HEAD_DIM), F32)],
        scratch_shapes=[pltpu.VMEM((nb, GROUP_WIDTH, GROUP_WIDTH), F32)],
        compiler_params=pltpu.CompilerParams(dimension_semantics=("arbitrary", "arbitrary"),
                                             vmem_limit_bytes=VMEM_LIMIT),
        name="hgrn_prompt",
    )(proj3, lb_logits.astype(F32), jnp.tile(norm_w.astype(F32), N_HEADS).reshape(1, GROUP_WIDTH))


DEC_SEQS = 128
DEC_LEN = 4


def _head_rows(h):
    return pl.ds(pl.multiple_of(h * HEAD_DIM, HEAD_DIM), HEAD_DIM)


def _recur_head(load_s, store_s, n_keys, decay_fn, k_fn, q_fn, v_blocks):
    def body(kk, accs):
        s = load_s(kk)
        accs = list(accs)
        for t in range(DEC_LEN):
            s = decay_fn(t, kk) * s + k_fn(t, kk) * v_blocks[t]
            accs[t] = accs[t] + q_fn(t, kk) * s
        store_s(kk, s)
        return tuple(accs)

    zero = jnp.zeros((HEAD_DIM, DEC_SEQS), F32)
    return lax.fori_loop(0, n_keys, body, (zero,) * DEC_LEN)


def _dec_ret_kernel(blk_ref, cos_ref, sin_ref, st_ref, o_ref, so_ref, q_scr, k_scr, v_scr, o_scr):
    h = pl.program_id(0)

    @pl.when(h == 0)
    def _():
        for t in range(DEC_LEN):
            blk = blk_ref[t]
            rq, rk, rv = blk[:, 0:256], blk[:, 256:512], blk[:, 512:768]
            cosv, sinv = cos_ref[t:t + 1, :], sin_ref[t:t + 1, :]
            q_scr[t] = (rq * cosv + _swap_halves(rq) * sinv).T
            k_scr[t] = ((rk * cosv + _swap_halves(rk) * sinv) * (HEAD_DIM ** -0.5)).T
            v_scr[t] = rv.T

    lg = jnp.where(h == 0, LOG_GAMMA[0], jnp.where(h == 1, LOG_GAMMA[1], jnp.where(h == 2, LOG_GAMMA[2], LOG_GAMMA[3])))
    gamma = jnp.exp(jnp.full((1, DEC_SEQS), lg, F32))
    hr = _head_rows(h)
    v_blocks = [v_scr[t, hr, :] for t in range(DEC_LEN)]
    accs = _recur_head(
        lambda kk: st_ref[0, 0, kk], functools.partial(_store_state, so_ref), HEAD_DIM,
        lambda t, kk: gamma,
        lambda t, kk: k_scr[t, pl.ds(h * HEAD_DIM + kk, 1), :],
        lambda t, kk: q_scr[t, pl.ds(h * HEAD_DIM + kk, 1), :],
        v_blocks)
    for t in range(DEC_LEN):
        o_scr[t, hr, :] = accs[t]

    @pl.when(h == N_HEADS - 1)
    def _():
        ones_bd = _block_ones(GROUP_WIDTH, HEAD_DIM)
        for t in range(DEC_LEN):
            o = o_scr[t].T
            ss = _head_sumsq(o, ones_bd)
            o_ref[t] = o * lax.rsqrt(ss * (1.0 / HEAD_DIM) + EPS) * _silu(blk_ref[t][:, 768:1024])


def _store_state(so_ref, kk, s):
    so_ref[0, 0, kk] = s


def _without_ref(kernel_fn, idx):
    def wrapped(*refs):
        return kernel_fn(*refs[:idx], *refs[idx + 1:])
    return wrapped


def _dec_call(kernel_fn, name, col, ins, in_specs, n_tok_scr, state_view, layer, carried, extra_scratch=()):
    blk_spec = pl.BlockSpec((DEC_LEN, DEC_SEQS, 1024), lambda h: (0, 0, col // 1024))
    st_spec = pl.BlockSpec((1, 1) + state_view.shape[2:], lambda h: (layer, h, 0, 0, 0))
    tok_scr = pltpu.VMEM((DEC_LEN, GROUP_WIDTH, DEC_SEQS), F32)
    ins = tuple(ins) + (state_view, carried)
    specs = [blk_spec] + in_specs + [st_spec, pl.BlockSpec(memory_space=pl.ANY)]
    return pl.pallas_call(
        _without_ref(kernel_fn, len(ins) - 1),
        grid=(N_HEADS,),
        in_specs=specs,
        out_specs=[pl.BlockSpec((DEC_LEN, DEC_SEQS, GROUP_WIDTH), lambda h: (0, 0, 0)), st_spec],
        out_shape=[jax.ShapeDtypeStruct((DEC_LEN, DEC_SEQS, GROUP_WIDTH), F32),
                   jax.ShapeDtypeStruct(state_view.shape, F32)],
        scratch_shapes=[tok_scr] * n_tok_scr + list(extra_scratch),
        input_output_aliases={len(ins) - 1: 1},
        compiler_params=pltpu.CompilerParams(dimension_semantics=("arbitrary",), vmem_limit_bytes=VMEM_LIMIT),
        name=name,
    )(*ins)


def _fixed1(shape):
    return pl.BlockSpec(shape, lambda h: (0,) * len(shape))


def _dec_ret(projd, cos_t, sin_t, state_view, layer, carried):
    return _dec_call(_dec_ret_kernel, "retention_decode", COL_RET, (projd, cos_t, sin_t),
                     [_fixed1((DEC_LEN, GROUP_WIDTH)), _fixed1((DEC_LEN, GROUP_WIDTH))], 4,
                     state_view, layer, carried)


def _dec_hgrn_kernel(blk_ref, lb_ref, nw_ref, st_ref, o_ref, so_ref, q_scr, k_scr, v_scr, f_scr, o_scr, *, layer):
    h = pl.program_id(0)

    @pl.when(h == 0)
    def _():
        lb = _hgrn_lower_bound(lb_ref[...], layer)
        for t in range(DEC_LEN):
            blk = blk_ref[t]
            f = lb + (1.0 - lb) * _sigmoid(blk[:, 256:512])
            q_scr[t] = _sigmoid(blk[:, 0:256]).T
            k_scr[t] = (1.0 - f).T
            v_scr[t] = blk[:, 512:768].T
            f_scr[t] = f.T

    hr = _head_rows(h)
    v_blocks = [v_scr[t, hr, :] for t in range(DEC_LEN)]
    row = lambda scr: (lambda t, kk: scr[t, pl.ds(h * HEAD_DIM + kk, 1), :])
    accs = _recur_head(lambda kk: st_ref[0, 0, kk], functools.partial(_store_state, so_ref), HEAD_DIM,
                       row(f_scr), row(k_scr), row(q_scr), v_blocks)
    for t in range(DEC_LEN):
        o_scr[t, hr, :] = accs[t]

    @pl.when(h == N_HEADS - 1)
    def _():
        ones_bd = _block_ones(GROUP_WIDTH, HEAD_DIM)
        for t in range(DEC_LEN):
            o = o_scr[t].T
            ss = _head_sumsq(o, ones_bd)
            o_ref[t] = o * lax.rsqrt(ss * (1.0 / HEAD_DIM) + EPS) * nw_ref[...] * _silu(blk_ref[t][:, 768:1024])


def _dec_hgrn(projd, lb_logits, norm_w, state_view, layer, carried):
    return _dec_call(functools.partial(_dec_hgrn_kernel, layer=layer), "hgrn_decode", COL_HGRN,
                     (projd, lb_logits.astype(F32), jnp.tile(norm_w.astype(F32), N_HEADS).reshape(1, GROUP_WIDTH)),
                     [_fixed1((DEPTH, GROUP_WIDTH)), _fixed1((1, GROUP_WIDTH))], 5, state_view, layer, carried)


def _hist_spec(layer):
    return pl.BlockSpec((1, CONV_WIDTH - 1, DEC_SEQS, 768), lambda h: (layer, 0, 0, 0))


def _dec_conv_silu(hist_ref, xs, w, bias):
    xe = [hist_ref[0, j] for j in range(CONV_WIDTH - 1)] + xs
    out = []
    for t in range(DEC_LEN):
        y = xe[t] * w[0:1, :]
        for j in range(1, CONV_WIDTH):
            y = y + xe[t + j] * w[j:j + 1, :]
        if bias is not None:
            y = y + bias
        out.append(_silu(y))
    return out


def _dec_ssd_kernel(blk_ref, small_ref, hist_ref, cw_ref, cb_ref, dtb_ref, alog_ref, dskip_ref, nw_ref, st_ref,
                    o_ref, so_ref, c_scr, b_scr, v_scr, a_scr, o_scr, x_scr):
    h = pl.program_id(0)

    @pl.when(h == 0)
    def _():
        xbc = _dec_conv_silu(hist_ref, [blk_ref[t][:, 256:1024] for t in range(DEC_LEN)], cw_ref[...], cb_ref[...])
        for t in range(DEC_LEN):
            xs = xbc[t][:, 0:256]
            dt = _softplus(_expand_small(small_ref[t], SMALL_SDT) + dtb_ref[...])
            x_scr[t] = xs
            v_scr[t] = (xs * dt).T
            b_scr[t] = xbc[t][:, 256:512].T
            c_scr[t] = xbc[t][:, 512:768].T
            a_scr[t] = jnp.exp(-jnp.exp(alog_ref[...]) * dt).T

    hr = _head_rows(h)
    g0 = (h // 2) * SSD_STATE
    v_blocks = [v_scr[t, hr, :] for t in range(DEC_LEN)]
    accs = _recur_head(
        lambda kk: st_ref[0, 0, kk], functools.partial(_store_state, so_ref), SSD_STATE,
        lambda t, kk: a_scr[t, pl.ds(h * HEAD_DIM, 1), :],
        lambda t, kk: b_scr[t, pl.ds(g0 + kk, 1), :],
        lambda t, kk: c_scr[t, pl.ds(g0 + kk, 1), :],
        v_blocks)
    for t in range(DEC_LEN):
        o_scr[t, hr, :] = accs[t]

    @pl.when(h == N_HEADS - 1)
    def _():
        for t in range(DEC_LEN):
            y = (o_scr[t].T + dskip_ref[...] * x_scr[t]) * _silu(blk_ref[t][:, 0:256])
            halves = [_rms_rows(y[:, gi * 128:(gi + 1) * 128]) for gi in range(2)]
            o_ref[t] = jnp.concatenate(halves, axis=1) * nw_ref[...]


def _dec_ssd(projd, hist, conv_w, conv_b, dt_bias, a_log, d_skip, norm_w, state_view, layer, carried):
    small_spec = pl.BlockSpec((DEC_LEN, DEC_SEQS, 128), lambda h: (0, 0, COL_SMALL // 128))
    return _dec_call(_dec_ssd_kernel, "ssd_decode", COL_SSD,
                     (projd, projd, hist, conv_w, conv_b.reshape(1, 768), _lane_rep(dt_bias), _lane_rep(a_log),
                      _lane_rep(d_skip), norm_w.reshape(1, GROUP_WIDTH)),
                     [small_spec, _hist_spec(layer), _fixed1((CONV_WIDTH, 768)),
                      _fixed1((1, 768))] + [_fixed1((1, GROUP_WIDTH))] * 4, 5, state_view, layer, carried,
                     extra_scratch=[pltpu.VMEM((DEC_LEN, DEC_SEQS, GROUP_WIDTH), F32)])


def _dec_gdn_kernel(blk_ref, small_ref, hist_ref, cw_ref, alog_ref, dtb_ref, nw_ref, st_ref,
                    o_ref, so_ref, q_scr, k_scr, v_scr, a_scr, b_scr, o_scr):
    h = pl.program_id(0)

    @pl.when(h == 0)
    def _():
        ones_bd = _block_ones(GROUP_WIDTH, HEAD_DIM)
        qkv = _dec_conv_silu(hist_ref, [blk_ref[t][:, 0:768] for t in range(DEC_LEN)], cw_ref[...], None)
        for t in range(DEC_LEN):
            gq, gk, gv = qkv[t][:, 0:256], qkv[t][:, 256:512], qkv[t][:, 512:768]
            q_scr[t] = (gq * lax.rsqrt(_head_sumsq(gq, ones_bd) + EPS) * (HEAD_DIM ** -0.5)).T
            k_scr[t] = (gk * lax.rsqrt(_head_sumsq(gk, ones_bd) + EPS)).T
            v_scr[t] = gv.T
            small = small_ref[t]
            b_scr[t] = _sigmoid(_expand_small(small, SMALL_GB)).T
            la = -jnp.exp(alog_ref[...]) * _softplus(_expand_small(small, SMALL_GA) + dtb_ref[...])
            a_scr[t] = jnp.exp(la).T

    hr = _head_rows(h)
    one_row = pl.ds(h * HEAD_DIM, 1)
    zero = jnp.zeros((HEAD_DIM, DEC_SEQS), F32)
    for t in range(DEC_LEN):
        a = a_scr[t, one_row, :]
        cur = st_ref if t == 0 else so_ref

        def kts(kk, r):
            return r + k_scr[t, pl.ds(h * HEAD_DIM + kk, 1), :] * cur[0, 0, kk]

        r = lax.fori_loop(0, HEAD_DIM, kts, zero)
        u = b_scr[t, one_row, :] * (v_scr[t, hr, :] - a * r)

        def upd(kk, acc):
            s = a * cur[0, 0, kk] + k_scr[t, pl.ds(h * HEAD_DIM + kk, 1), :] * u
            so_ref[0, 0, kk] = s
            return acc + q_scr[t, pl.ds(h * HEAD_DIM + kk, 1), :] * s

        o_scr[t, hr, :] = lax.fori_loop(0, HEAD_DIM, upd, zero)

    @pl.when(h == N_HEADS - 1)
    def _():
        ones_bd = _block_ones(GROUP_WIDTH, HEAD_DIM)
        for t in range(DEC_LEN):
            o = o_scr[t].T
            ss = _head_sumsq(o, ones_bd)
            o_ref[t] = o * lax.rsqrt(ss * (1.0 / HEAD_DIM) + EPS) * nw_ref[...] * _silu(blk_ref[t][:, 768:1024])


def _dec_gdn(projd, hist, conv_w, a_log, dt_bias, norm_w, state_view, layer, carried):
    small_spec = pl.BlockSpec((DEC_LEN, DEC_SEQS, 128), lambda h: (0, 0, COL_SMALL // 128))
    return _dec_call(_dec_gdn_kernel, "gdn_decode", COL_GDN,
                     (projd, projd, hist, conv_w, _lane_rep(a_log), _lane_rep(dt_bias),
                      jnp.tile(norm_w.astype(F32), N_HEADS).reshape(1, GROUP_WIDTH)),
                     [small_spec, _hist_spec(layer), _fixed1((CONV_WIDTH, 768))]
                     + [_fixed1((1, GROUP_WIDTH))] * 3, 6, state_view, layer, carried)


def _reorder_cols(w):
    lead = w.shape[:-1]
    small = jnp.concatenate([w[..., 2048:2056], w[..., 4104:4108],
                             jnp.zeros(lead + (P_PAD - COL_SMALL - 12,), w.dtype)], axis=-1)
    return jnp.concatenate([w[..., 0:2048], w[..., 2056:4104], small], axis=-1)


def _prep_w_in(w):
    return _reorder_cols(w).astype(BF16)


def _w_in_prep_kernel(w_ref, tail_ref, o_ref):
    x = w_ref[0]
    o_ref[0, :, 0:COL_RET] = x[:, 0:COL_RET].astype(BF16)
    o_ref[0, :, COL_RET:COL_SMALL] = x[:, COL_RET + 8:COL_SMALL + 8].astype(BF16)
    lane = _iota((x.shape[0], 128), 1)
    small = jnp.where(lane < 8, x[:, COL_RET:COL_RET + 128], jnp.where(lane < 12, tail_ref[0], 0.0))
    o_ref[0, :, COL_SMALL:P_PAD] = small.astype(BF16)


def _prep_w_in_all(w_in):
    d, r, p = w_in.shape
    rows = 256
    return pl.pallas_call(
        _w_in_prep_kernel,
        grid=(d, r // rows),
        in_specs=[pl.BlockSpec((1, rows, p), lambda l, i: (l, i, 0)),
                  pl.BlockSpec((1, rows, 128), lambda l, i: (l, i, COL_SMALL // 128))],
        out_specs=pl.BlockSpec((1, rows, P_PAD), lambda l, i: (l, i, 0)),
        out_shape=jax.ShapeDtypeStruct((d, r, P_PAD), BF16),
        compiler_params=pltpu.CompilerParams(dimension_semantics=("arbitrary", "arbitrary"),
                                             vmem_limit_bytes=VMEM_LIMIT),
        name="w_in_prep",
    )(w_in, w_in)


def _prep_w_in_t(w_in):
    wt = jnp.transpose(w_in, (0, 2, 1))
    pad = jnp.zeros((wt.shape[0], P_PAD - COL_SMALL - 12, wt.shape[2]), wt.dtype)
    return jnp.concatenate([wt[:, 0:2048], wt[:, 2056:4104], wt[:, 2048:2056], wt[:, 4104:4108], pad],
                           axis=1).astype(BF16)


def _rotary_tables(pos):
    half = HEAD_DIM // 2
    inv_freq = RET_THETA ** (-jnp.arange(half, dtype=F32) / half)
    ang = pos.astype(F32)[:, None] * inv_freq[None, :]
    cos, sin = jnp.cos(ang), jnp.sin(ang)
    cos_t = jnp.tile(cos, (1, 2 * N_HEADS))
    sin_t = jnp.tile(jnp.concatenate([-sin, sin], axis=1), (1, N_HEADS))
    return cos_t, sin_t


RET_CHUNK = 256
SSD_CHUNK = 256
GDN_CHUNK = 64
HGRN_ROWS = 128


def _forward(x_prompt, x_sample, states, p, past_len):
    st_hg, st_gd, st_gc, st_rt, st_sd, st_sc = states
    bp, lp, _ = x_prompt.shape
    nd, ld, _ = x_sample.shape
    xp = x_prompt.astype(F32).reshape(bp * lp, D_MODEL)
    xd = jnp.transpose(x_sample.astype(F32), (1, 0, 2)).reshape(ld * nd, D_MODEL)
    cos_p, sin_p = _rotary_tables(jnp.arange(lp))
    cos_d, sin_d = _rotary_tables(past_len + jnp.arange(ld))
    outs = {k: [] for k in ("hp", "gp", "gcp", "gcs", "rp", "sp", "scp", "scs")}
    w_in_all = _prep_w_in_t(p["w_in"].astype(F32))
    wo, wu, wd = (p[k].astype(BF16) for k in ("w_out", "w_up", "w_down"))
    norm_mix = p["norm_mix"].astype(F32).reshape(DEPTH, 1, D_MODEL)
    norm_ffn = p["norm_ffn"].astype(F32).reshape(DEPTH, 1, D_MODEL)
    sv_hg, sv_gd, sv_rt, sv_sd = (jnp.transpose(s.astype(F32), (0, 2, 3, 4, 1)) for s in (st_hg, st_gd, st_rt, st_sd))
    hist_g = jnp.transpose(st_gc.astype(F32), (0, 2, 1, 3))
    hist_s = jnp.transpose(st_sc.astype(F32), (0, 2, 1, 3))
    new_hg, new_gd, new_rt, new_sd = (jnp.zeros(s.shape, F32) for s in (sv_hg, sv_gd, sv_rt, sv_sd))
    for l in range(DEPTH):
        pp = _proj(xp, norm_mix, w_in_all, l).reshape(bp, lp, P_PAD)
        pd = _proj(xd, norm_mix, w_in_all, l).reshape(ld, nd, P_PAD)

        oa, sa = _hgrn_prompt(pp, p["hgrn_lb_logits"], p["hgrn_norm"][l], l, HGRN_ROWS)
        ob, sb = _gdn_prompt(pp, p["gdn_conv_w"][l], p["gdn_a_log"][l], p["gdn_dt_bias"][l], p["gdn_norm"][l],
                             GDN_CHUNK)
        oc, sc = _ret_prompt(pp, cos_p, sin_p, RET_CHUNK)
        od, sd = _ssd_prompt(pp, p["ssd_conv_w"][l], p["ssd_conv_b"][l], p["ssd_dt_bias"][l], p["ssd_a_log"][l],
                             p["ssd_d"][l], p["ssd_norm"][l], SSD_CHUNK)
        outs["hp"].append(sa.reshape(bp, N_HEADS, HEAD_DIM, HEAD_DIM))
        outs["gp"].append(sb.reshape(bp, N_HEADS, HEAD_DIM, HEAD_DIM))
        outs["rp"].append(sc.reshape(bp, N_HEADS, HEAD_DIM, HEAD_DIM))
        outs["sp"].append(sd.reshape(bp, N_HEADS, SSD_STATE, HEAD_DIM))
        outs["gcp"].append(pp[:, lp - 3:, COL_GDN:COL_GDN + 768])
        outs["scp"].append(pp[:, lp - 3:, COL_SSD + 256:COL_SSD + 1024])
        xp = _out_ffn(xp, [o.reshape(bp * lp, GROUP_WIDTH) for o in (oa, ob, oc, od)], wo, norm_ffn, wu, wd,
                      p["norm_final"], l)

        da, new_hg = _dec_hgrn(pd, p["hgrn_lb_logits"], p["hgrn_norm"][l], sv_hg, l, new_hg)
        db, new_gd = _dec_gdn(pd, hist_g, p["gdn_conv_w"][l], p["gdn_a_log"][l], p["gdn_dt_bias"][l],
                              p["gdn_norm"][l], sv_gd, l, new_gd)
        dc, new_rt = _dec_ret(pd, cos_d, sin_d, sv_rt, l, new_rt)
        dd, new_sd = _dec_ssd(pd, hist_s, p["ssd_conv_w"][l], p["ssd_conv_b"][l], p["ssd_dt_bias"][l],
                              p["ssd_a_log"][l], p["ssd_d"][l], p["ssd_norm"][l], sv_sd, l, new_sd)
        outs["gcs"].append(jnp.transpose(pd[ld - 3:, :, COL_GDN:COL_GDN + 768], (1, 0, 2)))
        outs["scs"].append(jnp.transpose(pd[ld - 3:, :, COL_SSD + 256:COL_SSD + 1024], (1, 0, 2)))
        xd = _out_ffn(xd, [o.reshape(ld * nd, GROUP_WIDTH) for o in (da, db, dc, dd)], wo, norm_ffn, wu, wd,
                      p["norm_final"], l)

    y_prompt = xp.reshape(bp, lp, D_MODEL)
    y_sample = jnp.transpose(xd.reshape(ld, nd, D_MODEL), (1, 0, 2))
    st = {k: jnp.stack(v) for k, v in outs.items()}
    hs, gs, rs, ss = (jnp.transpose(s, (0, 4, 1, 2, 3)) for s in (new_hg, new_gd, new_rt, new_sd))
    return (y_prompt, y_sample, st["hp"], hs, st["gp"], gs, st["gcp"], st["gcs"],
            st["rp"], rs, st["sp"], ss, st["scp"], st["scs"])


def kernel(x_prompt, x_sample, state_hgrn, state_gdn, state_gdn_conv, state_ret, state_ssd, state_ssd_conv,
           norm_mix, w_in, hgrn_lb_logits, hgrn_norm, gdn_conv_w, gdn_a_log, gdn_dt_bias, gdn_norm,
           ssd_conv_w, ssd_conv_b, ssd_dt_bias, ssd_a_log, ssd_d, ssd_norm,
           w_out, norm_ffn, w_up, w_down, norm_final):
    params = dict(norm_mix=norm_mix, w_in=w_in, hgrn_lb_logits=hgrn_lb_logits, hgrn_norm=hgrn_norm,
                  gdn_conv_w=gdn_conv_w, gdn_a_log=gdn_a_log, gdn_dt_bias=gdn_dt_bias, gdn_norm=gdn_norm,
                  ssd_conv_w=ssd_conv_w, ssd_conv_b=ssd_conv_b, ssd_dt_bias=ssd_dt_bias, ssd_a_log=ssd_a_log,
                  ssd_d=ssd_d, ssd_norm=ssd_norm, w_out=w_out, norm_ffn=norm_ffn, w_up=w_up,
                  w_down=w_down, norm_final=norm_final)
    states = (state_hgrn, state_gdn, state_gdn_conv, state_ret, state_ssd, state_ssd_conv)
    return _forward(x_prompt, x_sample, states, params, 16384)
```

```python
import functools
import math

import numpy as np
import jax
import jax.numpy as jnp
from jax import lax
from jax.experimental import pallas as pl
from jax.experimental.pallas import tpu as pltpu

F32 = jnp.float32
BF16 = jnp.bfloat16

D_MODEL = 1024
GROUP_WIDTH = 256
HEAD_DIM = 64
N_HEADS = 4
CONV_WIDTH = 4
SSD_STATE = 128
D_FF = 4096
RET_THETA = 10000.0
EPS = 1e-6
DEPTH = 2

COL_HGRN = 0
COL_GDN = 1024
COL_RET = 2048
COL_SSD = 3072
COL_SMALL = 4096
P_PAD = 4224
SMALL_GA, SMALL_GB, SMALL_SDT = 0, 4, 8

VMEM_LIMIT = 56 * 1024 * 1024
LOG_GAMMA = [math.log(1.0 - 2.0 ** (-5.0 - h)) for h in range(N_HEADS)]


def _dot(a, b):
    return jnp.dot(a, b, preferred_element_type=F32)


def _dot_nt(a, b):
    return lax.dot_general(a, b, (((1,), (1,)), ((), ())), preferred_element_type=F32)


def _dot_tn(a, b):
    return lax.dot_general(a, b, (((0,), (0,)), ((), ())), preferred_element_type=F32)


def _round_robin(gens):
    live = list(gens)
    while live:
        nxt = []
        for g in live:
            try:
                next(g)
                nxt.append(g)
            except StopIteration:
                pass
        live = nxt


def _split3(x):
    hi = x.astype(BF16)
    r1 = x - hi.astype(F32)
    mid = r1.astype(BF16)
    lo = (r1 - mid.astype(F32)).astype(BF16)
    return hi, mid, lo


def _exact_dot(x, sel):
    hi, mid, lo = _split3(x)
    return _dot(hi, sel) + _dot(mid, sel) + _dot(lo, sel)


def _exact_dot_left(sel, x):
    hi, mid, lo = _split3(x)
    return _dot(sel, hi) + _dot(sel, mid) + _dot(sel, lo)


def _iota(shape, dim):
    return lax.broadcasted_iota(jnp.int32, shape, dim)


def _head_of_lane(n_lanes, width=HEAD_DIM):
    return _iota((1, n_lanes), 1) // width


def _head_masks(n_lanes=GROUP_WIDTH, width=HEAD_DIM):
    hl = _head_of_lane(n_lanes, width)
    return [hl == h for h in range(n_lanes // width)]


def _stack_heads(x, masks):
    return jnp.concatenate([jnp.where(m, x, jnp.zeros_like(x)) for m in masks], axis=0)


def _unstack_heads(y, masks, c):
    out = jnp.where(masks[0], y[0:c], 0.0)
    for h in range(1, len(masks)):
        out = out + jnp.where(masks[h], y[h * c:(h + 1) * c], 0.0)
    return out


def _block_ones(n, width, dtype=BF16):
    r = _iota((n, n), 0) // width
    c = _iota((n, n), 1) // width
    return (r == c).astype(dtype)


def _block_mask(n, rwidth, cwidth):
    return (_iota((n, n), 0) // rwidth) == (_iota((n, n), 1) // cwidth)


def _lower_tri(c, dtype=BF16):
    return (_iota((c, c), 0) >= _iota((c, c), 1)).astype(dtype)


def _cumsum_rows(x, c):
    return _exact_dot_left(_lower_tri(c), x)


def _sigmoid(x):
    return 1.0 / (1.0 + jnp.exp(-x))


def _silu(x):
    return x * _sigmoid(x)


def _softplus(x):
    return jnp.maximum(x, 0.0) + jnp.log(1.0 + jnp.exp(-jnp.abs(x)))


def _rms_rows(x):
    return x * lax.rsqrt(jnp.mean(x * x, axis=-1, keepdims=True) + EPS)


def _head_sumsq(x, ones_bd):
    sq = x * x
    hi = sq.astype(BF16)
    lo = (sq - hi.astype(F32)).astype(BF16)
    return _dot(hi, ones_bd) + _dot(lo, ones_bd)


def _expand_small(small, first_lane):
    r = _iota((128, GROUP_WIDTH), 0)
    c = _iota((128, GROUP_WIDTH), 1) // HEAD_DIM
    sel = (r == c + first_lane).astype(BF16)
    return _exact_dot(small, sel)


def _decay_diff_operands(g):
    hi, mid, lo = (x.astype(F32) for x in _split3(g))
    pos = _iota(g.shape, 1) % HEAD_DIM
    a = jnp.where(pos == 0, hi, jnp.where(pos == 1, mid, jnp.where(pos == 2, lo,
                  jnp.where(pos < 6, 1.0, 0.0))))
    b = jnp.where(pos < 3, 1.0, jnp.where(pos == 3, -hi, jnp.where(pos == 4, -mid,
                  jnp.where(pos == 5, -lo, 0.0))))
    return a, b


def _extract_blocks(s_wide, rows, width):
    sel = ((_iota((GROUP_WIDTH, width), 0) % width) == _iota((GROUP_WIDTH, width), 1)).astype(BF16)
    return _exact_dot(s_wide, sel)


def _proj_kernel(x_ref, nw_ref, w_ref, o_ref):
    h = _rms_rows(x_ref[...]) * nw_ref[0]
    o_ref[...] = _dot_nt(h.astype(BF16), w_ref[0])


def _proj(x2d, norm_w, w_bf16, layer):
    t = x2d.shape[0]
    tm = min(t, 512)
    return pl.pallas_call(
        _proj_kernel,
        grid=(t // tm,),
        in_specs=[pl.BlockSpec((tm, D_MODEL), lambda i: (i, 0)),
                  pl.BlockSpec((1, 1, D_MODEL), lambda i: (layer, 0, 0)),
                  pl.BlockSpec((1, P_PAD, D_MODEL), lambda i: (layer, 0, 0))],
        out_specs=pl.BlockSpec((tm, P_PAD), lambda i: (i, 0)),
        out_shape=jax.ShapeDtypeStruct((t, P_PAD), F32),
        compiler_params=pltpu.CompilerParams(dimension_semantics=("arbitrary",),
                                             vmem_limit_bytes=VMEM_LIMIT),
        name="norm_in_proj",
    )(x2d, norm_w, w_bf16)


def _ffn_kernel(x_ref, oa_ref, ob_ref, oc_ref, od_ref, wo_ref, nf_ref, wu_ref, wd_ref, nfin_ref,
                o_ref, *, final):
    mix = jnp.concatenate([oa_ref[...], ob_ref[...], oc_ref[...], od_ref[...]], axis=1)
    x = x_ref[...] + _dot(mix.astype(BF16), wo_ref[0])
    h = (_rms_rows(x) * nf_ref[0]).astype(BF16)
    acc = x
    ft = 1024
    for t in range(D_FF // ft):
        up = _dot(h, wu_ref[0, :, t * ft:(t + 1) * ft])
        up = jnp.square(jnp.maximum(up, 0.0)).astype(BF16)
        acc = acc + _dot(up, wd_ref[0, t * ft:(t + 1) * ft, :])
    if final:
        acc = _rms_rows(acc) * nfin_ref[...]
    o_ref[...] = acc


FFN_ROWS = 512


def _out_ffn(x2d, mixes, wo, nf, wu, wd, nfin, layer):
    t = x2d.shape[0]
    tm = min(t, FFN_ROWS)
    row = lambda i: (i, 0)
    lay = lambda i: (layer, 0, 0)
    return pl.pallas_call(
        functools.partial(_ffn_kernel, final=(layer == DEPTH - 1)),
        grid=(t // tm,),
        in_specs=[pl.BlockSpec((tm, D_MODEL), row)]
                 + [pl.BlockSpec((tm, GROUP_WIDTH), row)] * 4
                 + [pl.BlockSpec((1, D_MODEL, D_MODEL), lay),
                    pl.BlockSpec((1, 1, D_MODEL), lay),
                    pl.BlockSpec((1, D_MODEL, D_FF), lay),
                    pl.BlockSpec((1, D_FF, D_MODEL), lay),
                    pl.BlockSpec((1, D_MODEL), lambda i: (0, 0))],
        out_specs=pl.BlockSpec((tm, D_MODEL), row),
        out_shape=jax.ShapeDtypeStruct((t, D_MODEL), F32),
        compiler_params=pltpu.CompilerParams(dimension_semantics=("arbitrary",),
                                             vmem_limit_bytes=VMEM_LIMIT),
        name="out_proj_ffn",
    )(x2d, *mixes, wo, nf, wu, wd, nfin.reshape(1, D_MODEL))


def _swap_halves(x):
    first = (_iota((1, 128), 1) % HEAD_DIM) < (HEAD_DIM // 2)
    parts = []
    for p in range(GROUP_WIDTH // 128):
        xp = x[:, p * 128:(p + 1) * 128]
        parts.append(jnp.where(first, pltpu.roll(xp, 96, 1), pltpu.roll(xp, 32, 1)))
    return jnp.concatenate(parts, axis=1)


def _conv_silu(xe_ref, halo, x, w, bias, first_chunk, c):
    xe_ref[0:8, :] = jnp.where(first_chunk, jnp.zeros_like(halo), halo)
    xe_ref[8:, :] = x
    y = w[3:4, :] * x
    for j in range(CONV_WIDTH - 1):
        y = y + w[j:j + 1, :] * xe_ref[5 + j:5 + j + c, :]
    if bias is not None:
        y = y + bias
    return _silu(y)


def _ret_prompt_kernel(blk_ref, cos_ref, sin_ref, o_ref, st_ref, s_scr, *, c, n_chunks, nb):
    ci = pl.program_id(1)

    @pl.when(ci == 0)
    def _():
        s_scr[...] = jnp.zeros_like(s_scr)

    cosv, sinv = cos_ref[...], sin_ref[...]
    masks = _head_masks()
    hl = _head_of_lane(GROUP_WIDTH)
    lg = jnp.full((1, GROUP_WIDTH), LOG_GAMMA[0], F32)
    for h in range(1, N_HEADS):
        lg = jnp.where(hl == h, LOG_GAMMA[h], lg)
    ri = _iota((c, 1), 0).astype(F32)
    dij = (_iota((c, c), 0) - _iota((c, c), 1)).astype(F32)
    causal = dij >= 0.0
    decay = jnp.concatenate(
        [jnp.where(causal, jnp.exp(jnp.maximum(dij, 0.0) * LOG_GAMMA[h]), 0.0) for h in range(N_HEADS)],
        axis=0)
    q_scale = jnp.exp((ri + 1.0) * lg)
    k_scale = jnp.exp((float(c - 1) - ri) * lg) * (HEAD_DIM ** -0.5)
    s_scale = jnp.exp(float(c) * lg)
    bd_mask = _block_mask(GROUP_WIDTH, HEAD_DIM, HEAD_DIM)
    ones_bd = _block_ones(GROUP_WIDTH, HEAD_DIM)

    def one_sequence(sq):
        blk = blk_ref[sq]
        rq, rk, rv, rg = (blk[:, i * GROUP_WIDTH:(i + 1) * GROUP_WIDTH] for i in range(4))
        q = rq * cosv + _swap_halves(rq) * sinv
        k = rk * cosv + _swap_halves(rk) * sinv
        v = rv.astype(BF16)
        s = s_scr[sq]
        qk = _dot_nt(_stack_heads(q, masks).astype(BF16), k.astype(BF16))
        o_inter = _dot((q * q_scale).astype(BF16), s.astype(BF16))
        ds = _dot_tn((k * k_scale).astype(BF16), v)
        yield
        scores = qk * (decay * (HEAD_DIM ** -0.5))
        pv = _dot(scores.astype(BF16), v)
        s_scr[sq] = s_scale * s + jnp.where(bd_mask, ds, 0.0)
        yield
        o = _unstack_heads(pv, masks, c) + o_inter
        ss = _head_sumsq(o, ones_bd)
        yield
        o_ref[sq] = o * lax.rsqrt(ss * (1.0 / HEAD_DIM) + EPS) * _silu(rg)

    _round_robin([one_sequence(sq) for sq in range(nb)])

    @pl.when(ci == n_chunks - 1)
    def _():
        for sq in range(nb):
            st_ref[sq] = _extract_blocks(s_scr[sq], GROUP_WIDTH, HEAD_DIM)


PROMPT_SEQS_PER_STEP = 8
GDN_SEQS_PER_STEP = 8
PROMPT_ROWS_PER_STEP = 2048


def _ret_prompt(proj3, cos_t, sin_t, c):
    b, l, _ = proj3.shape
    n = l // c
    nb = math.gcd(b, min(PROMPT_SEQS_PER_STEP, max(1, PROMPT_ROWS_PER_STEP // (l // n))))
    return pl.pallas_call(
        functools.partial(_ret_prompt_kernel, c=c, n_chunks=n, nb=nb),
        grid=(b // nb, n),
        in_specs=[pl.BlockSpec((nb, c, 1024), lambda bi, ci: (bi, ci, COL_RET // 1024)),
                  pl.BlockSpec((c, GROUP_WIDTH), lambda bi, ci: (ci, 0)),
                  pl.BlockSpec((c, GROUP_WIDTH), lambda bi, ci: (ci, 0))],
        out_specs=[pl.BlockSpec((nb, c, GROUP_WIDTH), lambda bi, ci: (bi, ci, 0)),
                   pl.BlockSpec((nb, GROUP_WIDTH, HEAD_DIM), lambda bi, ci: (bi, 0, 0))],
        out_shape=[jax.ShapeDtypeStruct((b, l, GROUP_WIDTH), F32),
                   jax.ShapeDtypeStruct((b, GROUP_WIDTH, HEAD_DIM), F32)],
        scratch_shapes=[pltpu.VMEM((nb, GROUP_WIDTH, GROUP_WIDTH), F32)],
        compiler_params=pltpu.CompilerParams(dimension_semantics=("arbitrary", "arbitrary"),
                                             vmem_limit_bytes=VMEM_LIMIT),
        name="retention_prompt",
    )(proj3, cos_t, sin_t)


def _ssd_prompt_kernel(blk_ref, halo_ref, small_ref, cw_ref, cb_ref, dtb_ref, alog_ref, dskip_ref, nw_ref,
                       o_ref, st_ref, s_scr, xe_scr, *, c, n_chunks, nb):
    ci = pl.program_id(1)

    @pl.when(ci == 0)
    def _():
        s_scr[...] = jnp.zeros_like(s_scr)

    masks = _head_masks()
    causal = _iota((c, c), 0) >= _iota((c, c), 1)
    causal4 = jnp.concatenate([causal] * N_HEADS, axis=0)
    group_mask = _block_mask(GROUP_WIDTH, 128, 128)
    tri = _lower_tri(c)
    neg_a = -jnp.exp(alog_ref[...]) * LOG2E

    def one_sequence(sq):
        blk = blk_ref[sq]
        sz = blk[:, 0:GROUP_WIDTH]
        xbc = _conv_silu(xe_scr.at[sq], halo_ref[sq][:, GROUP_WIDTH:], blk[:, GROUP_WIDTH:], cw_ref[...],
                         cb_ref[...], ci == 0, c)
        xs = xbc[:, 0:256]
        bmat = xbc[:, 256:512].astype(BF16)
        cmat = xbc[:, 512:768].astype(BF16)
        s = s_scr[sq]
        cb = [_dot_nt(cmat[:, gi * 128:(gi + 1) * 128], bmat[:, gi * 128:(gi + 1) * 128]) for gi in range(2)]
        y_inter = _dot(cmat, s.astype(BF16))
        dt = _softplus(_expand_small(small_ref[sq], SMALL_SDT) + dtb_ref[...])
        yield
        g = _exact_dot_left(tri, neg_a * dt)
        yield
        g_last = g[c - 1:c, :]
        da, db = _decay_diff_operands(g)
        diff = _dot_nt(_stack_heads(da, masks).astype(BF16), db.astype(BF16))
        v = xs * dt
        vend = v * jnp.exp2(g_last - g)
        ds = _dot_tn(bmat, vend.astype(BF16))
        yield
        decay = jnp.where(causal4, jnp.exp2(diff), 0.0)
        scores = jnp.concatenate([cb[0], cb[0], cb[1], cb[1]], axis=0) * decay
        pv = _dot(scores.astype(BF16), v.astype(BF16))
        s_scr[sq] = jnp.exp2(g_last) * s + jnp.where(group_mask, ds, 0.0)
        yield
        y = _unstack_heads(pv, masks, c) + y_inter * jnp.exp2(g)
        y = (y + dskip_ref[...] * xs) * _silu(sz)
        halves = [_rms_rows(y[:, gi * 128:(gi + 1) * 128]) for gi in range(2)]
        o_ref[sq] = jnp.concatenate(halves, axis=1) * nw_ref[...]

    _round_robin([one_sequence(sq) for sq in range(nb)])

    @pl.when(ci == n_chunks - 1)
    def _():
        for sq in range(nb):
            for h in range(N_HEADS):
                gi = h // 2
                rows = jnp.where(masks[h], s_scr[sq, gi * 128:(gi + 1) * 128, :], 0.0)
                st_ref[sq, h * 128:(h + 1) * 128, :] = _extract_blocks(rows, 128, HEAD_DIM)


def _lane_rep(p):
    return jnp.repeat(p.astype(F32), HEAD_DIM).reshape(1, GROUP_WIDTH)


def _ssd_prompt(proj3, conv_w, conv_b, dt_bias, a_log, d_skip, norm_w, c):
    b, l, _ = proj3.shape
    n = l // c
    fixed = lambda bi, ci: (0, 0)
    nb = math.gcd(b, min(PROMPT_SEQS_PER_STEP, max(1, PROMPT_ROWS_PER_STEP // (l // n))))
    return pl.pallas_call(
        functools.partial(_ssd_prompt_kernel, c=c, n_chunks=n, nb=nb),
        grid=(b // nb, n),
        in_specs=[pl.BlockSpec((nb, c, 1024), lambda bi, ci: (bi, ci, COL_SSD // 1024)),
                  pl.BlockSpec((nb, 8, 1024), lambda bi, ci: (bi, jnp.maximum(ci * (c // 8) - 1, 0), COL_SSD // 1024)),
                  pl.BlockSpec((nb, c, 128), lambda bi, ci: (bi, ci, COL_SMALL // 128)),
                  pl.BlockSpec((CONV_WIDTH, 768), fixed),
                  pl.BlockSpec((1, 768), fixed),
                  pl.BlockSpec((1, GROUP_WIDTH), fixed),
                  pl.BlockSpec((1, GROUP_WIDTH), fixed),
                  pl.BlockSpec((1, GROUP_WIDTH), fixed),
                  pl.BlockSpec((1, GROUP_WIDTH), fixed)],
        out_specs=[pl.BlockSpec((nb, c, GROUP_WIDTH), lambda bi, ci: (bi, ci, 0)),
                   pl.BlockSpec((nb, N_HEADS * SSD_STATE, HEAD_DIM), lambda bi, ci: (bi, 0, 0))],
        out_shape=[jax.ShapeDtypeStruct((b, l, GROUP_WIDTH), F32),
                   jax.ShapeDtypeStruct((b, N_HEADS * SSD_STATE, HEAD_DIM), F32)],
        scratch_shapes=[pltpu.VMEM((nb, GROUP_WIDTH, GROUP_WIDTH), F32),
                        pltpu.VMEM((nb, c + 8, 768), F32)],
        compiler_params=pltpu.CompilerParams(dimension_semantics=("arbitrary", "arbitrary"),
                                             vmem_limit_bytes=VMEM_LIMIT),
        name="ssd_prompt",
    )(proj3, proj3, proj3, conv_w, conv_b.reshape(1, 768), _lane_rep(dt_bias), _lane_rep(a_log),
      _lane_rep(d_skip), norm_w.reshape(1, GROUP_WIDTH))


def _gdn_prompt_kernel(blk_ref, halo_ref, small_ref, cw_ref, alog_ref, dtb_ref, nw_ref,
                       o_ref, st_ref, s_scr, xe_scr, m_scr, *, c, n_chunks, nb):
    ci = pl.program_id(1)
    hc = 2 * c
    n_lvl = int(math.log2(c))

    @pl.when(ci == 0)
    def _():
        s_scr[...] = jnp.zeros_like(s_scr)

    @pl.when((pl.program_id(0) == 0) & (ci == 0))
    def _():
        rr = _iota((hc, hc), 0)
        cc = _iota((hc, hc), 1)
        same = (rr // c) == (cc // c)
        m_scr[0] = (same & (rr >= cc)).astype(F32)
        m_scr[1] = (same & (rr > cc)).astype(F32)
        for lv in range(n_lvl):
            sz = 1 << lv
            off = ((rr // (2 * sz)) == (cc // (2 * sz))) & (((rr // sz) % 2) == 1) & (((cc // sz) % 2) == 0)
            m_scr[2 + lv] = off.astype(F32)

    ones_bd = _block_ones(GROUP_WIDTH, HEAD_DIM)
    bd_mask = _block_mask(GROUP_WIDTH, HEAD_DIM, HEAD_DIM)
    masks = _head_masks()
    pair_masks = [masks[0:2], masks[2:4]]
    tri = _lower_tri(c)
    neg_a = -jnp.exp(alog_ref[...]) * LOG2E

    def one_sequence(sq):
        blk = blk_ref[sq]
        gz = blk[:, 768:1024]
        qkv = _conv_silu(xe_scr.at[sq], halo_ref[sq][:, 0:768], blk[:, 0:768], cw_ref[...], None, ci == 0, c)
        gq, gk, v = qkv[:, 0:256], qkv[:, 256:512], qkv[:, 512:768]
        q = gq * lax.rsqrt(_head_sumsq(gq, ones_bd) + EPS) * (HEAD_DIM ** -0.5)
        k = gk * lax.rsqrt(_head_sumsq(gk, ones_bd) + EPS)
        yield
        small = small_ref[sq]
        beta = _sigmoid(_expand_small(small, SMALL_GB))
        g = _exact_dot_left(tri, neg_a * _softplus(_expand_small(small, SMALL_GA) + dtb_ref[...]))
        g_last = g[c - 1:c, :]
        eg = jnp.exp2(g)
        yield
        da, db = _decay_diff_operands(g)
        bk = beta * k
        bkg = bk * eg
        bv = beta * v
        a_mat, p_mat, x = [], [], []
        for pm in pair_masks:
            diff = _dot_nt(_stack_heads(da, pm).astype(BF16), _stack_heads(db, pm).astype(BF16))
            k_st = _stack_heads(k, pm).astype(BF16)
            kk = _dot_nt(_stack_heads(bk, pm).astype(BF16), k_st)
            qk = _dot_nt(_stack_heads(q, pm).astype(BF16), k_st)
            decay = jnp.exp2(jnp.minimum(diff, 0.0))
            a_mat.append(kk * (decay * m_scr[1]))
            p_mat.append((qk * (decay * m_scr[0])).astype(BF16))
            x.append(jnp.concatenate([_stack_heads(bkg, pm), _stack_heads(bv, pm)], axis=1))
        yield

        n_mat = [-(a * m_scr[2]) for a in a_mat]
        for lv in range(1, n_lvl):
            a_off = [a * m_scr[2 + lv] for a in a_mat]
            m = [ao + _dot(ao.astype(BF16), n.astype(BF16)) for ao, n in zip(a_off, n_mat)]
            yield
            n_mat = [n - mm - _dot(n.astype(BF16), mm.astype(BF16)) for n, mm in zip(n_mat, m)]
            yield
        x = [xx + _dot(n.astype(BF16), xx.astype(BF16)) for xx, n in zip(x, n_mat)]
        yield
        w = x[0][0:c, 0:256] + x[0][c:2 * c, 0:256] + x[1][0:c, 0:256] + x[1][c:2 * c, 0:256]
        u0 = x[0][0:c, 256:512] + x[0][c:2 * c, 256:512] + x[1][0:c, 256:512] + x[1][c:2 * c, 256:512]

        s = s_scr[sq]
        s_bf = s.astype(BF16)
        u = u0 - _dot(w.astype(BF16), s_bf)
        o = _dot((q * eg).astype(BF16), s_bf)
        yield
        pu = [_dot(pmat, _stack_heads(u, pm).astype(BF16)) for pmat, pm in zip(p_mat, pair_masks)]
        kend = k * jnp.exp2(g_last - g)
        ds = _dot_tn(kend.astype(BF16), u.astype(BF16))
        yield
        for pu_p in pu:
            o = o + pu_p[0:c] + pu_p[c:2 * c]
        s_scr[sq] = jnp.exp2(g_last) * s + jnp.where(bd_mask, ds, 0.0)
        ss = _head_sumsq(o, ones_bd)
        o_ref[sq] = o * lax.rsqrt(ss * (1.0 / HEAD_DIM) + EPS) * nw_ref[...] * _silu(gz)

    _round_robin([one_sequence(sq) for sq in range(nb)])

    @pl.when(ci == n_chunks - 1)
    def _():
        for sq in range(nb):
            st_ref[sq] = _extract_blocks(s_scr[sq], GROUP_WIDTH, HEAD_DIM)


def _gdn_prompt(proj3, conv_w, a_log, dt_bias, norm_w, c):
    b, l, _ = proj3.shape
    n = l // c
    fixed = lambda bi, ci: (0, 0)
    nb = math.gcd(b, GDN_SEQS_PER_STEP)
    hc = 2 * c
    return pl.pallas_call(
        functools.partial(_gdn_prompt_kernel, c=c, n_chunks=n, nb=nb),
        grid=(b // nb, n),
        in_specs=[pl.BlockSpec((nb, c, 1024), lambda bi, ci: (bi, ci, COL_GDN // 1024)),
                  pl.BlockSpec((nb, 8, 1024), lambda bi, ci: (bi, jnp.maximum(ci * (c // 8) - 1, 0), COL_GDN // 1024)),
                  pl.BlockSpec((nb, c, 128), lambda bi, ci: (bi, ci, COL_SMALL // 128)),
                  pl.BlockSpec((CONV_WIDTH, 768), fixed),
                  pl.BlockSpec((1, GROUP_WIDTH), fixed),
                  pl.BlockSpec((1, GROUP_WIDTH), fixed),
                  pl.BlockSpec((1, GROUP_WIDTH), fixed)],
        out_specs=[pl.BlockSpec((nb, c, GROUP_WIDTH), lambda bi, ci: (bi, ci, 0)),
                   pl.BlockSpec((nb, GROUP_WIDTH, HEAD_DIM), lambda bi, ci: (bi, 0, 0))],
        out_shape=[jax.ShapeDtypeStruct((b, l, GROUP_WIDTH), F32),
                   jax.ShapeDtypeStruct((b, GROUP_WIDTH, HEAD_DIM), F32)],
        scratch_shapes=[pltpu.VMEM((nb, GROUP_WIDTH, GROUP_WIDTH), F32),
                        pltpu.VMEM((nb, c + 8, 768), F32),
                        pltpu.VMEM((2 + int(math.log2(c)), hc, hc), F32)],
        compiler_params=pltpu.CompilerParams(dimension_semantics=("arbitrary", "arbitrary"),
                                             vmem_limit_bytes=VMEM_LIMIT),
        name="gdn_prompt",
    )(proj3, proj3, proj3, conv_w, _lane_rep(a_log), _lane_rep(dt_bias),
      jnp.tile(norm_w.astype(F32), N_HEADS).reshape(1, GROUP_WIDTH))


HGRN_SUB = 16
LOG2E = 1.4426950408889634


def _hgrn_lower_bound(logits, layer):
    rows = [logits[d:d + 1, :] for d in range(DEPTH)]
    mx = functools.reduce(jnp.maximum, rows)
    es = [jnp.exp(x - mx) for x in rows]
    tot = functools.reduce(lambda a, b: a + b, es)
    sm = [e / tot for e in es]
    acc = sm[0]
    for d in range(1, layer + 1):
        acc = acc + sm[d]
    return acc - sm[0]


def _hgrn_prompt_kernel(blk_ref, lb_ref, nw_ref, o_ref, st_ref, s_scr, *, r, n_chunks, layer, nb):
    ci = pl.program_id(1)
    sub = HGRN_SUB
    n_sub = r // sub

    @pl.when(ci == 0)
    def _():
        s_scr[...] = jnp.zeros_like(s_scr)

    lb = _hgrn_lower_bound(lb_ref[...], layer)
    rr = _iota((r, r), 0)
    cc = _iota((r, r), 1)
    same_sub = (rr // sub) == (cc // sub)
    cum_sel = (same_sub & (rr >= cc)).astype(BF16)
    tot_sel = same_sub.astype(BF16)
    ones_bd = _block_ones(GROUP_WIDTH, HEAD_DIM)
    masks = _head_masks()
    half = sub // 2
    i8 = _iota((half, 1), 0)

    def one_sequence(sq):
        blk = blk_ref[sq]
        hq, hf, hi, hg = (blk[:, i * GROUP_WIDTH:(i + 1) * GROUP_WIDTH] for i in range(4))
        f = lb + (1.0 - lb) * _sigmoid(hf)
        q = _sigmoid(hq)
        k = 1.0 - f
        v = hi
        logf = jnp.log(f)
        g = _exact_dot_left(cum_sel, logf)
        g_tot = _exact_dot_left(tot_sel, logf)
        yield
        a2 = (g + jnp.log(q)) * LOG2E
        h2 = (g - jnp.log(k)) * LOG2E
        gt2 = g_tot * LOG2E
        qt = _stack_heads(jnp.exp2(a2), masks).astype(BF16)
        kh = _stack_heads(jnp.exp2(gt2 - h2), masks).astype(BF16)
        v_heads = [v[:, h * HEAD_DIM:(h + 1) * HEAD_DIM].astype(BF16) for h in range(N_HEADS)]

        def sub_rows(x_st, lo):
            return jnp.concatenate([x_st[h * r + lo:h * r + lo + sub] for h in range(N_HEADS)], axis=0)

        s = s_scr[sq]
        outs = []
        for j in range(n_sub):
            lo = j * sub
            v_j = v[lo:lo + sub]
            a_lo, a_hi, h_j = a2[lo:lo + half], a2[lo + half:lo + sub], h2[lo:lo + sub]
            lo_blocks, hi_blocks = [], []
            for jj in range(sub):
                h_row = h_j[jj:jj + 1, :]
                if jj < half:
                    e_lo = jnp.exp2(a_lo - h_row)
                    lo_blocks.append(e_lo if jj == 0 else jnp.where(i8 >= jj, e_lo, 0.0))
                    hi_blocks.append(jnp.exp2(a_hi - h_row))
                else:
                    e_hi = jnp.exp2(a_hi - h_row)
                    hi_blocks.append(e_hi if jj == half else jnp.where(i8 >= jj - half, e_hi, 0.0))
            sc = _dot(jnp.concatenate(lo_blocks + hi_blocks, axis=0).astype(BF16), ones_bd)
            oi = _dot_nt(sub_rows(qt, lo), s.astype(BF16))
            v_rows = jnp.concatenate([vh[lo:lo + sub] for vh in v_heads], axis=0)
            ds = _dot_tn(v_rows, sub_rows(kh, lo))
            yield
            o_inter = jnp.concatenate([oi[h * sub:(h + 1) * sub] for h in range(N_HEADS)], axis=1)
            n_lo = half * half
            o_lo = sc[0:half] * v_j[0:1, :]
            o_hi = sc[n_lo:n_lo + half] * v_j[0:1, :]
            for jj in range(1, sub):
                if jj < half:
                    o_lo = o_lo + sc[jj * half:(jj + 1) * half] * v_j[jj:jj + 1, :]
                o_hi = o_hi + sc[n_lo + jj * half:n_lo + (jj + 1) * half] * v_j[jj:jj + 1, :]
            outs.append(jnp.concatenate([o_lo, o_hi], axis=0) + o_inter)
            s = jnp.exp2(gt2[lo:lo + 1, :]) * s + ds
        s_scr[sq] = s

        o = jnp.concatenate(outs, axis=0)
        ss = _head_sumsq(o, ones_bd)
        yield
        o_ref[sq] = o * lax.rsqrt(ss * (1.0 / HEAD_DIM) + EPS) * nw_ref[...] * _silu(hg)

    _round_robin([one_sequence(sq) for sq in range(nb)])

    @pl.when(ci == n_chunks - 1)
    def _():
        for sq in range(nb):
            st_ref[sq] = s_scr[sq].T


def _hgrn_prompt(proj3, lb_logits, norm_w, layer, r):
    b, l, _ = proj3.shape
    n = l // r
    fixed = lambda bi, ci: (0, 0)
    nb = math.gcd(b, min(PROMPT_SEQS_PER_STEP, max(1, PROMPT_ROWS_PER_STEP // (l // n))))
    return pl.pallas_call(
        functools.partial(_hgrn_prompt_kernel, r=r, n_chunks=n, layer=layer, nb=nb),
        grid=(b // nb, n),
        in_specs=[pl.BlockSpec((nb, r, 1024), lambda bi, ci: (bi, ci, COL_HGRN // 1024)),
                  pl.BlockSpec((DEPTH, GROUP_WIDTH), fixed),
                  pl.BlockSpec((1, GROUP_WIDTH), fixed)],
        out_specs=[pl.BlockSpec((nb, r, GROUP_WIDTH), lambda bi, ci: (bi, ci, 0)),
                   pl.BlockSpec((nb, GROUP_WIDTH, HEAD_DIM), lambda bi, ci: (bi, 0, 0))],
        out_shape=[jax.ShapeDtypeStruct((b, l, GROUP_WIDTH), F32),
                   jax.ShapeDtypeStruct((b, GROUP_WIDTH, HEAD_DIM), F32)],
        scratch_shapes=[pltpu.VMEM((nb, HEAD_DIM, GROUP_WIDTH), F32)],
        compiler_params=pltpu.CompilerParams(dimension_semantics=("arbitrary", "arbitrary"),
                                             vmem_limit_bytes=VMEM_LIMIT),
        name="hgrn_prompt",
    )(proj3, lb_logits.astype(F32), jnp.tile(norm_w.astype(F32), N_HEADS).reshape(1, GROUP_WIDTH))


DEC_SEQS = 128
DEC_LEN = 4


def _head_rows(h):
    return pl.ds(pl.multiple_of(h * HEAD_DIM, HEAD_DIM), HEAD_DIM)


def _recur_head(load_s, store_s, n_keys, decay_fn, k_fn, q_fn, v_blocks):
    def body(kk, accs):
        s = load_s(kk)
        accs = list(accs)
        for t in range(DEC_LEN):
            s = decay_fn(t, kk) * s + k_fn(t, kk) * v_blocks[t]
            accs[t] = accs[t] + q_fn(t, kk) * s
        store_s(kk, s)
        return tuple(accs)

    zero = jnp.zeros((HEAD_DIM, DEC_SEQS), F32)
    return lax.fori_loop(0, n_keys, body, (zero,) * DEC_LEN)


def _dec_ret_kernel(blk_ref, cos_ref, sin_ref, st_ref, o_ref, so_ref, q_scr, k_scr, v_scr, o_scr):
    h = pl.program_id(0)

    @pl.when(h == 0)
    def _():
        for t in range(DEC_LEN):
            blk = blk_ref[t]
            rq, rk, rv = blk[:, 0:256], blk[:, 256:512], blk[:, 512:768]
            cosv, sinv = cos_ref[t:t + 1, :], sin_ref[t:t + 1, :]
            q_scr[t] = (rq * cosv + _swap_halves(rq) * sinv).T
            k_scr[t] = ((rk * cosv + _swap_halves(rk) * sinv) * (HEAD_DIM ** -0.5)).T
            v_scr[t] = rv.T

    lg = jnp.where(h == 0, LOG_GAMMA[0], jnp.where(h == 1, LOG_GAMMA[1], jnp.where(h == 2, LOG_GAMMA[2], LOG_GAMMA[3])))
    gamma = jnp.exp(jnp.full((1, DEC_SEQS), lg, F32))
    hr = _head_rows(h)
    v_blocks = [v_scr[t, hr, :] for t in range(DEC_LEN)]
    accs = _recur_head(
        lambda kk: st_ref[0, 0, kk], functools.partial(_store_state, so_ref), HEAD_DIM,
        lambda t, kk: gamma,
        lambda t, kk: k_scr[t, pl.ds(h * HEAD_DIM + kk, 1), :],
        lambda t, kk: q_scr[t, pl.ds(h * HEAD_DIM + kk, 1), :],
        v_blocks)
    for t in range(DEC_LEN):
        o_scr[t, hr, :] = accs[t]

    @pl.when(h == N_HEADS - 1)
    def _():
        ones_bd = _block_ones(GROUP_WIDTH, HEAD_DIM)
        for t in range(DEC_LEN):
            o = o_scr[t].T
            ss = _head_sumsq(o, ones_bd)
            o_ref[t] = o * lax.rsqrt(ss * (1.0 / HEAD_DIM) + EPS) * _silu(blk_ref[t][:, 768:1024])


def _store_state(so_ref, kk, s):
    so_ref[0, 0, kk] = s


def _without_ref(kernel_fn, idx):
    def wrapped(*refs):
        return kernel_fn(*refs[:idx], *refs[idx + 1:])
    return wrapped


def _dec_call(kernel_fn, name, col, ins, in_specs, n_tok_scr, state_view, layer, carried, extra_scratch=()):
    blk_spec = pl.BlockSpec((DEC_LEN, DEC_SEQS, 1024), lambda h: (0, 0, col // 1024))
    st_spec = pl.BlockSpec((1, 1) + state_view.shape[2:], lambda h: (layer, h, 0, 0, 0))
    tok_scr = pltpu.VMEM((DEC_LEN, GROUP_WIDTH, DEC_SEQS), F32)
    ins = tuple(ins) + (state_view, carried)
    specs = [blk_spec] + in_specs + [st_spec, pl.BlockSpec(memory_space=pl.ANY)]
    return pl.pallas_call(
        _without_ref(kernel_fn, len(ins) - 1),
        grid=(N_HEADS,),
        in_specs=specs,
        out_specs=[pl.BlockSpec((DEC_LEN, DEC_SEQS, GROUP_WIDTH), lambda h: (0, 0, 0)), st_spec],
        out_shape=[jax.ShapeDtypeStruct((DEC_LEN, DEC_SEQS, GROUP_WIDTH), F32),
                   jax.ShapeDtypeStruct(state_view.shape, F32)],
        scratch_shapes=[tok_scr] * n_tok_scr + list(extra_scratch),
        input_output_aliases={len(ins) - 1: 1},
        compiler_params=pltpu.CompilerParams(dimension_semantics=("arbitrary",), vmem_limit_bytes=VMEM_LIMIT),
        name=name,
    )(*ins)


def _fixed1(shape):
    return pl.BlockSpec(shape, lambda h: (0,) * len(shape))


def _dec_ret(projd, cos_t, sin_t, state_view, layer, carried):
    return _dec_call(_dec_ret_kernel, "retention_decode", COL_RET, (projd, cos_t, sin_t),
                     [_fixed1((DEC_LEN, GROUP_WIDTH)), _fixed1((DEC_LEN, GROUP_WIDTH))], 4,
                     state_view, layer, carried)


def _dec_hgrn_kernel(blk_ref, lb_ref, nw_ref, st_ref, o_ref, so_ref, q_scr, k_scr, v_scr, f_scr, o_scr, *, layer):
    h = pl.program_id(0)

    @pl.when(h == 0)
    def _():
        lb = _hgrn_lower_bound(lb_ref[...], layer)
        for t in range(DEC_LEN):
            blk = blk_ref[t]
            f = lb + (1.0 - lb) * _sigmoid(blk[:, 256:512])
            q_scr[t] = _sigmoid(blk[:, 0:256]).T
            k_scr[t] = (1.0 - f).T
            v_scr[t] = blk[:, 512:768].T
            f_scr[t] = f.T

    hr = _head_rows(h)
    v_blocks = [v_scr[t, hr, :] for t in range(DEC_LEN)]
    row = lambda scr: (lambda t, kk: scr[t, pl.ds(h * HEAD_DIM + kk, 1), :])
    accs = _recur_head(lambda kk: st_ref[0, 0, kk], functools.partial(_store_state, so_ref), HEAD_DIM,
                       row(f_scr), row(k_scr), row(q_scr), v_blocks)
    for t in range(DEC_LEN):
        o_scr[t, hr, :] = accs[t]

    @pl.when(h == N_HEADS - 1)
    def _():
        ones_bd = _block_ones(GROUP_WIDTH, HEAD_DIM)
        for t in range(DEC_LEN):
            o = o_scr[t].T
            ss = _head_sumsq(o, ones_bd)
            o_ref[t] = o * lax.rsqrt(ss * (1.0 / HEAD_DIM) + EPS) * nw_ref[...] * _silu(blk_ref[t][:, 768:1024])


def _dec_hgrn(projd, lb_logits, norm_w, state_view, layer, carried):
    return _dec_call(functools.partial(_dec_hgrn_kernel, layer=layer), "hgrn_decode", COL_HGRN,
                     (projd, lb_logits.astype(F32), jnp.tile(norm_w.astype(F32), N_HEADS).reshape(1, GROUP_WIDTH)),
                     [_fixed1((DEPTH, GROUP_WIDTH)), _fixed1((1, GROUP_WIDTH))], 5, state_view, layer, carried)


def _hist_spec(layer):
    return pl.BlockSpec((1, CONV_WIDTH - 1, DEC_SEQS, 768), lambda h: (layer, 0, 0, 0))


def _dec_conv_silu(hist_ref, xs, w, bias):
    xe = [hist_ref[0, j] for j in range(CONV_WIDTH - 1)] + xs
    out = []
    for t in range(DEC_LEN):
        y = xe[t] * w[0:1, :]
        for j in range(1, CONV_WIDTH):
            y = y + xe[t + j] * w[j:j + 1, :]
        if bias is not None:
            y = y + bias
        out.append(_silu(y))
    return out


def _dec_ssd_kernel(blk_ref, small_ref, hist_ref, cw_ref, cb_ref, dtb_ref, alog_ref, dskip_ref, nw_ref, st_ref,
                    o_ref, so_ref, c_scr, b_scr, v_scr, a_scr, o_scr, x_scr):
    h = pl.program_id(0)

    @pl.when(h == 0)
    def _():
        xbc = _dec_conv_silu(hist_ref, [blk_ref[t][:, 256:1024] for t in range(DEC_LEN)], cw_ref[...], cb_ref[...])
        for t in range(DEC_LEN):
            xs = xbc[t][:, 0:256]
            dt = _softplus(_expand_small(small_ref[t], SMALL_SDT) + dtb_ref[...])
            x_scr[t] = xs
            v_scr[t] = (xs * dt).T
            b_scr[t] = xbc[t][:, 256:512].T
            c_scr[t] = xbc[t][:, 512:768].T
            a_scr[t] = jnp.exp(-jnp.exp(alog_ref[...]) * dt).T

    hr = _head_rows(h)
    g0 = (h // 2) * SSD_STATE
    v_blocks = [v_scr[t, hr, :] for t in range(DEC_LEN)]
    accs = _recur_head(
        lambda kk: st_ref[0, 0, kk], functools.partial(_store_state, so_ref), SSD_STATE,
        lambda t, kk: a_scr[t, pl.ds(h * HEAD_DIM, 1), :],
        lambda t, kk: b_scr[t, pl.ds(g0 + kk, 1), :],
        lambda t, kk: c_scr[t, pl.ds(g0 + kk, 1), :],
        v_blocks)
    for t in range(DEC_LEN):
        o_scr[t, hr, :] = accs[t]

    @pl.when(h == N_HEADS - 1)
    def _():
        for t in range(DEC_LEN):
            y = (o_scr[t].T + dskip_ref[...] * x_scr[t]) * _silu(blk_ref[t][:, 0:256])
            halves = [_rms_rows(y[:, gi * 128:(gi + 1) * 128]) for gi in range(2)]
            o_ref[t] = jnp.concatenate(halves, axis=1) * nw_ref[...]


def _dec_ssd(projd, hist, conv_w, conv_b, dt_bias, a_log, d_skip, norm_w, state_view, layer, carried):
    small_spec = pl.BlockSpec((DEC_LEN, DEC_SEQS, 128), lambda h: (0, 0, COL_SMALL // 128))
    return _dec_call(_dec_ssd_kernel, "ssd_decode", COL_SSD,
                     (projd, projd, hist, conv_w, conv_b.reshape(1, 768), _lane_rep(dt_bias), _lane_rep(a_log),
                      _lane_rep(d_skip), norm_w.reshape(1, GROUP_WIDTH)),
                     [small_spec, _hist_spec(layer), _fixed1((CONV_WIDTH, 768)),
                      _fixed1((1, 768))] + [_fixed1((1, GROUP_WIDTH))] * 4, 5, state_view, layer, carried,
                     extra_scratch=[pltpu.VMEM((DEC_LEN, DEC_SEQS, GROUP_WIDTH), F32)])


def _dec_gdn_kernel(blk_ref, small_ref, hist_ref, cw_ref, alog_ref, dtb_ref, nw_ref, st_ref,
                    o_ref, so_ref, q_scr, k_scr, v_scr, a_scr, b_scr, o_scr):
    h = pl.program_id(0)

    @pl.when(h == 0)
    def _():
        ones_bd = _block_ones(GROUP_WIDTH, HEAD_DIM)
        qkv = _dec_conv_silu(hist_ref, [blk_ref[t][:, 0:768] for t in range(DEC_LEN)], cw_ref[...], None)
        for t in range(DEC_LEN):
            gq, gk, gv = qkv[t][:, 0:256], qkv[t][:, 256:512], qkv[t][:, 512:768]
            q_scr[t] = (gq * lax.rsqrt(_head_sumsq(gq, ones_bd) + EPS) * (HEAD_DIM ** -0.5)).T
            k_scr[t] = (gk * lax.rsqrt(_head_sumsq(gk, ones_bd) + EPS)).T
            v_scr[t] = gv.T
            small = small_ref[t]
            b_scr[t] = _sigmoid(_expand_small(small, SMALL_GB)).T
            la = -jnp.exp(alog_ref[...]) * _softplus(_expand_small(small, SMALL_GA) + dtb_ref[...])
            a_scr[t] = jnp.exp(la).T

    hr = _head_rows(h)
    one_row = pl.ds(h * HEAD_DIM, 1)
    zero = jnp.zeros((HEAD_DIM, DEC_SEQS), F32)
    for t in range(DEC_LEN):
        a = a_scr[t, one_row, :]
        cur = st_ref if t == 0 else so_ref

        def kts(kk, r):
            return r + k_scr[t, pl.ds(h * HEAD_DIM + kk, 1), :] * cur[0, 0, kk]

        r = lax.fori_loop(0, HEAD_DIM, kts, zero)
        u = b_scr[t, one_row, :] * (v_scr[t, hr, :] - a * r)

        def upd(kk, acc):
            s = a * cur[0, 0, kk] + k_scr[t, pl.ds(h * HEAD_DIM + kk, 1), :] * u
            so_ref[0, 0, kk] = s
            return acc + q_scr[t, pl.ds(h * HEAD_DIM + kk, 1), :] * s

        o_scr[t, hr, :] = lax.fori_loop(0, HEAD_DIM, upd, zero)

    @pl.when(h == N_HEADS - 1)
    def _():
        ones_bd = _block_ones(GROUP_WIDTH, HEAD_DIM)
        for t in range(DEC_LEN):
            o = o_scr[t].T
            ss = _head_sumsq(o, ones_bd)
            o_ref[t] = o * lax.rsqrt(ss * (1.0 / HEAD_DIM) + EPS) * nw_ref[...] * _silu(blk_ref[t][:, 768:1024])


def _dec_gdn(projd, hist, conv_w, a_log, dt_bias, norm_w, state_view, layer, carried):
    small_spec = pl.BlockSpec((DEC_LEN, DEC_SEQS, 128), lambda h: (0, 0, COL_SMALL // 128))
    return _dec_call(_dec_gdn_kernel, "gdn_decode", COL_GDN,
                     (projd, projd, hist, conv_w, _lane_rep(a_log), _lane_rep(dt_bias),
                      jnp.tile(norm_w.astype(F32), N_HEADS).reshape(1, GROUP_WIDTH)),
                     [small_spec, _hist_spec(layer), _fixed1((CONV_WIDTH, 768))]
                     + [_fixed1((1, GROUP_WIDTH))] * 3, 6, state_view, layer, carried)


W_PREP_ROWS = 128


def _w_in_t_prep_kernel(a_ref, b_ref, o_ref):
    j = pl.program_id(0)
    n_plain = COL_RET // W_PREP_ROWS
    n_main = COL_SMALL // W_PREP_ROWS
    row = _iota((W_PREP_ROWS, 1), 0)
    for l in range(DEPTH):
        a = a_ref[:, l, :]
        b = b_ref[:, l, :]
        shifted = jnp.concatenate([a[8:], b[:8]], axis=0)
        small = jnp.where(row < 8, a, jnp.where(row < 12, b, 0.0))
        out = jnp.where(j < n_plain, a, jnp.where(j < n_main, shifted, small))
        o_ref[l] = out.astype(BF16)


def _prep_w_in_t(w_in):
    wt = jnp.transpose(w_in, (2, 0, 1))
    n_plain = COL_RET // W_PREP_ROWS
    n_main = COL_SMALL // W_PREP_ROWS

    def a_idx(j):
        return (jnp.where(j < n_main, j, n_plain), 0, 0)

    def b_idx(j):
        return (jnp.where(j < n_plain, j, jnp.minimum(j + 1, n_main)), 0, 0)

    blk = (W_PREP_ROWS, DEPTH, D_MODEL)
    return pl.pallas_call(
        _w_in_t_prep_kernel,
        grid=(P_PAD // W_PREP_ROWS,),
        in_specs=[pl.BlockSpec(blk, a_idx), pl.BlockSpec(blk, b_idx)],
        out_specs=pl.BlockSpec((DEPTH, W_PREP_ROWS, D_MODEL), lambda j: (0, j, 0)),
        out_shape=jax.ShapeDtypeStruct((DEPTH, P_PAD, D_MODEL), BF16),
        compiler_params=pltpu.CompilerParams(dimension_semantics=("arbitrary",), vmem_limit_bytes=VMEM_LIMIT),
        name="w_in_prep",
    )(wt, wt)


def _rotary_tables(pos):
    half = HEAD_DIM // 2
    inv_freq = RET_THETA ** (-jnp.arange(half, dtype=F32) / half)
    ang = pos.astype(F32)[:, None] * inv_freq[None, :]
    cos, sin = jnp.cos(ang), jnp.sin(ang)
    cos_t = jnp.tile(cos, (1, 2 * N_HEADS))
    sin_t = jnp.tile(jnp.concatenate([-sin, sin], axis=1), (1, N_HEADS))
    return cos_t, sin_t


RET_CHUNK = 256
SSD_CHUNK = 256
GDN_CHUNK = 64
HGRN_ROWS = 128


def _forward(x_prompt, x_sample, states, p, past_len):
    st_hg, st_gd, st_gc, st_rt, st_sd, st_sc = states
    bp, lp, _ = x_prompt.shape
    nd, ld, _ = x_sample.shape
    xp = x_prompt.astype(F32).reshape(bp * lp, D_MODEL)
    xd = jnp.transpose(x_sample.astype(F32), (1, 0, 2)).reshape(ld * nd, D_MODEL)
    cos_p, sin_p = _rotary_tables(jnp.arange(lp))
    cos_d, sin_d = _rotary_tables(past_len + jnp.arange(ld))
    outs = {k: [] for k in ("hp", "gp", "gcp", "gcs", "rp", "sp", "scp", "scs")}
    w_in_all = _prep_w_in_t(p["w_in"].astype(F32))
    wo, wu, wd = (p[k].astype(BF16) for k in ("w_out", "w_up", "w_down"))
    norm_mix = p["norm_mix"].astype(F32).reshape(DEPTH, 1, D_MODEL)
    norm_ffn = p["norm_ffn"].astype(F32).reshape(DEPTH, 1, D_MODEL)
    sv_hg, sv_gd, sv_rt, sv_sd = (jnp.transpose(s.astype(F32), (0, 2, 3, 4, 1)) for s in (st_hg, st_gd, st_rt, st_sd))
    hist_g = jnp.transpose(st_gc.astype(F32), (0, 2, 1, 3))
    hist_s = jnp.transpose(st_sc.astype(F32), (0, 2, 1, 3))
    new_hg, new_gd, new_rt, new_sd = (jnp.zeros(s.shape, F32) for s in (sv_hg, sv_gd, sv_rt, sv_sd))
    for l in range(DEPTH):
        pp = _proj(xp, norm_mix, w_in_all, l).reshape(bp, lp, P_PAD)
        pd = _proj(xd, norm_mix, w_in_all, l).reshape(ld, nd, P_PAD)

        oa, sa = _hgrn_prompt(pp, p["hgrn_lb_logits"], p["hgrn_norm"][l], l, HGRN_ROWS)
        ob, sb = _gdn_prompt(pp, p["gdn_conv_w"][l], p["gdn_a_log"][l], p["gdn_dt_bias"][l], p["gdn_norm"][l],
                             GDN_CHUNK)
        oc, sc = _ret_prompt(pp, cos_p, sin_p, RET_CHUNK)
        od, sd = _ssd_prompt(pp, p["ssd_conv_w"][l], p["ssd_conv_b"][l], p["ssd_dt_bias"][l], p["ssd_a_log"][l],
                             p["ssd_d"][l], p["ssd_norm"][l], SSD_CHUNK)
        outs["hp"].append(sa.reshape(bp, N_HEADS, HEAD_DIM, HEAD_DIM))
        outs["gp"].append(sb.reshape(bp, N_HEADS, HEAD_DIM, HEAD_DIM))
        outs["rp"].append(sc.reshape(bp, N_HEADS, HEAD_DIM, HEAD_DIM))
        outs["sp"].append(sd.reshape(bp, N_HEADS, SSD_STATE, HEAD_DIM))
        outs["gcp"].append(pp[:, lp - 3:, COL_GDN:COL_GDN + 768])
        outs["scp"].append(pp[:, lp - 3:, COL_SSD + 256:COL_SSD + 1024])
        xp = _out_ffn(xp, [o.reshape(bp * lp, GROUP_WIDTH) for o in (oa, ob, oc, od)], wo, norm_ffn, wu, wd,
                      p["norm_final"], l)

        da, new_hg = _dec_hgrn(pd, p["hgrn_lb_logits"], p["hgrn_norm"][l], sv_hg, l, new_hg)
        db, new_gd = _dec_gdn(pd, hist_g, p["gdn_conv_w"][l], p["gdn_a_log"][l], p["gdn_dt_bias"][l],
                              p["gdn_norm"][l], sv_gd, l, new_gd)
        dc, new_rt = _dec_ret(pd, cos_d, sin_d, sv_rt, l, new_rt)
        dd, new_sd = _dec_ssd(pd, hist_s, p["ssd_conv_w"][l], p["ssd_conv_b"][l], p["ssd_dt_bias"][l],
                              p["ssd_a_log"][l], p["ssd_d"][l], p["ssd_norm"][l], sv_sd, l, new_sd)
        outs["gcs"].append(jnp.transpose(pd[ld - 3:, :, COL_GDN:COL_GDN + 768], (1, 0, 2)))
        outs["scs"].append(jnp.transpose(pd[ld - 3:, :, COL_SSD + 256:COL_SSD + 1024], (1, 0, 2)))
        xd = _out_ffn(xd, [o.reshape(ld * nd, GROUP_WIDTH) for o in (da, db, dc, dd)], wo, norm_ffn, wu, wd,
                      p["norm_final"], l)

    y_prompt = xp.reshape(bp, lp, D_MODEL)
    y_sample = jnp.transpose(xd.reshape(ld, nd, D_MODEL), (1, 0, 2))
    st = {k: jnp.stack(v) for k, v in outs.items()}
    hs, gs, rs, ss = (jnp.transpose(s, (0, 4, 1, 2, 3)) for s in (new_hg, new_gd, new_rt, new_sd))
    return (y_prompt, y_sample, st["hp"], hs, st["gp"], gs, st["gcp"], st["gcs"],
            st["rp"], rs, st["sp"], ss, st["scp"], st["scs"])


def kernel(x_prompt, x_sample, state_hgrn, state_gdn, state_gdn_conv, state_ret, state_ssd, state_ssd_conv,
           norm_mix, w_in, hgrn_lb_logits, hgrn_norm, gdn_conv_w, gdn_a_log, gdn_dt_bias, gdn_norm,
           ssd_conv_w, ssd_conv_b, ssd_dt_bias, ssd_a_log, ssd_d, ssd_norm,
           w_out, norm_ffn, w_up, w_down, norm_final):
    params = dict(norm_mix=norm_mix, w_in=w_in, hgrn_lb_logits=hgrn_lb_logits, hgrn_norm=hgrn_norm,
                  gdn_conv_w=gdn_conv_w, gdn_a_log=gdn_a_log, gdn_dt_bias=gdn_dt_bias, gdn_norm=gdn_norm,
                  ssd_conv_w=ssd_conv_w, ssd_conv_b=ssd_conv_b, ssd_dt_bias=ssd_dt_bias, ssd_a_log=ssd_a_log,
                  ssd_d=ssd_d, ssd_norm=ssd_norm, w_out=w_out, norm_ffn=norm_ffn, w_up=w_up,
                  w_down=w_down, norm_final=norm_final)
    states = (state_hgrn, state_gdn, state_gdn_conv, state_ret, state_ssd, state_ssd_conv)
    return _forward(x_prompt, x_sample, states, params, 16384)
```

```python
import functools
import math

import numpy as np
import jax
import jax.numpy as jnp
from jax import lax
from jax.experimental import pallas as pl
from jax.experimental.pallas import tpu as pltpu

F32 = jnp.float32
BF16 = jnp.bfloat16

D_MODEL = 1024
GROUP_WIDTH = 256
HEAD_DIM = 64
N_HEADS = 4
CONV_WIDTH = 4
SSD_STATE = 128
D_FF = 4096
RET_THETA = 10000.0
EPS = 1e-6
DEPTH = 2

COL_HGRN = 0
COL_GDN = 1024
COL_RET = 2048
COL_SSD = 3072
COL_SMALL = 4096
P_PAD = 4224
SMALL_GA, SMALL_GB, SMALL_SDT = 0, 4, 8

VMEM_LIMIT = 56 * 1024 * 1024
LOG_GAMMA = [math.log(1.0 - 2.0 ** (-5.0 - h)) for h in range(N_HEADS)]


def _dot(a, b):
    return jnp.dot(a, b, preferred_element_type=F32)


def _dot_nt(a, b):
    return lax.dot_general(a, b, (((1,), (1,)), ((), ())), preferred_element_type=F32)


def _dot_tn(a, b):
    return lax.dot_general(a, b, (((0,), (0,)), ((), ())), preferred_element_type=F32)


def _round_robin(gens):
    live = list(gens)
    while live:
        nxt = []
        for g in live:
            try:
                next(g)
                nxt.append(g)
            except StopIteration:
                pass
        live = nxt


def _split3(x):
    hi = x.astype(BF16)
    r1 = x - hi.astype(F32)
    mid = r1.astype(BF16)
    lo = (r1 - mid.astype(F32)).astype(BF16)
    return hi, mid, lo


def _exact_dot(x, sel):
    hi, mid, lo = _split3(x)
    return _dot(hi, sel) + _dot(mid, sel) + _dot(lo, sel)


def _exact_dot_left(sel, x):
    hi, mid, lo = _split3(x)
    return _dot(sel, hi) + _dot(sel, mid) + _dot(sel, lo)


def _iota(shape, dim):
    return lax.broadcasted_iota(jnp.int32, shape, dim)


def _head_of_lane(n_lanes, width=HEAD_DIM):
    return _iota((1, n_lanes), 1) // width


def _head_masks(n_lanes=GROUP_WIDTH, width=HEAD_DIM):
    hl = _head_of_lane(n_lanes, width)
    return [hl == h for h in range(n_lanes // width)]


def _stack_heads(x, masks):
    return jnp.concatenate([jnp.where(m, x, jnp.zeros_like(x)) for m in masks], axis=0)


def _unstack_heads(y, masks, c):
    out = jnp.where(masks[0], y[0:c], 0.0)
    for h in range(1, len(masks)):
        out = out + jnp.where(masks[h], y[h * c:(h + 1) * c], 0.0)
    return out


def _block_ones(n, width, dtype=BF16):
    r = _iota((n, n), 0) // width
    c = _iota((n, n), 1) // width
    return (r == c).astype(dtype)


def _block_mask(n, rwidth, cwidth):
    return (_iota((n, n), 0) // rwidth) == (_iota((n, n), 1) // cwidth)


def _lower_tri(c, dtype=BF16):
    return (_iota((c, c), 0) >= _iota((c, c), 1)).astype(dtype)


def _cumsum_rows(x, c):
    return _exact_dot_left(_lower_tri(c), x)


def _sigmoid(x):
    return 1.0 / (1.0 + jnp.exp(-x))


def _silu(x):
    return x * _sigmoid(x)


def _softplus(x):
    return jnp.maximum(x, 0.0) + jnp.log(1.0 + jnp.exp(-jnp.abs(x)))


def _rms_rows(x):
    return x * lax.rsqrt(jnp.mean(x * x, axis=-1, keepdims=True) + EPS)


def _head_sumsq(x, ones_bd):
    sq = x * x
    hi = sq.astype(BF16)
    lo = (sq - hi.astype(F32)).astype(BF16)
    return _dot(hi, ones_bd) + _dot(lo, ones_bd)


def _expand_small(small, first_lane):
    r = _iota((128, GROUP_WIDTH), 0)
    c = _iota((128, GROUP_WIDTH), 1) // HEAD_DIM
    sel = (r == c + first_lane).astype(BF16)
    return _exact_dot(small, sel)


def _decay_diff_operands(g):
    hi, mid, lo = (x.astype(F32) for x in _split3(g))
    pos = _iota(g.shape, 1) % HEAD_DIM
    a = jnp.where(pos == 0, hi, jnp.where(pos == 1, mid, jnp.where(pos == 2, lo,
                  jnp.where(pos < 6, 1.0, 0.0))))
    b = jnp.where(pos < 3, 1.0, jnp.where(pos == 3, -hi, jnp.where(pos == 4, -mid,
                  jnp.where(pos == 5, -lo, 0.0))))
    return a, b


def _extract_blocks(s_wide, rows, width):
    sel = ((_iota((GROUP_WIDTH, width), 0) % width) == _iota((GROUP_WIDTH, width), 1)).astype(BF16)
    return _exact_dot(s_wide, sel)


def _proj_kernel(x_ref, nw_ref, w_ref, o_ref):
    h = _rms_rows(x_ref[...]) * nw_ref[0]
    o_ref[...] = _dot_nt(h.astype(BF16), w_ref[0])


PROJ_ROWS = 256


def _proj(x2d, norm_w, w_bf16, layer):
    t = x2d.shape[0]
    tm = min(t, PROJ_ROWS)
    return pl.pallas_call(
        _proj_kernel,
        grid=(t // tm,),
        in_specs=[pl.BlockSpec((tm, D_MODEL), lambda i: (i, 0)),
                  pl.BlockSpec((1, 1, D_MODEL), lambda i: (layer, 0, 0)),
                  pl.BlockSpec((1, P_PAD, D_MODEL), lambda i: (layer, 0, 0))],
        out_specs=pl.BlockSpec((tm, P_PAD), lambda i: (i, 0)),
        out_shape=jax.ShapeDtypeStruct((t, P_PAD), F32),
        compiler_params=pltpu.CompilerParams(dimension_semantics=("arbitrary",),
                                             vmem_limit_bytes=VMEM_LIMIT),
        name="norm_in_proj",
    )(x2d, norm_w, w_bf16)


def _ffn_kernel(x_ref, oa_ref, ob_ref, oc_ref, od_ref, wo_ref, nf_ref, wu_ref, wd_ref, nfin_ref,
                o_ref, *, final):
    mix = jnp.concatenate([oa_ref[...], ob_ref[...], oc_ref[...], od_ref[...]], axis=1)
    x = x_ref[...] + _dot(mix.astype(BF16), wo_ref[0])
    h = (_rms_rows(x) * nf_ref[0]).astype(BF16)
    acc = x
    ft = 1024
    for t in range(D_FF // ft):
        up = _dot(h, wu_ref[0, :, t * ft:(t + 1) * ft])
        up = jnp.square(jnp.maximum(up, 0.0)).astype(BF16)
        acc = acc + _dot(up, wd_ref[0, t * ft:(t + 1) * ft, :])
    if final:
        acc = _rms_rows(acc) * nfin_ref[...]
    o_ref[...] = acc


FFN_ROWS = 512


def _out_ffn(x2d, mixes, wo, nf, wu, wd, nfin, layer):
    t = x2d.shape[0]
    tm = min(t, FFN_ROWS)
    row = lambda i: (i, 0)
    lay = lambda i: (layer, 0, 0)
    return pl.pallas_call(
        functools.partial(_ffn_kernel, final=(layer == DEPTH - 1)),
        grid=(t // tm,),
        in_specs=[pl.BlockSpec((tm, D_MODEL), row)]
                 + [pl.BlockSpec((tm, GROUP_WIDTH), row)] * 4
                 + [pl.BlockSpec((1, D_MODEL, D_MODEL), lay),
                    pl.BlockSpec((1, 1, D_MODEL), lay),
                    pl.BlockSpec((1, D_MODEL, D_FF), lay),
                    pl.BlockSpec((1, D_FF, D_MODEL), lay),
                    pl.BlockSpec((1, D_MODEL), lambda i: (0, 0))],
        out_specs=pl.BlockSpec((tm, D_MODEL), row),
        out_shape=jax.ShapeDtypeStruct((t, D_MODEL), F32),
        compiler_params=pltpu.CompilerParams(dimension_semantics=("arbitrary",),
                                             vmem_limit_bytes=VMEM_LIMIT),
        name="out_proj_ffn",
    )(x2d, *mixes, wo, nf, wu, wd, nfin.reshape(1, D_MODEL))


def _swap_halves(x):
    first = (_iota((1, 128), 1) % HEAD_DIM) < (HEAD_DIM // 2)
    parts = []
    for p in range(GROUP_WIDTH // 128):
        xp = x[:, p * 128:(p + 1) * 128]
        parts.append(jnp.where(first, pltpu.roll(xp, 96, 1), pltpu.roll(xp, 32, 1)))
    return jnp.concatenate(parts, axis=1)


def _conv_silu(xe_ref, halo, x, w, bias, first_chunk, c):
    xe_ref[0:8, :] = jnp.where(first_chunk, jnp.zeros_like(halo), halo)
    xe_ref[8:, :] = x
    y = w[3:4, :] * x
    for j in range(CONV_WIDTH - 1):
        y = y + w[j:j + 1, :] * xe_ref[5 + j:5 + j + c, :]
    if bias is not None:
        y = y + bias
    return _silu(y)


def _ret_prompt_kernel(blk_ref, cos_ref, sin_ref, o_ref, st_ref, s_scr, *, c, n_chunks, nb):
    ci = pl.program_id(1)

    @pl.when(ci == 0)
    def _():
        s_scr[...] = jnp.zeros_like(s_scr)

    cosv, sinv = cos_ref[...], sin_ref[...]
    masks = _head_masks()
    hl = _head_of_lane(GROUP_WIDTH)
    lg = jnp.full((1, GROUP_WIDTH), LOG_GAMMA[0], F32)
    for h in range(1, N_HEADS):
        lg = jnp.where(hl == h, LOG_GAMMA[h], lg)
    ri = _iota((c, 1), 0).astype(F32)
    dij = (_iota((c, c), 0) - _iota((c, c), 1)).astype(F32)
    causal = dij >= 0.0
    decay = jnp.concatenate(
        [jnp.where(causal, jnp.exp(jnp.maximum(dij, 0.0) * LOG_GAMMA[h]), 0.0) for h in range(N_HEADS)],
        axis=0)
    q_scale = jnp.exp((ri + 1.0) * lg)
    k_scale = jnp.exp((float(c - 1) - ri) * lg) * (HEAD_DIM ** -0.5)
    s_scale = jnp.exp(float(c) * lg)
    bd_mask = _block_mask(GROUP_WIDTH, HEAD_DIM, HEAD_DIM)
    ones_bd = _block_ones(GROUP_WIDTH, HEAD_DIM)

    def one_sequence(sq):
        blk = blk_ref[sq]
        rq, rk, rv, rg = (blk[:, i * GROUP_WIDTH:(i + 1) * GROUP_WIDTH] for i in range(4))
        q = rq * cosv + _swap_halves(rq) * sinv
        k = rk * cosv + _swap_halves(rk) * sinv
        v = rv.astype(BF16)
        s = s_scr[sq]
        qk = _dot_nt(_stack_heads(q, masks).astype(BF16), k.astype(BF16))
        o_inter = _dot((q * q_scale).astype(BF16), s.astype(BF16))
        ds = _dot_tn((k * k_scale).astype(BF16), v)
        yield
        scores = qk * (decay * (HEAD_DIM ** -0.5))
        pv = _dot(scores.astype(BF16), v)
        s_scr[sq] = s_scale * s + jnp.where(bd_mask, ds, 0.0)
        yield
        o = _unstack_heads(pv, masks, c) + o_inter
        ss = _head_sumsq(o, ones_bd)
        yield
        o_ref[sq] = o * lax.rsqrt(ss * (1.0 / HEAD_DIM) + EPS) * _silu(rg)

    _round_robin([one_sequence(sq) for sq in range(nb)])

    @pl.when(ci == n_chunks - 1)
    def _():
        for sq in range(nb):
            st_ref[sq] = _extract_blocks(s_scr[sq], GROUP_WIDTH, HEAD_DIM)


PROMPT_SEQS_PER_STEP = 8
GDN_SEQS_PER_STEP = 8
PROMPT_ROWS_PER_STEP = 2048


def _ret_prompt(proj3, cos_t, sin_t, c):
    b, l, _ = proj3.shape
    n = l // c
    nb = math.gcd(b, min(PROMPT_SEQS_PER_STEP, max(1, PROMPT_ROWS_PER_STEP // (l // n))))
    return pl.pallas_call(
        functools.partial(_ret_prompt_kernel, c=c, n_chunks=n, nb=nb),
        grid=(b // nb, n),
        in_specs=[pl.BlockSpec((nb, c, 1024), lambda bi, ci: (bi, ci, COL_RET // 1024)),
                  pl.BlockSpec((c, GROUP_WIDTH), lambda bi, ci: (ci, 0)),
                  pl.BlockSpec((c, GROUP_WIDTH), lambda bi, ci: (ci, 0))],
        out_specs=[pl.BlockSpec((nb, c, GROUP_WIDTH), lambda bi, ci: (bi, ci, 0)),
                   pl.BlockSpec((nb, GROUP_WIDTH, HEAD_DIM), lambda bi, ci: (bi, 0, 0))],
        out_shape=[jax.ShapeDtypeStruct((b, l, GROUP_WIDTH), F32),
                   jax.ShapeDtypeStruct((b, GROUP_WIDTH, HEAD_DIM), F32)],
        scratch_shapes=[pltpu.VMEM((nb, GROUP_WIDTH, GROUP_WIDTH), F32)],
        compiler_params=pltpu.CompilerParams(dimension_semantics=("arbitrary", "arbitrary"),
                                             vmem_limit_bytes=VMEM_LIMIT),
        name="retention_prompt",
    )(proj3, cos_t, sin_t)


def _ssd_prompt_kernel(blk_ref, halo_ref, small_ref, cw_ref, cb_ref, dtb_ref, alog_ref, dskip_ref, nw_ref,
                       o_ref, st_ref, s_scr, xe_scr, *, c, n_chunks, nb):
    ci = pl.program_id(1)

    @pl.when(ci == 0)
    def _():
        s_scr[...] = jnp.zeros_like(s_scr)

    masks = _head_masks()
    causal = _iota((c, c), 0) >= _iota((c, c), 1)
    causal4 = jnp.concatenate([causal] * N_HEADS, axis=0)
    group_mask = _block_mask(GROUP_WIDTH, 128, 128)
    tri = _lower_tri(c)
    neg_a = -jnp.exp(alog_ref[...]) * LOG2E

    def one_sequence(sq):
        blk = blk_ref[sq]
        sz = blk[:, 0:GROUP_WIDTH]
        xbc = _conv_silu(xe_scr.at[sq], halo_ref[sq][:, GROUP_WIDTH:], blk[:, GROUP_WIDTH:], cw_ref[...],
                         cb_ref[...], ci == 0, c)
        xs = xbc[:, 0:256]
        bmat = xbc[:, 256:512].astype(BF16)
        cmat = xbc[:, 512:768].astype(BF16)
        s = s_scr[sq]
        cb = [_dot_nt(cmat[:, gi * 128:(gi + 1) * 128], bmat[:, gi * 128:(gi + 1) * 128]) for gi in range(2)]
        y_inter = _dot(cmat, s.astype(BF16))
        dt = _softplus(_expand_small(small_ref[sq], SMALL_SDT) + dtb_ref[...])
        yield
        g = _exact_dot_left(tri, neg_a * dt)
        yield
        g_last = g[c - 1:c, :]
        da, db = _decay_diff_operands(g)
        diff = _dot_nt(_stack_heads(da, masks).astype(BF16), db.astype(BF16))
        v = xs * dt
        vend = v * jnp.exp2(g_last - g)
        ds = _dot_tn(bmat, vend.astype(BF16))
        yield
        decay = jnp.where(causal4, jnp.exp2(diff), 0.0)
        scores = jnp.concatenate([cb[0], cb[0], cb[1], cb[1]], axis=0) * decay
        pv = _dot(scores.astype(BF16), v.astype(BF16))
        s_scr[sq] = jnp.exp2(g_last) * s + jnp.where(group_mask, ds, 0.0)
        yield
        y = _unstack_heads(pv, masks, c) + y_inter * jnp.exp2(g)
        y = (y + dskip_ref[...] * xs) * _silu(sz)
        halves = [_rms_rows(y[:, gi * 128:(gi + 1) * 128]) for gi in range(2)]
        o_ref[sq] = jnp.concatenate(halves, axis=1) * nw_ref[...]

    _round_robin([one_sequence(sq) for sq in range(nb)])

    @pl.when(ci == n_chunks - 1)
    def _():
        for sq in range(nb):
            for h in range(N_HEADS):
                gi = h // 2
                rows = jnp.where(masks[h], s_scr[sq, gi * 128:(gi + 1) * 128, :], 0.0)
                st_ref[sq, h * 128:(h + 1) * 128, :] = _extract_blocks(rows, 128, HEAD_DIM)


def _lane_rep(p):
    return jnp.repeat(p.astype(F32), HEAD_DIM).reshape(1, GROUP_WIDTH)


def _ssd_prompt(proj3, conv_w, conv_b, dt_bias, a_log, d_skip, norm_w, c):
    b, l, _ = proj3.shape
    n = l // c
    fixed = lambda bi, ci: (0, 0)
    nb = math.gcd(b, min(PROMPT_SEQS_PER_STEP, max(1, PROMPT_ROWS_PER_STEP // (l // n))))
    return pl.pallas_call(
        functools.partial(_ssd_prompt_kernel, c=c, n_chunks=n, nb=nb),
        grid=(b // nb, n),
        in_specs=[pl.BlockSpec((nb, c, 1024), lambda bi, ci: (bi, ci, COL_SSD // 1024)),
                  pl.BlockSpec((nb, 8, 1024), lambda bi, ci: (bi, jnp.maximum(ci * (c // 8) - 1, 0), COL_SSD // 1024)),
                  pl.BlockSpec((nb, c, 128), lambda bi, ci: (bi, ci, COL_SMALL // 128)),
                  pl.BlockSpec((CONV_WIDTH, 768), fixed),
                  pl.BlockSpec((1, 768), fixed),
                  pl.BlockSpec((1, GROUP_WIDTH), fixed),
                  pl.BlockSpec((1, GROUP_WIDTH), fixed),
                  pl.BlockSpec((1, GROUP_WIDTH), fixed),
                  pl.BlockSpec((1, GROUP_WIDTH), fixed)],
        out_specs=[pl.BlockSpec((nb, c, GROUP_WIDTH), lambda bi, ci: (bi, ci, 0)),
                   pl.BlockSpec((nb, N_HEADS * SSD_STATE, HEAD_DIM), lambda bi, ci: (bi, 0, 0))],
        out_shape=[jax.ShapeDtypeStruct((b, l, GROUP_WIDTH), F32),
                   jax.ShapeDtypeStruct((b, N_HEADS * SSD_STATE, HEAD_DIM), F32)],
        scratch_shapes=[pltpu.VMEM((nb, GROUP_WIDTH, GROUP_WIDTH), F32),
                        pltpu.VMEM((nb, c + 8, 768), F32)],
        compiler_params=pltpu.CompilerParams(dimension_semantics=("arbitrary", "arbitrary"),
                                             vmem_limit_bytes=VMEM_LIMIT),
        name="ssd_prompt",
    )(proj3, proj3, proj3, conv_w, conv_b.reshape(1, 768), _lane_rep(dt_bias), _lane_rep(a_log),
      _lane_rep(d_skip), norm_w.reshape(1, GROUP_WIDTH))


def _gdn_prompt_kernel(blk_ref, halo_ref, small_ref, cw_ref, alog_ref, dtb_ref, nw_ref,
                       o_ref, st_ref, s_scr, xe_scr, m_scr, *, c, n_chunks, nb):
    ci = pl.program_id(1)
    hc = 2 * c
    n_lvl = int(math.log2(c))

    @pl.when(ci == 0)
    def _():
        s_scr[...] = jnp.zeros_like(s_scr)

    @pl.when((pl.program_id(0) == 0) & (ci == 0))
    def _():
        rr = _iota((hc, hc), 0)
        cc = _iota((hc, hc), 1)
        same = (rr // c) == (cc // c)
        m_scr[0] = (same & (rr >= cc)).astype(F32)
        m_scr[1] = (same & (rr > cc)).astype(F32)
        for lv in range(n_lvl):
            sz = 1 << lv
            off = ((rr // (2 * sz)) == (cc // (2 * sz))) & (((rr // sz) % 2) == 1) & (((cc // sz) % 2) == 0)
            m_scr[2 + lv] = off.astype(F32)

    ones_bd = _block_ones(GROUP_WIDTH, HEAD_DIM)
    bd_mask = _block_mask(GROUP_WIDTH, HEAD_DIM, HEAD_DIM)
    masks = _head_masks()
    pair_masks = [masks[0:2], masks[2:4]]
    tri = _lower_tri(c)
    neg_a = -jnp.exp(alog_ref[...]) * LOG2E

    def one_sequence(sq):
        blk = blk_ref[sq]
        gz = blk[:, 768:1024]
        qkv = _conv_silu(xe_scr.at[sq], halo_ref[sq][:, 0:768], blk[:, 0:768], cw_ref[...], None, ci == 0, c)
        gq, gk, v = qkv[:, 0:256], qkv[:, 256:512], qkv[:, 512:768]
        q = gq * lax.rsqrt(_head_sumsq(gq, ones_bd) + EPS) * (HEAD_DIM ** -0.5)
        k = gk * lax.rsqrt(_head_sumsq(gk, ones_bd) + EPS)
        yield
        small = small_ref[sq]
        beta = _sigmoid(_expand_small(small, SMALL_GB))
        g = _exact_dot_left(tri, neg_a * _softplus(_expand_small(small, SMALL_GA) + dtb_ref[...]))
        g_last = g[c - 1:c, :]
        eg = jnp.exp2(g)
        yield
        da, db = _decay_diff_operands(g)
        bk = beta * k
        bkg = bk * eg
        bv = beta * v
        a_mat, p_mat, x = [], [], []
        for pm in pair_masks:
            diff = _dot_nt(_stack_heads(da, pm).astype(BF16), _stack_heads(db, pm).astype(BF16))
            k_st = _stack_heads(k, pm).astype(BF16)
            kk = _dot_nt(_stack_heads(bk, pm).astype(BF16), k_st)
            qk = _dot_nt(_stack_heads(q, pm).astype(BF16), k_st)
            decay = jnp.exp2(jnp.minimum(diff, 0.0))
            a_mat.append(kk * (decay * m_scr[1]))
            p_mat.append((qk * (decay * m_scr[0])).astype(BF16))
            x.append(jnp.concatenate([_stack_heads(bkg, pm), _stack_heads(bv, pm)], axis=1))
        yield

        n_mat = [-(a * m_scr[2]) for a in a_mat]
        for lv in range(1, n_lvl):
            a_off = [a * m_scr[2 + lv] for a in a_mat]
            m = [ao + _dot(ao.astype(BF16), n.astype(BF16)) for ao, n in zip(a_off, n_mat)]
            yield
            n_mat = [n - mm - _dot(n.astype(BF16), mm.astype(BF16)) for n, mm in zip(n_mat, m)]
            yield
        x = [xx + _dot(n.astype(BF16), xx.astype(BF16)) for xx, n in zip(x, n_mat)]
        yield
        w = x[0][0:c, 0:256] + x[0][c:2 * c, 0:256] + x[1][0:c, 0:256] + x[1][c:2 * c, 0:256]
        u0 = x[0][0:c, 256:512] + x[0][c:2 * c, 256:512] + x[1][0:c, 256:512] + x[1][c:2 * c, 256:512]

        s = s_scr[sq]
        s_bf = s.astype(BF16)
        u = u0 - _dot(w.astype(BF16), s_bf)
        o = _dot((q * eg).astype(BF16), s_bf)
        yield
        pu = [_dot(pmat, _stack_heads(u, pm).astype(BF16)) for pmat, pm in zip(p_mat, pair_masks)]
        kend = k * jnp.exp2(g_last - g)
        ds = _dot_tn(kend.astype(BF16), u.astype(BF16))
        yield
        for pu_p in pu:
            o = o + pu_p[0:c] + pu_p[c:2 * c]
        s_scr[sq] = jnp.exp2(g_last) * s + jnp.where(bd_mask, ds, 0.0)
        ss = _head_sumsq(o, ones_bd)
        o_ref[sq] = o * lax.rsqrt(ss * (1.0 / HEAD_DIM) + EPS) * nw_ref[...] * _silu(gz)

    _round_robin([one_sequence(sq) for sq in range(nb)])

    @pl.when(ci == n_chunks - 1)
    def _():
        for sq in range(nb):
            st_ref[sq] = _extract_blocks(s_scr[sq], GROUP_WIDTH, HEAD_DIM)


def _gdn_prompt(proj3, conv_w, a_log, dt_bias, norm_w, c):
    b, l, _ = proj3.shape
    n = l // c
    fixed = lambda bi, ci: (0, 0)
    nb = math.gcd(b, GDN_SEQS_PER_STEP)
    hc = 2 * c
    return pl.pallas_call(
        functools.partial(_gdn_prompt_kernel, c=c, n_chunks=n, nb=nb),
        grid=(b // nb, n),
        in_specs=[pl.BlockSpec((nb, c, 1024), lambda bi, ci: (bi, ci, COL_GDN // 1024)),
                  pl.BlockSpec((nb, 8, 1024), lambda bi, ci: (bi, jnp.maximum(ci * (c // 8) - 1, 0), COL_GDN // 1024)),
                  pl.BlockSpec((nb, c, 128), lambda bi, ci: (bi, ci, COL_SMALL // 128)),
                  pl.BlockSpec((CONV_WIDTH, 768), fixed),
                  pl.BlockSpec((1, GROUP_WIDTH), fixed),
                  pl.BlockSpec((1, GROUP_WIDTH), fixed),
                  pl.BlockSpec((1, GROUP_WIDTH), fixed)],
        out_specs=[pl.BlockSpec((nb, c, GROUP_WIDTH), lambda bi, ci: (bi, ci, 0)),
                   pl.BlockSpec((nb, GROUP_WIDTH, HEAD_DIM), lambda bi, ci: (bi, 0, 0))],
        out_shape=[jax.ShapeDtypeStruct((b, l, GROUP_WIDTH), F32),
                   jax.ShapeDtypeStruct((b, GROUP_WIDTH, HEAD_DIM), F32)],
        scratch_shapes=[pltpu.VMEM((nb, GROUP_WIDTH, GROUP_WIDTH), F32),
                        pltpu.VMEM((nb, c + 8, 768), F32),
                        pltpu.VMEM((2 + int(math.log2(c)), hc, hc), F32)],
        compiler_params=pltpu.CompilerParams(dimension_semantics=("arbitrary", "arbitrary"),
                                             vmem_limit_bytes=VMEM_LIMIT),
        name="gdn_prompt",
    )(proj3, proj3, proj3, conv_w, _lane_rep(a_log), _lane_rep(dt_bias),
      jnp.tile(norm_w.astype(F32), N_HEADS).reshape(1, GROUP_WIDTH))


HGRN_SUB = 16
LOG2E = 1.4426950408889634


def _hgrn_lower_bound(logits, layer):
    rows = [logits[d:d + 1, :] for d in range(DEPTH)]
    mx = functools.reduce(jnp.maximum, rows)
    es = [jnp.exp(x - mx) for x in rows]
    tot = functools.reduce(lambda a, b: a + b, es)
    sm = [e / tot for e in es]
    acc = sm[0]
    for d in range(1, layer + 1):
        acc = acc + sm[d]
    return acc - sm[0]


def _hgrn_prompt_kernel(blk_ref, lb_ref, nw_ref, o_ref, st_ref, s_scr, *, r, n_chunks, layer, nb):
    ci = pl.program_id(1)
    sub = HGRN_SUB
    n_sub = r // sub

    @pl.when(ci == 0)
    def _():
        s_scr[...] = jnp.zeros_like(s_scr)

    lb = _hgrn_lower_bound(lb_ref[...], layer)
    rr = _iota((r, r), 0)
    cc = _iota((r, r), 1)
    same_sub = (rr // sub) == (cc // sub)
    cum_sel = (same_sub & (rr >= cc)).astype(BF16)
    tot_sel = same_sub.astype(BF16)
    ones_bd = _block_ones(GROUP_WIDTH, HEAD_DIM)
    masks = _head_masks()
    half = sub // 2
    i8 = _iota((half, 1), 0)

    def one_sequence(sq):
        blk = blk_ref[sq]
        hq, hf, hi, hg = (blk[:, i * GROUP_WIDTH:(i + 1) * GROUP_WIDTH] for i in range(4))
        f = lb + (1.0 - lb) * _sigmoid(hf)
        q = _sigmoid(hq)
        k = 1.0 - f
        v = hi
        logf = jnp.log(f)
        g = _exact_dot_left(cum_sel, logf)
        g_tot = _exact_dot_left(tot_sel, logf)
        yield
        a2 = (g + jnp.log(q)) * LOG2E
        h2 = (g - jnp.log(k)) * LOG2E
        gt2 = g_tot * LOG2E
        qt = _stack_heads(jnp.exp2(a2), masks).astype(BF16)
        kh = _stack_heads(jnp.exp2(gt2 - h2), masks).astype(BF16)
        v_heads = [v[:, h * HEAD_DIM:(h + 1) * HEAD_DIM].astype(BF16) for h in range(N_HEADS)]

        def sub_rows(x_st, lo):
            return jnp.concatenate([x_st[h * r + lo:h * r + lo + sub] for h in range(N_HEADS)], axis=0)

        s = s_scr[sq]
        outs = []
        for j in range(n_sub):
            lo = j * sub
            v_j = v[lo:lo + sub]
            a_lo, a_hi, h_j = a2[lo:lo + half], a2[lo + half:lo + sub], h2[lo:lo + sub]
            lo_blocks, hi_blocks = [], []
            for jj in range(sub):
                h_row = h_j[jj:jj + 1, :]
                if jj < half:
                    e_lo = jnp.exp2(a_lo - h_row)
                    lo_blocks.append(e_lo if jj == 0 else jnp.where(i8 >= jj, e_lo, 0.0))
                    hi_blocks.append(jnp.exp2(a_hi - h_row))
                else:
                    e_hi = jnp.exp2(a_hi - h_row)
                    hi_blocks.append(e_hi if jj == half else jnp.where(i8 >= jj - half, e_hi, 0.0))
            sc = _dot(jnp.concatenate(lo_blocks + hi_blocks, axis=0).astype(BF16), ones_bd)
            oi = _dot_nt(sub_rows(qt, lo), s.astype(BF16))
            v_rows = jnp.concatenate([vh[lo:lo + sub] for vh in v_heads], axis=0)
            ds = _dot_tn(v_rows, sub_rows(kh, lo))
            yield
            o_inter = jnp.concatenate([oi[h * sub:(h + 1) * sub] for h in range(N_HEADS)], axis=1)
            n_lo = half * half
            o_lo = sc[0:half] * v_j[0:1, :]
            o_hi = sc[n_lo:n_lo + half] * v_j[0:1, :]
            for jj in range(1, sub):
                if jj < half:
                    o_lo = o_lo + sc[jj * half:(jj + 1) * half] * v_j[jj:jj + 1, :]
                o_hi = o_hi + sc[n_lo + jj * half:n_lo + (jj + 1) * half] * v_j[jj:jj + 1, :]
            outs.append(jnp.concatenate([o_lo, o_hi], axis=0) + o_inter)
            s = jnp.exp2(gt2[lo:lo + 1, :]) * s + ds
        s_scr[sq] = s

        o = jnp.concatenate(outs, axis=0)
        ss = _head_sumsq(o, ones_bd)
        yield
        o_ref[sq] = o * lax.rsqrt(ss * (1.0 / HEAD_DIM) + EPS) * nw_ref[...] * _silu(hg)

    _round_robin([one_sequence(sq) for sq in range(nb)])

    @pl.when(ci == n_chunks - 1)
    def _():
        for sq in range(nb):
            st_ref[sq] = s_scr[sq].T


def _hgrn_prompt(proj3, lb_logits, norm_w, layer, r):
    b, l, _ = proj3.shape
    n = l // r
    fixed = lambda bi, ci: (0, 0)
    nb = math.gcd(b, min(PROMPT_SEQS_PER_STEP, max(1, PROMPT_ROWS_PER_STEP // (l // n))))
    return pl.pallas_call(
        functools.partial(_hgrn_prompt_kernel, r=r, n_chunks=n, layer=layer, nb=nb),
        grid=(b // nb, n),
        in_specs=[pl.BlockSpec((nb, r, 1024), lambda bi, ci: (bi, ci, COL_HGRN // 1024)),
                  pl.BlockSpec((DEPTH, GROUP_WIDTH), fixed),
                  pl.BlockSpec((1, GROUP_WIDTH), fixed)],
        out_specs=[pl.BlockSpec((nb, r, GROUP_WIDTH), lambda bi, ci: (bi, ci, 0)),
                   pl.BlockSpec((nb, GROUP_WIDTH, HEAD_DIM), lambda bi, ci: (bi, 0, 0))],
        out_shape=[jax.ShapeDtypeStruct((b, l, GROUP_WIDTH), F32),
                   jax.ShapeDtypeStruct((b, GROUP_WIDTH, HEAD_DIM), F32)],
        scratch_shapes=[pltpu.VMEM((nb, HEAD_DIM, GROUP_WIDTH), F32)],
        compiler_params=pltpu.CompilerParams(dimension_semantics=("arbitrary", "arbitrary"),
                                             vmem_limit_bytes=VMEM_LIMIT),
        name="hgrn_prompt",
    )(proj3, lb_logits.astype(F32), jnp.tile(norm_w.astype(F32), N_HEADS).reshape(1, GROUP_WIDTH))


DEC_SEQS = 128
DEC_LEN = 4


def _head_rows(h):
    return pl.ds(pl.multiple_of(h * HEAD_DIM, HEAD_DIM), HEAD_DIM)


def _recur_head(load_s, store_s, n_keys, decay_fn, k_fn, q_fn, v_blocks):
    def body(kk, accs):
        s = load_s(kk)
        accs = list(accs)
        for t in range(DEC_LEN):
            s = decay_fn(t, kk) * s + k_fn(t, kk) * v_blocks[t]
            accs[t] = accs[t] + q_fn(t, kk) * s
        store_s(kk, s)
        return tuple(accs)

    zero = jnp.zeros((HEAD_DIM, DEC_SEQS), F32)
    return lax.fori_loop(0, n_keys, body, (zero,) * DEC_LEN)


def _dec_ret_kernel(blk_ref, cos_ref, sin_ref, st_ref, o_ref, so_ref, q_scr, k_scr, v_scr, o_scr):
    h = pl.program_id(0)

    @pl.when(h == 0)
    def _():
        for t in range(DEC_LEN):
            blk = blk_ref[t]
            rq, rk, rv = blk[:, 0:256], blk[:, 256:512], blk[:, 512:768]
            cosv, sinv = cos_ref[t:t + 1, :], sin_ref[t:t + 1, :]
            q_scr[t] = (rq * cosv + _swap_halves(rq) * sinv).T
            k_scr[t] = ((rk * cosv + _swap_halves(rk) * sinv) * (HEAD_DIM ** -0.5)).T
            v_scr[t] = rv.T

    lg = jnp.where(h == 0, LOG_GAMMA[0], jnp.where(h == 1, LOG_GAMMA[1], jnp.where(h == 2, LOG_GAMMA[2], LOG_GAMMA[3])))
    gamma = jnp.exp(jnp.full((1, DEC_SEQS), lg, F32))
    hr = _head_rows(h)
    v_blocks = [v_scr[t, hr, :] for t in range(DEC_LEN)]
    accs = _recur_head(
        lambda kk: st_ref[0, 0, kk], functools.partial(_store_state, so_ref), HEAD_DIM,
        lambda t, kk: gamma,
        lambda t, kk: k_scr[t, pl.ds(h * HEAD_DIM + kk, 1), :],
        lambda t, kk: q_scr[t, pl.ds(h * HEAD_DIM + kk, 1), :],
        v_blocks)
    for t in range(DEC_LEN):
        o_scr[t, hr, :] = accs[t]

    @pl.when(h == N_HEADS - 1)
    def _():
        ones_bd = _block_ones(GROUP_WIDTH, HEAD_DIM)
        for t in range(DEC_LEN):
            o = o_scr[t].T
            ss = _head_sumsq(o, ones_bd)
            o_ref[t] = o * lax.rsqrt(ss * (1.0 / HEAD_DIM) + EPS) * _silu(blk_ref[t][:, 768:1024])


def _store_state(so_ref, kk, s):
    so_ref[0, 0, kk] = s


def _without_ref(kernel_fn, idx):
    def wrapped(*refs):
        return kernel_fn(*refs[:idx], *refs[idx + 1:])
    return wrapped


def _zero_later_layers(kernel_fn, so_index):
    def wrapped(*refs):
        kernel_fn(*refs)
        so_ref = refs[so_index]
        so_ref[1:] = jnp.zeros((so_ref.shape[0] - 1,) + so_ref.shape[1:], F32)
    return wrapped


def _dec_call(kernel_fn, name, col, ins, in_specs, n_tok_scr, state_view, layer, carried, extra_scratch=()):
    blk_spec = pl.BlockSpec((DEC_LEN, DEC_SEQS, 1024), lambda h: (0, 0, col // 1024))
    st_spec = pl.BlockSpec((1, 1) + state_view.shape[2:], lambda h: (layer, h, 0, 0, 0))
    tok_scr = pltpu.VMEM((DEC_LEN, GROUP_WIDTH, DEC_SEQS), F32)
    ins = tuple(ins) + (state_view,)
    specs = [blk_spec] + in_specs + [st_spec]
    if carried is None:
        assert layer == 0
        so_spec = pl.BlockSpec((DEPTH, 1) + state_view.shape[2:], lambda h: (0, h, 0, 0, 0))
        kernel_fn = _zero_later_layers(kernel_fn, len(ins) + 1)
        aliases = {}
    else:
        so_spec = st_spec
        kernel_fn = _without_ref(kernel_fn, len(ins))
        aliases = {len(ins): 1}
        ins = ins + (carried,)
        specs = specs + [pl.BlockSpec(memory_space=pl.ANY)]
    return pl.pallas_call(
        kernel_fn,
        grid=(N_HEADS,),
        in_specs=specs,
        out_specs=[pl.BlockSpec((DEC_LEN, DEC_SEQS, GROUP_WIDTH), lambda h: (0, 0, 0)), so_spec],
        out_shape=[jax.ShapeDtypeStruct((DEC_LEN, DEC_SEQS, GROUP_WIDTH), F32),
                   jax.ShapeDtypeStruct(state_view.shape, F32)],
        scratch_shapes=[tok_scr] * n_tok_scr + list(extra_scratch),
        input_output_aliases=aliases,
        compiler_params=pltpu.CompilerParams(dimension_semantics=("arbitrary",), vmem_limit_bytes=VMEM_LIMIT),
        name=name,
    )(*ins)


def _fixed1(shape):
    return pl.BlockSpec(shape, lambda h: (0,) * len(shape))


def _dec_ret(projd, cos_t, sin_t, state_view, layer, carried):
    return _dec_call(_dec_ret_kernel, "retention_decode", COL_RET, (projd, cos_t, sin_t),
                     [_fixed1((DEC_LEN, GROUP_WIDTH)), _fixed1((DEC_LEN, GROUP_WIDTH))], 4,
                     state_view, layer, carried)


def _dec_hgrn_kernel(blk_ref, lb_ref, nw_ref, st_ref, o_ref, so_ref, q_scr, k_scr, v_scr, f_scr, o_scr, *, layer):
    h = pl.program_id(0)

    @pl.when(h == 0)
    def _():
        lb = _hgrn_lower_bound(lb_ref[...], layer)
        for t in range(DEC_LEN):
            blk = blk_ref[t]
            f = lb + (1.0 - lb) * _sigmoid(blk[:, 256:512])
            q_scr[t] = _sigmoid(blk[:, 0:256]).T
            k_scr[t] = (1.0 - f).T
            v_scr[t] = blk[:, 512:768].T
            f_scr[t] = f.T

    hr = _head_rows(h)
    v_blocks = [v_scr[t, hr, :] for t in range(DEC_LEN)]
    row = lambda scr: (lambda t, kk: scr[t, pl.ds(h * HEAD_DIM + kk, 1), :])
    accs = _recur_head(lambda kk: st_ref[0, 0, kk], functools.partial(_store_state, so_ref), HEAD_DIM,
                       row(f_scr), row(k_scr), row(q_scr), v_blocks)
    for t in range(DEC_LEN):
        o_scr[t, hr, :] = accs[t]

    @pl.when(h == N_HEADS - 1)
    def _():
        ones_bd = _block_ones(GROUP_WIDTH, HEAD_DIM)
        for t in range(DEC_LEN):
            o = o_scr[t].T
            ss = _head_sumsq(o, ones_bd)
            o_ref[t] = o * lax.rsqrt(ss * (1.0 / HEAD_DIM) + EPS) * nw_ref[...] * _silu(blk_ref[t][:, 768:1024])


def _dec_hgrn(projd, lb_logits, norm_w, state_view, layer, carried):
    return _dec_call(functools.partial(_dec_hgrn_kernel, layer=layer), "hgrn_decode", COL_HGRN,
                     (projd, lb_logits.astype(F32), jnp.tile(norm_w.astype(F32), N_HEADS).reshape(1, GROUP_WIDTH)),
                     [_fixed1((DEPTH, GROUP_WIDTH)), _fixed1((1, GROUP_WIDTH))], 5, state_view, layer, carried)


def _hist_spec(layer):
    return pl.BlockSpec((1, CONV_WIDTH - 1, DEC_SEQS, 768), lambda h: (layer, 0, 0, 0))


def _dec_conv_silu(hist_ref, xs, w, bias):
    xe = [hist_ref[0, j] for j in range(CONV_WIDTH - 1)] + xs
    out = []
    for t in range(DEC_LEN):
        y = xe[t] * w[0:1, :]
        for j in range(1, CONV_WIDTH):
            y = y + xe[t + j] * w[j:j + 1, :]
        if bias is not None:
            y = y + bias
        out.append(_silu(y))
    return out


def _dec_ssd_kernel(blk_ref, small_ref, hist_ref, cw_ref, cb_ref, dtb_ref, alog_ref, dskip_ref, nw_ref, st_ref,
                    o_ref, so_ref, c_scr, b_scr, v_scr, a_scr, o_scr, x_scr):
    h = pl.program_id(0)

    @pl.when(h == 0)
    def _():
        xbc = _dec_conv_silu(hist_ref, [blk_ref[t][:, 256:1024] for t in range(DEC_LEN)], cw_ref[...], cb_ref[...])
        for t in range(DEC_LEN):
            xs = xbc[t][:, 0:256]
            dt = _softplus(_expand_small(small_ref[t], SMALL_SDT) + dtb_ref[...])
            x_scr[t] = xs
            v_scr[t] = (xs * dt).T
            b_scr[t] = xbc[t][:, 256:512].T
            c_scr[t] = xbc[t][:, 512:768].T
            a_scr[t] = jnp.exp(-jnp.exp(alog_ref[...]) * dt).T

    hr = _head_rows(h)
    g0 = (h // 2) * SSD_STATE
    v_blocks = [v_scr[t, hr, :] for t in range(DEC_LEN)]
    accs = _recur_head(
        lambda kk: st_ref[0, 0, kk], functools.partial(_store_state, so_ref), SSD_STATE,
        lambda t, kk: a_scr[t, pl.ds(h * HEAD_DIM, 1), :],
        lambda t, kk: b_scr[t, pl.ds(g0 + kk, 1), :],
        lambda t, kk: c_scr[t, pl.ds(g0 + kk, 1), :],
        v_blocks)
    for t in range(DEC_LEN):
        o_scr[t, hr, :] = accs[t]

    @pl.when(h == N_HEADS - 1)
    def _():
        for t in range(DEC_LEN):
            y = (o_scr[t].T + dskip_ref[...] * x_scr[t]) * _silu(blk_ref[t][:, 0:256])
            halves = [_rms_rows(y[:, gi * 128:(gi + 1) * 128]) for gi in range(2)]
            o_ref[t] = jnp.concatenate(halves, axis=1) * nw_ref[...]


def _dec_ssd(projd, hist, conv_w, conv_b, dt_bias, a_log, d_skip, norm_w, state_view, layer, carried):
    small_spec = pl.BlockSpec((DEC_LEN, DEC_SEQS, 128), lambda h: (0, 0, COL_SMALL // 128))
    return _dec_call(_dec_ssd_kernel, "ssd_decode", COL_SSD,
                     (projd, projd, hist, conv_w, conv_b.reshape(1, 768), _lane_rep(dt_bias), _lane_rep(a_log),
                      _lane_rep(d_skip), norm_w.reshape(1, GROUP_WIDTH)),
                     [small_spec, _hist_spec(layer), _fixed1((CONV_WIDTH, 768)),
                      _fixed1((1, 768))] + [_fixed1((1, GROUP_WIDTH))] * 4, 5, state_view, layer, carried,
                     extra_scratch=[pltpu.VMEM((DEC_LEN, DEC_SEQS, GROUP_WIDTH), F32)])


def _dec_gdn_kernel(blk_ref, small_ref, hist_ref, cw_ref, alog_ref, dtb_ref, nw_ref, st_ref,
                    o_ref, so_ref, q_scr, k_scr, v_scr, a_scr, b_scr, o_scr):
    h = pl.program_id(0)

    @pl.when(h == 0)
    def _():
        ones_bd = _block_ones(GROUP_WIDTH, HEAD_DIM)
        qkv = _dec_conv_silu(hist_ref, [blk_ref[t][:, 0:768] for t in range(DEC_LEN)], cw_ref[...], None)
        for t in range(DEC_LEN):
            gq, gk, gv = qkv[t][:, 0:256], qkv[t][:, 256:512], qkv[t][:, 512:768]
            q_scr[t] = (gq * lax.rsqrt(_head_sumsq(gq, ones_bd) + EPS) * (HEAD_DIM ** -0.5)).T
            k_scr[t] = (gk * lax.rsqrt(_head_sumsq(gk, ones_bd) + EPS)).T
            v_scr[t] = gv.T
            small = small_ref[t]
            b_scr[t] = _sigmoid(_expand_small(small, SMALL_GB)).T
            la = -jnp.exp(alog_ref[...]) * _softplus(_expand_small(small, SMALL_GA) + dtb_ref[...])
            a_scr[t] = jnp.exp(la).T

    hr = _head_rows(h)
    one_row = pl.ds(h * HEAD_DIM, 1)
    zero = jnp.zeros((HEAD_DIM, DEC_SEQS), F32)
    for t in range(DEC_LEN):
        a = a_scr[t, one_row, :]
        cur = st_ref if t == 0 else so_ref

        def kts(kk, r):
            return r + k_scr[t, pl.ds(h * HEAD_DIM + kk, 1), :] * cur[0, 0, kk]

        r = lax.fori_loop(0, HEAD_DIM, kts, zero)
        u = b_scr[t, one_row, :] * (v_scr[t, hr, :] - a * r)

        def upd(kk, acc):
            s = a * cur[0, 0, kk] + k_scr[t, pl.ds(h * HEAD_DIM + kk, 1), :] * u
            so_ref[0, 0, kk] = s
            return acc + q_scr[t, pl.ds(h * HEAD_DIM + kk, 1), :] * s

        o_scr[t, hr, :] = lax.fori_loop(0, HEAD_DIM, upd, zero)

    @pl.when(h == N_HEADS - 1)
    def _():
        ones_bd = _block_ones(GROUP_WIDTH, HEAD_DIM)
        for t in range(DEC_LEN):
            o = o_scr[t].T
            ss = _head_sumsq(o, ones_bd)
            o_ref[t] = o * lax.rsqrt(ss * (1.0 / HEAD_DIM) + EPS) * nw_ref[...] * _silu(blk_ref[t][:, 768:1024])


def _dec_gdn(projd, hist, conv_w, a_log, dt_bias, norm_w, state_view, layer, carried):
    small_spec = pl.BlockSpec((DEC_LEN, DEC_SEQS, 128), lambda h: (0, 0, COL_SMALL // 128))
    return _dec_call(_dec_gdn_kernel, "gdn_decode", COL_GDN,
                     (projd, projd, hist, conv_w, _lane_rep(a_log), _lane_rep(dt_bias),
                      jnp.tile(norm_w.astype(F32), N_HEADS).reshape(1, GROUP_WIDTH)),
                     [small_spec, _hist_spec(layer), _fixed1((CONV_WIDTH, 768))]
                     + [_fixed1((1, GROUP_WIDTH))] * 3, 6, state_view, layer, carried)


W_PREP_ROWS = 128


def _w_in_t_prep_kernel(a_ref, b_ref, o_ref):
    j = pl.program_id(0)
    n_plain = COL_RET // W_PREP_ROWS
    n_main = COL_SMALL // W_PREP_ROWS
    row = _iota((W_PREP_ROWS, 1), 0)
    for l in range(DEPTH):
        a = a_ref[:, l, :]
        b = b_ref[:, l, :]
        shifted = jnp.concatenate([a[8:], b[:8]], axis=0)
        small = jnp.where(row < 8, a, jnp.where(row < 12, b, 0.0))
        out = jnp.where(j < n_plain, a, jnp.where(j < n_main, shifted, small))
        o_ref[l] = out.astype(BF16)


def _prep_w_in_t(w_in):
    wt = jnp.transpose(w_in, (2, 0, 1))
    n_plain = COL_RET // W_PREP_ROWS
    n_main = COL_SMALL // W_PREP_ROWS

    def a_idx(j):
        return (jnp.where(j < n_main, j, n_plain), 0, 0)

    def b_idx(j):
        return (jnp.where(j < n_plain, j, jnp.minimum(j + 1, n_main)), 0, 0)

    blk = (W_PREP_ROWS, DEPTH, D_MODEL)
    return pl.pallas_call(
        _w_in_t_prep_kernel,
        grid=(P_PAD // W_PREP_ROWS,),
        in_specs=[pl.BlockSpec(blk, a_idx), pl.BlockSpec(blk, b_idx)],
        out_specs=pl.BlockSpec((DEPTH, W_PREP_ROWS, D_MODEL), lambda j: (0, j, 0)),
        out_shape=jax.ShapeDtypeStruct((DEPTH, P_PAD, D_MODEL), BF16),
        compiler_params=pltpu.CompilerParams(dimension_semantics=("arbitrary",), vmem_limit_bytes=VMEM_LIMIT),
        name="w_in_prep",
    )(wt, wt)


def _rotary_tables(pos):
    half = HEAD_DIM // 2
    inv_freq = RET_THETA ** (-jnp.arange(half, dtype=F32) / half)
    ang = pos.astype(F32)[:, None] * inv_freq[None, :]
    cos, sin = jnp.cos(ang), jnp.sin(ang)
    cos_t = jnp.tile(cos, (1, 2 * N_HEADS))
    sin_t = jnp.tile(jnp.concatenate([-sin, sin], axis=1), (1, N_HEADS))
    return cos_t, sin_t


RET_CHUNK = 256
SSD_CHUNK = 256
GDN_CHUNK = 64
HGRN_ROWS = 128


def _forward(x_prompt, x_sample, states, p, past_len):
    st_hg, st_gd, st_gc, st_rt, st_sd, st_sc = states
    bp, lp, _ = x_prompt.shape
    nd, ld, _ = x_sample.shape
    xp = x_prompt.astype(F32).reshape(bp * lp, D_MODEL)
    xd = jnp.transpose(x_sample.astype(F32), (1, 0, 2)).reshape(ld * nd, D_MODEL)
    cos_p, sin_p = _rotary_tables(jnp.arange(lp))
    cos_d, sin_d = _rotary_tables(past_len + jnp.arange(ld))
    outs = {k: [] for k in ("hp", "gp", "gcp", "gcs", "rp", "sp", "scp", "scs")}
    w_in_all = _prep_w_in_t(p["w_in"].astype(F32))
    wo, wu, wd = (p[k].astype(BF16) for k in ("w_out", "w_up", "w_down"))
    norm_mix = p["norm_mix"].astype(F32).reshape(DEPTH, 1, D_MODEL)
    norm_ffn = p["norm_ffn"].astype(F32).reshape(DEPTH, 1, D_MODEL)
    sv_hg, sv_gd, sv_rt, sv_sd = (jnp.transpose(s.astype(F32), (0, 2, 3, 4, 1)) for s in (st_hg, st_gd, st_rt, st_sd))
    hist_g = jnp.transpose(st_gc.astype(F32), (0, 2, 1, 3))
    hist_s = jnp.transpose(st_sc.astype(F32), (0, 2, 1, 3))
    new_hg = new_gd = new_rt = new_sd = None
    for l in range(DEPTH):
        pp = _proj(xp, norm_mix, w_in_all, l).reshape(bp, lp, P_PAD)
        pd = _proj(xd, norm_mix, w_in_all, l).reshape(ld, nd, P_PAD)

        oa, sa = _hgrn_prompt(pp, p["hgrn_lb_logits"], p["hgrn_norm"][l], l, HGRN_ROWS)
        ob, sb = _gdn_prompt(pp, p["gdn_conv_w"][l], p["gdn_a_log"][l], p["gdn_dt_bias"][l], p["gdn_norm"][l],
                             GDN_CHUNK)
        oc, sc = _ret_prompt(pp, cos_p, sin_p, RET_CHUNK)
        od, sd = _ssd_prompt(pp, p["ssd_conv_w"][l], p["ssd_conv_b"][l], p["ssd_dt_bias"][l], p["ssd_a_log"][l],
                             p["ssd_d"][l], p["ssd_norm"][l], SSD_CHUNK)
        outs["hp"].append(sa.reshape(bp, N_HEADS, HEAD_DIM, HEAD_DIM))
        outs["gp"].append(sb.reshape(bp, N_HEADS, HEAD_DIM, HEAD_DIM))
        outs["rp"].append(sc.reshape(bp, N_HEADS, HEAD_DIM, HEAD_DIM))
        outs["sp"].append(sd.reshape(bp, N_HEADS, SSD_STATE, HEAD_DIM))
        outs["gcp"].append(pp[:, lp - 3:, COL_GDN:COL_GDN + 768])
        outs["scp"].append(pp[:, lp - 3:, COL_SSD + 256:COL_SSD + 1024])
        xp = _out_ffn(xp, [o.reshape(bp * lp, GROUP_WIDTH) for o in (oa, ob, oc, od)], wo, norm_ffn, wu, wd,
                      p["norm_final"], l)

        da, new_hg = _dec_hgrn(pd, p["hgrn_lb_logits"], p["hgrn_norm"][l], sv_hg, l, new_hg)
        db, new_gd = _dec_gdn(pd, hist_g, p["gdn_conv_w"][l], p["gdn_a_log"][l], p["gdn_dt_bias"][l],
                              p["gdn_norm"][l], sv_gd, l, new_gd)
        dc, new_rt = _dec_ret(pd, cos_d, sin_d, sv_rt, l, new_rt)
        dd, new_sd = _dec_ssd(pd, hist_s, p["ssd_conv_w"][l], p["ssd_conv_b"][l], p["ssd_dt_bias"][l],
                              p["ssd_a_log"][l], p["ssd_d"][l], p["ssd_norm"][l], sv_sd, l, new_sd)
        outs["gcs"].append(jnp.transpose(pd[ld - 3:, :, COL_GDN:COL_GDN + 768], (1, 0, 2)))
        outs["scs"].append(jnp.transpose(pd[ld - 3:, :, COL_SSD + 256:COL_SSD + 1024], (1, 0, 2)))
        xd = _out_ffn(xd, [o.reshape(ld * nd, GROUP_WIDTH) for o in (da, db, dc, dd)], wo, norm_ffn, wu, wd,
                      p["norm_final"], l)

    y_prompt = xp.reshape(bp, lp, D_MODEL)
    y_sample = jnp.transpose(xd.reshape(ld, nd, D_MODEL), (1, 0, 2))
    st = {k: jnp.stack(v) for k, v in outs.items()}
    hs, gs, rs, ss = (jnp.transpose(s, (0, 4, 1, 2, 3)) for s in (new_hg, new_gd, new_rt, new_sd))
    return (y_prompt, y_sample, st["hp"], hs, st["gp"], gs, st["gcp"], st["gcs"],
            st["rp"], rs, st["sp"], ss, st["scp"], st["scs"])


def kernel(x_prompt, x_sample, state_hgrn, state_gdn, state_gdn_conv, state_ret, state_ssd, state_ssd_conv,
           norm_mix, w_in, hgrn_lb_logits, hgrn_norm, gdn_conv_w, gdn_a_log, gdn_dt_bias, gdn_norm,
           ssd_conv_w, ssd_conv_b, ssd_dt_bias, ssd_a_log, ssd_d, ssd_norm,
           w_out, norm_ffn, w_up, w_down, norm_final):
    params = dict(norm_mix=norm_mix, w_in=w_in, hgrn_lb_logits=hgrn_lb_logits, hgrn_norm=hgrn_norm,
                  gdn_conv_w=gdn_conv_w, gdn_a_log=gdn_a_log, gdn_dt_bias=gdn_dt_bias, gdn_norm=gdn_norm,
                  ssd_conv_w=ssd_conv_w, ssd_conv_b=ssd_conv_b, ssd_dt_bias=ssd_dt_bias, ssd_a_log=ssd_a_log,
                  ssd_d=ssd_d, ssd_norm=ssd_norm, w_out=w_out, norm_ffn=norm_ffn, w_up=w_up,
                  w_down=w_down, norm_final=norm_final)
    states = (state_hgrn, state_gdn, state_gdn_conv, state_ret, state_ssd, state_ssd_conv)
    return _forward(x_prompt, x_sample, states, params, 16384)
```

```python
import functools
import math

import numpy as np
import jax
import jax.numpy as jnp
from jax import lax
from jax.experimental import pallas as pl
from jax.experimental.pallas import tpu as pltpu

F32 = jnp.float32
BF16 = jnp.bfloat16

D_MODEL = 1024
GROUP_WIDTH = 256
HEAD_DIM = 64
N_HEADS = 4
CONV_WIDTH = 4
SSD_STATE = 128
D_FF = 4096
RET_THETA = 10000.0
EPS = 1e-6
DEPTH = 2

COL_HGRN = 0
COL_GDN = 1024
COL_RET = 2048
COL_SSD = 3072
COL_SMALL = 4096
P_PAD = 4224
SMALL_GA, SMALL_GB, SMALL_SDT = 0, 4, 8

VMEM_LIMIT = 56 * 1024 * 1024
LOG_GAMMA = [math.log(1.0 - 2.0 ** (-5.0 - h)) for h in range(N_HEADS)]


def _dot(a, b):
    return jnp.dot(a, b, preferred_element_type=F32)


def _dot_nt(a, b):
    return lax.dot_general(a, b, (((1,), (1,)), ((), ())), preferred_element_type=F32)


def _dot_tn(a, b):
    return lax.dot_general(a, b, (((0,), (0,)), ((), ())), preferred_element_type=F32)


def _round_robin(gens):
    live = list(gens)
    while live:
        nxt = []
        for g in live:
            try:
                next(g)
                nxt.append(g)
            except StopIteration:
                pass
        live = nxt


def _split3(x):
    hi = x.astype(BF16)
    r1 = x - hi.astype(F32)
    mid = r1.astype(BF16)
    lo = (r1 - mid.astype(F32)).astype(BF16)
    return hi, mid, lo


def _exact_dot(x, sel):
    hi, mid, lo = _split3(x)
    return _dot(hi, sel) + _dot(mid, sel) + _dot(lo, sel)


def _exact_dot_left(sel, x):
    hi, mid, lo = _split3(x)
    return _dot(sel, hi) + _dot(sel, mid) + _dot(sel, lo)


def _iota(shape, dim):
    return lax.broadcasted_iota(jnp.int32, shape, dim)


def _head_of_lane(n_lanes, width=HEAD_DIM):
    return _iota((1, n_lanes), 1) // width


def _head_masks(n_lanes=GROUP_WIDTH, width=HEAD_DIM):
    hl = _head_of_lane(n_lanes, width)
    return [hl == h for h in range(n_lanes // width)]


def _stack_heads(x, masks):
    return jnp.concatenate([jnp.where(m, x, jnp.zeros_like(x)) for m in masks], axis=0)


def _unstack_heads(y, masks, c):
    out = jnp.where(masks[0], y[0:c], 0.0)
    for h in range(1, len(masks)):
        out = out + jnp.where(masks[h], y[h * c:(h + 1) * c], 0.0)
    return out


def _block_ones(n, width, dtype=BF16):
    r = _iota((n, n), 0) // width
    c = _iota((n, n), 1) // width
    return (r == c).astype(dtype)


def _block_mask(n, rwidth, cwidth):
    return (_iota((n, n), 0) // rwidth) == (_iota((n, n), 1) // cwidth)


def _lower_tri(c, dtype=BF16):
    return (_iota((c, c), 0) >= _iota((c, c), 1)).astype(dtype)


def _sigmoid(x):
    return 1.0 / (1.0 + jnp.exp(-x))


def _silu(x):
    return x * _sigmoid(x)


def _softplus(x):
    return jnp.maximum(x, 0.0) + jnp.log(1.0 + jnp.exp(-jnp.abs(x)))


def _rms_rows(x):
    return x * lax.rsqrt(jnp.mean(x * x, axis=-1, keepdims=True) + EPS)


def _head_sumsq(x, ones_bd):
    sq = x * x
    hi = sq.astype(BF16)
    lo = (sq - hi.astype(F32)).astype(BF16)
    return _dot(hi, ones_bd) + _dot(lo, ones_bd)


def _expand_small(small, first_lane):
    r = _iota((128, GROUP_WIDTH), 0)
    c = _iota((128, GROUP_WIDTH), 1) // HEAD_DIM
    sel = (r == c + first_lane).astype(BF16)
    return _exact_dot(small, sel)


def _decay_diff_operands(g):
    hi, mid, lo = (x.astype(F32) for x in _split3(g))
    pos = _iota(g.shape, 1) % HEAD_DIM
    a = jnp.where(pos == 0, hi, jnp.where(pos == 1, mid, jnp.where(pos == 2, lo,
                  jnp.where(pos < 6, 1.0, 0.0))))
    b = jnp.where(pos < 3, 1.0, jnp.where(pos == 3, -hi, jnp.where(pos == 4, -mid,
                  jnp.where(pos == 5, -lo, 0.0))))
    return a, b


def _extract_blocks(s_wide, rows, width):
    sel = ((_iota((GROUP_WIDTH, width), 0) % width) == _iota((GROUP_WIDTH, width), 1)).astype(BF16)
    return _exact_dot(s_wide, sel)


def _proj_kernel(x_ref, nw_ref, w_ref, o_ref):
    h = _rms_rows(x_ref[...]) * nw_ref[0]
    o_ref[...] = _dot_nt(h.astype(BF16), w_ref[0])


PROJ_ROWS = 512


def _proj(x2d, norm_w, w_bf16, layer):
    t = x2d.shape[0]
    tm = min(t, PROJ_ROWS)
    return pl.pallas_call(
        _proj_kernel,
        grid=(t // tm,),
        in_specs=[pl.BlockSpec((tm, D_MODEL), lambda i: (i, 0)),
                  pl.BlockSpec((1, 1, D_MODEL), lambda i: (layer, 0, 0)),
                  pl.BlockSpec((1, P_PAD, D_MODEL), lambda i: (layer, 0, 0))],
        out_specs=pl.BlockSpec((tm, P_PAD), lambda i: (i, 0)),
        out_shape=jax.ShapeDtypeStruct((t, P_PAD), F32),
        compiler_params=pltpu.CompilerParams(dimension_semantics=("arbitrary",),
                                             vmem_limit_bytes=VMEM_LIMIT),
        name="norm_in_proj",
    )(x2d, norm_w, w_bf16)


def _ffn_kernel(x_ref, oa_ref, ob_ref, oc_ref, od_ref, wo_ref, nf_ref, wu_ref, wd_ref, nfin_ref,
                o_ref, *, final):
    mix = jnp.concatenate([oa_ref[...], ob_ref[...], oc_ref[...], od_ref[...]], axis=1)
    x = x_ref[...] + _dot(mix, wo_ref[0])
    h = (_rms_rows(x) * nf_ref[0]).astype(BF16)
    acc = x
    ft = 1024
    for t in range(D_FF // ft):
        up = _dot(h, wu_ref[0, :, t * ft:(t + 1) * ft])
        up = jnp.square(jnp.maximum(up, 0.0)).astype(BF16)
        acc = acc + _dot(up, wd_ref[0, t * ft:(t + 1) * ft, :])
    if final:
        acc = _rms_rows(acc) * nfin_ref[...]
    o_ref[...] = acc


FFN_ROWS = 512


def _out_ffn(x2d, mixes, wo, nf, wu, wd, nfin, layer):
    t = x2d.shape[0]
    tm = min(t, FFN_ROWS)
    row = lambda i: (i, 0)
    lay = lambda i: (layer, 0, 0)
    return pl.pallas_call(
        functools.partial(_ffn_kernel, final=(layer == DEPTH - 1)),
        grid=(t // tm,),
        in_specs=[pl.BlockSpec((tm, D_MODEL), row)]
                 + [pl.BlockSpec((tm, GROUP_WIDTH), row)] * 4
                 + [pl.BlockSpec((1, D_MODEL, D_MODEL), lay),
                    pl.BlockSpec((1, 1, D_MODEL), lay),
                    pl.BlockSpec((1, D_MODEL, D_FF), lay),
                    pl.BlockSpec((1, D_FF, D_MODEL), lay),
                    pl.BlockSpec((1, D_MODEL), lambda i: (0, 0))],
        out_specs=pl.BlockSpec((tm, D_MODEL), row),
        out_shape=jax.ShapeDtypeStruct((t, D_MODEL), F32),
        compiler_params=pltpu.CompilerParams(dimension_semantics=("arbitrary",),
                                             vmem_limit_bytes=VMEM_LIMIT),
        name="out_proj_ffn",
    )(x2d, *mixes, wo, nf, wu, wd, nfin.reshape(1, D_MODEL))


def _swap_halves(x):
    first = (_iota((1, 128), 1) % HEAD_DIM) < (HEAD_DIM // 2)
    parts = []
    for p in range(GROUP_WIDTH // 128):
        xp = x[:, p * 128:(p + 1) * 128]
        parts.append(jnp.where(first, pltpu.roll(xp, 96, 1), pltpu.roll(xp, 32, 1)))
    return jnp.concatenate(parts, axis=1)


def _conv_silu(xe_ref, halo, x, w, bias, first_chunk, c):
    xe_ref[0:8, :] = jnp.where(first_chunk, jnp.zeros_like(halo), halo)
    xe_ref[8:, :] = x
    y = w[3:4, :] * x
    for j in range(CONV_WIDTH - 1):
        y = y + w[j:j + 1, :] * xe_ref[5 + j:5 + j + c, :]
    if bias is not None:
        y = y + bias
    return _silu(y)


def _ret_prompt_kernel(blk_ref, cos_ref, sin_ref, o_ref, st_ref, s_scr, *, c, n_chunks, nb):
    ci = pl.program_id(1)

    @pl.when(ci == 0)
    def _():
        s_scr[...] = jnp.zeros_like(s_scr)

    cosv, sinv = cos_ref[...], sin_ref[...]
    masks = _head_masks()
    hl = _head_of_lane(GROUP_WIDTH)
    lg = jnp.full((1, GROUP_WIDTH), LOG_GAMMA[0], F32)
    for h in range(1, N_HEADS):
        lg = jnp.where(hl == h, LOG_GAMMA[h], lg)
    ri = _iota((c, 1), 0).astype(F32)
    dij = (_iota((c, c), 0) - _iota((c, c), 1)).astype(F32)
    causal = dij >= 0.0
    decay = jnp.concatenate(
        [jnp.where(causal, jnp.exp(jnp.maximum(dij, 0.0) * LOG_GAMMA[h]), 0.0) for h in range(N_HEADS)],
        axis=0)
    q_scale = jnp.exp((ri + 1.0) * lg)
    k_scale = jnp.exp((float(c - 1) - ri) * lg) * (HEAD_DIM ** -0.5)
    s_scale = jnp.exp(float(c) * lg)
    bd_mask = _block_mask(GROUP_WIDTH, HEAD_DIM, HEAD_DIM)
    ones_bd = _block_ones(GROUP_WIDTH, HEAD_DIM)

    def one_sequence(sq):
        blk = blk_ref[sq]
        rq, rk, rv, rg = (blk[:, i * GROUP_WIDTH:(i + 1) * GROUP_WIDTH] for i in range(4))
        q = rq * cosv + _swap_halves(rq) * sinv
        k = rk * cosv + _swap_halves(rk) * sinv
        v = rv.astype(BF16)
        s = s_scr[sq]
        qk = _dot_nt(_stack_heads(q, masks).astype(BF16), k.astype(BF16))
        o_inter = _dot((q * q_scale).astype(BF16), s.astype(BF16))
        ds = _dot_tn((k * k_scale).astype(BF16), v)
        yield
        scores = qk * (decay * (HEAD_DIM ** -0.5))
        pv = _dot(scores.astype(BF16), v)
        s_scr[sq] = s_scale * s + jnp.where(bd_mask, ds, 0.0)
        yield
        o = _unstack_heads(pv, masks, c) + o_inter
        ss = _head_sumsq(o, ones_bd)
        yield
        o_ref[sq] = (o * lax.rsqrt(ss * (1.0 / HEAD_DIM) + EPS) * _silu(rg)).astype(BF16)

    _round_robin([one_sequence(sq) for sq in range(nb)])

    @pl.when(ci == n_chunks - 1)
    def _():
        for sq in range(nb):
            st_ref[sq] = _extract_blocks(s_scr[sq], GROUP_WIDTH, HEAD_DIM)


PROMPT_SEQS_PER_STEP = 8
GDN_SEQS_PER_STEP = 8
PROMPT_ROWS_PER_STEP = 2048


def _ret_prompt(proj3, cos_t, sin_t, c):
    b, l, _ = proj3.shape
    n = l // c
    nb = math.gcd(b, min(PROMPT_SEQS_PER_STEP, max(1, PROMPT_ROWS_PER_STEP // (l // n))))
    return pl.pallas_call(
        functools.partial(_ret_prompt_kernel, c=c, n_chunks=n, nb=nb),
        grid=(b // nb, n),
        in_specs=[pl.BlockSpec((nb, c, 1024), lambda bi, ci: (bi, ci, COL_RET // 1024)),
                  pl.BlockSpec((c, GROUP_WIDTH), lambda bi, ci: (ci, 0)),
                  pl.BlockSpec((c, GROUP_WIDTH), lambda bi, ci: (ci, 0))],
        out_specs=[pl.BlockSpec((nb, c, GROUP_WIDTH), lambda bi, ci: (bi, ci, 0)),
                   pl.BlockSpec((nb, GROUP_WIDTH, HEAD_DIM), lambda bi, ci: (bi, 0, 0))],
        out_shape=[jax.ShapeDtypeStruct((b, l, GROUP_WIDTH), BF16),
                   jax.ShapeDtypeStruct((b, GROUP_WIDTH, HEAD_DIM), F32)],
        scratch_shapes=[pltpu.VMEM((nb, GROUP_WIDTH, GROUP_WIDTH), F32)],
        compiler_params=pltpu.CompilerParams(dimension_semantics=("arbitrary", "arbitrary"),
                                             vmem_limit_bytes=VMEM_LIMIT),
        name="retention_prompt",
    )(proj3, cos_t, sin_t)


def _ssd_prompt_kernel(blk_ref, halo_ref, small_ref, cw_ref, cb_ref, dtb_ref, alog_ref, dskip_ref, nw_ref,
                       o_ref, st_ref, s_scr, xe_scr, *, c, n_chunks, nb):
    ci = pl.program_id(1)

    @pl.when(ci == 0)
    def _():
        s_scr[...] = jnp.zeros_like(s_scr)

    masks = _head_masks()
    causal = _iota((c, c), 0) >= _iota((c, c), 1)
    causal4 = jnp.concatenate([causal] * N_HEADS, axis=0)
    group_mask = _block_mask(GROUP_WIDTH, 128, 128)
    tri = _lower_tri(c)
    neg_a = -jnp.exp(alog_ref[...]) * LOG2E

    def one_sequence(sq):
        blk = blk_ref[sq]
        sz = blk[:, 0:GROUP_WIDTH]
        xbc = _conv_silu(xe_scr.at[sq], halo_ref[sq][:, GROUP_WIDTH:], blk[:, GROUP_WIDTH:], cw_ref[...],
                         cb_ref[...], ci == 0, c)
        xs = xbc[:, 0:256]
        bmat = xbc[:, 256:512].astype(BF16)
        cmat = xbc[:, 512:768].astype(BF16)
        s = s_scr[sq]
        cb = [_dot_nt(cmat[:, gi * 128:(gi + 1) * 128], bmat[:, gi * 128:(gi + 1) * 128]) for gi in range(2)]
        y_inter = _dot(cmat, s.astype(BF16))
        dt = _softplus(_expand_small(small_ref[sq], SMALL_SDT) + dtb_ref[...])
        yield
        g = _exact_dot_left(tri, neg_a * dt)
        yield
        g_last = g[c - 1:c, :]
        da, db = _decay_diff_operands(g)
        diff = _dot_nt(_stack_heads(da, masks).astype(BF16), db.astype(BF16))
        v = xs * dt
        vend = v * jnp.exp2(g_last - g)
        ds = _dot_tn(bmat, vend.astype(BF16))
        yield
        decay = jnp.where(causal4, jnp.exp2(diff), 0.0)
        scores = jnp.concatenate([cb[0], cb[0], cb[1], cb[1]], axis=0) * decay
        pv = _dot(scores.astype(BF16), v.astype(BF16))
        s_scr[sq] = jnp.exp2(g_last) * s + jnp.where(group_mask, ds, 0.0)
        yield
        y = _unstack_heads(pv, masks, c) + y_inter * jnp.exp2(g)
        y = (y + dskip_ref[...] * xs) * _silu(sz)
        halves = [_rms_rows(y[:, gi * 128:(gi + 1) * 128]) for gi in range(2)]
        o_ref[sq] = (jnp.concatenate(halves, axis=1) * nw_ref[...]).astype(BF16)

    _round_robin([one_sequence(sq) for sq in range(nb)])

    @pl.when(ci == n_chunks - 1)
    def _():
        for sq in range(nb):
            for h in range(N_HEADS):
                gi = h // 2
                rows = jnp.where(masks[h], s_scr[sq, gi * 128:(gi + 1) * 128, :], 0.0)
                st_ref[sq, h * 128:(h + 1) * 128, :] = _extract_blocks(rows, 128, HEAD_DIM)


def _lane_rep(p):
    return jnp.repeat(p.astype(F32), HEAD_DIM).reshape(1, GROUP_WIDTH)


def _ssd_prompt(proj3, conv_w, conv_b, dt_bias, a_log, d_skip, norm_w, c):
    b, l, _ = proj3.shape
    n = l // c
    fixed = lambda bi, ci: (0, 0)
    nb = math.gcd(b, min(PROMPT_SEQS_PER_STEP, max(1, PROMPT_ROWS_PER_STEP // (l // n))))
    return pl.pallas_call(
        functools.partial(_ssd_prompt_kernel, c=c, n_chunks=n, nb=nb),
        grid=(b // nb, n),
        in_specs=[pl.BlockSpec((nb, c, 1024), lambda bi, ci: (bi, ci, COL_SSD // 1024)),
                  pl.BlockSpec((nb, 8, 1024), lambda bi, ci: (bi, jnp.maximum(ci * (c // 8) - 1, 0), COL_SSD // 1024)),
                  pl.BlockSpec((nb, c, 128), lambda bi, ci: (bi, ci, COL_SMALL // 128)),
                  pl.BlockSpec((CONV_WIDTH, 768), fixed),
                  pl.BlockSpec((1, 768), fixed),
                  pl.BlockSpec((1, GROUP_WIDTH), fixed),
                  pl.BlockSpec((1, GROUP_WIDTH), fixed),
                  pl.BlockSpec((1, GROUP_WIDTH), fixed),
                  pl.BlockSpec((1, GROUP_WIDTH), fixed)],
        out_specs=[pl.BlockSpec((nb, c, GROUP_WIDTH), lambda bi, ci: (bi, ci, 0)),
                   pl.BlockSpec((nb, N_HEADS * SSD_STATE, HEAD_DIM), lambda bi, ci: (bi, 0, 0))],
        out_shape=[jax.ShapeDtypeStruct((b, l, GROUP_WIDTH), BF16),
                   jax.ShapeDtypeStruct((b, N_HEADS * SSD_STATE, HEAD_DIM), F32)],
        scratch_shapes=[pltpu.VMEM((nb, GROUP_WIDTH, GROUP_WIDTH), F32),
                        pltpu.VMEM((nb, c + 8, 768), F32)],
        compiler_params=pltpu.CompilerParams(dimension_semantics=("arbitrary", "arbitrary"),
                                             vmem_limit_bytes=VMEM_LIMIT),
        name="ssd_prompt",
    )(proj3, proj3, proj3, conv_w, conv_b.reshape(1, 768), _lane_rep(dt_bias), _lane_rep(a_log),
      _lane_rep(d_skip), norm_w.reshape(1, GROUP_WIDTH))


def _gdn_prompt_kernel(blk_ref, halo_ref, small_ref, cw_ref, alog_ref, dtb_ref, nw_ref,
                       o_ref, st_ref, s_scr, xe_scr, m_scr, *, c, n_chunks, nb):
    ci = pl.program_id(1)
    hc = 2 * c
    n_lvl = int(math.log2(c))

    @pl.when(ci == 0)
    def _():
        s_scr[...] = jnp.zeros_like(s_scr)

    @pl.when((pl.program_id(0) == 0) & (ci == 0))
    def _():
        rr = _iota((hc, hc), 0)
        cc = _iota((hc, hc), 1)
        same = (rr // c) == (cc // c)
        m_scr[0] = (same & (rr >= cc)).astype(F32)
        m_scr[1] = (same & (rr > cc)).astype(F32)
        for lv in range(n_lvl):
            sz = 1 << lv
            off = ((rr // (2 * sz)) == (cc // (2 * sz))) & (((rr // sz) % 2) == 1) & (((cc // sz) % 2) == 0)
            m_scr[2 + lv] = off.astype(F32)

    ones_bd = _block_ones(GROUP_WIDTH, HEAD_DIM)
    bd_mask = _block_mask(GROUP_WIDTH, HEAD_DIM, HEAD_DIM)
    masks = _head_masks()
    pair_masks = [masks[0:2], masks[2:4]]
    tri = _lower_tri(c)
    neg_a = -jnp.exp(alog_ref[...]) * LOG2E

    def one_sequence(sq):
        blk = blk_ref[sq]
        gz = blk[:, 768:1024]
        qkv = _conv_silu(xe_scr.at[sq], halo_ref[sq][:, 0:768], blk[:, 0:768], cw_ref[...], None, ci == 0, c)
        gq, gk, v = qkv[:, 0:256], qkv[:, 256:512], qkv[:, 512:768]
        q = gq * lax.rsqrt(_head_sumsq(gq, ones_bd) + EPS) * (HEAD_DIM ** -0.5)
        k = gk * lax.rsqrt(_head_sumsq(gk, ones_bd) + EPS)
        yield
        small = small_ref[sq]
        beta = _sigmoid(_expand_small(small, SMALL_GB))
        g = _exact_dot_left(tri, neg_a * _softplus(_expand_small(small, SMALL_GA) + dtb_ref[...]))
        g_last = g[c - 1:c, :]
        eg = jnp.exp2(g)
        yield
        da, db = _decay_diff_operands(g)
        bk = beta * k
        bkg = bk * eg
        bv = beta * v
        a_mat, p_mat, x = [], [], []
        for pm in pair_masks:
            diff = _dot_nt(_stack_heads(da, pm).astype(BF16), _stack_heads(db, pm).astype(BF16))
            k_st = _stack_heads(k, pm).astype(BF16)
            kk = _dot_nt(_stack_heads(bk, pm).astype(BF16), k_st)
            qk = _dot_nt(_stack_heads(q, pm).astype(BF16), k_st)
            decay = jnp.exp2(jnp.minimum(diff, 0.0))
            a_mat.append(kk * (decay * m_scr[1]))
            p_mat.append((qk * (decay * m_scr[0])).astype(BF16))
            x.append(jnp.concatenate([_stack_heads(bkg, pm), _stack_heads(bv, pm)], axis=1))
        yield

        n_mat = [-(a * m_scr[2]) for a in a_mat]
        for lv in range(1, n_lvl):
            a_off = [a * m_scr[2 + lv] for a in a_mat]
            m = [ao + _dot(ao.astype(BF16), n.astype(BF16)) for ao, n in zip(a_off, n_mat)]
            yield
            n_mat = [n - mm - _dot(n.astype(BF16), mm.astype(BF16)) for n, mm in zip(n_mat, m)]
            yield
        x = [xx + _dot(n.astype(BF16), xx.astype(BF16)) for xx, n in zip(x, n_mat)]
        yield
        w = x[0][0:c, 0:256] + x[0][c:2 * c, 0:256] + x[1][0:c, 0:256] + x[1][c:2 * c, 0:256]
        u0 = x[0][0:c, 256:512] + x[0][c:2 * c, 256:512] + x[1][0:c, 256:512] + x[1][c:2 * c, 256:512]

        s = s_scr[sq]
        s_bf = s.astype(BF16)
        u = u0 - _dot(w.astype(BF16), s_bf)
        o = _dot((q * eg).astype(BF16), s_bf)
        yield
        pu = [_dot(pmat, _stack_heads(u, pm).astype(BF16)) for pmat, pm in zip(p_mat, pair_masks)]
        kend = k * jnp.exp2(g_last - g)
        ds = _dot_tn(kend.astype(BF16), u.astype(BF16))
        yield
        for pu_p in pu:
            o = o + pu_p[0:c] + pu_p[c:2 * c]
        s_scr[sq] = jnp.exp2(g_last) * s + jnp.where(bd_mask, ds, 0.0)
        ss = _head_sumsq(o, ones_bd)
        o_ref[sq] = (o * lax.rsqrt(ss * (1.0 / HEAD_DIM) + EPS) * nw_ref[...] * _silu(gz)).astype(BF16)

    _round_robin([one_sequence(sq) for sq in range(nb)])

    @pl.when(ci == n_chunks - 1)
    def _():
        for sq in range(nb):
            st_ref[sq] = _extract_blocks(s_scr[sq], GROUP_WIDTH, HEAD_DIM)


def _gdn_prompt(proj3, conv_w, a_log, dt_bias, norm_w, c):
    b, l, _ = proj3.shape
    n = l // c
    fixed = lambda bi, ci: (0, 0)
    nb = math.gcd(b, GDN_SEQS_PER_STEP)
    hc = 2 * c
    return pl.pallas_call(
        functools.partial(_gdn_prompt_kernel, c=c, n_chunks=n, nb=nb),
        grid=(b // nb, n),
        in_specs=[pl.BlockSpec((nb, c, 1024), lambda bi, ci: (bi, ci, COL_GDN // 1024)),
                  pl.BlockSpec((nb, 8, 1024), lambda bi, ci: (bi, jnp.maximum(ci * (c // 8) - 1, 0), COL_GDN // 1024)),
                  pl.BlockSpec((nb, c, 128), lambda bi, ci: (bi, ci, COL_SMALL // 128)),
                  pl.BlockSpec((CONV_WIDTH, 768), fixed),
                  pl.BlockSpec((1, GROUP_WIDTH), fixed),
                  pl.BlockSpec((1, GROUP_WIDTH), fixed),
                  pl.BlockSpec((1, GROUP_WIDTH), fixed)],
        out_specs=[pl.BlockSpec((nb, c, GROUP_WIDTH), lambda bi, ci: (bi, ci, 0)),
                   pl.BlockSpec((nb, GROUP_WIDTH, HEAD_DIM), lambda bi, ci: (bi, 0, 0))],
        out_shape=[jax.ShapeDtypeStruct((b, l, GROUP_WIDTH), BF16),
                   jax.ShapeDtypeStruct((b, GROUP_WIDTH, HEAD_DIM), F32)],
        scratch_shapes=[pltpu.VMEM((nb, GROUP_WIDTH, GROUP_WIDTH), F32),
                        pltpu.VMEM((nb, c + 8, 768), F32),
                        pltpu.VMEM((2 + int(math.log2(c)), hc, hc), F32)],
        compiler_params=pltpu.CompilerParams(dimension_semantics=("arbitrary", "arbitrary"),
                                             vmem_limit_bytes=VMEM_LIMIT),
        name="gdn_prompt",
    )(proj3, proj3, proj3, conv_w, _lane_rep(a_log), _lane_rep(dt_bias),
      jnp.tile(norm_w.astype(F32), N_HEADS).reshape(1, GROUP_WIDTH))


HGRN_SUB = 16
LOG2E = 1.4426950408889634


def _hgrn_lower_bound(logits, layer):
    rows = [logits[d:d + 1, :] for d in range(DEPTH)]
    mx = functools.reduce(jnp.maximum, rows)
    es = [jnp.exp(x - mx) for x in rows]
    tot = functools.reduce(lambda a, b: a + b, es)
    sm = [e / tot for e in es]
    acc = sm[0]
    for d in range(1, layer + 1):
        acc = acc + sm[d]
    return acc - sm[0]


def _hgrn_prompt_kernel(blk_ref, lb_ref, nw_ref, o_ref, st_ref, s_scr, *, r, n_chunks, layer, nb):
    ci = pl.program_id(1)
    sub = HGRN_SUB
    n_sub = r // sub

    @pl.when(ci == 0)
    def _():
        s_scr[...] = jnp.zeros_like(s_scr)

    lb = _hgrn_lower_bound(lb_ref[...], layer)
    rr = _iota((r, r), 0)
    cc = _iota((r, r), 1)
    same_sub = (rr // sub) == (cc // sub)
    cum_sel = (same_sub & (rr >= cc)).astype(BF16)
    tot_sel = same_sub.astype(BF16)
    ones_bd = _block_ones(GROUP_WIDTH, HEAD_DIM)
    masks = _head_masks()
    half = sub // 2
    i8 = _iota((half, 1), 0)

    def one_sequence(sq):
        blk = blk_ref[sq]
        hq, hf, hi, hg = (blk[:, i * GROUP_WIDTH:(i + 1) * GROUP_WIDTH] for i in range(4))
        f = lb + (1.0 - lb) * _sigmoid(hf)
        q = _sigmoid(hq)
        k = 1.0 - f
        v = hi
        logf = jnp.log(f)
        g = _exact_dot_left(cum_sel, logf)
        g_tot = _exact_dot_left(tot_sel, logf)
        yield
        a2 = (g + jnp.log(q)) * LOG2E
        h2 = (g - jnp.log(k)) * LOG2E
        gt2 = g_tot * LOG2E
        qt = _stack_heads(jnp.exp2(a2), masks).astype(BF16)
        kh = _stack_heads(jnp.exp2(gt2 - h2), masks).astype(BF16)
        v_heads = [v[:, h * HEAD_DIM:(h + 1) * HEAD_DIM].astype(BF16) for h in range(N_HEADS)]

        def sub_rows(x_st, lo):
            return jnp.concatenate([x_st[h * r + lo:h * r + lo + sub] for h in range(N_HEADS)], axis=0)

        s = s_scr[sq]
        outs = []
        for j in range(n_sub):
            lo = j * sub
            v_j = v[lo:lo + sub]
            a_lo, a_hi, h_j = a2[lo:lo + half], a2[lo + half:lo + sub], h2[lo:lo + sub]
            lo_blocks, hi_blocks = [], []
            for jj in range(sub):
                h_row = h_j[jj:jj + 1, :]
                if jj < half:
                    e_lo = jnp.exp2(a_lo - h_row)
                    lo_blocks.append(e_lo if jj == 0 else jnp.where(i8 >= jj, e_lo, 0.0))
                    hi_blocks.append(jnp.exp2(a_hi - h_row))
                else:
                    e_hi = jnp.exp2(a_hi - h_row)
                    hi_blocks.append(e_hi if jj == half else jnp.where(i8 >= jj - half, e_hi, 0.0))
            sc = _dot(jnp.concatenate(lo_blocks + hi_blocks, axis=0).astype(BF16), ones_bd)
            oi = _dot_nt(sub_rows(qt, lo), s.astype(BF16))
            v_rows = jnp.concatenate([vh[lo:lo + sub] for vh in v_heads], axis=0)
            ds = _dot_tn(v_rows, sub_rows(kh, lo))
            yield
            o_inter = jnp.concatenate([oi[h * sub:(h + 1) * sub] for h in range(N_HEADS)], axis=1)
            n_lo = half * half
            o_lo = sc[0:half] * v_j[0:1, :]
            o_hi = sc[n_lo:n_lo + half] * v_j[0:1, :]
            for jj in range(1, sub):
                if jj < half:
                    o_lo = o_lo + sc[jj * half:(jj + 1) * half] * v_j[jj:jj + 1, :]
                o_hi = o_hi + sc[n_lo + jj * half:n_lo + (jj + 1) * half] * v_j[jj:jj + 1, :]
            outs.append(jnp.concatenate([o_lo, o_hi], axis=0) + o_inter)
            s = jnp.exp2(gt2[lo:lo + 1, :]) * s + ds
        s_scr[sq] = s

        o = jnp.concatenate(outs, axis=0)
        ss = _head_sumsq(o, ones_bd)
        yield
        o_ref[sq] = (o * lax.rsqrt(ss * (1.0 / HEAD_DIM) + EPS) * nw_ref[...] * _silu(hg)).astype(BF16)

    _round_robin([one_sequence(sq) for sq in range(nb)])

    @pl.when(ci == n_chunks - 1)
    def _():
        for sq in range(nb):
            st_ref[sq] = s_scr[sq].T


def _hgrn_prompt(proj3, lb_logits, norm_w, layer, r):
    b, l, _ = proj3.shape
    n = l // r
    fixed = lambda bi, ci: (0, 0)
    nb = math.gcd(b, min(PROMPT_SEQS_PER_STEP, max(1, PROMPT_ROWS_PER_STEP // (l // n))))
    return pl.pallas_call(
        functools.partial(_hgrn_prompt_kernel, r=r, n_chunks=n, layer=layer, nb=nb),
        grid=(b // nb, n),
        in_specs=[pl.BlockSpec((nb, r, 1024), lambda bi, ci: (bi, ci, COL_HGRN // 1024)),
                  pl.BlockSpec((DEPTH, GROUP_WIDTH), fixed),
                  pl.BlockSpec((1, GROUP_WIDTH), fixed)],
        out_specs=[pl.BlockSpec((nb, r, GROUP_WIDTH), lambda bi, ci: (bi, ci, 0)),
                   pl.BlockSpec((nb, GROUP_WIDTH, HEAD_DIM), lambda bi, ci: (bi, 0, 0))],
        out_shape=[jax.ShapeDtypeStruct((b, l, GROUP_WIDTH), BF16),
                   jax.ShapeDtypeStruct((b, GROUP_WIDTH, HEAD_DIM), F32)],
        scratch_shapes=[pltpu.VMEM((nb, HEAD_DIM, GROUP_WIDTH), F32)],
        compiler_params=pltpu.CompilerParams(dimension_semantics=("arbitrary", "arbitrary"),
                                             vmem_limit_bytes=VMEM_LIMIT),
        name="hgrn_prompt",
    )(proj3, lb_logits.astype(F32), jnp.tile(norm_w.astype(F32), N_HEADS).reshape(1, GROUP_WIDTH))


DEC_SEQS = 128
DEC_LEN = 4


def _head_rows(h):
    return pl.ds(pl.multiple_of(h * HEAD_DIM, HEAD_DIM), HEAD_DIM)


def _recur_head(load_s, store_s, n_keys, decay_fn, k_fn, q_fn, v_blocks):
    def body(kk, accs):
        s = load_s(kk)
        accs = list(accs)
        for t in range(DEC_LEN):
            s = decay_fn(t, kk) * s + k_fn(t, kk) * v_blocks[t]
            accs[t] = accs[t] + q_fn(t, kk) * s
        store_s(kk, s)
        return tuple(accs)

    zero = jnp.zeros((HEAD_DIM, DEC_SEQS), F32)
    return lax.fori_loop(0, n_keys, body, (zero,) * DEC_LEN)


def _dec_ret_kernel(blk_ref, cos_ref, sin_ref, st_ref, o_ref, so_ref, q_scr, k_scr, v_scr, o_scr):
    h = pl.program_id(0)

    @pl.when(h == 0)
    def _():
        for t in range(DEC_LEN):
            blk = blk_ref[t]
            rq, rk, rv = blk[:, 0:256], blk[:, 256:512], blk[:, 512:768]
            cosv, sinv = cos_ref[t:t + 1, :], sin_ref[t:t + 1, :]
            q_scr[t] = (rq * cosv + _swap_halves(rq) * sinv).T
            k_scr[t] = ((rk * cosv + _swap_halves(rk) * sinv) * (HEAD_DIM ** -0.5)).T
            v_scr[t] = rv.T

    lg = jnp.where(h == 0, LOG_GAMMA[0], jnp.where(h == 1, LOG_GAMMA[1], jnp.where(h == 2, LOG_GAMMA[2], LOG_GAMMA[3])))
    gamma = jnp.exp(jnp.full((1, DEC_SEQS), lg, F32))
    hr = _head_rows(h)
    v_blocks = [v_scr[t, hr, :] for t in range(DEC_LEN)]
    accs = _recur_head(
        lambda kk: st_ref[0, 0, kk], functools.partial(_store_state, so_ref), HEAD_DIM,
        lambda t, kk: gamma,
        lambda t, kk: k_scr[t, pl.ds(h * HEAD_DIM + kk, 1), :],
        lambda t, kk: q_scr[t, pl.ds(h * HEAD_DIM + kk, 1), :],
        v_blocks)
    for t in range(DEC_LEN):
        o_scr[t, hr, :] = accs[t]

    @pl.when(h == N_HEADS - 1)
    def _():
        ones_bd = _block_ones(GROUP_WIDTH, HEAD_DIM)
        for t in range(DEC_LEN):
            o = o_scr[t].T
            ss = _head_sumsq(o, ones_bd)
            o_ref[t] = (o * lax.rsqrt(ss * (1.0 / HEAD_DIM) + EPS) * _silu(blk_ref[t][:, 768:1024])).astype(BF16)


def _store_state(so_ref, kk, s):
    so_ref[0, 0, kk] = s


def _without_ref(kernel_fn, idx):
    def wrapped(*refs):
        return kernel_fn(*refs[:idx], *refs[idx + 1:])
    return wrapped


def _zero_later_layers(kernel_fn, so_index):
    def wrapped(*refs):
        kernel_fn(*refs)
        so_ref = refs[so_index]
        so_ref[1:] = jnp.zeros((so_ref.shape[0] - 1,) + so_ref.shape[1:], F32)
    return wrapped


def _dec_call(kernel_fn, name, col, ins, in_specs, n_tok_scr, state_view, layer, carried, extra_scratch=()):
    blk_spec = pl.BlockSpec((DEC_LEN, DEC_SEQS, 1024), lambda h: (0, 0, col // 1024))
    st_spec = pl.BlockSpec((1, 1) + state_view.shape[2:], lambda h: (layer, h, 0, 0, 0))
    tok_scr = pltpu.VMEM((DEC_LEN, GROUP_WIDTH, DEC_SEQS), F32)
    ins = tuple(ins) + (state_view,)
    specs = [blk_spec] + in_specs + [st_spec]
    if carried is None:
        assert layer == 0
        so_spec = pl.BlockSpec((DEPTH, 1) + state_view.shape[2:], lambda h: (0, h, 0, 0, 0))
        kernel_fn = _zero_later_layers(kernel_fn, len(ins) + 1)
        aliases = {}
    else:
        so_spec = st_spec
        kernel_fn = _without_ref(kernel_fn, len(ins))
        aliases = {len(ins): 1}
        ins = ins + (carried,)
        specs = specs + [pl.BlockSpec(memory_space=pl.ANY)]
    return pl.pallas_call(
        kernel_fn,
        grid=(N_HEADS,),
        in_specs=specs,
        out_specs=[pl.BlockSpec((DEC_LEN, DEC_SEQS, GROUP_WIDTH), lambda h: (0, 0, 0)), so_spec],
        out_shape=[jax.ShapeDtypeStruct((DEC_LEN, DEC_SEQS, GROUP_WIDTH), BF16),
                   jax.ShapeDtypeStruct(state_view.shape, F32)],
        scratch_shapes=[tok_scr] * n_tok_scr + list(extra_scratch),
        input_output_aliases=aliases,
        compiler_params=pltpu.CompilerParams(dimension_semantics=("arbitrary",), vmem_limit_bytes=VMEM_LIMIT),
        name=name,
    )(*ins)


def _fixed1(shape):
    return pl.BlockSpec(shape, lambda h: (0,) * len(shape))


def _dec_ret(projd, cos_t, sin_t, state_view, layer, carried):
    return _dec_call(_dec_ret_kernel, "retention_decode", COL_RET, (projd, cos_t, sin_t),
                     [_fixed1((DEC_LEN, GROUP_WIDTH)), _fixed1((DEC_LEN, GROUP_WIDTH))], 4,
                     state_view, layer, carried)


def _dec_hgrn_kernel(blk_ref, lb_ref, nw_ref, st_ref, o_ref, so_ref, q_scr, k_scr, v_scr, f_scr, o_scr, *, layer):
    h = pl.program_id(0)

    @pl.when(h == 0)
    def _():
        lb = _hgrn_lower_bound(lb_ref[...], layer)
        for t in range(DEC_LEN):
            blk = blk_ref[t]
            f = lb + (1.0 - lb) * _sigmoid(blk[:, 256:512])
            q_scr[t] = _sigmoid(blk[:, 0:256]).T
            k_scr[t] = (1.0 - f).T
            v_scr[t] = blk[:, 512:768].T
            f_scr[t] = f.T

    hr = _head_rows(h)
    v_blocks = [v_scr[t, hr, :] for t in range(DEC_LEN)]
    row = lambda scr: (lambda t, kk: scr[t, pl.ds(h * HEAD_DIM + kk, 1), :])
    accs = _recur_head(lambda kk: st_ref[0, 0, kk], functools.partial(_store_state, so_ref), HEAD_DIM,
                       row(f_scr), row(k_scr), row(q_scr), v_blocks)
    for t in range(DEC_LEN):
        o_scr[t, hr, :] = accs[t]

    @pl.when(h == N_HEADS - 1)
    def _():
        ones_bd = _block_ones(GROUP_WIDTH, HEAD_DIM)
        for t in range(DEC_LEN):
            o = o_scr[t].T
            ss = _head_sumsq(o, ones_bd)
            o_ref[t] = (o * lax.rsqrt(ss * (1.0 / HEAD_DIM) + EPS) * nw_ref[...]
                        * _silu(blk_ref[t][:, 768:1024])).astype(BF16)


def _dec_hgrn(projd, lb_logits, norm_w, state_view, layer, carried):
    return _dec_call(functools.partial(_dec_hgrn_kernel, layer=layer), "hgrn_decode", COL_HGRN,
                     (projd, lb_logits.astype(F32), jnp.tile(norm_w.astype(F32), N_HEADS).reshape(1, GROUP_WIDTH)),
                     [_fixed1((DEPTH, GROUP_WIDTH)), _fixed1((1, GROUP_WIDTH))], 5, state_view, layer, carried)


def _hist_spec(layer):
    return pl.BlockSpec((1, CONV_WIDTH - 1, DEC_SEQS, 768), lambda h: (layer, 0, 0, 0))


def _dec_conv_silu(hist_ref, xs, w, bias):
    xe = [hist_ref[0, j] for j in range(CONV_WIDTH - 1)] + xs
    out = []
    for t in range(DEC_LEN):
        y = xe[t] * w[0:1, :]
        for j in range(1, CONV_WIDTH):
            y = y + xe[t + j] * w[j:j + 1, :]
        if bias is not None:
            y = y + bias
        out.append(_silu(y))
    return out


def _dec_ssd_kernel(blk_ref, small_ref, hist_ref, cw_ref, cb_ref, dtb_ref, alog_ref, dskip_ref, nw_ref, st_ref,
                    o_ref, so_ref, c_scr, b_scr, v_scr, a_scr, o_scr, x_scr):
    h = pl.program_id(0)

    @pl.when(h == 0)
    def _():
        xbc = _dec_conv_silu(hist_ref, [blk_ref[t][:, 256:1024] for t in range(DEC_LEN)], cw_ref[...], cb_ref[...])
        for t in range(DEC_LEN):
            xs = xbc[t][:, 0:256]
            dt = _softplus(_expand_small(small_ref[t], SMALL_SDT) + dtb_ref[...])
            x_scr[t] = xs
            v_scr[t] = (xs * dt).T
            b_scr[t] = xbc[t][:, 256:512].T
            c_scr[t] = xbc[t][:, 512:768].T
            a_scr[t] = jnp.exp(-jnp.exp(alog_ref[...]) * dt).T

    hr = _head_rows(h)
    g0 = (h // 2) * SSD_STATE
    v_blocks = [v_scr[t, hr, :] for t in range(DEC_LEN)]
    accs = _recur_head(
        lambda kk: st_ref[0, 0, kk], functools.partial(_store_state, so_ref), SSD_STATE,
        lambda t, kk: a_scr[t, pl.ds(h * HEAD_DIM, 1), :],
        lambda t, kk: b_scr[t, pl.ds(g0 + kk, 1), :],
        lambda t, kk: c_scr[t, pl.ds(g0 + kk, 1), :],
        v_blocks)
    for t in range(DEC_LEN):
        o_scr[t, hr, :] = accs[t]

    @pl.when(h == N_HEADS - 1)
    def _():
        for t in range(DEC_LEN):
            y = (o_scr[t].T + dskip_ref[...] * x_scr[t]) * _silu(blk_ref[t][:, 0:256])
            halves = [_rms_rows(y[:, gi * 128:(gi + 1) * 128]) for gi in range(2)]
            o_ref[t] = (jnp.concatenate(halves, axis=1) * nw_ref[...]).astype(BF16)


def _dec_ssd(projd, hist, conv_w, conv_b, dt_bias, a_log, d_skip, norm_w, state_view, layer, carried):
    small_spec = pl.BlockSpec((DEC_LEN, DEC_SEQS, 128), lambda h: (0, 0, COL_SMALL // 128))
    return _dec_call(_dec_ssd_kernel, "ssd_decode", COL_SSD,
                     (projd, projd, hist, conv_w, conv_b.reshape(1, 768), _lane_rep(dt_bias), _lane_rep(a_log),
                      _lane_rep(d_skip), norm_w.reshape(1, GROUP_WIDTH)),
                     [small_spec, _hist_spec(layer), _fixed1((CONV_WIDTH, 768)),
                      _fixed1((1, 768))] + [_fixed1((1, GROUP_WIDTH))] * 4, 5, state_view, layer, carried,
                     extra_scratch=[pltpu.VMEM((DEC_LEN, DEC_SEQS, GROUP_WIDTH), F32)])


def _dec_gdn_kernel(blk_ref, small_ref, hist_ref, cw_ref, alog_ref, dtb_ref, nw_ref, st_ref,
                    o_ref, so_ref, q_scr, k_scr, v_scr, a_scr, b_scr, o_scr):
    h = pl.program_id(0)

    @pl.when(h == 0)
    def _():
        ones_bd = _block_ones(GROUP_WIDTH, HEAD_DIM)
        qkv = _dec_conv_silu(hist_ref, [blk_ref[t][:, 0:768] for t in range(DEC_LEN)], cw_ref[...], None)
        for t in range(DEC_LEN):
            gq, gk, gv = qkv[t][:, 0:256], qkv[t][:, 256:512], qkv[t][:, 512:768]
            q_scr[t] = (gq * lax.rsqrt(_head_sumsq(gq, ones_bd) + EPS) * (HEAD_DIM ** -0.5)).T
            k_scr[t] = (gk * lax.rsqrt(_head_sumsq(gk, ones_bd) + EPS)).T
            v_scr[t] = gv.T
            small = small_ref[t]
            b_scr[t] = _sigmoid(_expand_small(small, SMALL_GB)).T
            la = -jnp.exp(alog_ref[...]) * _softplus(_expand_small(small, SMALL_GA) + dtb_ref[...])
            a_scr[t] = jnp.exp(la).T

    hr = _head_rows(h)
    one_row = pl.ds(h * HEAD_DIM, 1)
    zero = jnp.zeros((HEAD_DIM, DEC_SEQS), F32)
    for t in range(DEC_LEN):
        a = a_scr[t, one_row, :]
        cur = st_ref if t == 0 else so_ref

        def kts(kk, r):
            return r + k_scr[t, pl.ds(h * HEAD_DIM + kk, 1), :] * cur[0, 0, kk]

        r = lax.fori_loop(0, HEAD_DIM, kts, zero)
        u = b_scr[t, one_row, :] * (v_scr[t, hr, :] - a * r)

        def upd(kk, acc):
            s = a * cur[0, 0, kk] + k_scr[t, pl.ds(h * HEAD_DIM + kk, 1), :] * u
            so_ref[0, 0, kk] = s
            return acc + q_scr[t, pl.ds(h * HEAD_DIM + kk, 1), :] * s

        o_scr[t, hr, :] = lax.fori_loop(0, HEAD_DIM, upd, zero)

    @pl.when(h == N_HEADS - 1)
    def _():
        ones_bd = _block_ones(GROUP_WIDTH, HEAD_DIM)
        for t in range(DEC_LEN):
            o = o_scr[t].T
            ss = _head_sumsq(o, ones_bd)
            o_ref[t] = (o * lax.rsqrt(ss * (1.0 / HEAD_DIM) + EPS) * nw_ref[...]
                        * _silu(blk_ref[t][:, 768:1024])).astype(BF16)


def _dec_gdn(projd, hist, conv_w, a_log, dt_bias, norm_w, state_view, layer, carried):
    small_spec = pl.BlockSpec((DEC_LEN, DEC_SEQS, 128), lambda h: (0, 0, COL_SMALL // 128))
    return _dec_call(_dec_gdn_kernel, "gdn_decode", COL_GDN,
                     (projd, projd, hist, conv_w, _lane_rep(a_log), _lane_rep(dt_bias),
                      jnp.tile(norm_w.astype(F32), N_HEADS).reshape(1, GROUP_WIDTH)),
                     [small_spec, _hist_spec(layer), _fixed1((CONV_WIDTH, 768))]
                     + [_fixed1((1, GROUP_WIDTH))] * 3, 6, state_view, layer, carried)


W_PREP_ROWS = 128


def _w_in_t_prep_kernel(a_ref, b_ref, o_ref):
    j = pl.program_id(0)
    n_plain = COL_RET // W_PREP_ROWS
    n_main = COL_SMALL // W_PREP_ROWS
    row = _iota((W_PREP_ROWS, 1), 0)
    for l in range(DEPTH):
        a = a_ref[:, l, :]
        b = b_ref[:, l, :]
        shifted = jnp.concatenate([a[8:], b[:8]], axis=0)
        small = jnp.where(row < 8, a, jnp.where(row < 12, b, 0.0))
        out = jnp.where(j < n_plain, a, jnp.where(j < n_main, shifted, small))
        o_ref[l] = out.astype(BF16)


def _prep_w_in_t(w_in):
    wt = jnp.transpose(w_in, (2, 0, 1))
    n_plain = COL_RET // W_PREP_ROWS
    n_main = COL_SMALL // W_PREP_ROWS

    def a_idx(j):
        return (jnp.where(j < n_main, j, n_plain), 0, 0)

    def b_idx(j):
        return (jnp.where(j < n_plain, j, jnp.minimum(j + 1, n_main)), 0, 0)

    blk = (W_PREP_ROWS, DEPTH, D_MODEL)
    return pl.pallas_call(
        _w_in_t_prep_kernel,
        grid=(P_PAD // W_PREP_ROWS,),
        in_specs=[pl.BlockSpec(blk, a_idx), pl.BlockSpec(blk, b_idx)],
        out_specs=pl.BlockSpec((DEPTH, W_PREP_ROWS, D_MODEL), lambda j: (0, j, 0)),
        out_shape=jax.ShapeDtypeStruct((DEPTH, P_PAD, D_MODEL), BF16),
        compiler_params=pltpu.CompilerParams(dimension_semantics=("arbitrary",), vmem_limit_bytes=VMEM_LIMIT),
        name="w_in_prep",
    )(wt, wt)


def _rotary_tables(pos):
    half = HEAD_DIM // 2
    inv_freq = RET_THETA ** (-jnp.arange(half, dtype=F32) / half)
    ang = pos.astype(F32)[:, None] * inv_freq[None, :]
    cos, sin = jnp.cos(ang), jnp.sin(ang)
    cos_t = jnp.tile(cos, (1, 2 * N_HEADS))
    sin_t = jnp.tile(jnp.concatenate([-sin, sin], axis=1), (1, N_HEADS))
    return cos_t, sin_t


RET_CHUNK = 256
SSD_CHUNK = 256
GDN_CHUNK = 64
HGRN_ROWS = 128


def _forward(x_prompt, x_sample, states, p, past_len):
    st_hg, st_gd, st_gc, st_rt, st_sd, st_sc = states
    bp, lp, _ = x_prompt.shape
    nd, ld, _ = x_sample.shape
    xp = x_prompt.astype(F32).reshape(bp * lp, D_MODEL)
    xd = jnp.transpose(x_sample.astype(F32), (1, 0, 2)).reshape(ld * nd, D_MODEL)
    cos_p, sin_p = _rotary_tables(jnp.arange(lp))
    cos_d, sin_d = _rotary_tables(past_len + jnp.arange(ld))
    outs = {k: [] for k in ("hp", "gp", "gcp", "gcs", "rp", "sp", "scp", "scs")}
    w_in_all = _prep_w_in_t(p["w_in"].astype(F32))
    wo, wu, wd = (p[k].astype(BF16) for k in ("w_out", "w_up", "w_down"))
    norm_mix = p["norm_mix"].astype(F32).reshape(DEPTH, 1, D_MODEL)
    norm_ffn = p["norm_ffn"].astype(F32).reshape(DEPTH, 1, D_MODEL)
    sv_hg, sv_gd, sv_rt, sv_sd = (jnp.transpose(s.astype(F32), (0, 2, 3, 4, 1)) for s in (st_hg, st_gd, st_rt, st_sd))
    hist_g = jnp.transpose(st_gc.astype(F32), (0, 2, 1, 3))
    hist_s = jnp.transpose(st_sc.astype(F32), (0, 2, 1, 3))
    new_hg = new_gd = new_rt = new_sd = None
    for l in range(DEPTH):
        pp = _proj(xp, norm_mix, w_in_all, l).reshape(bp, lp, P_PAD)
        pd = _proj(xd, norm_mix, w_in_all, l).reshape(ld, nd, P_PAD)

        oa, sa = _hgrn_prompt(pp, p["hgrn_lb_logits"], p["hgrn_norm"][l], l, HGRN_ROWS)
        ob, sb = _gdn_prompt(pp, p["gdn_conv_w"][l], p["gdn_a_log"][l], p["gdn_dt_bias"][l], p["gdn_norm"][l],
                             GDN_CHUNK)
        oc, sc = _ret_prompt(pp, cos_p, sin_p, RET_CHUNK)
        od, sd = _ssd_prompt(pp, p["ssd_conv_w"][l], p["ssd_conv_b"][l], p["ssd_dt_bias"][l], p["ssd_a_log"][l],
                             p["ssd_d"][l], p["ssd_norm"][l], SSD_CHUNK)
        outs["hp"].append(sa.reshape(bp, N_HEADS, HEAD_DIM, HEAD_DIM))
        outs["gp"].append(sb.reshape(bp, N_HEADS, HEAD_DIM, HEAD_DIM))
        outs["rp"].append(sc.reshape(bp, N_HEADS, HEAD_DIM, HEAD_DIM))
        outs["sp"].append(sd.reshape(bp, N_HEADS, SSD_STATE, HEAD_DIM))
        outs["gcp"].append(pp[:, lp - 3:, COL_GDN:COL_GDN + 768])
        outs["scp"].append(pp[:, lp - 3:, COL_SSD + 256:COL_SSD + 1024])
        xp = _out_ffn(xp, [o.reshape(bp * lp, GROUP_WIDTH) for o in (oa, ob, oc, od)], wo, norm_ffn, wu, wd,
                      p["norm_final"], l)

        da, new_hg = _dec_hgrn(pd, p["hgrn_lb_logits"], p["hgrn_norm"][l], sv_hg, l, new_hg)
        db, new_gd = _dec_gdn(pd, hist_g, p["gdn_conv_w"][l], p["gdn_a_log"][l], p["gdn_dt_bias"][l],
                              p["gdn_norm"][l], sv_gd, l, new_gd)
        dc, new_rt = _dec_ret(pd, cos_d, sin_d, sv_rt, l, new_rt)
        dd, new_sd = _dec_ssd(pd, hist_s, p["ssd_conv_w"][l], p["ssd_conv_b"][l], p["ssd_dt_bias"][l],
                              p["ssd_a_log"][l], p["ssd_d"][l], p["ssd_norm"][l], sv_sd, l, new_sd)
        outs["gcs"].append(jnp.transpose(pd[ld - 3:, :, COL_GDN:COL_GDN + 768], (1, 0, 2)))
        outs["scs"].append(jnp.transpose(pd[ld - 3:, :, COL_SSD + 256:COL_SSD + 1024], (1, 0, 2)))
        xd = _out_ffn(xd, [o.reshape(ld * nd, GROUP_WIDTH) for o in (da, db, dc, dd)], wo, norm_ffn, wu, wd,
                      p["norm_final"], l)

    y_prompt = xp.reshape(bp, lp, D_MODEL)
    y_sample = jnp.transpose(xd.reshape(ld, nd, D_MODEL), (1, 0, 2))
    st = {k: jnp.stack(v) for k, v in outs.items()}
    hs, gs, rs, ss = (jnp.transpose(s, (0, 4, 1, 2, 3)) for s in (new_hg, new_gd, new_rt, new_sd))
    return (y_prompt, y_sample, st["hp"], hs, st["gp"], gs, st["gcp"], st["gcs"],
            st["rp"], rs, st["sp"], ss, st["scp"], st["scs"])


def kernel(x_prompt, x_sample, state_hgrn, state_gdn, state_gdn_conv, state_ret, state_ssd, state_ssd_conv,
           norm_mix, w_in, hgrn_lb_logits, hgrn_norm, gdn_conv_w, gdn_a_log, gdn_dt_bias, gdn_norm,
           ssd_conv_w, ssd_conv_b, ssd_dt_bias, ssd_a_log, ssd_d, ssd_norm,
           w_out, norm_ffn, w_up, w_down, norm_final):
    params = dict(norm_mix=norm_mix, w_in=w_in, hgrn_lb_logits=hgrn_lb_logits, hgrn_norm=hgrn_norm,
                  gdn_conv_w=gdn_conv_w, gdn_a_log=gdn_a_log, gdn_dt_bias=gdn_dt_bias, gdn_norm=gdn_norm,
                  ssd_conv_w=ssd_conv_w, ssd_conv_b=ssd_conv_b, ssd_dt_bias=ssd_dt_bias, ssd_a_log=ssd_a_log,
                  ssd_d=ssd_d, ssd_norm=ssd_norm, w_out=w_out, norm_ffn=norm_ffn, w_up=w_up,
                  w_down=w_down, norm_final=norm_final)
    states = (state_hgrn, state_gdn, state_gdn_conv, state_ret, state_ssd, state_ssd_conv)
    return _forward(x_prompt, x_sample, states, params, 16384)
```

```python
import functools
import math

import numpy as np
import jax
import jax.numpy as jnp
from jax import lax
from jax.experimental import pallas as pl
from jax.experimental.pallas import tpu as pltpu

F32 = jnp.float32
BF16 = jnp.bfloat16

D_MODEL = 1024
GROUP_WIDTH = 256
HEAD_DIM = 64
N_HEADS = 4
CONV_WIDTH = 4
SSD_STATE = 128
D_FF = 4096
RET_THETA = 10000.0
EPS = 1e-6
DEPTH = 2

COL_HGRN = 0
COL_GDN = 1024
COL_RET = 2048
COL_SSD = 3072
COL_SMALL = 4096
P_PAD = 4224
SMALL_GA, SMALL_GB, SMALL_SDT = 0, 4, 8

VMEM_LIMIT = 56 * 1024 * 1024
LOG_GAMMA = [math.log(1.0 - 2.0 ** (-5.0 - h)) for h in range(N_HEADS)]


def _dot(a, b):
    return jnp.dot(a, b, preferred_element_type=F32)


def _dot_nt(a, b):
    return lax.dot_general(a, b, (((1,), (1,)), ((), ())), preferred_element_type=F32)


def _dot_tn(a, b):
    return lax.dot_general(a, b, (((0,), (0,)), ((), ())), preferred_element_type=F32)


def _round_robin(gens):
    live = list(gens)
    while live:
        nxt = []
        for g in live:
            try:
                next(g)
                nxt.append(g)
            except StopIteration:
                pass
        live = nxt


def _split3(x):
    hi = x.astype(BF16)
    r1 = x - hi.astype(F32)
    mid = r1.astype(BF16)
    lo = (r1 - mid.astype(F32)).astype(BF16)
    return hi, mid, lo


def _exact_dot(x, sel):
    hi, mid, lo = _split3(x)
    return _dot(hi, sel) + _dot(mid, sel) + _dot(lo, sel)


def _exact_dot_left(sel, x):
    hi, mid, lo = _split3(x)
    return _dot(sel, hi) + _dot(sel, mid) + _dot(sel, lo)


def _iota(shape, dim):
    return lax.broadcasted_iota(jnp.int32, shape, dim)


def _head_of_lane(n_lanes, width=HEAD_DIM):
    return _iota((1, n_lanes), 1) // width


def _head_masks(n_lanes=GROUP_WIDTH, width=HEAD_DIM):
    hl = _head_of_lane(n_lanes, width)
    return [hl == h for h in range(n_lanes // width)]


def _stack_heads(x, masks):
    return jnp.concatenate([jnp.where(m, x, jnp.zeros_like(x)) for m in masks], axis=0)


def _unstack_heads(y, masks, c):
    out = jnp.where(masks[0], y[0:c], 0.0)
    for h in range(1, len(masks)):
        out = out + jnp.where(masks[h], y[h * c:(h + 1) * c], 0.0)
    return out


def _block_ones(n, width, dtype=BF16):
    r = _iota((n, n), 0) // width
    c = _iota((n, n), 1) // width
    return (r == c).astype(dtype)


def _block_mask(n, rwidth, cwidth):
    return (_iota((n, n), 0) // rwidth) == (_iota((n, n), 1) // cwidth)


def _lower_tri(c, dtype=BF16):
    return (_iota((c, c), 0) >= _iota((c, c), 1)).astype(dtype)


def _sigmoid(x):
    return 1.0 / (1.0 + jnp.exp(-x))


def _silu(x):
    return x * _sigmoid(x)


def _softplus(x):
    return jnp.maximum(x, 0.0) + jnp.log(1.0 + jnp.exp(-jnp.abs(x)))


def _rms_rows(x):
    return x * lax.rsqrt(jnp.mean(x * x, axis=-1, keepdims=True) + EPS)


def _head_sumsq(x, ones_bd):
    sq = x * x
    hi = sq.astype(BF16)
    lo = (sq - hi.astype(F32)).astype(BF16)
    return _dot(hi, ones_bd) + _dot(lo, ones_bd)


def _expand_small(small, *first_lanes):
    r = _iota((128, GROUP_WIDTH), 0)
    c = _iota((128, GROUP_WIDTH), 1) // HEAD_DIM
    sel = jnp.concatenate([(r == c + fl).astype(BF16) for fl in first_lanes], axis=1)
    out = _exact_dot(small, sel)
    res = [out[:, i * GROUP_WIDTH:(i + 1) * GROUP_WIDTH] for i in range(len(first_lanes))]
    return res[0] if len(res) == 1 else res


def _decay_diff_operands(g):
    hi, mid, lo = (x.astype(F32) for x in _split3(g))
    pos = _iota(g.shape, 1) % HEAD_DIM
    a = jnp.where(pos == 0, hi, jnp.where(pos == 1, mid, jnp.where(pos == 2, lo,
                  jnp.where(pos < 6, 1.0, 0.0))))
    b = jnp.where(pos < 3, 1.0, jnp.where(pos == 3, -hi, jnp.where(pos == 4, -mid,
                  jnp.where(pos == 5, -lo, 0.0))))
    return a, b


def _extract_blocks(s_wide, rows, width):
    sel = ((_iota((GROUP_WIDTH, width), 0) % width) == _iota((GROUP_WIDTH, width), 1)).astype(BF16)
    return _exact_dot(s_wide, sel)


def _proj_kernel(x_ref, nw_ref, w_ref, o_ref):
    h = _rms_rows(x_ref[...]) * nw_ref[0]
    o_ref[...] = _dot_nt(h.astype(BF16), w_ref[0])


PROJ_ROWS = 512


def _proj(x2d, norm_w, w_bf16, layer):
    t = x2d.shape[0]
    tm = min(t, PROJ_ROWS)
    return pl.pallas_call(
        _proj_kernel,
        grid=(t // tm,),
        in_specs=[pl.BlockSpec((tm, D_MODEL), lambda i: (i, 0)),
                  pl.BlockSpec((1, 1, D_MODEL), lambda i: (layer, 0, 0)),
                  pl.BlockSpec((1, P_PAD, D_MODEL), lambda i: (layer, 0, 0))],
        out_specs=pl.BlockSpec((tm, P_PAD), lambda i: (i, 0)),
        out_shape=jax.ShapeDtypeStruct((t, P_PAD), F32),
        compiler_params=pltpu.CompilerParams(dimension_semantics=("arbitrary",),
                                             vmem_limit_bytes=VMEM_LIMIT),
        name="norm_in_proj",
    )(x2d, norm_w, w_bf16)


def _ffn_kernel(x_ref, oa_ref, ob_ref, oc_ref, od_ref, wo_ref, nf_ref, wu_ref, wd_ref, nfin_ref,
                o_ref, *, final):
    mix = jnp.concatenate([oa_ref[...], ob_ref[...], oc_ref[...], od_ref[...]], axis=1)
    x = x_ref[...] + _dot(mix, wo_ref[0])
    h = (_rms_rows(x) * nf_ref[0]).astype(BF16)
    acc = x
    ft = 1024
    for t in range(D_FF // ft):
        up = _dot(h, wu_ref[0, :, t * ft:(t + 1) * ft])
        up = jnp.square(jnp.maximum(up, 0.0)).astype(BF16)
        acc = acc + _dot(up, wd_ref[0, t * ft:(t + 1) * ft, :])
    if final:
        acc = _rms_rows(acc) * nfin_ref[...]
    o_ref[...] = acc


FFN_ROWS = 512


def _out_ffn(x2d, mixes, wo, nf, wu, wd, nfin, layer):
    t = x2d.shape[0]
    tm = min(t, FFN_ROWS)
    row = lambda i: (i, 0)
    lay = lambda i: (layer, 0, 0)
    return pl.pallas_call(
        functools.partial(_ffn_kernel, final=(layer == DEPTH - 1)),
        grid=(t // tm,),
        in_specs=[pl.BlockSpec((tm, D_MODEL), row)]
                 + [pl.BlockSpec((tm, GROUP_WIDTH), row)] * 4
                 + [pl.BlockSpec((1, D_MODEL, D_MODEL), lay),
                    pl.BlockSpec((1, 1, D_MODEL), lay),
                    pl.BlockSpec((1, D_MODEL, D_FF), lay),
                    pl.BlockSpec((1, D_FF, D_MODEL), lay),
                    pl.BlockSpec((1, D_MODEL), lambda i: (0, 0))],
        out_specs=pl.BlockSpec((tm, D_MODEL), row),
        out_shape=jax.ShapeDtypeStruct((t, D_MODEL), F32),
        compiler_params=pltpu.CompilerParams(dimension_semantics=("arbitrary",),
                                             vmem_limit_bytes=VMEM_LIMIT),
        name="out_proj_ffn",
    )(x2d, *mixes, wo, nf, wu, wd, nfin.reshape(1, D_MODEL))


def _swap_halves(x):
    first = (_iota((1, 128), 1) % HEAD_DIM) < (HEAD_DIM // 2)
    parts = []
    for p in range(GROUP_WIDTH // 128):
        xp = x[:, p * 128:(p + 1) * 128]
        parts.append(jnp.where(first, pltpu.roll(xp, 96, 1), pltpu.roll(xp, 32, 1)))
    return jnp.concatenate(parts, axis=1)


def _conv_silu(xe_ref, halo, x, w, bias, first_chunk, c):
    xe_ref[0:8, :] = jnp.where(first_chunk, jnp.zeros_like(halo), halo)
    xe_ref[8:, :] = x
    y = w[3:4, :] * x
    for j in range(CONV_WIDTH - 1):
        y = y + w[j:j + 1, :] * xe_ref[5 + j:5 + j + c, :]
    if bias is not None:
        y = y + bias
    return _silu(y)


def _ret_prompt_kernel(blk_ref, cos_ref, sin_ref, o_ref, st_ref, s_scr, *, c, n_chunks, nb):
    ci = pl.program_id(1)

    @pl.when(ci == 0)
    def _():
        s_scr[...] = jnp.zeros_like(s_scr)

    cosv, sinv = cos_ref[...], sin_ref[...]
    masks = _head_masks()
    hl = _head_of_lane(GROUP_WIDTH)
    lg = jnp.full((1, GROUP_WIDTH), LOG_GAMMA[0], F32)
    for h in range(1, N_HEADS):
        lg = jnp.where(hl == h, LOG_GAMMA[h], lg)
    ri = _iota((c, 1), 0).astype(F32)
    dij = (_iota((c, c), 0) - _iota((c, c), 1)).astype(F32)
    causal = dij >= 0.0
    decay = jnp.concatenate(
        [jnp.where(causal, jnp.exp(jnp.maximum(dij, 0.0) * LOG_GAMMA[h]), 0.0) for h in range(N_HEADS)],
        axis=0)
    q_scale = jnp.exp((ri + 1.0) * lg)
    k_scale = jnp.exp((float(c - 1) - ri) * lg) * (HEAD_DIM ** -0.5)
    s_scale = jnp.exp(float(c) * lg)
    bd_mask = _block_mask(GROUP_WIDTH, HEAD_DIM, HEAD_DIM)
    ones_bd = _block_ones(GROUP_WIDTH, HEAD_DIM)

    def one_sequence(sq):
        blk = blk_ref[sq]
        rq, rk, rv, rg = (blk[:, i * GROUP_WIDTH:(i + 1) * GROUP_WIDTH] for i in range(4))
        q = rq * cosv + _swap_halves(rq) * sinv
        k = rk * cosv + _swap_halves(rk) * sinv
        v = rv.astype(BF16)
        s = s_scr[sq]
        qk = _dot_nt(_stack_heads(q, masks).astype(BF16), k.astype(BF16))
        o_inter = _dot((q * q_scale).astype(BF16), s.astype(BF16))
        ds = _dot_tn((k * k_scale).astype(BF16), v)
        yield
        scores = qk * (decay * (HEAD_DIM ** -0.5))
        pv = _dot(scores.astype(BF16), v)
        s_scr[sq] = s_scale * s + jnp.where(bd_mask, ds, 0.0)
        yield
        o = _unstack_heads(pv, masks, c) + o_inter
        ss = _head_sumsq(o, ones_bd)
        yield
        o_ref[sq] = (o * lax.rsqrt(ss * (1.0 / HEAD_DIM) + EPS) * _silu(rg)).astype(BF16)

    _round_robin([one_sequence(sq) for sq in range(nb)])

    @pl.when(ci == n_chunks - 1)
    def _():
        for sq in range(nb):
            st_ref[sq] = _extract_blocks(s_scr[sq], GROUP_WIDTH, HEAD_DIM)


PROMPT_SEQS_PER_STEP = 8
GDN_SEQS_PER_STEP = 8
PROMPT_ROWS_PER_STEP = 2048


def _ret_prompt(proj3, cos_t, sin_t, c):
    b, l, _ = proj3.shape
    n = l // c
    nb = math.gcd(b, min(PROMPT_SEQS_PER_STEP, max(1, PROMPT_ROWS_PER_STEP // (l // n))))
    return pl.pallas_call(
        functools.partial(_ret_prompt_kernel, c=c, n_chunks=n, nb=nb),
        grid=(b // nb, n),
        in_specs=[pl.BlockSpec((nb, c, 1024), lambda bi, ci: (bi, ci, COL_RET // 1024)),
                  pl.BlockSpec((c, GROUP_WIDTH), lambda bi, ci: (ci, 0)),
                  pl.BlockSpec((c, GROUP_WIDTH), lambda bi, ci: (ci, 0))],
        out_specs=[pl.BlockSpec((nb, c, GROUP_WIDTH), lambda bi, ci: (bi, ci, 0)),
                   pl.BlockSpec((nb, GROUP_WIDTH, HEAD_DIM), lambda bi, ci: (bi, 0, 0))],
        out_shape=[jax.ShapeDtypeStruct((b, l, GROUP_WIDTH), BF16),
                   jax.ShapeDtypeStruct((b, GROUP_WIDTH, HEAD_DIM), F32)],
        scratch_shapes=[pltpu.VMEM((nb, GROUP_WIDTH, GROUP_WIDTH), F32)],
        compiler_params=pltpu.CompilerParams(dimension_semantics=("arbitrary", "arbitrary"),
                                             vmem_limit_bytes=VMEM_LIMIT),
        name="retention_prompt",
    )(proj3, cos_t, sin_t)


def _ssd_prompt_kernel(blk_ref, halo_ref, small_ref, cw_ref, cb_ref, dtb_ref, alog_ref, dskip_ref, nw_ref,
                       o_ref, st_ref, s_scr, xe_scr, *, c, n_chunks, nb):
    ci = pl.program_id(1)

    @pl.when(ci == 0)
    def _():
        s_scr[...] = jnp.zeros_like(s_scr)

    masks = _head_masks()
    causal = _iota((c, c), 0) >= _iota((c, c), 1)
    causal4 = jnp.concatenate([causal] * N_HEADS, axis=0)
    group_mask = _block_mask(GROUP_WIDTH, 128, 128)
    tri = _lower_tri(c)
    neg_a = -jnp.exp(alog_ref[...]) * LOG2E

    def one_sequence(sq):
        blk = blk_ref[sq]
        sz = blk[:, 0:GROUP_WIDTH]
        xbc = _conv_silu(xe_scr.at[sq], halo_ref[sq][:, GROUP_WIDTH:], blk[:, GROUP_WIDTH:], cw_ref[...],
                         cb_ref[...], ci == 0, c)
        xs = xbc[:, 0:256]
        bmat = xbc[:, 256:512].astype(BF16)
        cmat = xbc[:, 512:768].astype(BF16)
        s = s_scr[sq]
        cb = [_dot_nt(cmat[:, gi * 128:(gi + 1) * 128], bmat[:, gi * 128:(gi + 1) * 128]) for gi in range(2)]
        y_inter = _dot(cmat, s.astype(BF16))
        dt = _softplus(_expand_small(small_ref[sq], SMALL_SDT) + dtb_ref[...])
        yield
        g = _exact_dot_left(tri, neg_a * dt)
        yield
        g_last = g[c - 1:c, :]
        da, db = _decay_diff_operands(g)
        diff = _dot_nt(_stack_heads(da, masks).astype(BF16), db.astype(BF16))
        v = xs * dt
        vend = v * jnp.exp2(g_last - g)
        ds = _dot_tn(bmat, vend.astype(BF16))
        yield
        decay = jnp.where(causal4, jnp.exp2(diff), 0.0)
        scores = jnp.concatenate([cb[0], cb[0], cb[1], cb[1]], axis=0) * decay
        pv = _dot(scores.astype(BF16), v.astype(BF16))
        s_scr[sq] = jnp.exp2(g_last) * s + jnp.where(group_mask, ds, 0.0)
        yield
        y = _unstack_heads(pv, masks, c) + y_inter * jnp.exp2(g)
        y = (y + dskip_ref[...] * xs) * _silu(sz)
        halves = [_rms_rows(y[:, gi * 128:(gi + 1) * 128]) for gi in range(2)]
        o_ref[sq] = (jnp.concatenate(halves, axis=1) * nw_ref[...]).astype(BF16)

    _round_robin([one_sequence(sq) for sq in range(nb)])

    @pl.when(ci == n_chunks - 1)
    def _():
        for sq in range(nb):
            for h in range(N_HEADS):
                gi = h // 2
                rows = jnp.where(masks[h], s_scr[sq, gi * 128:(gi + 1) * 128, :], 0.0)
                st_ref[sq, h * 128:(h + 1) * 128, :] = _extract_blocks(rows, 128, HEAD_DIM)


def _lane_rep(p):
    return jnp.repeat(p.astype(F32), HEAD_DIM).reshape(1, GROUP_WIDTH)


def _ssd_prompt(proj3, conv_w, conv_b, dt_bias, a_log, d_skip, norm_w, c):
    b, l, _ = proj3.shape
    n = l // c
    fixed = lambda bi, ci: (0, 0)
    nb = math.gcd(b, min(PROMPT_SEQS_PER_STEP, max(1, PROMPT_ROWS_PER_STEP // (l // n))))
    return pl.pallas_call(
        functools.partial(_ssd_prompt_kernel, c=c, n_chunks=n, nb=nb),
        grid=(b // nb, n),
        in_specs=[pl.BlockSpec((nb, c, 1024), lambda bi, ci: (bi, ci, COL_SSD // 1024)),
                  pl.BlockSpec((nb, 8, 1024), lambda bi, ci: (bi, jnp.maximum(ci * (c // 8) - 1, 0), COL_SSD // 1024)),
                  pl.BlockSpec((nb, c, 128), lambda bi, ci: (bi, ci, COL_SMALL // 128)),
                  pl.BlockSpec((CONV_WIDTH, 768), fixed),
                  pl.BlockSpec((1, 768), fixed),
                  pl.BlockSpec((1, GROUP_WIDTH), fixed),
                  pl.BlockSpec((1, GROUP_WIDTH), fixed),
                  pl.BlockSpec((1, GROUP_WIDTH), fixed),
                  pl.BlockSpec((1, GROUP_WIDTH), fixed)],
        out_specs=[pl.BlockSpec((nb, c, GROUP_WIDTH), lambda bi, ci: (bi, ci, 0)),
                   pl.BlockSpec((nb, N_HEADS * SSD_STATE, HEAD_DIM), lambda bi, ci: (bi, 0, 0))],
        out_shape=[jax.ShapeDtypeStruct((b, l, GROUP_WIDTH), BF16),
                   jax.ShapeDtypeStruct((b, N_HEADS * SSD_STATE, HEAD_DIM), F32)],
        scratch_shapes=[pltpu.VMEM((nb, GROUP_WIDTH, GROUP_WIDTH), F32),
                        pltpu.VMEM((nb, c + 8, 768), F32)],
        compiler_params=pltpu.CompilerParams(dimension_semantics=("arbitrary", "arbitrary"),
                                             vmem_limit_bytes=VMEM_LIMIT),
        name="ssd_prompt",
    )(proj3, proj3, proj3, conv_w, conv_b.reshape(1, 768), _lane_rep(dt_bias), _lane_rep(a_log),
      _lane_rep(d_skip), norm_w.reshape(1, GROUP_WIDTH))


def _gdn_prompt_kernel(blk_ref, halo_ref, small_ref, cw_ref, alog_ref, dtb_ref, nw_ref,
                       o_ref, st_ref, s_scr, xe_scr, m_scr, *, c, n_chunks, nb):
    ci = pl.program_id(1)
    hc = 2 * c
    n_lvl = int(math.log2(c))

    @pl.when(ci == 0)
    def _():
        s_scr[...] = jnp.zeros_like(s_scr)

    @pl.when((pl.program_id(0) == 0) & (ci == 0))
    def _():
        rr = _iota((hc, hc), 0)
        cc = _iota((hc, hc), 1)
        same = (rr // c) == (cc // c)
        m_scr[0] = (same & (rr >= cc)).astype(F32)
        m_scr[1] = (same & (rr > cc)).astype(F32)
        for lv in range(n_lvl):
            sz = 1 << lv
            off = ((rr // (2 * sz)) == (cc // (2 * sz))) & (((rr // sz) % 2) == 1) & (((cc // sz) % 2) == 0)
            m_scr[2 + lv] = off.astype(F32)

    ones_bd = _block_ones(GROUP_WIDTH, HEAD_DIM)
    bd_mask = _block_mask(GROUP_WIDTH, HEAD_DIM, HEAD_DIM)
    masks = _head_masks()
    pair_masks = [masks[0:2], masks[2:4]]
    tri = _lower_tri(c)
    neg_a = -jnp.exp(alog_ref[...]) * LOG2E

    def one_sequence(sq):
        blk = blk_ref[sq]
        gz = blk[:, 768:1024]
        qkv = _conv_silu(xe_scr.at[sq], halo_ref[sq][:, 0:768], blk[:, 0:768], cw_ref[...], None, ci == 0, c)
        gq, gk, v = qkv[:, 0:256], qkv[:, 256:512], qkv[:, 512:768]
        q = gq * lax.rsqrt(_head_sumsq(gq, ones_bd) + EPS) * (HEAD_DIM ** -0.5)
        k = gk * lax.rsqrt(_head_sumsq(gk, ones_bd) + EPS)
        yield
        small = small_ref[sq]
        gb_x, ga_x = _expand_small(small, SMALL_GB, SMALL_GA)
        beta = _sigmoid(gb_x)
        g = _exact_dot_left(tri, neg_a * _softplus(ga_x + dtb_ref[...]))
        g_last = g[c - 1:c, :]
        eg = jnp.exp2(g)
        yield
        da, db = _decay_diff_operands(g)
        bk = beta * k
        bkg = bk * eg
        bv = beta * v
        a_mat, p_mat, x = [], [], []
        for pm in pair_masks:
            diff = _dot_nt(_stack_heads(da, pm).astype(BF16), _stack_heads(db, pm).astype(BF16))
            k_st = _stack_heads(k, pm).astype(BF16)
            kk = _dot_nt(_stack_heads(bk, pm).astype(BF16), k_st)
            qk = _dot_nt(_stack_heads(q, pm).astype(BF16), k_st)
            decay = jnp.exp2(jnp.minimum(diff, 0.0))
            a_mat.append(kk * (decay * m_scr[1]))
            p_mat.append((qk * (decay * m_scr[0])).astype(BF16))
            x.append(jnp.concatenate([_stack_heads(bkg, pm), _stack_heads(bv, pm)], axis=1))
        yield

        n_mat = [-(a * m_scr[2]) for a in a_mat]
        for lv in range(1, n_lvl):
            a_off = [a * m_scr[2 + lv] for a in a_mat]
            m = [ao + _dot(ao.astype(BF16), n.astype(BF16)) for ao, n in zip(a_off, n_mat)]
            yield
            n_mat = [n - mm - _dot(n.astype(BF16), mm.astype(BF16)) for n, mm in zip(n_mat, m)]
            yield
        x = [xx + _dot(n.astype(BF16), xx.astype(BF16)) for xx, n in zip(x, n_mat)]
        yield
        w = x[0][0:c, 0:256] + x[0][c:2 * c, 0:256] + x[1][0:c, 0:256] + x[1][c:2 * c, 0:256]
        u0 = x[0][0:c, 256:512] + x[0][c:2 * c, 256:512] + x[1][0:c, 256:512] + x[1][c:2 * c, 256:512]

        s = s_scr[sq]
        s_bf = s.astype(BF16)
        u = u0 - _dot(w.astype(BF16), s_bf)
        o = _dot((q * eg).astype(BF16), s_bf)
        yield
        pu = [_dot(pmat, _stack_heads(u, pm).astype(BF16)) for pmat, pm in zip(p_mat, pair_masks)]
        kend = k * jnp.exp2(g_last - g)
        ds = _dot_tn(kend.astype(BF16), u.astype(BF16))
        yield
        for pu_p in pu:
            o = o + pu_p[0:c] + pu_p[c:2 * c]
        s_scr[sq] = jnp.exp2(g_last) * s + jnp.where(bd_mask, ds, 0.0)
        ss = _head_sumsq(o, ones_bd)
        o_ref[sq] = (o * lax.rsqrt(ss * (1.0 / HEAD_DIM) + EPS) * nw_ref[...] * _silu(gz)).astype(BF16)

    _round_robin([one_sequence(sq) for sq in range(nb)])

    @pl.when(ci == n_chunks - 1)
    def _():
        for sq in range(nb):
            st_ref[sq] = _extract_blocks(s_scr[sq], GROUP_WIDTH, HEAD_DIM)


def _gdn_prompt(proj3, conv_w, a_log, dt_bias, norm_w, c):
    b, l, _ = proj3.shape
    n = l // c
    fixed = lambda bi, ci: (0, 0)
    nb = math.gcd(b, GDN_SEQS_PER_STEP)
    hc = 2 * c
    return pl.pallas_call(
        functools.partial(_gdn_prompt_kernel, c=c, n_chunks=n, nb=nb),
        grid=(b // nb, n),
        in_specs=[pl.BlockSpec((nb, c, 1024), lambda bi, ci: (bi, ci, COL_GDN // 1024)),
                  pl.BlockSpec((nb, 8, 1024), lambda bi, ci: (bi, jnp.maximum(ci * (c // 8) - 1, 0), COL_GDN // 1024)),
                  pl.BlockSpec((nb, c, 128), lambda bi, ci: (bi, ci, COL_SMALL // 128)),
                  pl.BlockSpec((CONV_WIDTH, 768), fixed),
                  pl.BlockSpec((1, GROUP_WIDTH), fixed),
                  pl.BlockSpec((1, GROUP_WIDTH), fixed),
                  pl.BlockSpec((1, GROUP_WIDTH), fixed)],
        out_specs=[pl.BlockSpec((nb, c, GROUP_WIDTH), lambda bi, ci: (bi, ci, 0)),
                   pl.BlockSpec((nb, GROUP_WIDTH, HEAD_DIM), lambda bi, ci: (bi, 0, 0))],
        out_shape=[jax.ShapeDtypeStruct((b, l, GROUP_WIDTH), BF16),
                   jax.ShapeDtypeStruct((b, GROUP_WIDTH, HEAD_DIM), F32)],
        scratch_shapes=[pltpu.VMEM((nb, GROUP_WIDTH, GROUP_WIDTH), F32),
                        pltpu.VMEM((nb, c + 8, 768), F32),
                        pltpu.VMEM((2 + int(math.log2(c)), hc, hc), F32)],
        compiler_params=pltpu.CompilerParams(dimension_semantics=("arbitrary", "arbitrary"),
                                             vmem_limit_bytes=VMEM_LIMIT),
        name="gdn_prompt",
    )(proj3, proj3, proj3, conv_w, _lane_rep(a_log), _lane_rep(dt_bias),
      jnp.tile(norm_w.astype(F32), N_HEADS).reshape(1, GROUP_WIDTH))


HGRN_SUB = 16
LOG2E = 1.4426950408889634


def _hgrn_lower_bound(logits, layer):
    rows = [logits[d:d + 1, :] for d in range(DEPTH)]
    mx = functools.reduce(jnp.maximum, rows)
    es = [jnp.exp(x - mx) for x in rows]
    tot = functools.reduce(lambda a, b: a + b, es)
    sm = [e / tot for e in es]
    acc = sm[0]
    for d in range(1, layer + 1):
        acc = acc + sm[d]
    return acc - sm[0]


def _hgrn_prompt_kernel(blk_ref, lb_ref, nw_ref, o_ref, st_ref, s_scr, *, r, n_chunks, layer, nb):
    ci = pl.program_id(1)
    sub = HGRN_SUB
    n_sub = r // sub

    @pl.when(ci == 0)
    def _():
        s_scr[...] = jnp.zeros_like(s_scr)

    lb = _hgrn_lower_bound(lb_ref[...], layer)
    rr = _iota((r, r), 0)
    cc = _iota((r, r), 1)
    same_sub = (rr // sub) == (cc // sub)
    cum_sel = (same_sub & (rr >= cc)).astype(BF16)
    tot_sel = same_sub.astype(BF16)
    ones_bd = _block_ones(GROUP_WIDTH, HEAD_DIM)
    masks = _head_masks()
    half = sub // 2
    i8 = _iota((half, 1), 0)

    def one_sequence(sq):
        blk = blk_ref[sq]
        hq, hf, hi, hg = (blk[:, i * GROUP_WIDTH:(i + 1) * GROUP_WIDTH] for i in range(4))
        f = lb + (1.0 - lb) * _sigmoid(hf)
        q = _sigmoid(hq)
        k = 1.0 - f
        v = hi
        logf = jnp.log(f)
        g = _exact_dot_left(cum_sel, logf)
        g_tot = _exact_dot_left(tot_sel, logf)
        yield
        a2 = (g + jnp.log(q)) * LOG2E
        h2 = (g - jnp.log(k)) * LOG2E
        gt2 = g_tot * LOG2E
        qt = _stack_heads(jnp.exp2(a2), masks).astype(BF16)
        kh = _stack_heads(jnp.exp2(gt2 - h2), masks).astype(BF16)
        v_heads = [v[:, h * HEAD_DIM:(h + 1) * HEAD_DIM].astype(BF16) for h in range(N_HEADS)]

        def sub_rows(x_st, lo):
            return jnp.concatenate([x_st[h * r + lo:h * r + lo + sub] for h in range(N_HEADS)], axis=0)

        s = s_scr[sq]
        outs = []
        for j in range(n_sub):
            lo = j * sub
            v_j = v[lo:lo + sub]
            a_lo, a_hi, h_j = a2[lo:lo + half], a2[lo + half:lo + sub], h2[lo:lo + sub]
            lo_blocks, hi_blocks = [], []
            for jj in range(sub):
                h_row = h_j[jj:jj + 1, :]
                if jj < half:
                    e_lo = jnp.exp2(a_lo - h_row)
                    lo_blocks.append(e_lo if jj == 0 else jnp.where(i8 >= jj, e_lo, 0.0))
                    hi_blocks.append(jnp.exp2(a_hi - h_row))
                else:
                    e_hi = jnp.exp2(a_hi - h_row)
                    hi_blocks.append(e_hi if jj == half else jnp.where(i8 >= jj - half, e_hi, 0.0))
            sc = _dot(jnp.concatenate(lo_blocks + hi_blocks, axis=0).astype(BF16), ones_bd)
            oi = _dot_nt(sub_rows(qt, lo), s.astype(BF16))
            v_rows = jnp.concatenate([vh[lo:lo + sub] for vh in v_heads], axis=0)
            ds = _dot_tn(v_rows, sub_rows(kh, lo))
            yield
            o_inter = jnp.concatenate([oi[h * sub:(h + 1) * sub] for h in range(N_HEADS)], axis=1)
            n_lo = half * half
            o_lo = sc[0:half] * v_j[0:1, :]
            o_hi = sc[n_lo:n_lo + half] * v_j[0:1, :]
            for jj in range(1, sub):
                if jj < half:
                    o_lo = o_lo + sc[jj * half:(jj + 1) * half] * v_j[jj:jj + 1, :]
                o_hi = o_hi + sc[n_lo + jj * half:n_lo + (jj + 1) * half] * v_j[jj:jj + 1, :]
            outs.append(jnp.concatenate([o_lo, o_hi], axis=0) + o_inter)
            s = jnp.exp2(gt2[lo:lo + 1, :]) * s + ds
        s_scr[sq] = s

        o = jnp.concatenate(outs, axis=0)
        ss = _head_sumsq(o, ones_bd)
        yield
        o_ref[sq] = (o * lax.rsqrt(ss * (1.0 / HEAD_DIM) + EPS) * nw_ref[...] * _silu(hg)).astype(BF16)

    _round_robin([one_sequence(sq) for sq in range(nb)])

    @pl.when(ci == n_chunks - 1)
    def _():
        for sq in range(nb):
            st_ref[sq] = s_scr[sq].T


def _hgrn_prompt(proj3, lb_logits, norm_w, layer, r):
    b, l, _ = proj3.shape
    n = l // r
    fixed = lambda bi, ci: (0, 0)
    nb = math.gcd(b, min(PROMPT_SEQS_PER_STEP, max(1, PROMPT_ROWS_PER_STEP // (l // n))))
    return pl.pallas_call(
        functools.partial(_hgrn_prompt_kernel, r=r, n_chunks=n, layer=layer, nb=nb),
        grid=(b // nb, n),
        in_specs=[pl.BlockSpec((nb, r, 1024), lambda bi, ci: (bi, ci, COL_HGRN // 1024)),
                  pl.BlockSpec((DEPTH, GROUP_WIDTH), fixed),
                  pl.BlockSpec((1, GROUP_WIDTH), fixed)],
        out_specs=[pl.BlockSpec((nb, r, GROUP_WIDTH), lambda bi, ci: (bi, ci, 0)),
                   pl.BlockSpec((nb, GROUP_WIDTH, HEAD_DIM), lambda bi, ci: (bi, 0, 0))],
        out_shape=[jax.ShapeDtypeStruct((b, l, GROUP_WIDTH), BF16),
                   jax.ShapeDtypeStruct((b, GROUP_WIDTH, HEAD_DIM), F32)],
        scratch_shapes=[pltpu.VMEM((nb, HEAD_DIM, GROUP_WIDTH), F32)],
        compiler_params=pltpu.CompilerParams(dimension_semantics=("arbitrary", "arbitrary"),
                                             vmem_limit_bytes=VMEM_LIMIT),
        name="hgrn_prompt",
    )(proj3, lb_logits.astype(F32), jnp.tile(norm_w.astype(F32), N_HEADS).reshape(1, GROUP_WIDTH))


DEC_SEQS = 128
DEC_LEN = 4


def _head_rows(h):
    return pl.ds(pl.multiple_of(h * HEAD_DIM, HEAD_DIM), HEAD_DIM)


def _recur_head(load_s, store_s, n_keys, decay_fn, k_fn, q_fn, v_blocks):
    def body(kk, accs):
        s = load_s(kk)
        accs = list(accs)
        for t in range(DEC_LEN):
            s = decay_fn(t, kk) * s + k_fn(t, kk) * v_blocks[t]
            accs[t] = accs[t] + q_fn(t, kk) * s
        store_s(kk, s)
        return tuple(accs)

    zero = jnp.zeros((HEAD_DIM, DEC_SEQS), F32)
    return lax.fori_loop(0, n_keys, body, (zero,) * DEC_LEN)


def _dec_ret_kernel(blk_ref, cos_ref, sin_ref, st_ref, o_ref, so_ref, q_scr, k_scr, v_scr, o_scr):
    h = pl.program_id(0)

    @pl.when(h == 0)
    def _():
        for t in range(DEC_LEN):
            blk = blk_ref[t]
            rq, rk, rv = blk[:, 0:256], blk[:, 256:512], blk[:, 512:768]
            cosv, sinv = cos_ref[t:t + 1, :], sin_ref[t:t + 1, :]
            q_scr[t] = (rq * cosv + _swap_halves(rq) * sinv).T
            k_scr[t] = ((rk * cosv + _swap_halves(rk) * sinv) * (HEAD_DIM ** -0.5)).T
            v_scr[t] = rv.T

    lg = jnp.where(h == 0, LOG_GAMMA[0], jnp.where(h == 1, LOG_GAMMA[1], jnp.where(h == 2, LOG_GAMMA[2], LOG_GAMMA[3])))
    gamma = jnp.exp(jnp.full((1, DEC_SEQS), lg, F32))
    hr = _head_rows(h)
    v_blocks = [v_scr[t, hr, :] for t in range(DEC_LEN)]
    accs = _recur_head(
        lambda kk: st_ref[0, 0, kk], functools.partial(_store_state, so_ref), HEAD_DIM,
        lambda t, kk: gamma,
        lambda t, kk: k_scr[t, pl.ds(h * HEAD_DIM + kk, 1), :],
        lambda t, kk: q_scr[t, pl.ds(h * HEAD_DIM + kk, 1), :],
        v_blocks)
    for t in range(DEC_LEN):
        o_scr[t, hr, :] = accs[t]

    @pl.when(h == N_HEADS - 1)
    def _():
        ones_bd = _block_ones(GROUP_WIDTH, HEAD_DIM)
        for t in range(DEC_LEN):
            o = o_scr[t].T
            ss = _head_sumsq(o, ones_bd)
            o_ref[t] = (o * lax.rsqrt(ss * (1.0 / HEAD_DIM) + EPS) * _silu(blk_ref[t][:, 768:1024])).astype(BF16)


def _store_state(so_ref, kk, s):
    so_ref[0, 0, kk] = s


def _without_ref(kernel_fn, idx):
    def wrapped(*refs):
        return kernel_fn(*refs[:idx], *refs[idx + 1:])
    return wrapped


def _zero_later_layers(kernel_fn, so_index):
    def wrapped(*refs):
        kernel_fn(*refs)
        so_ref = refs[so_index]
        so_ref[1:] = jnp.zeros((so_ref.shape[0] - 1,) + so_ref.shape[1:], F32)
    return wrapped


def _dec_call(kernel_fn, name, col, ins, in_specs, n_tok_scr, state_view, layer, carried, extra_scratch=()):
    blk_spec = pl.BlockSpec((DEC_LEN, DEC_SEQS, 1024), lambda h: (0, 0, col // 1024))
    st_spec = pl.BlockSpec((1, 1) + state_view.shape[2:], lambda h: (layer, h, 0, 0, 0))
    tok_scr = pltpu.VMEM((DEC_LEN, GROUP_WIDTH, DEC_SEQS), F32)
    ins = tuple(ins) + (state_view,)
    specs = [blk_spec] + in_specs + [st_spec]
    if carried is None:
        assert layer == 0
        so_spec = pl.BlockSpec((DEPTH, 1) + state_view.shape[2:], lambda h: (0, h, 0, 0, 0))
        kernel_fn = _zero_later_layers(kernel_fn, len(ins) + 1)
        aliases = {}
    else:
        so_spec = st_spec
        kernel_fn = _without_ref(kernel_fn, len(ins))
        aliases = {len(ins): 1}
        ins = ins + (carried,)
        specs = specs + [pl.BlockSpec(memory_space=pl.ANY)]
    return pl.pallas_call(
        kernel_fn,
        grid=(N_HEADS,),
        in_specs=specs,
        out_specs=[pl.BlockSpec((DEC_LEN, DEC_SEQS, GROUP_WIDTH), lambda h: (0, 0, 0)), so_spec],
        out_shape=[jax.ShapeDtypeStruct((DEC_LEN, DEC_SEQS, GROUP_WIDTH), BF16),
                   jax.ShapeDtypeStruct(state_view.shape, F32)],
        scratch_shapes=[tok_scr] * n_tok_scr + list(extra_scratch),
        input_output_aliases=aliases,
        compiler_params=pltpu.CompilerParams(dimension_semantics=("arbitrary",), vmem_limit_bytes=VMEM_LIMIT),
        name=name,
    )(*ins)


def _fixed1(shape):
    return pl.BlockSpec(shape, lambda h: (0,) * len(shape))


def _dec_ret(projd, cos_t, sin_t, state_view, layer, carried):
    return _dec_call(_dec_ret_kernel, "retention_decode", COL_RET, (projd, cos_t, sin_t),
                     [_fixed1((DEC_LEN, GROUP_WIDTH)), _fixed1((DEC_LEN, GROUP_WIDTH))], 4,
                     state_view, layer, carried)


def _dec_hgrn_kernel(blk_ref, lb_ref, nw_ref, st_ref, o_ref, so_ref, q_scr, k_scr, v_scr, f_scr, o_scr, *, layer):
    h = pl.program_id(0)

    @pl.when(h == 0)
    def _():
        lb = _hgrn_lower_bound(lb_ref[...], layer)
        for t in range(DEC_LEN):
            blk = blk_ref[t]
            f = lb + (1.0 - lb) * _sigmoid(blk[:, 256:512])
            q_scr[t] = _sigmoid(blk[:, 0:256]).T
            k_scr[t] = (1.0 - f).T
            v_scr[t] = blk[:, 512:768].T
            f_scr[t] = f.T

    hr = _head_rows(h)
    v_blocks = [v_scr[t, hr, :] for t in range(DEC_LEN)]
    row = lambda scr: (lambda t, kk: scr[t, pl.ds(h * HEAD_DIM + kk, 1), :])
    accs = _recur_head(lambda kk: st_ref[0, 0, kk], functools.partial(_store_state, so_ref), HEAD_DIM,
                       row(f_scr), row(k_scr), row(q_scr), v_blocks)
    for t in range(DEC_LEN):
        o_scr[t, hr, :] = accs[t]

    @pl.when(h == N_HEADS - 1)
    def _():
        ones_bd = _block_ones(GROUP_WIDTH, HEAD_DIM)
        for t in range(DEC_LEN):
            o = o_scr[t].T
            ss = _head_sumsq(o, ones_bd)
            o_ref[t] = (o * lax.rsqrt(ss * (1.0 / HEAD_DIM) + EPS) * nw_ref[...]
                        * _silu(blk_ref[t][:, 768:1024])).astype(BF16)


def _dec_hgrn(projd, lb_logits, norm_w, state_view, layer, carried):
    return _dec_call(functools.partial(_dec_hgrn_kernel, layer=layer), "hgrn_decode", COL_HGRN,
                     (projd, lb_logits.astype(F32), jnp.tile(norm_w.astype(F32), N_HEADS).reshape(1, GROUP_WIDTH)),
                     [_fixed1((DEPTH, GROUP_WIDTH)), _fixed1((1, GROUP_WIDTH))], 5, state_view, layer, carried)


def _hist_spec(layer):
    return pl.BlockSpec((1, CONV_WIDTH - 1, DEC_SEQS, 768), lambda h: (layer, 0, 0, 0))


def _dec_conv_silu(hist_ref, xs, w, bias):
    xe = [hist_ref[0, j] for j in range(CONV_WIDTH - 1)] + xs
    out = []
    for t in range(DEC_LEN):
        y = xe[t] * w[0:1, :]
        for j in range(1, CONV_WIDTH):
            y = y + xe[t + j] * w[j:j + 1, :]
        if bias is not None:
            y = y + bias
        out.append(_silu(y))
    return out


def _dec_ssd_kernel(blk_ref, small_ref, hist_ref, cw_ref, cb_ref, dtb_ref, alog_ref, dskip_ref, nw_ref, st_ref,
                    o_ref, so_ref, c_scr, b_scr, v_scr, a_scr, o_scr, x_scr):
    h = pl.program_id(0)

    @pl.when(h == 0)
    def _():
        xbc = _dec_conv_silu(hist_ref, [blk_ref[t][:, 256:1024] for t in range(DEC_LEN)], cw_ref[...], cb_ref[...])
        for t in range(DEC_LEN):
            xs = xbc[t][:, 0:256]
            dt = _softplus(_expand_small(small_ref[t], SMALL_SDT) + dtb_ref[...])
            x_scr[t] = xs
            v_scr[t] = (xs * dt).T
            b_scr[t] = xbc[t][:, 256:512].T
            c_scr[t] = xbc[t][:, 512:768].T
            a_scr[t] = jnp.exp(-jnp.exp(alog_ref[...]) * dt).T

    hr = _head_rows(h)
    g0 = (h // 2) * SSD_STATE
    v_blocks = [v_scr[t, hr, :] for t in range(DEC_LEN)]
    accs = _recur_head(
        lambda kk: st_ref[0, 0, kk], functools.partial(_store_state, so_ref), SSD_STATE,
        lambda t, kk: a_scr[t, pl.ds(h * HEAD_DIM, 1), :],
        lambda t, kk: b_scr[t, pl.ds(g0 + kk, 1), :],
        lambda t, kk: c_scr[t, pl.ds(g0 + kk, 1), :],
        v_blocks)
    for t in range(DEC_LEN):
        o_scr[t, hr, :] = accs[t]

    @pl.when(h == N_HEADS - 1)
    def _():
        for t in range(DEC_LEN):
            y = (o_scr[t].T + dskip_ref[...] * x_scr[t]) * _silu(blk_ref[t][:, 0:256])
            halves = [_rms_rows(y[:, gi * 128:(gi + 1) * 128]) for gi in range(2)]
            o_ref[t] = (jnp.concatenate(halves, axis=1) * nw_ref[...]).astype(BF16)


def _dec_ssd(projd, hist, conv_w, conv_b, dt_bias, a_log, d_skip, norm_w, state_view, layer, carried):
    small_spec = pl.BlockSpec((DEC_LEN, DEC_SEQS, 128), lambda h: (0, 0, COL_SMALL // 128))
    return _dec_call(_dec_ssd_kernel, "ssd_decode", COL_SSD,
                     (projd, projd, hist, conv_w, conv_b.reshape(1, 768), _lane_rep(dt_bias), _lane_rep(a_log),
                      _lane_rep(d_skip), norm_w.reshape(1, GROUP_WIDTH)),
                     [small_spec, _hist_spec(layer), _fixed1((CONV_WIDTH, 768)),
                      _fixed1((1, 768))] + [_fixed1((1, GROUP_WIDTH))] * 4, 5, state_view, layer, carried,
                     extra_scratch=[pltpu.VMEM((DEC_LEN, DEC_SEQS, GROUP_WIDTH), F32)])


def _dec_gdn_kernel(blk_ref, small_ref, hist_ref, cw_ref, alog_ref, dtb_ref, nw_ref, st_ref,
                    o_ref, so_ref, q_scr, k_scr, v_scr, a_scr, b_scr, o_scr):
    h = pl.program_id(0)

    @pl.when(h == 0)
    def _():
        ones_bd = _block_ones(GROUP_WIDTH, HEAD_DIM)
        qkv = _dec_conv_silu(hist_ref, [blk_ref[t][:, 0:768] for t in range(DEC_LEN)], cw_ref[...], None)
        for t in range(DEC_LEN):
            gq, gk, gv = qkv[t][:, 0:256], qkv[t][:, 256:512], qkv[t][:, 512:768]
            q_scr[t] = (gq * lax.rsqrt(_head_sumsq(gq, ones_bd) + EPS) * (HEAD_DIM ** -0.5)).T
            k_scr[t] = (gk * lax.rsqrt(_head_sumsq(gk, ones_bd) + EPS)).T
            v_scr[t] = gv.T
            small = small_ref[t]
            gb_x, ga_x = _expand_small(small, SMALL_GB, SMALL_GA)
            b_scr[t] = _sigmoid(gb_x).T
            la = -jnp.exp(alog_ref[...]) * _softplus(ga_x + dtb_ref[...])
            a_scr[t] = jnp.exp(la).T

    hr = _head_rows(h)
    one_row = pl.ds(h * HEAD_DIM, 1)
    zero = jnp.zeros((HEAD_DIM, DEC_SEQS), F32)
    for t in range(DEC_LEN):
        a = a_scr[t, one_row, :]
        cur = st_ref if t == 0 else so_ref

        def kts(kk, r):
            return r + k_scr[t, pl.ds(h * HEAD_DIM + kk, 1), :] * cur[0, 0, kk]

        r = lax.fori_loop(0, HEAD_DIM, kts, zero)
        u = b_scr[t, one_row, :] * (v_scr[t, hr, :] - a * r)

        def upd(kk, acc):
            s = a * cur[0, 0, kk] + k_scr[t, pl.ds(h * HEAD_DIM + kk, 1), :] * u
            so_ref[0, 0, kk] = s
            return acc + q_scr[t, pl.ds(h * HEAD_DIM + kk, 1), :] * s

        o_scr[t, hr, :] = lax.fori_loop(0, HEAD_DIM, upd, zero)

    @pl.when(h == N_HEADS - 1)
    def _():
        ones_bd = _block_ones(GROUP_WIDTH, HEAD_DIM)
        for t in range(DEC_LEN):
            o = o_scr[t].T
            ss = _head_sumsq(o, ones_bd)
            o_ref[t] = (o * lax.rsqrt(ss * (1.0 / HEAD_DIM) + EPS) * nw_ref[...]
                        * _silu(blk_ref[t][:, 768:1024])).astype(BF16)


def _dec_gdn(projd, hist, conv_w, a_log, dt_bias, norm_w, state_view, layer, carried):
    small_spec = pl.BlockSpec((DEC_LEN, DEC_SEQS, 128), lambda h: (0, 0, COL_SMALL // 128))
    return _dec_call(_dec_gdn_kernel, "gdn_decode", COL_GDN,
                     (projd, projd, hist, conv_w, _lane_rep(a_log), _lane_rep(dt_bias),
                      jnp.tile(norm_w.astype(F32), N_HEADS).reshape(1, GROUP_WIDTH)),
                     [small_spec, _hist_spec(layer), _fixed1((CONV_WIDTH, 768))]
                     + [_fixed1((1, GROUP_WIDTH))] * 3, 6, state_view, layer, carried)


W_PREP_ROWS = 128


def _w_in_t_prep_kernel(a_ref, b_ref, o_ref):
    j = pl.program_id(0)
    n_plain = COL_RET // W_PREP_ROWS
    n_main = COL_SMALL // W_PREP_ROWS
    row = _iota((W_PREP_ROWS, 1), 0)
    for l in range(DEPTH):
        a = a_ref[:, l, :]
        b = b_ref[:, l, :]
        shifted = jnp.concatenate([a[8:], b[:8]], axis=0)
        small = jnp.where(row < 8, a, jnp.where(row < 12, b, 0.0))
        out = jnp.where(j < n_plain, a, jnp.where(j < n_main, shifted, small))
        o_ref[l] = out.astype(BF16)


def _prep_w_in_t(w_in):
    wt = jnp.transpose(w_in, (2, 0, 1))
    n_plain = COL_RET // W_PREP_ROWS
    n_main = COL_SMALL // W_PREP_ROWS

    def a_idx(j):
        return (jnp.where(j < n_main, j, n_plain), 0, 0)

    def b_idx(j):
        return (jnp.where(j < n_plain, j, jnp.minimum(j + 1, n_main)), 0, 0)

    blk = (W_PREP_ROWS, DEPTH, D_MODEL)
    return pl.pallas_call(
        _w_in_t_prep_kernel,
        grid=(P_PAD // W_PREP_ROWS,),
        in_specs=[pl.BlockSpec(blk, a_idx), pl.BlockSpec(blk, b_idx)],
        out_specs=pl.BlockSpec((DEPTH, W_PREP_ROWS, D_MODEL), lambda j: (0, j, 0)),
        out_shape=jax.ShapeDtypeStruct((DEPTH, P_PAD, D_MODEL), BF16),
        compiler_params=pltpu.CompilerParams(dimension_semantics=("arbitrary",), vmem_limit_bytes=VMEM_LIMIT),
        name="w_in_prep",
    )(wt, wt)


def _rotary_tables(pos):
    half = HEAD_DIM // 2
    inv_freq = RET_THETA ** (-jnp.arange(half, dtype=F32) / half)
    ang = pos.astype(F32)[:, None] * inv_freq[None, :]
    cos, sin = jnp.cos(ang), jnp.sin(ang)
    cos_t = jnp.tile(cos, (1, 2 * N_HEADS))
    sin_t = jnp.tile(jnp.concatenate([-sin, sin], axis=1), (1, N_HEADS))
    return cos_t, sin_t


RET_CHUNK = 256
SSD_CHUNK = 256
GDN_CHUNK = 64
HGRN_ROWS = 128


def _forward(x_prompt, x_sample, states, p, past_len):
    st_hg, st_gd, st_gc, st_rt, st_sd, st_sc = states
    bp, lp, _ = x_prompt.shape
    nd, ld, _ = x_sample.shape
    xp = x_prompt.astype(F32).reshape(bp * lp, D_MODEL)
    xd = jnp.transpose(x_sample.astype(F32), (1, 0, 2)).reshape(ld * nd, D_MODEL)
    cos_p, sin_p = _rotary_tables(jnp.arange(lp))
    cos_d, sin_d = _rotary_tables(past_len + jnp.arange(ld))
    outs = {k: [] for k in ("hp", "gp", "gcp", "gcs", "rp", "sp", "scp", "scs")}
    w_in_all = _prep_w_in_t(p["w_in"].astype(F32))
    wo, wu, wd = (p[k].astype(BF16) for k in ("w_out", "w_up", "w_down"))
    norm_mix = p["norm_mix"].astype(F32).reshape(DEPTH, 1, D_MODEL)
    norm_ffn = p["norm_ffn"].astype(F32).reshape(DEPTH, 1, D_MODEL)
    sv_hg, sv_gd, sv_rt, sv_sd = (jnp.transpose(s.astype(F32), (0, 2, 3, 4, 1)) for s in (st_hg, st_gd, st_rt, st_sd))
    hist_g = jnp.transpose(st_gc.astype(F32), (0, 2, 1, 3))
    hist_s = jnp.transpose(st_sc.astype(F32), (0, 2, 1, 3))
    new_hg = new_gd = new_rt = new_sd = None
    for l in range(DEPTH):
        pp = _proj(xp, norm_mix, w_in_all, l).reshape(bp, lp, P_PAD)
        pd = _proj(xd, norm_mix, w_in_all, l).reshape(ld, nd, P_PAD)

        oa, sa = _hgrn_prompt(pp, p["hgrn_lb_logits"], p["hgrn_norm"][l], l, HGRN_ROWS)
        ob, sb = _gdn_prompt(pp, p["gdn_conv_w"][l], p["gdn_a_log"][l], p["gdn_dt_bias"][l], p["gdn_norm"][l],
                             GDN_CHUNK)
        oc, sc = _ret_prompt(pp, cos_p, sin_p, RET_CHUNK)
        od, sd = _ssd_prompt(pp, p["ssd_conv_w"][l], p["ssd_conv_b"][l], p["ssd_dt_bias"][l], p["ssd_a_log"][l],
                             p["ssd_d"][l], p["ssd_norm"][l], SSD_CHUNK)
        outs["hp"].append(sa.reshape(bp, N_HEADS, HEAD_DIM, HEAD_DIM))
        outs["gp"].append(sb.reshape(bp, N_HEADS, HEAD_DIM, HEAD_DIM))
        outs["rp"].append(sc.reshape(bp, N_HEADS, HEAD_DIM, HEAD_DIM))
        outs["sp"].append(sd.reshape(bp, N_HEADS, SSD_STATE, HEAD_DIM))
        outs["gcp"].append(pp[:, lp - 3:, COL_GDN:COL_GDN + 768])
        outs["scp"].append(pp[:, lp - 3:, COL_SSD + 256:COL_SSD + 1024])
        xp = _out_ffn(xp, [o.reshape(bp * lp, GROUP_WIDTH) for o in (oa, ob, oc, od)], wo, norm_ffn, wu, wd,
                      p["norm_final"], l)

        da, new_hg = _dec_hgrn(pd, p["hgrn_lb_logits"], p["hgrn_norm"][l], sv_hg, l, new_hg)
        db, new_gd = _dec_gdn(pd, hist_g, p["gdn_conv_w"][l], p["gdn_a_log"][l], p["gdn_dt_bias"][l],
                              p["gdn_norm"][l], sv_gd, l, new_gd)
        dc, new_rt = _dec_ret(pd, cos_d, sin_d, sv_rt, l, new_rt)
        dd, new_sd = _dec_ssd(pd, hist_s, p["ssd_conv_w"][l], p["ssd_conv_b"][l], p["ssd_dt_bias"][l],
                              p["ssd_a_log"][l], p["ssd_d"][l], p["ssd_norm"][l], sv_sd, l, new_sd)
        outs["gcs"].append(jnp.transpose(pd[ld - 3:, :, COL_GDN:COL_GDN + 768], (1, 0, 2)))
        outs["scs"].append(jnp.transpose(pd[ld - 3:, :, COL_SSD + 256:COL_SSD + 1024], (1, 0, 2)))
        xd = _out_ffn(xd, [o.reshape(ld * nd, GROUP_WIDTH) for o in (da, db, dc, dd)], wo, norm_ffn, wu, wd,
                      p["norm_final"], l)

    y_prompt = xp.reshape(bp, lp, D_MODEL)
    y_sample = jnp.transpose(xd.reshape(ld, nd, D_MODEL), (1, 0, 2))
    st = {k: jnp.stack(v) for k, v in outs.items()}
    hs, gs, rs, ss = (jnp.transpose(s, (0, 4, 1, 2, 3)) for s in (new_hg, new_gd, new_rt, new_sd))
    return (y_prompt, y_sample, st["hp"], hs, st["gp"], gs, st["gcp"], st["gcs"],
            st["rp"], rs, st["sp"], ss, st["scp"], st["scs"])


def kernel(x_prompt, x_sample, state_hgrn, state_gdn, state_gdn_conv, state_ret, state_ssd, state_ssd_conv,
           norm_mix, w_in, hgrn_lb_logits, hgrn_norm, gdn_conv_w, gdn_a_log, gdn_dt_bias, gdn_norm,
           ssd_conv_w, ssd_conv_b, ssd_dt_bias, ssd_a_log, ssd_d, ssd_norm,
           w_out, norm_ffn, w_up, w_down, norm_final):
    params = dict(norm_mix=norm_mix, w_in=w_in, hgrn_lb_logits=hgrn_lb_logits, hgrn_norm=hgrn_norm,
                  gdn_conv_w=gdn_conv_w, gdn_a_log=gdn_a_log, gdn_dt_bias=gdn_dt_bias, gdn_norm=gdn_norm,
                  ssd_conv_w=ssd_conv_w, ssd_conv_b=ssd_conv_b, ssd_dt_bias=ssd_dt_bias, ssd_a_log=ssd_a_log,
                  ssd_d=ssd_d, ssd_norm=ssd_norm, w_out=w_out, norm_ffn=norm_ffn, w_up=w_up,
                  w_down=w_down, norm_final=norm_final)
    states = (state_hgrn, state_gdn, state_gdn_conv, state_ret, state_ssd, state_ssd_conv)
    return _forward(x_prompt, x_sample, states, params, 16384)
```

```python
import functools
import math

import numpy as np
import jax
import jax.numpy as jnp
from jax import lax
from jax.experimental import pallas as pl
from jax.experimental.pallas import tpu as pltpu

F32 = jnp.float32
BF16 = jnp.bfloat16

D_MODEL = 1024
GROUP_WIDTH = 256
HEAD_DIM = 64
N_HEADS = 4
CONV_WIDTH = 4
SSD_STATE = 128
D_FF = 4096
RET_THETA = 10000.0
EPS = 1e-6
DEPTH = 2

COL_HGRN = 0
COL_GDN = 1024
COL_RET = 2048
COL_SSD = 3072
COL_SMALL = 4096
P_PAD = 4224
SMALL_GA, SMALL_GB, SMALL_SDT = 0, 4, 8

VMEM_LIMIT = 56 * 1024 * 1024
LOG_GAMMA = [math.log(1.0 - 2.0 ** (-5.0 - h)) for h in range(N_HEADS)]


def _dot(a, b):
    return jnp.dot(a, b, preferred_element_type=F32)


def _dot_nt(a, b):
    return lax.dot_general(a, b, (((1,), (1,)), ((), ())), preferred_element_type=F32)


def _dot_tn(a, b):
    return lax.dot_general(a, b, (((0,), (0,)), ((), ())), preferred_element_type=F32)


def _round_robin(gens):
    live = list(gens)
    while live:
        nxt = []
        for g in live:
            try:
                next(g)
                nxt.append(g)
            except StopIteration:
                pass
        live = nxt


def _split3(x):
    hi = x.astype(BF16)
    r1 = x - hi.astype(F32)
    mid = r1.astype(BF16)
    lo = (r1 - mid.astype(F32)).astype(BF16)
    return hi, mid, lo


def _exact_dot(x, sel):
    hi, mid, lo = _split3(x)
    return _dot(hi, sel) + _dot(mid, sel) + _dot(lo, sel)


def _exact_dot_left(sel, x):
    hi, mid, lo = _split3(x)
    return _dot(sel, hi) + _dot(sel, mid) + _dot(sel, lo)


def _iota(shape, dim):
    return lax.broadcasted_iota(jnp.int32, shape, dim)


def _head_of_lane(n_lanes, width=HEAD_DIM):
    return _iota((1, n_lanes), 1) // width


def _head_masks(n_lanes=GROUP_WIDTH, width=HEAD_DIM):
    hl = _head_of_lane(n_lanes, width)
    return [hl == h for h in range(n_lanes // width)]


def _stack_heads(x, masks):
    return jnp.concatenate([jnp.where(m, x, jnp.zeros_like(x)) for m in masks], axis=0)


def _unstack_heads(y, masks, c):
    out = jnp.where(masks[0], y[0:c], 0.0)
    for h in range(1, len(masks)):
        out = out + jnp.where(masks[h], y[h * c:(h + 1) * c], 0.0)
    return out


def _block_ones(n, width, dtype=BF16):
    r = _iota((n, n), 0) // width
    c = _iota((n, n), 1) // width
    return (r == c).astype(dtype)


def _block_mask(n, rwidth, cwidth):
    return (_iota((n, n), 0) // rwidth) == (_iota((n, n), 1) // cwidth)


def _lower_tri(c, dtype=BF16):
    return (_iota((c, c), 0) >= _iota((c, c), 1)).astype(dtype)


def _sigmoid(x):
    return 1.0 / (1.0 + jnp.exp(-x))


def _silu(x):
    return x * _sigmoid(x)


def _softplus(x):
    return jnp.maximum(x, 0.0) + jnp.log(1.0 + jnp.exp(-jnp.abs(x)))


def _rms_rows(x):
    return x * lax.rsqrt(jnp.mean(x * x, axis=-1, keepdims=True) + EPS)


def _head_sumsq(x, ones_bd):
    sq = x * x
    hi = sq.astype(BF16)
    lo = (sq - hi.astype(F32)).astype(BF16)
    return _dot(hi, ones_bd) + _dot(lo, ones_bd)


def _expand_small(small, *first_lanes):
    r = _iota((128, GROUP_WIDTH), 0)
    c = _iota((128, GROUP_WIDTH), 1) // HEAD_DIM
    sel = jnp.concatenate([(r == c + fl).astype(BF16) for fl in first_lanes], axis=1)
    out = _exact_dot(small, sel)
    res = [out[:, i * GROUP_WIDTH:(i + 1) * GROUP_WIDTH] for i in range(len(first_lanes))]
    return res[0] if len(res) == 1 else res


def _decay_diff_operands(g):
    hi, mid, lo = (x.astype(F32) for x in _split3(g))
    pos = _iota(g.shape, 1) % HEAD_DIM
    a = jnp.where(pos == 0, hi, jnp.where(pos == 1, mid, jnp.where(pos == 2, lo,
                  jnp.where(pos < 6, 1.0, 0.0))))
    b = jnp.where(pos < 3, 1.0, jnp.where(pos == 3, -hi, jnp.where(pos == 4, -mid,
                  jnp.where(pos == 5, -lo, 0.0))))
    return a, b


def _extract_blocks(s_wide, rows, width):
    sel = ((_iota((GROUP_WIDTH, width), 0) % width) == _iota((GROUP_WIDTH, width), 1)).astype(BF16)
    return _exact_dot(s_wide, sel)


def _proj_kernel(x_ref, nw_ref, w_ref, o_ref):
    h = _rms_rows(x_ref[...]) * nw_ref[0]
    o_ref[...] = _dot_nt(h.astype(BF16), w_ref[0])


PROJ_ROWS = 512


def _proj(x2d, norm_w, w_bf16, layer):
    t = x2d.shape[0]
    tm = min(t, PROJ_ROWS)
    return pl.pallas_call(
        _proj_kernel,
        grid=(t // tm,),
        in_specs=[pl.BlockSpec((tm, D_MODEL), lambda i: (i, 0)),
                  pl.BlockSpec((1, 1, D_MODEL), lambda i: (layer, 0, 0)),
                  pl.BlockSpec((1, P_PAD, D_MODEL), lambda i: (layer, 0, 0))],
        out_specs=pl.BlockSpec((tm, P_PAD), lambda i: (i, 0)),
        out_shape=jax.ShapeDtypeStruct((t, P_PAD), F32),
        compiler_params=pltpu.CompilerParams(dimension_semantics=("arbitrary",),
                                             vmem_limit_bytes=VMEM_LIMIT),
        name="norm_in_proj",
    )(x2d, norm_w, w_bf16)


def _ffn_kernel(x_ref, oa_ref, ob_ref, oc_ref, od_ref, wo_ref, nf_ref, wu_ref, wd_ref, nfin_ref,
                o_ref, *, final):
    mix = jnp.concatenate([oa_ref[...], ob_ref[...], oc_ref[...], od_ref[...]], axis=1)
    x = x_ref[...] + _dot(mix, wo_ref[0])
    h = (_rms_rows(x) * nf_ref[0]).astype(BF16)
    acc = x
    ft = 1024
    for t in range(D_FF // ft):
        up = _dot(h, wu_ref[0, :, t * ft:(t + 1) * ft])
        up = jnp.square(jnp.maximum(up, 0.0)).astype(BF16)
        acc = acc + _dot(up, wd_ref[0, t * ft:(t + 1) * ft, :])
    if final:
        acc = _rms_rows(acc) * nfin_ref[...]
    o_ref[...] = acc


FFN_ROWS = 512


def _out_ffn(x2d, mixes, wo, nf, wu, wd, nfin, layer):
    t = x2d.shape[0]
    tm = min(t, FFN_ROWS)
    row = lambda i: (i, 0)
    lay = lambda i: (layer, 0, 0)
    return pl.pallas_call(
        functools.partial(_ffn_kernel, final=(layer == DEPTH - 1)),
        grid=(t // tm,),
        in_specs=[pl.BlockSpec((tm, D_MODEL), row)]
                 + [pl.BlockSpec((tm, GROUP_WIDTH), row)] * 4
                 + [pl.BlockSpec((1, D_MODEL, D_MODEL), lay),
                    pl.BlockSpec((1, 1, D_MODEL), lay),
                    pl.BlockSpec((1, D_MODEL, D_FF), lay),
                    pl.BlockSpec((1, D_FF, D_MODEL), lay),
                    pl.BlockSpec((1, D_MODEL), lambda i: (0, 0))],
        out_specs=pl.BlockSpec((tm, D_MODEL), row),
        out_shape=jax.ShapeDtypeStruct((t, D_MODEL), F32),
        compiler_params=pltpu.CompilerParams(dimension_semantics=("arbitrary",),
                                             vmem_limit_bytes=VMEM_LIMIT),
        name="out_proj_ffn",
    )(x2d, *mixes, wo, nf, wu, wd, nfin.reshape(1, D_MODEL))


def _swap_halves(x):
    first = (_iota((1, 128), 1) % HEAD_DIM) < (HEAD_DIM // 2)
    parts = []
    for p in range(GROUP_WIDTH // 128):
        xp = x[:, p * 128:(p + 1) * 128]
        parts.append(jnp.where(first, pltpu.roll(xp, 96, 1), pltpu.roll(xp, 32, 1)))
    return jnp.concatenate(parts, axis=1)


def _conv_silu(xe_ref, halo, x, w, bias, first_chunk, c):
    xe_ref[0:8, :] = jnp.where(first_chunk, jnp.zeros_like(halo), halo)
    xe_ref[8:, :] = x
    y = w[3:4, :] * x
    for j in range(CONV_WIDTH - 1):
        y = y + w[j:j + 1, :] * xe_ref[5 + j:5 + j + c, :]
    if bias is not None:
        y = y + bias
    return _silu(y)


def _ret_prompt_kernel(blk_ref, cos_ref, sin_ref, o_ref, st_ref, s_scr, *, c, n_chunks, nb):
    ci = pl.program_id(1)

    @pl.when(ci == 0)
    def _():
        s_scr[...] = jnp.zeros_like(s_scr)

    cosv, sinv = cos_ref[...], sin_ref[...]
    masks = _head_masks()
    hl = _head_of_lane(GROUP_WIDTH)
    lg = jnp.full((1, GROUP_WIDTH), LOG_GAMMA[0], F32)
    for h in range(1, N_HEADS):
        lg = jnp.where(hl == h, LOG_GAMMA[h], lg)
    ri = _iota((c, 1), 0).astype(F32)
    dij = (_iota((c, c), 0) - _iota((c, c), 1)).astype(F32)
    causal = dij >= 0.0
    decay = jnp.concatenate(
        [jnp.where(causal, jnp.exp(jnp.maximum(dij, 0.0) * LOG_GAMMA[h]), 0.0) for h in range(N_HEADS)],
        axis=0)
    q_scale = jnp.exp((ri + 1.0) * lg)
    k_scale = jnp.exp((float(c - 1) - ri) * lg) * (HEAD_DIM ** -0.5)
    s_scale = jnp.exp(float(c) * lg)
    bd_mask = _block_mask(GROUP_WIDTH, HEAD_DIM, HEAD_DIM)
    ones_bd = _block_ones(GROUP_WIDTH, HEAD_DIM)

    def one_sequence(sq):
        blk = blk_ref[sq]
        rq, rk, rv, rg = (blk[:, i * GROUP_WIDTH:(i + 1) * GROUP_WIDTH] for i in range(4))
        q = rq * cosv + _swap_halves(rq) * sinv
        k = rk * cosv + _swap_halves(rk) * sinv
        v = rv.astype(BF16)
        s = s_scr[sq]
        qk = _dot_nt(_stack_heads(q, masks).astype(BF16), k.astype(BF16))
        o_inter = _dot((q * q_scale).astype(BF16), s.astype(BF16))
        ds = _dot_tn((k * k_scale).astype(BF16), v)
        yield
        scores = qk * (decay * (HEAD_DIM ** -0.5))
        pv = _dot(scores.astype(BF16), v)
        s_scr[sq] = s_scale * s + jnp.where(bd_mask, ds, 0.0)
        yield
        o = _unstack_heads(pv, masks, c) + o_inter
        ss = _head_sumsq(o, ones_bd)
        yield
        o_ref[sq] = (o * lax.rsqrt(ss * (1.0 / HEAD_DIM) + EPS) * _silu(rg)).astype(BF16)

    _round_robin([one_sequence(sq) for sq in range(nb)])

    @pl.when(ci == n_chunks - 1)
    def _():
        for sq in range(nb):
            st_ref[sq] = _extract_blocks(s_scr[sq], GROUP_WIDTH, HEAD_DIM)


PROMPT_SEQS_PER_STEP = 8
GDN_SEQS_PER_STEP = 8
PROMPT_ROWS_PER_STEP = 2048


def _ret_prompt(proj3, cos_t, sin_t, c):
    b, l, _ = proj3.shape
    n = l // c
    nb = math.gcd(b, min(PROMPT_SEQS_PER_STEP, max(1, PROMPT_ROWS_PER_STEP // (l // n))))
    return pl.pallas_call(
        functools.partial(_ret_prompt_kernel, c=c, n_chunks=n, nb=nb),
        grid=(b // nb, n),
        in_specs=[pl.BlockSpec((nb, c, 1024), lambda bi, ci: (bi, ci, COL_RET // 1024)),
                  pl.BlockSpec((c, GROUP_WIDTH), lambda bi, ci: (ci, 0)),
                  pl.BlockSpec((c, GROUP_WIDTH), lambda bi, ci: (ci, 0))],
        out_specs=[pl.BlockSpec((nb, c, GROUP_WIDTH), lambda bi, ci: (bi, ci, 0)),
                   pl.BlockSpec((nb, GROUP_WIDTH, HEAD_DIM), lambda bi, ci: (bi, 0, 0))],
        out_shape=[jax.ShapeDtypeStruct((b, l, GROUP_WIDTH), BF16),
                   jax.ShapeDtypeStruct((b, GROUP_WIDTH, HEAD_DIM), F32)],
        scratch_shapes=[pltpu.VMEM((nb, GROUP_WIDTH, GROUP_WIDTH), F32)],
        compiler_params=pltpu.CompilerParams(dimension_semantics=("arbitrary", "arbitrary"),
                                             vmem_limit_bytes=VMEM_LIMIT),
        name="retention_prompt",
    )(proj3, cos_t, sin_t)


def _ssd_prompt_kernel(blk_ref, halo_ref, small_ref, cw_ref, cb_ref, dtb_ref, alog_ref, dskip_ref, nw_ref,
                       o_ref, st_ref, s_scr, xe_scr, *, c, n_chunks, nb):
    ci = pl.program_id(1)

    @pl.when(ci == 0)
    def _():
        s_scr[...] = jnp.zeros_like(s_scr)

    masks = _head_masks()
    causal = _iota((c, c), 0) >= _iota((c, c), 1)
    causal4 = jnp.concatenate([causal] * N_HEADS, axis=0)
    group_mask = _block_mask(GROUP_WIDTH, 128, 128)
    tri = _lower_tri(c)
    neg_a = -jnp.exp(alog_ref[...]) * LOG2E

    def one_sequence(sq):
        blk = blk_ref[sq]
        sz = blk[:, 0:GROUP_WIDTH]
        xbc = _conv_silu(xe_scr.at[sq], halo_ref[sq][:, GROUP_WIDTH:], blk[:, GROUP_WIDTH:], cw_ref[...],
                         cb_ref[...], ci == 0, c)
        xs = xbc[:, 0:256]
        bmat = xbc[:, 256:512].astype(BF16)
        cmat = xbc[:, 512:768].astype(BF16)
        s = s_scr[sq]
        cb = [_dot_nt(cmat[:, gi * 128:(gi + 1) * 128], bmat[:, gi * 128:(gi + 1) * 128]) for gi in range(2)]
        y_inter = _dot(cmat, s.astype(BF16))
        dt = _softplus(_expand_small(small_ref[sq], SMALL_SDT) + dtb_ref[...])
        yield
        g = _exact_dot_left(tri, neg_a * dt)
        yield
        g_last = g[c - 1:c, :]
        da, db = _decay_diff_operands(g)
        diff = _dot_nt(_stack_heads(da, masks).astype(BF16), db.astype(BF16))
        v = xs * dt
        vend = v * jnp.exp2(g_last - g)
        ds = _dot_tn(bmat, vend.astype(BF16))
        yield
        decay = jnp.where(causal4, jnp.exp2(diff), 0.0)
        scores = jnp.concatenate([cb[0], cb[0], cb[1], cb[1]], axis=0) * decay
        pv = _dot(scores.astype(BF16), v.astype(BF16))
        s_scr[sq] = jnp.exp2(g_last) * s + jnp.where(group_mask, ds, 0.0)
        yield
        y = _unstack_heads(pv, masks, c) + y_inter * jnp.exp2(g)
        y = (y + dskip_ref[...] * xs) * _silu(sz)
        halves = [_rms_rows(y[:, gi * 128:(gi + 1) * 128]) for gi in range(2)]
        o_ref[sq] = (jnp.concatenate(halves, axis=1) * nw_ref[...]).astype(BF16)

    _round_robin([one_sequence(sq) for sq in range(nb)])

    @pl.when(ci == n_chunks - 1)
    def _():
        for sq in range(nb):
            for h in range(N_HEADS):
                gi = h // 2
                rows = jnp.where(masks[h], s_scr[sq, gi * 128:(gi + 1) * 128, :], 0.0)
                st_ref[sq, h * 128:(h + 1) * 128, :] = _extract_blocks(rows, 128, HEAD_DIM)


def _lane_rep(p):
    return jnp.repeat(p.astype(F32), HEAD_DIM).reshape(1, GROUP_WIDTH)


def _ssd_prompt(proj3, conv_w, conv_b, dt_bias, a_log, d_skip, norm_w, c):
    b, l, _ = proj3.shape
    n = l // c
    fixed = lambda bi, ci: (0, 0)
    nb = math.gcd(b, min(PROMPT_SEQS_PER_STEP, max(1, PROMPT_ROWS_PER_STEP // (l // n))))
    return pl.pallas_call(
        functools.partial(_ssd_prompt_kernel, c=c, n_chunks=n, nb=nb),
        grid=(b // nb, n),
        in_specs=[pl.BlockSpec((nb, c, 1024), lambda bi, ci: (bi, ci, COL_SSD // 1024)),
                  pl.BlockSpec((nb, 8, 1024), lambda bi, ci: (bi, jnp.maximum(ci * (c // 8) - 1, 0), COL_SSD // 1024)),
                  pl.BlockSpec((nb, c, 128), lambda bi, ci: (bi, ci, COL_SMALL // 128)),
                  pl.BlockSpec((CONV_WIDTH, 768), fixed),
                  pl.BlockSpec((1, 768), fixed),
                  pl.BlockSpec((1, GROUP_WIDTH), fixed),
                  pl.BlockSpec((1, GROUP_WIDTH), fixed),
                  pl.BlockSpec((1, GROUP_WIDTH), fixed),
                  pl.BlockSpec((1, GROUP_WIDTH), fixed)],
        out_specs=[pl.BlockSpec((nb, c, GROUP_WIDTH), lambda bi, ci: (bi, ci, 0)),
                   pl.BlockSpec((nb, N_HEADS * SSD_STATE, HEAD_DIM), lambda bi, ci: (bi, 0, 0))],
        out_shape=[jax.ShapeDtypeStruct((b, l, GROUP_WIDTH), BF16),
                   jax.ShapeDtypeStruct((b, N_HEADS * SSD_STATE, HEAD_DIM), F32)],
        scratch_shapes=[pltpu.VMEM((nb, GROUP_WIDTH, GROUP_WIDTH), F32),
                        pltpu.VMEM((nb, c + 8, 768), F32)],
        compiler_params=pltpu.CompilerParams(dimension_semantics=("arbitrary", "arbitrary"),
                                             vmem_limit_bytes=VMEM_LIMIT),
        name="ssd_prompt",
    )(proj3, proj3, proj3, conv_w, conv_b.reshape(1, 768), _lane_rep(dt_bias), _lane_rep(a_log),
      _lane_rep(d_skip), norm_w.reshape(1, GROUP_WIDTH))


def _gdn_prompt_kernel(blk_ref, halo_ref, small_ref, cw_ref, alog_ref, dtb_ref, nw_ref,
                       o_ref, st_ref, s_scr, xe_scr, m_scr, *, c, n_chunks, nb):
    ci = pl.program_id(1)
    hc = 2 * c
    n_lvl = int(math.log2(c))

    @pl.when(ci == 0)
    def _():
        s_scr[...] = jnp.zeros_like(s_scr)

    @pl.when((pl.program_id(0) == 0) & (ci == 0))
    def _():
        rr = _iota((hc, hc), 0)
        cc = _iota((hc, hc), 1)
        same = (rr // c) == (cc // c)
        m_scr[0] = (same & (rr >= cc)).astype(F32)
        m_scr[1] = (same & (rr > cc)).astype(F32)
        for lv in range(n_lvl):
            sz = 1 << lv
            off = ((rr // (2 * sz)) == (cc // (2 * sz))) & (((rr // sz) % 2) == 1) & (((cc // sz) % 2) == 0)
            m_scr[2 + lv] = off.astype(F32)

    ones_bd = _block_ones(GROUP_WIDTH, HEAD_DIM)
    bd_mask = _block_mask(GROUP_WIDTH, HEAD_DIM, HEAD_DIM)
    masks = _head_masks()
    pair_masks = [masks[0:2], masks[2:4]]
    tri = _lower_tri(c)
    neg_a = -jnp.exp(alog_ref[...]) * LOG2E

    def one_sequence(sq):
        blk = blk_ref[sq]
        gz = blk[:, 768:1024]
        qkv = _conv_silu(xe_scr.at[sq], halo_ref[sq][:, 0:768], blk[:, 0:768], cw_ref[...], None, ci == 0, c)
        gq, gk, v = qkv[:, 0:256], qkv[:, 256:512], qkv[:, 512:768]
        ss_qk = _head_sumsq(jnp.concatenate([gq, gk], axis=0), ones_bd)
        q = gq * lax.rsqrt(ss_qk[0:c] + EPS) * (HEAD_DIM ** -0.5)
        k = gk * lax.rsqrt(ss_qk[c:2 * c] + EPS)
        yield
        small = small_ref[sq]
        gb_x, ga_x = _expand_small(small, SMALL_GB, SMALL_GA)
        beta = _sigmoid(gb_x)
        g = _exact_dot_left(tri, neg_a * _softplus(ga_x + dtb_ref[...]))
        g_last = g[c - 1:c, :]
        eg = jnp.exp2(g)
        yield
        da, db = _decay_diff_operands(g)
        bk = beta * k
        bkg = bk * eg
        bv = beta * v
        a_mat, p_mat, x = [], [], []
        for pm in pair_masks:
            diff = _dot_nt(_stack_heads(da, pm).astype(BF16), _stack_heads(db, pm).astype(BF16))
            k_st = _stack_heads(k, pm).astype(BF16)
            kq = _dot_nt(jnp.concatenate([_stack_heads(bk, pm), _stack_heads(q, pm)], axis=0).astype(BF16), k_st)
            kk, qk = kq[0:hc], kq[hc:2 * hc]
            decay = jnp.exp2(jnp.minimum(diff, 0.0))
            a_mat.append(kk * (decay * m_scr[1]))
            p_mat.append((qk * (decay * m_scr[0])).astype(BF16))
            x.append(jnp.concatenate([_stack_heads(bkg, pm), _stack_heads(bv, pm)], axis=1))
        yield

        n_mat = [-(a * m_scr[2]) for a in a_mat]
        for lv in range(1, n_lvl):
            a_off = [a * m_scr[2 + lv] for a in a_mat]
            m = [ao + _dot(ao.astype(BF16), n.astype(BF16)) for ao, n in zip(a_off, n_mat)]
            yield
            n_mat = [n - mm - _dot(n.astype(BF16), mm.astype(BF16)) for n, mm in zip(n_mat, m)]
            yield
        x = [xx + _dot(n.astype(BF16), xx.astype(BF16)) for xx, n in zip(x, n_mat)]
        yield
        w = x[0][0:c, 0:256] + x[0][c:2 * c, 0:256] + x[1][0:c, 0:256] + x[1][c:2 * c, 0:256]
        u0 = x[0][0:c, 256:512] + x[0][c:2 * c, 256:512] + x[1][0:c, 256:512] + x[1][c:2 * c, 256:512]

        s = s_scr[sq]
        s_bf = s.astype(BF16)
        ws_qs = _dot(jnp.concatenate([w, q * eg], axis=0).astype(BF16), s_bf)
        u = u0 - ws_qs[0:c]
        o = ws_qs[c:2 * c]
        yield
        pu = [_dot(pmat, _stack_heads(u, pm).astype(BF16)) for pmat, pm in zip(p_mat, pair_masks)]
        kend = k * jnp.exp2(g_last - g)
        ds = _dot_tn(kend.astype(BF16), u.astype(BF16))
        yield
        for pu_p in pu:
            o = o + pu_p[0:c] + pu_p[c:2 * c]
        s_scr[sq] = jnp.exp2(g_last) * s + jnp.where(bd_mask, ds, 0.0)
        ss = _head_sumsq(o, ones_bd)
        o_ref[sq] = (o * lax.rsqrt(ss * (1.0 / HEAD_DIM) + EPS) * nw_ref[...] * _silu(gz)).astype(BF16)

    _round_robin([one_sequence(sq) for sq in range(nb)])

    @pl.when(ci == n_chunks - 1)
    def _():
        for sq in range(nb):
            st_ref[sq] = _extract_blocks(s_scr[sq], GROUP_WIDTH, HEAD_DIM)


def _gdn_prompt(proj3, conv_w, a_log, dt_bias, norm_w, c):
    b, l, _ = proj3.shape
    n = l // c
    fixed = lambda bi, ci: (0, 0)
    nb = math.gcd(b, GDN_SEQS_PER_STEP)
    hc = 2 * c
    return pl.pallas_call(
        functools.partial(_gdn_prompt_kernel, c=c, n_chunks=n, nb=nb),
        grid=(b // nb, n),
        in_specs=[pl.BlockSpec((nb, c, 1024), lambda bi, ci: (bi, ci, COL_GDN // 1024)),
                  pl.BlockSpec((nb, 8, 1024), lambda bi, ci: (bi, jnp.maximum(ci * (c // 8) - 1, 0), COL_GDN // 1024)),
                  pl.BlockSpec((nb, c, 128), lambda bi, ci: (bi, ci, COL_SMALL // 128)),
                  pl.BlockSpec((CONV_WIDTH, 768), fixed),
                  pl.BlockSpec((1, GROUP_WIDTH), fixed),
                  pl.BlockSpec((1, GROUP_WIDTH), fixed),
                  pl.BlockSpec((1, GROUP_WIDTH), fixed)],
        out_specs=[pl.BlockSpec((nb, c, GROUP_WIDTH), lambda bi, ci: (bi, ci, 0)),
                   pl.BlockSpec((nb, GROUP_WIDTH, HEAD_DIM), lambda bi, ci: (bi, 0, 0))],
        out_shape=[jax.ShapeDtypeStruct((b, l, GROUP_WIDTH), BF16),
                   jax.ShapeDtypeStruct((b, GROUP_WIDTH, HEAD_DIM), F32)],
        scratch_shapes=[pltpu.VMEM((nb, GROUP_WIDTH, GROUP_WIDTH), F32),
                        pltpu.VMEM((nb, c + 8, 768), F32),
                        pltpu.VMEM((2 + int(math.log2(c)), hc, hc), F32)],
        compiler_params=pltpu.CompilerParams(dimension_semantics=("arbitrary", "arbitrary"),
                                             vmem_limit_bytes=VMEM_LIMIT),
        name="gdn_prompt",
    )(proj3, proj3, proj3, conv_w, _lane_rep(a_log), _lane_rep(dt_bias),
      jnp.tile(norm_w.astype(F32), N_HEADS).reshape(1, GROUP_WIDTH))


HGRN_SUB = 16
LOG2E = 1.4426950408889634


def _hgrn_lower_bound(logits, layer):
    rows = [logits[d:d + 1, :] for d in range(DEPTH)]
    mx = functools.reduce(jnp.maximum, rows)
    es = [jnp.exp(x - mx) for x in rows]
    tot = functools.reduce(lambda a, b: a + b, es)
    sm = [e / tot for e in es]
    acc = sm[0]
    for d in range(1, layer + 1):
        acc = acc + sm[d]
    return acc - sm[0]


def _hgrn_prompt_kernel(blk_ref, lb_ref, nw_ref, o_ref, st_ref, s_scr, *, r, n_chunks, layer, nb):
    ci = pl.program_id(1)
    sub = HGRN_SUB
    n_sub = r // sub

    @pl.when(ci == 0)
    def _():
        s_scr[...] = jnp.zeros_like(s_scr)

    lb = _hgrn_lower_bound(lb_ref[...], layer)
    rr = _iota((r, r), 0)
    cc = _iota((r, r), 1)
    same_sub = (rr // sub) == (cc // sub)
    cum_sel = (same_sub & (rr >= cc)).astype(BF16)
    tot_sel = same_sub.astype(BF16)
    ones_bd = _block_ones(GROUP_WIDTH, HEAD_DIM)
    masks = _head_masks()
    half = sub // 2
    i8 = _iota((half, 1), 0)

    def one_sequence(sq):
        blk = blk_ref[sq]
        hq, hf, hi, hg = (blk[:, i * GROUP_WIDTH:(i + 1) * GROUP_WIDTH] for i in range(4))
        f = lb + (1.0 - lb) * _sigmoid(hf)
        q = _sigmoid(hq)
        k = 1.0 - f
        v = hi
        logf = jnp.log(f)
        g = _exact_dot_left(cum_sel, logf)
        g_tot = _exact_dot_left(tot_sel, logf)
        yield
        a2 = (g + jnp.log(q)) * LOG2E
        h2 = (g - jnp.log(k)) * LOG2E
        gt2 = g_tot * LOG2E
        qt = _stack_heads(jnp.exp2(a2), masks).astype(BF16)
        kh = _stack_heads(jnp.exp2(gt2 - h2), masks).astype(BF16)
        v_heads = [v[:, h * HEAD_DIM:(h + 1) * HEAD_DIM].astype(BF16) for h in range(N_HEADS)]

        def sub_rows(x_st, lo):
            return jnp.concatenate([x_st[h * r + lo:h * r + lo + sub] for h in range(N_HEADS)], axis=0)

        s = s_scr[sq]
        outs = []
        for j in range(n_sub):
            lo = j * sub
            v_j = v[lo:lo + sub]
            a_lo, a_hi, h_j = a2[lo:lo + half], a2[lo + half:lo + sub], h2[lo:lo + sub]
            lo_blocks, hi_blocks = [], []
            for jj in range(sub):
                h_row = h_j[jj:jj + 1, :]
                if jj < half:
                    e_lo = jnp.exp2(a_lo - h_row)
                    lo_blocks.append(e_lo if jj == 0 else jnp.where(i8 >= jj, e_lo, 0.0))
                    hi_blocks.append(jnp.exp2(a_hi - h_row))
                else:
                    e_hi = jnp.exp2(a_hi - h_row)
                    hi_blocks.append(e_hi if jj == half else jnp.where(i8 >= jj - half, e_hi, 0.0))
            sc = _dot(jnp.concatenate(lo_blocks + hi_blocks, axis=0).astype(BF16), ones_bd)
            oi = _dot_nt(sub_rows(qt, lo), s.astype(BF16))
            v_rows = jnp.concatenate([vh[lo:lo + sub] for vh in v_heads], axis=0)
            ds = _dot_tn(v_rows, sub_rows(kh, lo))
            yield
            o_inter = jnp.concatenate([oi[h * sub:(h + 1) * sub] for h in range(N_HEADS)], axis=1)
            n_lo = half * half
            o_lo = sc[0:half] * v_j[0:1, :]
            o_hi = sc[n_lo:n_lo + half] * v_j[0:1, :]
            for jj in range(1, sub):
                if jj < half:
                    o_lo = o_lo + sc[jj * half:(jj + 1) * half] * v_j[jj:jj + 1, :]
                o_hi = o_hi + sc[n_lo + jj * half:n_lo + (jj + 1) * half] * v_j[jj:jj + 1, :]
            outs.append(jnp.concatenate([o_lo, o_hi], axis=0) + o_inter)
            s = jnp.exp2(gt2[lo:lo + 1, :]) * s + ds
        s_scr[sq] = s

        o = jnp.concatenate(outs, axis=0)
        ss = _head_sumsq(o, ones_bd)
        yield
        o_ref[sq] = (o * lax.rsqrt(ss * (1.0 / HEAD_DIM) + EPS) * nw_ref[...] * _silu(hg)).astype(BF16)

    _round_robin([one_sequence(sq) for sq in range(nb)])

    @pl.when(ci == n_chunks - 1)
    def _():
        for sq in range(nb):
            st_ref[sq] = s_scr[sq].T


def _hgrn_prompt(proj3, lb_logits, norm_w, layer, r):
    b, l, _ = proj3.shape
    n = l // r
    fixed = lambda bi, ci: (0, 0)
    nb = math.gcd(b, min(PROMPT_SEQS_PER_STEP, max(1, PROMPT_ROWS_PER_STEP // (l // n))))
    return pl.pallas_call(
        functools.partial(_hgrn_prompt_kernel, r=r, n_chunks=n, layer=layer, nb=nb),
        grid=(b // nb, n),
        in_specs=[pl.BlockSpec((nb, r, 1024), lambda bi, ci: (bi, ci, COL_HGRN // 1024)),
                  pl.BlockSpec((DEPTH, GROUP_WIDTH), fixed),
                  pl.BlockSpec((1, GROUP_WIDTH), fixed)],
        out_specs=[pl.BlockSpec((nb, r, GROUP_WIDTH), lambda bi, ci: (bi, ci, 0)),
                   pl.BlockSpec((nb, GROUP_WIDTH, HEAD_DIM), lambda bi, ci: (bi, 0, 0))],
        out_shape=[jax.ShapeDtypeStruct((b, l, GROUP_WIDTH), BF16),
                   jax.ShapeDtypeStruct((b, GROUP_WIDTH, HEAD_DIM), F32)],
        scratch_shapes=[pltpu.VMEM((nb, HEAD_DIM, GROUP_WIDTH), F32)],
        compiler_params=pltpu.CompilerParams(dimension_semantics=("arbitrary", "arbitrary"),
                                             vmem_limit_bytes=VMEM_LIMIT),
        name="hgrn_prompt",
    )(proj3, lb_logits.astype(F32), jnp.tile(norm_w.astype(F32), N_HEADS).reshape(1, GROUP_WIDTH))


DEC_SEQS = 128
DEC_LEN = 4


def _head_rows(h):
    return pl.ds(pl.multiple_of(h * HEAD_DIM, HEAD_DIM), HEAD_DIM)


def _recur_head(load_s, store_s, n_keys, decay_fn, k_fn, q_fn, v_blocks):
    def body(kk, accs):
        s = load_s(kk)
        accs = list(accs)
        for t in range(DEC_LEN):
            s = decay_fn(t, kk) * s + k_fn(t, kk) * v_blocks[t]
            accs[t] = accs[t] + q_fn(t, kk) * s
        store_s(kk, s)
        return tuple(accs)

    zero = jnp.zeros((HEAD_DIM, DEC_SEQS), F32)
    return lax.fori_loop(0, n_keys, body, (zero,) * DEC_LEN)


def _dec_ret_kernel(blk_ref, cos_ref, sin_ref, st_ref, o_ref, so_ref, q_scr, k_scr, v_scr, o_scr):
    h = pl.program_id(0)

    @pl.when(h == 0)
    def _():
        for t in range(DEC_LEN):
            blk = blk_ref[t]
            rq, rk, rv = blk[:, 0:256], blk[:, 256:512], blk[:, 512:768]
            cosv, sinv = cos_ref[t:t + 1, :], sin_ref[t:t + 1, :]
            q_scr[t] = (rq * cosv + _swap_halves(rq) * sinv).T
            k_scr[t] = ((rk * cosv + _swap_halves(rk) * sinv) * (HEAD_DIM ** -0.5)).T
            v_scr[t] = rv.T

    lg = jnp.where(h == 0, LOG_GAMMA[0], jnp.where(h == 1, LOG_GAMMA[1], jnp.where(h == 2, LOG_GAMMA[2], LOG_GAMMA[3])))
    gamma = jnp.exp(jnp.full((1, DEC_SEQS), lg, F32))
    hr = _head_rows(h)
    v_blocks = [v_scr[t, hr, :] for t in range(DEC_LEN)]
    accs = _recur_head(
        lambda kk: st_ref[0, 0, kk], functools.partial(_store_state, so_ref), HEAD_DIM,
        lambda t, kk: gamma,
        lambda t, kk: k_scr[t, pl.ds(h * HEAD_DIM + kk, 1), :],
        lambda t, kk: q_scr[t, pl.ds(h * HEAD_DIM + kk, 1), :],
        v_blocks)
    for t in range(DEC_LEN):
        o_scr[t, hr, :] = accs[t]

    @pl.when(h == N_HEADS - 1)
    def _():
        ones_bd = _block_ones(GROUP_WIDTH, HEAD_DIM)
        for t in range(DEC_LEN):
            o = o_scr[t].T
            ss = _head_sumsq(o, ones_bd)
            o_ref[t] = (o * lax.rsqrt(ss * (1.0 / HEAD_DIM) + EPS) * _silu(blk_ref[t][:, 768:1024])).astype(BF16)


def _store_state(so_ref, kk, s):
    so_ref[0, 0, kk] = s


def _without_ref(kernel_fn, idx):
    def wrapped(*refs):
        return kernel_fn(*refs[:idx], *refs[idx + 1:])
    return wrapped


def _zero_later_layers(kernel_fn, so_index):
    def wrapped(*refs):
        kernel_fn(*refs)
        so_ref = refs[so_index]
        so_ref[1:] = jnp.zeros((so_ref.shape[0] - 1,) + so_ref.shape[1:], F32)
    return wrapped


def _dec_call(kernel_fn, name, col, ins, in_specs, n_tok_scr, state_view, layer, carried, extra_scratch=()):
    blk_spec = pl.BlockSpec((DEC_LEN, DEC_SEQS, 1024), lambda h: (0, 0, col // 1024))
    st_spec = pl.BlockSpec((1, 1) + state_view.shape[2:], lambda h: (layer, h, 0, 0, 0))
    tok_scr = pltpu.VMEM((DEC_LEN, GROUP_WIDTH, DEC_SEQS), F32)
    ins = tuple(ins) + (state_view,)
    specs = [blk_spec] + in_specs + [st_spec]
    if carried is None:
        assert layer == 0
        so_spec = pl.BlockSpec((DEPTH, 1) + state_view.shape[2:], lambda h: (0, h, 0, 0, 0))
        kernel_fn = _zero_later_layers(kernel_fn, len(ins) + 1)
        aliases = {}
    else:
        so_spec = st_spec
        kernel_fn = _without_ref(kernel_fn, len(ins))
        aliases = {len(ins): 1}
        ins = ins + (carried,)
        specs = specs + [pl.BlockSpec(memory_space=pl.ANY)]
    return pl.pallas_call(
        kernel_fn,
        grid=(N_HEADS,),
        in_specs=specs,
        out_specs=[pl.BlockSpec((DEC_LEN, DEC_SEQS, GROUP_WIDTH), lambda h: (0, 0, 0)), so_spec],
        out_shape=[jax.ShapeDtypeStruct((DEC_LEN, DEC_SEQS, GROUP_WIDTH), BF16),
                   jax.ShapeDtypeStruct(state_view.shape, F32)],
        scratch_shapes=[tok_scr] * n_tok_scr + list(extra_scratch),
        input_output_aliases=aliases,
        compiler_params=pltpu.CompilerParams(dimension_semantics=("arbitrary",), vmem_limit_bytes=VMEM_LIMIT),
        name=name,
    )(*ins)


def _fixed1(shape):
    return pl.BlockSpec(shape, lambda h: (0,) * len(shape))


def _dec_ret(projd, cos_t, sin_t, state_view, layer, carried):
    return _dec_call(_dec_ret_kernel, "retention_decode", COL_RET, (projd, cos_t, sin_t),
                     [_fixed1((DEC_LEN, GROUP_WIDTH)), _fixed1((DEC_LEN, GROUP_WIDTH))], 4,
                     state_view, layer, carried)


def _dec_hgrn_kernel(blk_ref, lb_ref, nw_ref, st_ref, o_ref, so_ref, q_scr, k_scr, v_scr, f_scr, o_scr, *, layer):
    h = pl.program_id(0)

    @pl.when(h == 0)
    def _():
        lb = _hgrn_lower_bound(lb_ref[...], layer)
        for t in range(DEC_LEN):
            blk = blk_ref[t]
            f = lb + (1.0 - lb) * _sigmoid(blk[:, 256:512])
            q_scr[t] = _sigmoid(blk[:, 0:256]).T
            k_scr[t] = (1.0 - f).T
            v_scr[t] = blk[:, 512:768].T
            f_scr[t] = f.T

    hr = _head_rows(h)
    v_blocks = [v_scr[t, hr, :] for t in range(DEC_LEN)]
    row = lambda scr: (lambda t, kk: scr[t, pl.ds(h * HEAD_DIM + kk, 1), :])
    accs = _recur_head(lambda kk: st_ref[0, 0, kk], functools.partial(_store_state, so_ref), HEAD_DIM,
                       row(f_scr), row(k_scr), row(q_scr), v_blocks)
    for t in range(DEC_LEN):
        o_scr[t, hr, :] = accs[t]

    @pl.when(h == N_HEADS - 1)
    def _():
        ones_bd = _block_ones(GROUP_WIDTH, HEAD_DIM)
        for t in range(DEC_LEN):
            o = o_scr[t].T
            ss = _head_sumsq(o, ones_bd)
            o_ref[t] = (o * lax.rsqrt(ss * (1.0 / HEAD_DIM) + EPS) * nw_ref[...]
                        * _silu(blk_ref[t][:, 768:1024])).astype(BF16)


def _dec_hgrn(projd, lb_logits, norm_w, state_view, layer, carried):
    return _dec_call(functools.partial(_dec_hgrn_kernel, layer=layer), "hgrn_decode", COL_HGRN,
                     (projd, lb_logits.astype(F32), jnp.tile(norm_w.astype(F32), N_HEADS).reshape(1, GROUP_WIDTH)),
                     [_fixed1((DEPTH, GROUP_WIDTH)), _fixed1((1, GROUP_WIDTH))], 5, state_view, layer, carried)


def _hist_spec(layer):
    return pl.BlockSpec((1, CONV_WIDTH - 1, DEC_SEQS, 768), lambda h: (layer, 0, 0, 0))


def _dec_conv_silu(hist_ref, xs, w, bias):
    xe = [hist_ref[0, j] for j in range(CONV_WIDTH - 1)] + xs
    out = []
    for t in range(DEC_LEN):
        y = xe[t] * w[0:1, :]
        for j in range(1, CONV_WIDTH):
            y = y + xe[t + j] * w[j:j + 1, :]
        if bias is not None:
            y = y + bias
        out.append(_silu(y))
    return out


def _dec_ssd_kernel(blk_ref, small_ref, hist_ref, cw_ref, cb_ref, dtb_ref, alog_ref, dskip_ref, nw_ref, st_ref,
                    o_ref, so_ref, c_scr, b_scr, v_scr, a_scr, o_scr, x_scr):
    h = pl.program_id(0)

    @pl.when(h == 0)
    def _():
        xbc = _dec_conv_silu(hist_ref, [blk_ref[t][:, 256:1024] for t in range(DEC_LEN)], cw_ref[...], cb_ref[...])
        for t in range(DEC_LEN):
            xs = xbc[t][:, 0:256]
            dt = _softplus(_expand_small(small_ref[t], SMALL_SDT) + dtb_ref[...])
            x_scr[t] = xs
            v_scr[t] = (xs * dt).T
            b_scr[t] = xbc[t][:, 256:512].T
            c_scr[t] = xbc[t][:, 512:768].T
            a_scr[t] = jnp.exp(-jnp.exp(alog_ref[...]) * dt).T

    hr = _head_rows(h)
    g0 = (h // 2) * SSD_STATE
    v_blocks = [v_scr[t, hr, :] for t in range(DEC_LEN)]
    accs = _recur_head(
        lambda kk: st_ref[0, 0, kk], functools.partial(_store_state, so_ref), SSD_STATE,
        lambda t, kk: a_scr[t, pl.ds(h * HEAD_DIM, 1), :],
        lambda t, kk: b_scr[t, pl.ds(g0 + kk, 1), :],
        lambda t, kk: c_scr[t, pl.ds(g0 + kk, 1), :],
        v_blocks)
    for t in range(DEC_LEN):
        o_scr[t, hr, :] = accs[t]

    @pl.when(h == N_HEADS - 1)
    def _():
        for t in range(DEC_LEN):
            y = (o_scr[t].T + dskip_ref[...] * x_scr[t]) * _silu(blk_ref[t][:, 0:256])
            halves = [_rms_rows(y[:, gi * 128:(gi + 1) * 128]) for gi in range(2)]
            o_ref[t] = (jnp.concatenate(halves, axis=1) * nw_ref[...]).astype(BF16)


def _dec_ssd(projd, hist, conv_w, conv_b, dt_bias, a_log, d_skip, norm_w, state_view, layer, carried):
    small_spec = pl.BlockSpec((DEC_LEN, DEC_SEQS, 128), lambda h: (0, 0, COL_SMALL // 128))
    return _dec_call(_dec_ssd_kernel, "ssd_decode", COL_SSD,
                     (projd, projd, hist, conv_w, conv_b.reshape(1, 768), _lane_rep(dt_bias), _lane_rep(a_log),
                      _lane_rep(d_skip), norm_w.reshape(1, GROUP_WIDTH)),
                     [small_spec, _hist_spec(layer), _fixed1((CONV_WIDTH, 768)),
                      _fixed1((1, 768))] + [_fixed1((1, GROUP_WIDTH))] * 4, 5, state_view, layer, carried,
                     extra_scratch=[pltpu.VMEM((DEC_LEN, DEC_SEQS, GROUP_WIDTH), F32)])


def _dec_gdn_kernel(blk_ref, small_ref, hist_ref, cw_ref, alog_ref, dtb_ref, nw_ref, st_ref,
                    o_ref, so_ref, q_scr, k_scr, v_scr, a_scr, b_scr, o_scr):
    h = pl.program_id(0)

    @pl.when(h == 0)
    def _():
        ones_bd = _block_ones(GROUP_WIDTH, HEAD_DIM)
        qkv = _dec_conv_silu(hist_ref, [blk_ref[t][:, 0:768] for t in range(DEC_LEN)], cw_ref[...], None)
        for t in range(DEC_LEN):
            gq, gk, gv = qkv[t][:, 0:256], qkv[t][:, 256:512], qkv[t][:, 512:768]
            q_scr[t] = (gq * lax.rsqrt(_head_sumsq(gq, ones_bd) + EPS) * (HEAD_DIM ** -0.5)).T
            k_scr[t] = (gk * lax.rsqrt(_head_sumsq(gk, ones_bd) + EPS)).T
            v_scr[t] = gv.T
            small = small_ref[t]
            gb_x, ga_x = _expand_small(small, SMALL_GB, SMALL_GA)
            b_scr[t] = _sigmoid(gb_x).T
            la = -jnp.exp(alog_ref[...]) * _softplus(ga_x + dtb_ref[...])
            a_scr[t] = jnp.exp(la).T

    hr = _head_rows(h)
    one_row = pl.ds(h * HEAD_DIM, 1)
    zero = jnp.zeros((HEAD_DIM, DEC_SEQS), F32)
    for t in range(DEC_LEN):
        a = a_scr[t, one_row, :]
        cur = st_ref if t == 0 else so_ref

        def kts(kk, r):
            return r + k_scr[t, pl.ds(h * HEAD_DIM + kk, 1), :] * cur[0, 0, kk]

        r = lax.fori_loop(0, HEAD_DIM, kts, zero)
        u = b_scr[t, one_row, :] * (v_scr[t, hr, :] - a * r)

        def upd(kk, acc):
            s = a * cur[0, 0, kk] + k_scr[t, pl.ds(h * HEAD_DIM + kk, 1), :] * u
            so_ref[0, 0, kk] = s
            return acc + q_scr[t, pl.ds(h * HEAD_DIM + kk, 1), :] * s

        o_scr[t, hr, :] = lax.fori_loop(0, HEAD_DIM, upd, zero)

    @pl.when(h == N_HEADS - 1)
    def _():
        ones_bd = _block_ones(GROUP_WIDTH, HEAD_DIM)
        for t in range(DEC_LEN):
            o = o_scr[t].T
            ss = _head_sumsq(o, ones_bd)
            o_ref[t] = (o * lax.rsqrt(ss * (1.0 / HEAD_DIM) + EPS) * nw_ref[...]
                        * _silu(blk_ref[t][:, 768:1024])).astype(BF16)


def _dec_gdn(projd, hist, conv_w, a_log, dt_bias, norm_w, state_view, layer, carried):
    small_spec = pl.BlockSpec((DEC_LEN, DEC_SEQS, 128), lambda h: (0, 0, COL_SMALL // 128))
    return _dec_call(_dec_gdn_kernel, "gdn_decode", COL_GDN,
                     (projd, projd, hist, conv_w, _lane_rep(a_log), _lane_rep(dt_bias),
                      jnp.tile(norm_w.astype(F32), N_HEADS).reshape(1, GROUP_WIDTH)),
                     [small_spec, _hist_spec(layer), _fixed1((CONV_WIDTH, 768))]
                     + [_fixed1((1, GROUP_WIDTH))] * 3, 6, state_view, layer, carried)


W_PREP_ROWS = 128


def _w_in_t_prep_kernel(a_ref, b_ref, o_ref):
    j = pl.program_id(0)
    n_plain = COL_RET // W_PREP_ROWS
    n_main = COL_SMALL // W_PREP_ROWS
    row = _iota((W_PREP_ROWS, 1), 0)
    for l in range(DEPTH):
        a = a_ref[:, l, :]
        b = b_ref[:, l, :]
        shifted = jnp.concatenate([a[8:], b[:8]], axis=0)
        small = jnp.where(row < 8, a, jnp.where(row < 12, b, 0.0))
        out = jnp.where(j < n_plain, a, jnp.where(j < n_main, shifted, small))
        o_ref[l] = out.astype(BF16)


def _prep_w_in_t(w_in):
    wt = jnp.transpose(w_in, (2, 0, 1))
    n_plain = COL_RET // W_PREP_ROWS
    n_main = COL_SMALL // W_PREP_ROWS

    def a_idx(j):
        return (jnp.where(j < n_main, j, n_plain), 0, 0)

    def b_idx(j):
        return (jnp.where(j < n_plain, j, jnp.minimum(j + 1, n_main)), 0, 0)

    blk = (W_PREP_ROWS, DEPTH, D_MODEL)
    return pl.pallas_call(
        _w_in_t_prep_kernel,
        grid=(P_PAD // W_PREP_ROWS,),
        in_specs=[pl.BlockSpec(blk, a_idx), pl.BlockSpec(blk, b_idx)],
        out_specs=pl.BlockSpec((DEPTH, W_PREP_ROWS, D_MODEL), lambda j: (0, j, 0)),
        out_shape=jax.ShapeDtypeStruct((DEPTH, P_PAD, D_MODEL), BF16),
        compiler_params=pltpu.CompilerParams(dimension_semantics=("arbitrary",), vmem_limit_bytes=VMEM_LIMIT),
        name="w_in_prep",
    )(wt, wt)


def _rotary_tables(pos):
    half = HEAD_DIM // 2
    inv_freq = RET_THETA ** (-jnp.arange(half, dtype=F32) / half)
    ang = pos.astype(F32)[:, None] * inv_freq[None, :]
    cos, sin = jnp.cos(ang), jnp.sin(ang)
    cos_t = jnp.tile(cos, (1, 2 * N_HEADS))
    sin_t = jnp.tile(jnp.concatenate([-sin, sin], axis=1), (1, N_HEADS))
    return cos_t, sin_t


RET_CHUNK = 256
SSD_CHUNK = 256
GDN_CHUNK = 64
HGRN_ROWS = 128


def _forward(x_prompt, x_sample, states, p, past_len):
    st_hg, st_gd, st_gc, st_rt, st_sd, st_sc = states
    bp, lp, _ = x_prompt.shape
    nd, ld, _ = x_sample.shape
    xp = x_prompt.astype(F32).reshape(bp * lp, D_MODEL)
    xd = jnp.transpose(x_sample.astype(F32), (1, 0, 2)).reshape(ld * nd, D_MODEL)
    cos_p, sin_p = _rotary_tables(jnp.arange(lp))
    cos_d, sin_d = _rotary_tables(past_len + jnp.arange(ld))
    outs = {k: [] for k in ("hp", "gp", "gcp", "gcs", "rp", "sp", "scp", "scs")}
    w_in_all = _prep_w_in_t(p["w_in"].astype(F32))
    wo, wu, wd = (p[k].astype(BF16) for k in ("w_out", "w_up", "w_down"))
    norm_mix = p["norm_mix"].astype(F32).reshape(DEPTH, 1, D_MODEL)
    norm_ffn = p["norm_ffn"].astype(F32).reshape(DEPTH, 1, D_MODEL)
    sv_hg, sv_gd, sv_rt, sv_sd = (jnp.transpose(s.astype(F32), (0, 2, 3, 4, 1)) for s in (st_hg, st_gd, st_rt, st_sd))
    hist_g = jnp.transpose(st_gc.astype(F32), (0, 2, 1, 3))
    hist_s = jnp.transpose(st_sc.astype(F32), (0, 2, 1, 3))
    new_hg = new_gd = new_rt = new_sd = None
    for l in range(DEPTH):
        pp = _proj(xp, norm_mix, w_in_all, l).reshape(bp, lp, P_PAD)
        pd = _proj(xd, norm_mix, w_in_all, l).reshape(ld, nd, P_PAD)

        oa, sa = _hgrn_prompt(pp, p["hgrn_lb_logits"], p["hgrn_norm"][l], l, HGRN_ROWS)
        ob, sb = _gdn_prompt(pp, p["gdn_conv_w"][l], p["gdn_a_log"][l], p["gdn_dt_bias"][l], p["gdn_norm"][l],
                             GDN_CHUNK)
        oc, sc = _ret_prompt(pp, cos_p, sin_p, RET_CHUNK)
        od, sd = _ssd_prompt(pp, p["ssd_conv_w"][l], p["ssd_conv_b"][l], p["ssd_dt_bias"][l], p["ssd_a_log"][l],
                             p["ssd_d"][l], p["ssd_norm"][l], SSD_CHUNK)
        outs["hp"].append(sa.reshape(bp, N_HEADS, HEAD_DIM, HEAD_DIM))
        outs["gp"].append(sb.reshape(bp, N_HEADS, HEAD_DIM, HEAD_DIM))
        outs["rp"].append(sc.reshape(bp, N_HEADS, HEAD_DIM, HEAD_DIM))
        outs["sp"].append(sd.reshape(bp, N_HEADS, SSD_STATE, HEAD_DIM))
        outs["gcp"].append(pp[:, lp - 3:, COL_GDN:COL_GDN + 768])
        outs["scp"].append(pp[:, lp - 3:, COL_SSD + 256:COL_SSD + 1024])
        xp = _out_ffn(xp, [o.reshape(bp * lp, GROUP_WIDTH) for o in (oa, ob, oc, od)], wo, norm_ffn, wu, wd,
                      p["norm_final"], l)

        da, new_hg = _dec_hgrn(pd, p["hgrn_lb_logits"], p["hgrn_norm"][l], sv_hg, l, new_hg)
        db, new_gd = _dec_gdn(pd, hist_g, p["gdn_conv_w"][l], p["gdn_a_log"][l], p["gdn_dt_bias"][l],
                              p["gdn_norm"][l], sv_gd, l, new_gd)
        dc, new_rt = _dec_ret(pd, cos_d, sin_d, sv_rt, l, new_rt)
        dd, new_sd = _dec_ssd(pd, hist_s, p["ssd_conv_w"][l], p["ssd_conv_b"][l], p["ssd_dt_bias"][l],
                              p["ssd_a_log"][l], p["ssd_d"][l], p["ssd_norm"][l], sv_sd, l, new_sd)
        outs["gcs"].append(jnp.transpose(pd[ld - 3:, :, COL_GDN:COL_GDN + 768], (1, 0, 2)))
        outs["scs"].append(jnp.transpose(pd[ld - 3:, :, COL_SSD + 256:COL_SSD + 1024], (1, 0, 2)))
        xd = _out_ffn(xd, [o.reshape(ld * nd, GROUP_WIDTH) for o in (da, db, dc, dd)], wo, norm_ffn, wu, wd,
                      p["norm_final"], l)

    y_prompt = xp.reshape(bp, lp, D_MODEL)
    y_sample = jnp.transpose(xd.reshape(ld, nd, D_MODEL), (1, 0, 2))
    st = {k: jnp.stack(v) for k, v in outs.items()}
    hs, gs, rs, ss = (jnp.transpose(s, (0, 4, 1, 2, 3)) for s in (new_hg, new_gd, new_rt, new_sd))
    return (y_prompt, y_sample, st["hp"], hs, st["gp"], gs, st["gcp"], st["gcs"],
            st["rp"], rs, st["sp"], ss, st["scp"], st["scs"])


def kernel(x_prompt, x_sample, state_hgrn, state_gdn, state_gdn_conv, state_ret, state_ssd, state_ssd_conv,
           norm_mix, w_in, hgrn_lb_logits, hgrn_norm, gdn_conv_w, gdn_a_log, gdn_dt_bias, gdn_norm,
           ssd_conv_w, ssd_conv_b, ssd_dt_bias, ssd_a_log, ssd_d, ssd_norm,
           w_out, norm_ffn, w_up, w_down, norm_final):
    params = dict(norm_mix=norm_mix, w_in=w_in, hgrn_lb_logits=hgrn_lb_logits, hgrn_norm=hgrn_norm,
                  gdn_conv_w=gdn_conv_w, gdn_a_log=gdn_a_log, gdn_dt_bias=gdn_dt_bias, gdn_norm=gdn_norm,
                  ssd_conv_w=ssd_conv_w, ssd_conv_b=ssd_conv_b, ssd_dt_bias=ssd_dt_bias, ssd_a_log=ssd_a_log,
                  ssd_d=ssd_d, ssd_norm=ssd_norm, w_out=w_out, norm_ffn=norm_ffn, w_up=w_up,
                  w_down=w_down, norm_final=norm_final)
    states = (state_hgrn, state_gdn, state_gdn_conv, state_ret, state_ssd, state_ssd_conv)
    return _forward(x_prompt, x_sample, states, params, 16384)
```

```python
import functools
import math

import numpy as np
import jax
import jax.numpy as jnp
from jax import lax
from jax.experimental import pallas as pl
from jax.experimental.pallas import tpu as pltpu

F32 = jnp.float32
BF16 = jnp.bfloat16

D_MODEL = 1024
GROUP_WIDTH = 256
HEAD_DIM = 64
N_HEADS = 4
CONV_WIDTH = 4
SSD_STATE = 128
D_FF = 4096
RET_THETA = 10000.0
EPS = 1e-6
DEPTH = 2

COL_HGRN = 0
COL_GDN = 1024
COL_RET = 2048
COL_SSD = 3072
COL_SMALL = 4096
P_PAD = 4224
SMALL_GA, SMALL_GB, SMALL_SDT = 0, 4, 8

VMEM_LIMIT = 56 * 1024 * 1024
LOG_GAMMA = [math.log(1.0 - 2.0 ** (-5.0 - h)) for h in range(N_HEADS)]


def _dot(a, b):
    return jnp.dot(a, b, preferred_element_type=F32)


def _dot_nt(a, b):
    return lax.dot_general(a, b, (((1,), (1,)), ((), ())), preferred_element_type=F32)


def _dot_tn(a, b):
    return lax.dot_general(a, b, (((0,), (0,)), ((), ())), preferred_element_type=F32)


def _round_robin(gens):
    live = list(gens)
    while live:
        nxt = []
        for g in live:
            try:
                next(g)
                nxt.append(g)
            except StopIteration:
                pass
        live = nxt


def _split3(x):
    hi = x.astype(BF16)
    r1 = x - hi.astype(F32)
    mid = r1.astype(BF16)
    lo = (r1 - mid.astype(F32)).astype(BF16)
    return hi, mid, lo


def _exact_dot(x, sel):
    hi, mid, lo = _split3(x)
    return _dot(hi, sel) + _dot(mid, sel) + _dot(lo, sel)


def _exact_dot_left(sel, x):
    hi, mid, lo = _split3(x)
    return _dot(sel, hi) + _dot(sel, mid) + _dot(sel, lo)


def _iota(shape, dim):
    return lax.broadcasted_iota(jnp.int32, shape, dim)


def _head_of_lane(n_lanes, width=HEAD_DIM):
    return _iota((1, n_lanes), 1) // width


def _head_masks(n_lanes=GROUP_WIDTH, width=HEAD_DIM):
    hl = _head_of_lane(n_lanes, width)
    return [hl == h for h in range(n_lanes // width)]


def _stack_heads(x, masks):
    return jnp.concatenate([jnp.where(m, x, jnp.zeros_like(x)) for m in masks], axis=0)


def _unstack_heads(y, masks, c):
    out = jnp.where(masks[0], y[0:c], 0.0)
    for h in range(1, len(masks)):
        out = out + jnp.where(masks[h], y[h * c:(h + 1) * c], 0.0)
    return out


def _block_ones(n, width, dtype=BF16):
    r = _iota((n, n), 0) // width
    c = _iota((n, n), 1) // width
    return (r == c).astype(dtype)


def _block_mask(n, rwidth, cwidth):
    return (_iota((n, n), 0) // rwidth) == (_iota((n, n), 1) // cwidth)


def _lower_tri(c, dtype=BF16):
    return (_iota((c, c), 0) >= _iota((c, c), 1)).astype(dtype)


def _sigmoid(x):
    return 1.0 / (1.0 + jnp.exp(-x))


def _silu(x):
    return x * _sigmoid(x)


def _softplus(x):
    return jnp.maximum(x, 0.0) + jnp.log(1.0 + jnp.exp(-jnp.abs(x)))


def _rms_rows(x):
    return x * lax.rsqrt(jnp.mean(x * x, axis=-1, keepdims=True) + EPS)


def _head_sumsq(x, ones_bd):
    sq = x * x
    hi = sq.astype(BF16)
    lo = (sq - hi.astype(F32)).astype(BF16)
    return _dot(hi, ones_bd) + _dot(lo, ones_bd)


def _expand_small(small, *first_lanes):
    r = _iota((128, GROUP_WIDTH), 0)
    c = _iota((128, GROUP_WIDTH), 1) // HEAD_DIM
    sel = jnp.concatenate([(r == c + fl).astype(BF16) for fl in first_lanes], axis=1)
    out = _exact_dot(small, sel)
    res = [out[:, i * GROUP_WIDTH:(i + 1) * GROUP_WIDTH] for i in range(len(first_lanes))]
    return res[0] if len(res) == 1 else res


def _decay_diff_operands(g):
    hi, mid, lo = (x.astype(F32) for x in _split3(g))
    pos = _iota(g.shape, 1) % HEAD_DIM
    a = jnp.where(pos == 0, hi, jnp.where(pos == 1, mid, jnp.where(pos == 2, lo,
                  jnp.where(pos < 6, 1.0, 0.0))))
    b = jnp.where(pos < 3, 1.0, jnp.where(pos == 3, -hi, jnp.where(pos == 4, -mid,
                  jnp.where(pos == 5, -lo, 0.0))))
    return a, b


def _extract_blocks(s_wide, rows, width):
    sel = ((_iota((GROUP_WIDTH, width), 0) % width) == _iota((GROUP_WIDTH, width), 1)).astype(BF16)
    return _exact_dot(s_wide, sel)


def _proj_kernel(x_ref, nw_ref, w_ref, o_ref):
    h = _rms_rows(x_ref[...]) * nw_ref[0]
    o_ref[...] = _dot_nt(h.astype(BF16), w_ref[0])


PROJ_ROWS = 512


def _proj(x2d, norm_w, w_bf16, layer):
    t = x2d.shape[0]
    tm = min(t, PROJ_ROWS)
    return pl.pallas_call(
        _proj_kernel,
        grid=(t // tm,),
        in_specs=[pl.BlockSpec((tm, D_MODEL), lambda i: (i, 0)),
                  pl.BlockSpec((1, 1, D_MODEL), lambda i: (layer, 0, 0)),
                  pl.BlockSpec((1, P_PAD, D_MODEL), lambda i: (layer, 0, 0))],
        out_specs=pl.BlockSpec((tm, P_PAD), lambda i: (i, 0)),
        out_shape=jax.ShapeDtypeStruct((t, P_PAD), F32),
        compiler_params=pltpu.CompilerParams(dimension_semantics=("arbitrary",),
                                             vmem_limit_bytes=VMEM_LIMIT),
        name="norm_in_proj",
    )(x2d, norm_w, w_bf16)


def _ffn_kernel(x_ref, oa_ref, ob_ref, oc_ref, od_ref, wo_ref, nf_ref, wu_ref, wd_ref, nfin_ref,
                o_ref, *, final):
    mix = jnp.concatenate([oa_ref[...], ob_ref[...], oc_ref[...], od_ref[...]], axis=1)
    x = x_ref[...] + _dot(mix, wo_ref[0])
    h = (_rms_rows(x) * nf_ref[0]).astype(BF16)
    acc = x
    ft = 1024
    for t in range(D_FF // ft):
        up = _dot(h, wu_ref[0, :, t * ft:(t + 1) * ft])
        up = jnp.square(jnp.maximum(up, 0.0)).astype(BF16)
        acc = acc + _dot(up, wd_ref[0, t * ft:(t + 1) * ft, :])
    if final:
        acc = _rms_rows(acc) * nfin_ref[...]
    o_ref[...] = acc


FFN_ROWS = 512


def _out_ffn(x2d, mixes, wo, nf, wu, wd, nfin, layer):
    t = x2d.shape[0]
    tm = min(t, FFN_ROWS)
    row = lambda i: (i, 0)
    lay = lambda i: (layer, 0, 0)
    return pl.pallas_call(
        functools.partial(_ffn_kernel, final=(layer == DEPTH - 1)),
        grid=(t // tm,),
        in_specs=[pl.BlockSpec((tm, D_MODEL), row)]
                 + [pl.BlockSpec((tm, GROUP_WIDTH), row)] * 4
                 + [pl.BlockSpec((1, D_MODEL, D_MODEL), lay),
                    pl.BlockSpec((1, 1, D_MODEL), lay),
                    pl.BlockSpec((1, D_MODEL, D_FF), lay),
                    pl.BlockSpec((1, D_FF, D_MODEL), lay),
                    pl.BlockSpec((1, D_MODEL), lambda i: (0, 0))],
        out_specs=pl.BlockSpec((tm, D_MODEL), row),
        out_shape=jax.ShapeDtypeStruct((t, D_MODEL), F32),
        compiler_params=pltpu.CompilerParams(dimension_semantics=("arbitrary",),
                                             vmem_limit_bytes=VMEM_LIMIT),
        name="out_proj_ffn",
    )(x2d, *mixes, wo, nf, wu, wd, nfin.reshape(1, D_MODEL))


def _swap_halves(x):
    first = (_iota((1, 128), 1) % HEAD_DIM) < (HEAD_DIM // 2)
    parts = []
    for p in range(GROUP_WIDTH // 128):
        xp = x[:, p * 128:(p + 1) * 128]
        parts.append(jnp.where(first, pltpu.roll(xp, 96, 1), pltpu.roll(xp, 32, 1)))
    return jnp.concatenate(parts, axis=1)


def _conv_silu(xe_ref, halo, x, w, bias, first_chunk, c):
    xe_ref[0:8, :] = jnp.where(first_chunk, jnp.zeros_like(halo), halo)
    xe_ref[8:, :] = x
    y = w[3:4, :] * x
    for j in range(CONV_WIDTH - 1):
        y = y + w[j:j + 1, :] * xe_ref[5 + j:5 + j + c, :]
    if bias is not None:
        y = y + bias
    return _silu(y)


def _ret_prompt_kernel(blk_ref, cos_ref, sin_ref, o_ref, st_ref, s_scr, *, c, n_chunks, nb):
    ci = pl.program_id(1)

    @pl.when(ci == 0)
    def _():
        s_scr[...] = jnp.zeros_like(s_scr)

    cosv, sinv = cos_ref[...], sin_ref[...]
    masks = _head_masks()
    hl = _head_of_lane(GROUP_WIDTH)
    lg = jnp.full((1, GROUP_WIDTH), LOG_GAMMA[0], F32)
    for h in range(1, N_HEADS):
        lg = jnp.where(hl == h, LOG_GAMMA[h], lg)
    ri = _iota((c, 1), 0).astype(F32)
    dij = (_iota((c, c), 0) - _iota((c, c), 1)).astype(F32)
    causal = dij >= 0.0
    decay = jnp.concatenate(
        [jnp.where(causal, jnp.exp(jnp.maximum(dij, 0.0) * LOG_GAMMA[h]), 0.0) for h in range(N_HEADS)],
        axis=0)
    q_scale = jnp.exp((ri + 1.0) * lg)
    k_scale = jnp.exp((float(c - 1) - ri) * lg) * (HEAD_DIM ** -0.5)
    s_scale = jnp.exp(float(c) * lg)
    bd_mask = _block_mask(GROUP_WIDTH, HEAD_DIM, HEAD_DIM)
    ones_bd = _block_ones(GROUP_WIDTH, HEAD_DIM)

    def one_sequence(sq):
        blk = blk_ref[sq]
        rq, rk, rv, rg = (blk[:, i * GROUP_WIDTH:(i + 1) * GROUP_WIDTH] for i in range(4))
        q = rq * cosv + _swap_halves(rq) * sinv
        k = rk * cosv + _swap_halves(rk) * sinv
        v = rv.astype(BF16)
        s = s_scr[sq]
        qk = _dot_nt(_stack_heads(q, masks).astype(BF16), k.astype(BF16))
        o_inter = _dot((q * q_scale).astype(BF16), s.astype(BF16))
        ds = _dot_tn((k * k_scale).astype(BF16), v)
        yield
        scores = qk * (decay * (HEAD_DIM ** -0.5))
        pv = _dot(scores.astype(BF16), v)
        s_scr[sq] = s_scale * s + jnp.where(bd_mask, ds, 0.0)
        yield
        o = _unstack_heads(pv, masks, c) + o_inter
        ss = _head_sumsq(o, ones_bd)
        yield
        o_ref[sq] = (o * lax.rsqrt(ss * (1.0 / HEAD_DIM) + EPS) * _silu(rg)).astype(BF16)

    _round_robin([one_sequence(sq) for sq in range(nb)])

    @pl.when(ci == n_chunks - 1)
    def _():
        for sq in range(nb):
            st_ref[sq] = _extract_blocks(s_scr[sq], GROUP_WIDTH, HEAD_DIM)


PROMPT_SEQS_PER_STEP = 8
GDN_SEQS_PER_STEP = 8
PROMPT_ROWS_PER_STEP = 2048


def _ret_prompt(proj3, cos_t, sin_t, c):
    b, l, _ = proj3.shape
    n = l // c
    nb = math.gcd(b, min(PROMPT_SEQS_PER_STEP, max(1, PROMPT_ROWS_PER_STEP // (l // n))))
    return pl.pallas_call(
        functools.partial(_ret_prompt_kernel, c=c, n_chunks=n, nb=nb),
        grid=(b // nb, n),
        in_specs=[pl.BlockSpec((nb, c, 1024), lambda bi, ci: (bi, ci, COL_RET // 1024)),
                  pl.BlockSpec((c, GROUP_WIDTH), lambda bi, ci: (ci, 0)),
                  pl.BlockSpec((c, GROUP_WIDTH), lambda bi, ci: (ci, 0))],
        out_specs=[pl.BlockSpec((nb, c, GROUP_WIDTH), lambda bi, ci: (bi, ci, 0)),
                   pl.BlockSpec((nb, GROUP_WIDTH, HEAD_DIM), lambda bi, ci: (bi, 0, 0))],
        out_shape=[jax.ShapeDtypeStruct((b, l, GROUP_WIDTH), BF16),
                   jax.ShapeDtypeStruct((b, GROUP_WIDTH, HEAD_DIM), F32)],
        scratch_shapes=[pltpu.VMEM((nb, GROUP_WIDTH, GROUP_WIDTH), F32)],
        compiler_params=pltpu.CompilerParams(dimension_semantics=("arbitrary", "arbitrary"),
                                             vmem_limit_bytes=VMEM_LIMIT),
        name="retention_prompt",
    )(proj3, cos_t, sin_t)


def _ssd_prompt_kernel(blk_ref, halo_ref, small_ref, cw_ref, cb_ref, dtb_ref, alog_ref, dskip_ref, nw_ref,
                       o_ref, st_ref, s_scr, xe_scr, *, c, n_chunks, nb):
    ci = pl.program_id(1)

    @pl.when(ci == 0)
    def _():
        s_scr[...] = jnp.zeros_like(s_scr)

    masks = _head_masks()
    causal = _iota((c, c), 0) >= _iota((c, c), 1)
    causal4 = jnp.concatenate([causal] * N_HEADS, axis=0)
    group_mask = _block_mask(GROUP_WIDTH, 128, 128)
    tri = _lower_tri(c)
    neg_a = -jnp.exp(alog_ref[...]) * LOG2E

    def one_sequence(sq):
        blk = blk_ref[sq]
        sz = blk[:, 0:GROUP_WIDTH]
        xbc = _conv_silu(xe_scr.at[sq], halo_ref[sq][:, GROUP_WIDTH:], blk[:, GROUP_WIDTH:], cw_ref[...],
                         cb_ref[...], ci == 0, c)
        xs = xbc[:, 0:256]
        bmat = xbc[:, 256:512].astype(BF16)
        cmat = xbc[:, 512:768].astype(BF16)
        s = s_scr[sq]
        cb = [_dot_nt(cmat[:, gi * 128:(gi + 1) * 128], bmat[:, gi * 128:(gi + 1) * 128]) for gi in range(2)]
        y_inter = _dot(cmat, s.astype(BF16))
        dt = _softplus(_expand_small(small_ref[sq], SMALL_SDT) + dtb_ref[...])
        yield
        g = _exact_dot_left(tri, neg_a * dt)
        yield
        g_last = g[c - 1:c, :]
        da, db = _decay_diff_operands(g)
        diff = _dot_nt(_stack_heads(da, masks).astype(BF16), db.astype(BF16))
        v = xs * dt
        vend = v * jnp.exp2(g_last - g)
        ds = _dot_tn(bmat, vend.astype(BF16))
        yield
        decay = jnp.where(causal4, jnp.exp2(diff), 0.0)
        scores = jnp.concatenate([cb[0], cb[0], cb[1], cb[1]], axis=0) * decay
        pv = _dot(scores.astype(BF16), v.astype(BF16))
        s_scr[sq] = jnp.exp2(g_last) * s + jnp.where(group_mask, ds, 0.0)
        yield
        y = _unstack_heads(pv, masks, c) + y_inter * jnp.exp2(g)
        y = (y + dskip_ref[...] * xs) * _silu(sz)
        halves = [_rms_rows(y[:, gi * 128:(gi + 1) * 128]) for gi in range(2)]
        o_ref[sq] = (jnp.concatenate(halves, axis=1) * nw_ref[...]).astype(BF16)

    _round_robin([one_sequence(sq) for sq in range(nb)])

    @pl.when(ci == n_chunks - 1)
    def _():
        for sq in range(nb):
            for h in range(N_HEADS):
                gi = h // 2
                rows = jnp.where(masks[h], s_scr[sq, gi * 128:(gi + 1) * 128, :], 0.0)
                st_ref[sq, h * 128:(h + 1) * 128, :] = _extract_blocks(rows, 128, HEAD_DIM)


def _lane_rep(p):
    return jnp.repeat(p.astype(F32), HEAD_DIM).reshape(1, GROUP_WIDTH)


def _ssd_prompt(proj3, conv_w, conv_b, dt_bias, a_log, d_skip, norm_w, c):
    b, l, _ = proj3.shape
    n = l // c
    fixed = lambda bi, ci: (0, 0)
    nb = math.gcd(b, min(PROMPT_SEQS_PER_STEP, max(1, PROMPT_ROWS_PER_STEP // (l // n))))
    return pl.pallas_call(
        functools.partial(_ssd_prompt_kernel, c=c, n_chunks=n, nb=nb),
        grid=(b // nb, n),
        in_specs=[pl.BlockSpec((nb, c, 1024), lambda bi, ci: (bi, ci, COL_SSD // 1024)),
                  pl.BlockSpec((nb, 8, 1024), lambda bi, ci: (bi, jnp.maximum(ci * (c // 8) - 1, 0), COL_SSD // 1024)),
                  pl.BlockSpec((nb, c, 128), lambda bi, ci: (bi, ci, COL_SMALL // 128)),
                  pl.BlockSpec((CONV_WIDTH, 768), fixed),
                  pl.BlockSpec((1, 768), fixed),
                  pl.BlockSpec((1, GROUP_WIDTH), fixed),
                  pl.BlockSpec((1, GROUP_WIDTH), fixed),
                  pl.BlockSpec((1, GROUP_WIDTH), fixed),
                  pl.BlockSpec((1, GROUP_WIDTH), fixed)],
        out_specs=[pl.BlockSpec((nb, c, GROUP_WIDTH), lambda bi, ci: (bi, ci, 0)),
                   pl.BlockSpec((nb, N_HEADS * SSD_STATE, HEAD_DIM), lambda bi, ci: (bi, 0, 0))],
        out_shape=[jax.ShapeDtypeStruct((b, l, GROUP_WIDTH), BF16),
                   jax.ShapeDtypeStruct((b, N_HEADS * SSD_STATE, HEAD_DIM), F32)],
        scratch_shapes=[pltpu.VMEM((nb, GROUP_WIDTH, GROUP_WIDTH), F32),
                        pltpu.VMEM((nb, c + 8, 768), F32)],
        compiler_params=pltpu.CompilerParams(dimension_semantics=("arbitrary", "arbitrary"),
                                             vmem_limit_bytes=VMEM_LIMIT),
        name="ssd_prompt",
    )(proj3, proj3, proj3, conv_w, conv_b.reshape(1, 768), _lane_rep(dt_bias), _lane_rep(a_log),
      _lane_rep(d_skip), norm_w.reshape(1, GROUP_WIDTH))


def _gdn_prompt_kernel(blk_ref, halo_ref, small_ref, cw_ref, alog_ref, dtb_ref, nw_ref,
                       o_ref, st_ref, s_scr, xe_scr, m_scr, *, c, n_chunks, nb):
    ci = pl.program_id(1)
    hc = 2 * c
    n_lvl = int(math.log2(c))

    @pl.when(ci == 0)
    def _():
        s_scr[...] = jnp.zeros_like(s_scr)

    @pl.when((pl.program_id(0) == 0) & (ci == 0))
    def _():
        rr = _iota((hc, hc), 0)
        cc = _iota((hc, hc), 1)
        same = (rr // c) == (cc // c)
        m_scr[0] = (same & (rr >= cc)).astype(F32)
        m_scr[1] = (same & (rr > cc)).astype(F32)
        for lv in range(n_lvl):
            sz = 1 << lv
            off = ((rr // (2 * sz)) == (cc // (2 * sz))) & (((rr // sz) % 2) == 1) & (((cc // sz) % 2) == 0)
            m_scr[2 + lv] = off.astype(F32)

    ones_bd = _block_ones(GROUP_WIDTH, HEAD_DIM)
    bd_mask = _block_mask(GROUP_WIDTH, HEAD_DIM, HEAD_DIM)
    masks = _head_masks()
    pair_masks = [masks[0:2], masks[2:4]]
    tri = _lower_tri(c)
    neg_a = -jnp.exp(alog_ref[...]) * LOG2E

    def one_sequence(sq):
        blk = blk_ref[sq]
        gz = blk[:, 768:1024]
        qkv = _conv_silu(xe_scr.at[sq], halo_ref[sq][:, 0:768], blk[:, 0:768], cw_ref[...], None, ci == 0, c)
        gq, gk, v = qkv[:, 0:256], qkv[:, 256:512], qkv[:, 512:768]
        ss_qk = _head_sumsq(jnp.concatenate([gq, gk], axis=0), ones_bd)
        q = gq * lax.rsqrt(ss_qk[0:c] + EPS) * (HEAD_DIM ** -0.5)
        k = gk * lax.rsqrt(ss_qk[c:2 * c] + EPS)
        yield
        small = small_ref[sq]
        gb_x, ga_x = _expand_small(small, SMALL_GB, SMALL_GA)
        beta = _sigmoid(gb_x)
        g = _exact_dot_left(tri, neg_a * _softplus(ga_x + dtb_ref[...]))
        g_last = g[c - 1:c, :]
        eg = jnp.exp2(g)
        yield
        da, db = _decay_diff_operands(g)
        bk = beta * k
        bkg = bk * eg
        bv = beta * v
        a_mat, p_mat, x = [], [], []
        for pm in pair_masks:
            diff = _dot_nt(_stack_heads(da, pm).astype(BF16), _stack_heads(db, pm).astype(BF16))
            k_st = _stack_heads(k, pm).astype(BF16)
            kq = _dot_nt(jnp.concatenate([_stack_heads(bk, pm), _stack_heads(q, pm)], axis=0).astype(BF16), k_st)
            kk, qk = kq[0:hc], kq[hc:2 * hc]
            decay = jnp.exp2(jnp.minimum(diff, 0.0))
            a_mat.append(kk * (decay * m_scr[1]))
            p_mat.append((qk * (decay * m_scr[0])).astype(BF16))
            x.append(jnp.concatenate([_stack_heads(bkg, pm), _stack_heads(bv, pm)], axis=1))
        yield

        n_mat = [-(a * m_scr[2]) for a in a_mat]
        for lv in range(1, n_lvl):
            a_off = [a * m_scr[2 + lv] for a in a_mat]
            m = [ao + _dot(ao.astype(BF16), n.astype(BF16)) for ao, n in zip(a_off, n_mat)]
            yield
            n_mat = [n - mm - _dot(n.astype(BF16), mm.astype(BF16)) for n, mm in zip(n_mat, m)]
            yield
        x = [xx + _dot(n.astype(BF16), xx.astype(BF16)) for xx, n in zip(x, n_mat)]
        yield
        w = x[0][0:c, 0:256] + x[0][c:2 * c, 0:256] + x[1][0:c, 0:256] + x[1][c:2 * c, 0:256]
        u0 = x[0][0:c, 256:512] + x[0][c:2 * c, 256:512] + x[1][0:c, 256:512] + x[1][c:2 * c, 256:512]

        s = s_scr[sq]
        s_bf = s.astype(BF16)
        ws_qs = _dot(jnp.concatenate([w, q * eg], axis=0).astype(BF16), s_bf)
        u = u0 - ws_qs[0:c]
        o = ws_qs[c:2 * c]
        yield
        pu = [_dot(pmat, _stack_heads(u, pm).astype(BF16)) for pmat, pm in zip(p_mat, pair_masks)]
        kend = k * jnp.exp2(g_last - g)
        ds = _dot_tn(kend.astype(BF16), u.astype(BF16))
        yield
        for pu_p in pu:
            o = o + pu_p[0:c] + pu_p[c:2 * c]
        s_scr[sq] = jnp.exp2(g_last) * s + jnp.where(bd_mask, ds, 0.0)
        ss = _head_sumsq(o, ones_bd)
        o_ref[sq] = (o * lax.rsqrt(ss * (1.0 / HEAD_DIM) + EPS) * nw_ref[...] * _silu(gz)).astype(BF16)

    _round_robin([one_sequence(sq) for sq in range(nb)])

    @pl.when(ci == n_chunks - 1)
    def _():
        for sq in range(nb):
            st_ref[sq] = _extract_blocks(s_scr[sq], GROUP_WIDTH, HEAD_DIM)


def _gdn_prompt(proj3, conv_w, a_log, dt_bias, norm_w, c):
    b, l, _ = proj3.shape
    n = l // c
    fixed = lambda bi, ci: (0, 0)
    nb = math.gcd(b, GDN_SEQS_PER_STEP)
    hc = 2 * c
    return pl.pallas_call(
        functools.partial(_gdn_prompt_kernel, c=c, n_chunks=n, nb=nb),
        grid=(b // nb, n),
        in_specs=[pl.BlockSpec((nb, c, 1024), lambda bi, ci: (bi, ci, COL_GDN // 1024)),
                  pl.BlockSpec((nb, 8, 1024), lambda bi, ci: (bi, jnp.maximum(ci * (c // 8) - 1, 0), COL_GDN // 1024)),
                  pl.BlockSpec((nb, c, 128), lambda bi, ci: (bi, ci, COL_SMALL // 128)),
                  pl.BlockSpec((CONV_WIDTH, 768), fixed),
                  pl.BlockSpec((1, GROUP_WIDTH), fixed),
                  pl.BlockSpec((1, GROUP_WIDTH), fixed),
                  pl.BlockSpec((1, GROUP_WIDTH), fixed)],
        out_specs=[pl.BlockSpec((nb, c, GROUP_WIDTH), lambda bi, ci: (bi, ci, 0)),
                   pl.BlockSpec((nb, GROUP_WIDTH, HEAD_DIM), lambda bi, ci: (bi, 0, 0))],
        out_shape=[jax.ShapeDtypeStruct((b, l, GROUP_WIDTH), BF16),
                   jax.ShapeDtypeStruct((b, GROUP_WIDTH, HEAD_DIM), F32)],
        scratch_shapes=[pltpu.VMEM((nb, GROUP_WIDTH, GROUP_WIDTH), F32),
                        pltpu.VMEM((nb, c + 8, 768), F32),
                        pltpu.VMEM((2 + int(math.log2(c)), hc, hc), F32)],
        compiler_params=pltpu.CompilerParams(dimension_semantics=("arbitrary", "arbitrary"),
                                             vmem_limit_bytes=VMEM_LIMIT),
        name="gdn_prompt",
    )(proj3, proj3, proj3, conv_w, _lane_rep(a_log), _lane_rep(dt_bias),
      jnp.tile(norm_w.astype(F32), N_HEADS).reshape(1, GROUP_WIDTH))


HGRN_SUB = 16
LOG2E = 1.4426950408889634


def _hgrn_lower_bound(logits, layer):
    rows = [logits[d:d + 1, :] for d in range(DEPTH)]
    mx = functools.reduce(jnp.maximum, rows)
    es = [jnp.exp(x - mx) for x in rows]
    tot = functools.reduce(lambda a, b: a + b, es)
    sm = [e / tot for e in es]
    acc = sm[0]
    for d in range(1, layer + 1):
        acc = acc + sm[d]
    return acc - sm[0]


def _hgrn_prompt_kernel(blk_ref, lb_ref, nw_ref, o_ref, st_ref, s_scr, *, r, n_chunks, layer, nb):
    ci = pl.program_id(1)
    sub = HGRN_SUB
    n_sub = r // sub

    @pl.when(ci == 0)
    def _():
        s_scr[...] = jnp.zeros_like(s_scr)

    lb = _hgrn_lower_bound(lb_ref[...], layer)
    rr = _iota((r, r), 0)
    cc = _iota((r, r), 1)
    same_sub = (rr // sub) == (cc // sub)
    cum_sel = (same_sub & (rr >= cc)).astype(BF16)
    tot_sel = same_sub.astype(BF16)
    ones_bd = _block_ones(GROUP_WIDTH, HEAD_DIM)
    masks = _head_masks()
    half = sub // 2
    i8 = _iota((half, 1), 0)

    def one_sequence(sq):
        blk = blk_ref[sq]
        hq, hf, hi, hg = (blk[:, i * GROUP_WIDTH:(i + 1) * GROUP_WIDTH] for i in range(4))
        f = lb + (1.0 - lb) * _sigmoid(hf)
        q = _sigmoid(hq)
        k = 1.0 - f
        v = hi
        logf = jnp.log(f)
        g = _exact_dot_left(cum_sel, logf)
        g_tot = _exact_dot_left(tot_sel, logf)
        yield
        a2 = (g + jnp.log(q)) * LOG2E
        h2 = (g - jnp.log(k)) * LOG2E
        gt2 = g_tot * LOG2E
        qt = _stack_heads(jnp.exp2(a2), masks).astype(BF16)
        kh = _stack_heads(jnp.exp2(gt2 - h2), masks).astype(BF16)
        v_heads = [v[:, h * HEAD_DIM:(h + 1) * HEAD_DIM].astype(BF16) for h in range(N_HEADS)]

        def sub_rows(x_st, lo):
            return jnp.concatenate([x_st[h * r + lo:h * r + lo + sub] for h in range(N_HEADS)], axis=0)

        s = s_scr[sq]
        outs = []
        for j in range(n_sub):
            lo = j * sub
            v_j = v[lo:lo + sub]
            a_lo, a_hi, h_j = a2[lo:lo + half], a2[lo + half:lo + sub], h2[lo:lo + sub]
            lo_blocks, hi_blocks = [], []
            for jj in range(sub):
                h_row = h_j[jj:jj + 1, :]
                if jj < half:
                    e_lo = jnp.exp2(a_lo - h_row)
                    lo_blocks.append(e_lo if jj == 0 else jnp.where(i8 >= jj, e_lo, 0.0))
                    hi_blocks.append(jnp.exp2(a_hi - h_row))
                else:
                    e_hi = jnp.exp2(a_hi - h_row)
                    hi_blocks.append(e_hi if jj == half else jnp.where(i8 >= jj - half, e_hi, 0.0))
            sc = _dot(jnp.concatenate(lo_blocks + hi_blocks, axis=0).astype(BF16), ones_bd)
            oi = _dot_nt(sub_rows(qt, lo), s.astype(BF16))
            v_rows = jnp.concatenate([vh[lo:lo + sub] for vh in v_heads], axis=0)
            ds = _dot_tn(v_rows, sub_rows(kh, lo))
            yield
            o_inter = jnp.concatenate([oi[h * sub:(h + 1) * sub] for h in range(N_HEADS)], axis=1)
            n_lo = half * half
            o_lo = sc[0:half] * v_j[0:1, :]
            o_hi = sc[n_lo:n_lo + half] * v_j[0:1, :]
            for jj in range(1, sub):
                if jj < half:
                    o_lo = o_lo + sc[jj * half:(jj + 1) * half] * v_j[jj:jj + 1, :]
                o_hi = o_hi + sc[n_lo + jj * half:n_lo + (jj + 1) * half] * v_j[jj:jj + 1, :]
            outs.append(jnp.concatenate([o_lo, o_hi], axis=0) + o_inter)
            s = jnp.exp2(gt2[lo:lo + 1, :]) * s + ds
        s_scr[sq] = s

        o = jnp.concatenate(outs, axis=0)
        ss = _head_sumsq(o, ones_bd)
        yield
        o_ref[sq] = (o * lax.rsqrt(ss * (1.0 / HEAD_DIM) + EPS) * nw_ref[...] * _silu(hg)).astype(BF16)

    _round_robin([one_sequence(sq) for sq in range(nb)])

    @pl.when(ci == n_chunks - 1)
    def _():
        for sq in range(nb):
            st_ref[sq] = s_scr[sq].T


def _hgrn_prompt(proj3, lb_logits, norm_w, layer, r):
    b, l, _ = proj3.shape
    n = l // r
    fixed = lambda bi, ci: (0, 0)
    nb = math.gcd(b, min(PROMPT_SEQS_PER_STEP, max(1, PROMPT_ROWS_PER_STEP // (l // n))))
    return pl.pallas_call(
        functools.partial(_hgrn_prompt_kernel, r=r, n_chunks=n, layer=layer, nb=nb),
        grid=(b // nb, n),
        in_specs=[pl.BlockSpec((nb, r, 1024), lambda bi, ci: (bi, ci, COL_HGRN // 1024)),
                  pl.BlockSpec((DEPTH, GROUP_WIDTH), fixed),
                  pl.BlockSpec((1, GROUP_WIDTH), fixed)],
        out_specs=[pl.BlockSpec((nb, r, GROUP_WIDTH), lambda bi, ci: (bi, ci, 0)),
                   pl.BlockSpec((nb, GROUP_WIDTH, HEAD_DIM), lambda bi, ci: (bi, 0, 0))],
        out_shape=[jax.ShapeDtypeStruct((b, l, GROUP_WIDTH), BF16),
                   jax.ShapeDtypeStruct((b, GROUP_WIDTH, HEAD_DIM), F32)],
        scratch_shapes=[pltpu.VMEM((nb, HEAD_DIM, GROUP_WIDTH), F32)],
        compiler_params=pltpu.CompilerParams(dimension_semantics=("arbitrary", "arbitrary"),
                                             vmem_limit_bytes=VMEM_LIMIT),
        name="hgrn_prompt",
    )(proj3, lb_logits.astype(F32), jnp.tile(norm_w.astype(F32), N_HEADS).reshape(1, GROUP_WIDTH))


DEC_SEQS = 128
DEC_LEN = 4


def _head_rows(h):
    return pl.ds(pl.multiple_of(h * HEAD_DIM, HEAD_DIM), HEAD_DIM)


def _recur_head(load_s, store_s, n_keys, decay_fn, k_fn, q_fn, v_blocks):
    def body(kk, accs):
        s = load_s(kk)
        accs = list(accs)
        for t in range(DEC_LEN):
            s = decay_fn(t, kk) * s + k_fn(t, kk) * v_blocks[t]
            accs[t] = accs[t] + q_fn(t, kk) * s
        store_s(kk, s)
        return tuple(accs)

    zero = jnp.zeros((HEAD_DIM, DEC_SEQS), F32)
    return lax.fori_loop(0, n_keys, body, (zero,) * DEC_LEN)


def _dec_ret_kernel(blk_ref, cos_ref, sin_ref, st_ref, o_ref, so_ref, q_scr, k_scr, v_scr, o_scr):
    h = pl.program_id(0)

    @pl.when(h == 0)
    def _():
        for t in range(DEC_LEN):
            blk = blk_ref[t]
            rq, rk, rv = blk[:, 0:256], blk[:, 256:512], blk[:, 512:768]
            cosv, sinv = cos_ref[t:t + 1, :], sin_ref[t:t + 1, :]
            q_scr[t] = (rq * cosv + _swap_halves(rq) * sinv).T
            k_scr[t] = ((rk * cosv + _swap_halves(rk) * sinv) * (HEAD_DIM ** -0.5)).T
            v_scr[t] = rv.T

    lg = jnp.where(h == 0, LOG_GAMMA[0], jnp.where(h == 1, LOG_GAMMA[1], jnp.where(h == 2, LOG_GAMMA[2], LOG_GAMMA[3])))
    gamma = jnp.exp(jnp.full((1, DEC_SEQS), lg, F32))
    hr = _head_rows(h)
    v_blocks = [v_scr[t, hr, :] for t in range(DEC_LEN)]
    accs = _recur_head(
        lambda kk: st_ref[0, 0, kk], functools.partial(_store_state, so_ref), HEAD_DIM,
        lambda t, kk: gamma,
        lambda t, kk: k_scr[t, pl.ds(h * HEAD_DIM + kk, 1), :],
        lambda t, kk: q_scr[t, pl.ds(h * HEAD_DIM + kk, 1), :],
        v_blocks)
    for t in range(DEC_LEN):
        o_scr[t, hr, :] = accs[t]

    @pl.when(h == N_HEADS - 1)
    def _():
        ones_bd = _block_ones(GROUP_WIDTH, HEAD_DIM)
        for t in range(DEC_LEN):
            o = o_scr[t].T
            ss = _head_sumsq(o, ones_bd)
            o_ref[t] = (o * lax.rsqrt(ss * (1.0 / HEAD_DIM) + EPS) * _silu(blk_ref[t][:, 768:1024])).astype(BF16)


def _store_state(so_ref, kk, s):
    so_ref[0, 0, kk] = s


def _without_ref(kernel_fn, idx):
    def wrapped(*refs):
        return kernel_fn(*refs[:idx], *refs[idx + 1:])
    return wrapped


def _zero_later_layers(kernel_fn, so_index):
    def wrapped(*refs):
        kernel_fn(*refs)
        so_ref = refs[so_index]
        so_ref[1:] = jnp.zeros((so_ref.shape[0] - 1,) + so_ref.shape[1:], F32)
    return wrapped


def _dec_call(kernel_fn, name, col, ins, in_specs, n_tok_scr, state_view, layer, carried, extra_scratch=()):
    blk_spec = pl.BlockSpec((DEC_LEN, DEC_SEQS, 1024), lambda h: (0, 0, col // 1024))
    st_spec = pl.BlockSpec((1, 1) + state_view.shape[2:], lambda h: (layer, h, 0, 0, 0))
    tok_scr = pltpu.VMEM((DEC_LEN, GROUP_WIDTH, DEC_SEQS), F32)
    ins = tuple(ins) + (state_view,)
    specs = [blk_spec] + in_specs + [st_spec]
    if carried is None:
        assert layer == 0
        so_spec = pl.BlockSpec((DEPTH, 1) + state_view.shape[2:], lambda h: (0, h, 0, 0, 0))
        kernel_fn = _zero_later_layers(kernel_fn, len(ins) + 1)
        aliases = {}
    else:
        so_spec = st_spec
        kernel_fn = _without_ref(kernel_fn, len(ins))
        aliases = {len(ins): 1}
        ins = ins + (carried,)
        specs = specs + [pl.BlockSpec(memory_space=pl.ANY)]
    return pl.pallas_call(
        kernel_fn,
        grid=(N_HEADS,),
        in_specs=specs,
        out_specs=[pl.BlockSpec((DEC_LEN, DEC_SEQS, GROUP_WIDTH), lambda h: (0, 0, 0)), so_spec],
        out_shape=[jax.ShapeDtypeStruct((DEC_LEN, DEC_SEQS, GROUP_WIDTH), BF16),
                   jax.ShapeDtypeStruct(state_view.shape, F32)],
        scratch_shapes=[tok_scr] * n_tok_scr + list(extra_scratch),
        input_output_aliases=aliases,
        compiler_params=pltpu.CompilerParams(dimension_semantics=("arbitrary",), vmem_limit_bytes=VMEM_LIMIT),
        name=name,
    )(*ins)


def _fixed1(shape):
    return pl.BlockSpec(shape, lambda h: (0,) * len(shape))


def _dec_ret(projd, cos_t, sin_t, state_view, layer, carried):
    return _dec_call(_dec_ret_kernel, "retention_decode", COL_RET, (projd, cos_t, sin_t),
                     [_fixed1((DEC_LEN, GROUP_WIDTH)), _fixed1((DEC_LEN, GROUP_WIDTH))], 4,
                     state_view, layer, carried)


def _dec_hgrn_kernel(blk_ref, lb_ref, nw_ref, st_ref, o_ref, so_ref, q_scr, k_scr, v_scr, f_scr, o_scr, *, layer):
    h = pl.program_id(0)

    @pl.when(h == 0)
    def _():
        lb = _hgrn_lower_bound(lb_ref[...], layer)
        for t in range(DEC_LEN):
            blk = blk_ref[t]
            f = lb + (1.0 - lb) * _sigmoid(blk[:, 256:512])
            q_scr[t] = _sigmoid(blk[:, 0:256]).T
            k_scr[t] = (1.0 - f).T
            v_scr[t] = blk[:, 512:768].T
            f_scr[t] = f.T

    hr = _head_rows(h)
    v_blocks = [v_scr[t, hr, :] for t in range(DEC_LEN)]
    row = lambda scr: (lambda t, kk: scr[t, pl.ds(h * HEAD_DIM + kk, 1), :])
    accs = _recur_head(lambda kk: st_ref[0, 0, kk], functools.partial(_store_state, so_ref), HEAD_DIM,
                       row(f_scr), row(k_scr), row(q_scr), v_blocks)
    for t in range(DEC_LEN):
        o_scr[t, hr, :] = accs[t]

    @pl.when(h == N_HEADS - 1)
    def _():
        ones_bd = _block_ones(GROUP_WIDTH, HEAD_DIM)
        for t in range(DEC_LEN):
            o = o_scr[t].T
            ss = _head_sumsq(o, ones_bd)
            o_ref[t] = (o * lax.rsqrt(ss * (1.0 / HEAD_DIM) + EPS) * nw_ref[...]
                        * _silu(blk_ref[t][:, 768:1024])).astype(BF16)


def _dec_hgrn(projd, lb_logits, norm_w, state_view, layer, carried):
    return _dec_call(functools.partial(_dec_hgrn_kernel, layer=layer), "hgrn_decode", COL_HGRN,
                     (projd, lb_logits.astype(F32), jnp.tile(norm_w.astype(F32), N_HEADS).reshape(1, GROUP_WIDTH)),
                     [_fixed1((DEPTH, GROUP_WIDTH)), _fixed1((1, GROUP_WIDTH))], 5, state_view, layer, carried)


def _hist_spec(layer):
    return pl.BlockSpec((1, CONV_WIDTH - 1, DEC_SEQS, 768), lambda h: (layer, 0, 0, 0))


def _dec_conv_silu(hist_ref, xs, w, bias):
    xe = [hist_ref[0, j] for j in range(CONV_WIDTH - 1)] + xs
    out = []
    for t in range(DEC_LEN):
        y = xe[t] * w[0:1, :]
        for j in range(1, CONV_WIDTH):
            y = y + xe[t + j] * w[j:j + 1, :]
        if bias is not None:
            y = y + bias
        out.append(_silu(y))
    return out


def _dec_ssd_kernel(blk_ref, small_ref, hist_ref, cw_ref, cb_ref, dtb_ref, alog_ref, dskip_ref, nw_ref, st_ref,
                    o_ref, so_ref, c_scr, b_scr, v_scr, a_scr, o_scr, x_scr):
    h = pl.program_id(0)

    @pl.when(h == 0)
    def _():
        xbc = _dec_conv_silu(hist_ref, [blk_ref[t][:, 256:1024] for t in range(DEC_LEN)], cw_ref[...], cb_ref[...])
        for t in range(DEC_LEN):
            xs = xbc[t][:, 0:256]
            dt = _softplus(_expand_small(small_ref[t], SMALL_SDT) + dtb_ref[...])
            x_scr[t] = xs
            v_scr[t] = (xs * dt).T
            b_scr[t] = xbc[t][:, 256:512].T
            c_scr[t] = xbc[t][:, 512:768].T
            a_scr[t] = jnp.exp(-jnp.exp(alog_ref[...]) * dt).T

    hr = _head_rows(h)
    g0 = (h // 2) * SSD_STATE
    v_blocks = [v_scr[t, hr, :] for t in range(DEC_LEN)]
    accs = _recur_head(
        lambda kk: st_ref[0, 0, kk], functools.partial(_store_state, so_ref), SSD_STATE,
        lambda t, kk: a_scr[t, pl.ds(h * HEAD_DIM, 1), :],
        lambda t, kk: b_scr[t, pl.ds(g0 + kk, 1), :],
        lambda t, kk: c_scr[t, pl.ds(g0 + kk, 1), :],
        v_blocks)
    for t in range(DEC_LEN):
        o_scr[t, hr, :] = accs[t]

    @pl.when(h == N_HEADS - 1)
    def _():
        for t in range(DEC_LEN):
            y = (o_scr[t].T + dskip_ref[...] * x_scr[t]) * _silu(blk_ref[t][:, 0:256])
            halves = [_rms_rows(y[:, gi * 128:(gi + 1) * 128]) for gi in range(2)]
            o_ref[t] = (jnp.concatenate(halves, axis=1) * nw_ref[...]).astype(BF16)


def _dec_ssd(projd, hist, conv_w, conv_b, dt_bias, a_log, d_skip, norm_w, state_view, layer, carried):
    small_spec = pl.BlockSpec((DEC_LEN, DEC_SEQS, 128), lambda h: (0, 0, COL_SMALL // 128))
    return _dec_call(_dec_ssd_kernel, "ssd_decode", COL_SSD,
                     (projd, projd, hist, conv_w, conv_b.reshape(1, 768), _lane_rep(dt_bias), _lane_rep(a_log),
                      _lane_rep(d_skip), norm_w.reshape(1, GROUP_WIDTH)),
                     [small_spec, _hist_spec(layer), _fixed1((CONV_WIDTH, 768)),
                      _fixed1((1, 768))] + [_fixed1((1, GROUP_WIDTH))] * 4, 5, state_view, layer, carried,
                     extra_scratch=[pltpu.VMEM((DEC_LEN, DEC_SEQS, GROUP_WIDTH), F32)])


def _dec_gdn_kernel(blk_ref, small_ref, hist_ref, cw_ref, alog_ref, dtb_ref, nw_ref, st_ref,
                    o_ref, so_ref, q_scr, k_scr, v_scr, a_scr, b_scr, o_scr):
    h = pl.program_id(0)

    @pl.when(h == 0)
    def _():
        ones_bd = _block_ones(GROUP_WIDTH, HEAD_DIM)
        qkv = _dec_conv_silu(hist_ref, [blk_ref[t][:, 0:768] for t in range(DEC_LEN)], cw_ref[...], None)
        for t in range(DEC_LEN):
            gq, gk, gv = qkv[t][:, 0:256], qkv[t][:, 256:512], qkv[t][:, 512:768]
            q_scr[t] = (gq * lax.rsqrt(_head_sumsq(gq, ones_bd) + EPS) * (HEAD_DIM ** -0.5)).T
            k_scr[t] = (gk * lax.rsqrt(_head_sumsq(gk, ones_bd) + EPS)).T
            v_scr[t] = gv.T
            small = small_ref[t]
            gb_x, ga_x = _expand_small(small, SMALL_GB, SMALL_GA)
            b_scr[t] = _sigmoid(gb_x).T
            la = -jnp.exp(alog_ref[...]) * _softplus(ga_x + dtb_ref[...])
            a_scr[t] = jnp.exp(la).T

    hr = _head_rows(h)
    one_row = pl.ds(h * HEAD_DIM, 1)
    zero = jnp.zeros((HEAD_DIM, DEC_SEQS), F32)
    for t in range(DEC_LEN):
        a = a_scr[t, one_row, :]
        cur = st_ref if t == 0 else so_ref

        def kts(kk, r):
            return r + k_scr[t, pl.ds(h * HEAD_DIM + kk, 1), :] * cur[0, 0, kk]

        r = lax.fori_loop(0, HEAD_DIM, kts, zero)
        u = b_scr[t, one_row, :] * (v_scr[t, hr, :] - a * r)

        def upd(kk, acc):
            s = a * cur[0, 0, kk] + k_scr[t, pl.ds(h * HEAD_DIM + kk, 1), :] * u
            so_ref[0, 0, kk] = s
            return acc + q_scr[t, pl.ds(h * HEAD_DIM + kk, 1), :] * s

        o_scr[t, hr, :] = lax.fori_loop(0, HEAD_DIM, upd, zero)

    @pl.when(h == N_HEADS - 1)
    def _():
        ones_bd = _block_ones(GROUP_WIDTH, HEAD_DIM)
        for t in range(DEC_LEN):
            o = o_scr[t].T
            ss = _head_sumsq(o, ones_bd)
            o_ref[t] = (o * lax.rsqrt(ss * (1.0 / HEAD_DIM) + EPS) * nw_ref[...]
                        * _silu(blk_ref[t][:, 768:1024])).astype(BF16)


def _dec_gdn(projd, hist, conv_w, a_log, dt_bias, norm_w, state_view, layer, carried):
    small_spec = pl.BlockSpec((DEC_LEN, DEC_SEQS, 128), lambda h: (0, 0, COL_SMALL // 128))
    return _dec_call(_dec_gdn_kernel, "gdn_decode", COL_GDN,
                     (projd, projd, hist, conv_w, _lane_rep(a_log), _lane_rep(dt_bias),
                      jnp.tile(norm_w.astype(F32), N_HEADS).reshape(1, GROUP_WIDTH)),
                     [small_spec, _hist_spec(layer), _fixed1((CONV_WIDTH, 768))]
                     + [_fixed1((1, GROUP_WIDTH))] * 3, 6, state_view, layer, carried)


W_PREP_ROWS = 128


def _w_in_t_prep_kernel(a_ref, b_ref, o_ref):
    j = pl.program_id(0)
    n_plain = COL_RET // W_PREP_ROWS
    n_main = COL_SMALL // W_PREP_ROWS
    row = _iota((W_PREP_ROWS, 1), 0)
    for l in range(DEPTH):
        a = a_ref[:, l, :]
        b = b_ref[:, l, :]
        shifted = jnp.concatenate([a[8:], b[:8]], axis=0)
        small = jnp.where(row < 8, a, jnp.where(row < 12, b, 0.0))
        out = jnp.where(j < n_plain, a, jnp.where(j < n_main, shifted, small))
        o_ref[l] = out.astype(BF16)


def _prep_w_in_t(w_in):
    wt = jnp.transpose(w_in, (2, 0, 1))
    n_plain = COL_RET // W_PREP_ROWS
    n_main = COL_SMALL // W_PREP_ROWS

    def a_idx(j):
        return (jnp.where(j < n_main, j, n_plain), 0, 0)

    def b_idx(j):
        return (jnp.where(j < n_plain, j, jnp.minimum(j + 1, n_main)), 0, 0)

    blk = (W_PREP_ROWS, DEPTH, D_MODEL)
    return pl.pallas_call(
        _w_in_t_prep_kernel,
        grid=(P_PAD // W_PREP_ROWS,),
        in_specs=[pl.BlockSpec(blk, a_idx), pl.BlockSpec(blk, b_idx)],
        out_specs=pl.BlockSpec((DEPTH, W_PREP_ROWS, D_MODEL), lambda j: (0, j, 0)),
        out_shape=jax.ShapeDtypeStruct((DEPTH, P_PAD, D_MODEL), BF16),
        compiler_params=pltpu.CompilerParams(dimension_semantics=("arbitrary",), vmem_limit_bytes=VMEM_LIMIT),
        name="w_in_prep",
    )(wt, wt)


def _rotary_tables(pos):
    half = HEAD_DIM // 2
    inv_freq = RET_THETA ** (-jnp.arange(half, dtype=F32) / half)
    ang = pos.astype(F32)[:, None] * inv_freq[None, :]
    cos, sin = jnp.cos(ang), jnp.sin(ang)
    cos_t = jnp.tile(cos, (1, 2 * N_HEADS))
    sin_t = jnp.tile(jnp.concatenate([-sin, sin], axis=1), (1, N_HEADS))
    return cos_t, sin_t


RET_CHUNK = 256
SSD_CHUNK = 256
GDN_CHUNK = 64
HGRN_ROWS = 256


def _forward(x_prompt, x_sample, states, p, past_len):
    st_hg, st_gd, st_gc, st_rt, st_sd, st_sc = states
    bp, lp, _ = x_prompt.shape
    nd, ld, _ = x_sample.shape
    xp = x_prompt.astype(F32).reshape(bp * lp, D_MODEL)
    xd = jnp.transpose(x_sample.astype(F32), (1, 0, 2)).reshape(ld * nd, D_MODEL)
    cos_p, sin_p = _rotary_tables(jnp.arange(lp))
    cos_d, sin_d = _rotary_tables(past_len + jnp.arange(ld))
    outs = {k: [] for k in ("hp", "gp", "gcp", "gcs", "rp", "sp", "scp", "scs")}
    w_in_all = _prep_w_in_t(p["w_in"].astype(F32))
    wo, wu, wd = (p[k].astype(BF16) for k in ("w_out", "w_up", "w_down"))
    norm_mix = p["norm_mix"].astype(F32).reshape(DEPTH, 1, D_MODEL)
    norm_ffn = p["norm_ffn"].astype(F32).reshape(DEPTH, 1, D_MODEL)
    sv_hg, sv_gd, sv_rt, sv_sd = (jnp.transpose(s.astype(F32), (0, 2, 3, 4, 1)) for s in (st_hg, st_gd, st_rt, st_sd))
    hist_g = jnp.transpose(st_gc.astype(F32), (0, 2, 1, 3))
    hist_s = jnp.transpose(st_sc.astype(F32), (0, 2, 1, 3))
    new_hg = new_gd = new_rt = new_sd = None
    for l in range(DEPTH):
        pp = _proj(xp, norm_mix, w_in_all, l).reshape(bp, lp, P_PAD)
        pd = _proj(xd, norm_mix, w_in_all, l).reshape(ld, nd, P_PAD)

        oa, sa = _hgrn_prompt(pp, p["hgrn_lb_logits"], p["hgrn_norm"][l], l, HGRN_ROWS)
        ob, sb = _gdn_prompt(pp, p["gdn_conv_w"][l], p["gdn_a_log"][l], p["gdn_dt_bias"][l], p["gdn_norm"][l],
                             GDN_CHUNK)
        oc, sc = _ret_prompt(pp, cos_p, sin_p, RET_CHUNK)
        od, sd = _ssd_prompt(pp, p["ssd_conv_w"][l], p["ssd_conv_b"][l], p["ssd_dt_bias"][l], p["ssd_a_log"][l],
                             p["ssd_d"][l], p["ssd_norm"][l], SSD_CHUNK)
        outs["hp"].append(sa.reshape(bp, N_HEADS, HEAD_DIM, HEAD_DIM))
        outs["gp"].append(sb.reshape(bp, N_HEADS, HEAD_DIM, HEAD_DIM))
        outs["rp"].append(sc.reshape(bp, N_HEADS, HEAD_DIM, HEAD_DIM))
        outs["sp"].append(sd.reshape(bp, N_HEADS, SSD_STATE, HEAD_DIM))
        outs["gcp"].append(pp[:, lp - 3:, COL_GDN:COL_GDN + 768])
        outs["scp"].append(pp[:, lp - 3:, COL_SSD + 256:COL_SSD + 1024])
        xp = _out_ffn(xp, [o.reshape(bp * lp, GROUP_WIDTH) for o in (oa, ob, oc, od)], wo, norm_ffn, wu, wd,
                      p["norm_final"], l)

        da, new_hg = _dec_hgrn(pd, p["hgrn_lb_logits"], p["hgrn_norm"][l], sv_hg, l, new_hg)
        db, new_gd = _dec_gdn(pd, hist_g, p["gdn_conv_w"][l], p["gdn_a_log"][l], p["gdn_dt_bias"][l],
                              p["gdn_norm"][l], sv_gd, l, new_gd)
        dc, new_rt = _dec_ret(pd, cos_d, sin_d, sv_rt, l, new_rt)
        dd, new_sd = _dec_ssd(pd, hist_s, p["ssd_conv_w"][l], p["ssd_conv_b"][l], p["ssd_dt_bias"][l],
                              p["ssd_a_log"][l], p["ssd_d"][l], p["ssd_norm"][l], sv_sd, l, new_sd)
        outs["gcs"].append(jnp.transpose(pd[ld - 3:, :, COL_GDN:COL_GDN + 768], (1, 0, 2)))
        outs["scs"].append(jnp.transpose(pd[ld - 3:, :, COL_SSD + 256:COL_SSD + 1024], (1, 0, 2)))
        xd = _out_ffn(xd, [o.reshape(ld * nd, GROUP_WIDTH) for o in (da, db, dc, dd)], wo, norm_ffn, wu, wd,
                      p["norm_final"], l)

    y_prompt = xp.reshape(bp, lp, D_MODEL)
    y_sample = jnp.transpose(xd.reshape(ld, nd, D_MODEL), (1, 0, 2))
    st = {k: jnp.stack(v) for k, v in outs.items()}
    hs, gs, rs, ss = (jnp.transpose(s, (0, 4, 1, 2, 3)) for s in (new_hg, new_gd, new_rt, new_sd))
    return (y_prompt, y_sample, st["hp"], hs, st["gp"], gs, st["gcp"], st["gcs"],
            st["rp"], rs, st["sp"], ss, st["scp"], st["scs"])


def kernel(x_prompt, x_sample, state_hgrn, state_gdn, state_gdn_conv, state_ret, state_ssd, state_ssd_conv,
           norm_mix, w_in, hgrn_lb_logits, hgrn_norm, gdn_conv_w, gdn_a_log, gdn_dt_bias, gdn_norm,
           ssd_conv_w, ssd_conv_b, ssd_dt_bias, ssd_a_log, ssd_d, ssd_norm,
           w_out, norm_ffn, w_up, w_down, norm_final):
    params = dict(norm_mix=norm_mix, w_in=w_in, hgrn_lb_logits=hgrn_lb_logits, hgrn_norm=hgrn_norm,
                  gdn_conv_w=gdn_conv_w, gdn_a_log=gdn_a_log, gdn_dt_bias=gdn_dt_bias, gdn_norm=gdn_norm,
                  ssd_conv_w=ssd_conv_w, ssd_conv_b=ssd_conv_b, ssd_dt_bias=ssd_dt_bias, ssd_a_log=ssd_a_log,
                  ssd_d=ssd_d, ssd_norm=ssd_norm, w_out=w_out, norm_ffn=norm_ffn, w_up=w_up,
                  w_down=w_down, norm_final=norm_final)
    states = (state_hgrn, state_gdn, state_gdn_conv, state_ret, state_ssd, state_ssd_conv)
    return _forward(x_prompt, x_sample, states, params, 16384)
```
